```python
import math
import jax, jax.numpy as jnp
from jax import lax
import numpy as np

D_MODEL = 1024
BATCH = 8
SEQ = 2048
DEPTH = 1

N_MEM = 256
EPS = 1e-6
MLA_HEADS = 8
MLA_NOPE = 64
MLA_ROPE = 32
MLA_V = 64
MLA_Q_RANK = 384
MLA_KV_RANK = 256
ROPE_THETA = 10000.0
Q_BLOCK = 128
GLA_HEADS = 4
GLA_DK = 64
GLA_DV = 128
GLA_GATE_RANK = 16
GLA_TAU = 16.0
GLA_CHUNK = 64
X_HEADS = 4
X_DH = 128
D_FF = 2816
N_BRANCH = 3
IN_WIDTHS = (MLA_Q_RANK, MLA_KV_RANK, MLA_ROPE,
             GLA_HEADS * GLA_DK, GLA_HEADS * GLA_DK, GLA_HEADS * GLA_DV, GLA_GATE_RANK, GLA_HEADS * GLA_DV,
             X_HEADS * X_DH,
             N_BRANCH * D_MODEL)
D_IN = (MLA_Q_RANK + MLA_KV_RANK + MLA_ROPE + 2 * GLA_HEADS * GLA_DK + 2 * GLA_HEADS * GLA_DV
        + GLA_GATE_RANK + X_HEADS * X_DH + N_BRANCH * D_MODEL)

kernel_name = "hybrid_mla_gla_xattn_macaron"


def _rmsnorm(x, g):
    x32 = x.astype(jnp.float32)
    y = x32 * lax.rsqrt(jnp.mean(x32 * x32, axis=-1, keepdims=True) + EPS)
    return (y * g.astype(jnp.float32)).astype(x.dtype)


def _swiglu(x, wg, wu, wd):
    return (jax.nn.silu(x @ wg) * (x @ wu)) @ wd


def _rope(x, pos):
    half = x.shape[-1] // 2
    inv_freq = ROPE_THETA ** (-jnp.arange(half, dtype=jnp.float32) / half)
    ang = pos.astype(jnp.float32)[..., None] * inv_freq
    cos = jnp.cos(ang).astype(x.dtype)
    sin = jnp.sin(ang).astype(x.dtype)
    x1, x2 = x[..., :half], x[..., half:]
    return jnp.concatenate([x1 * cos - x2 * sin, x1 * sin + x2 * cos], axis=-1)


def _split_cols(z, widths):
    outs, start = [], 0
    for w in widths:
        outs.append(z[..., start:start + w])
        start += w
    return outs


def _mla(cq, ckv, krope, pos, q_norm, w_uq, kv_norm, w_ukv):
    B, S, _ = cq.shape
    H = MLA_HEADS
    q = (_rmsnorm(cq, q_norm) @ w_uq).reshape(B, S, H, MLA_NOPE + MLA_ROPE).transpose(0, 2, 1, 3)
    q = jnp.concatenate([q[..., :MLA_NOPE], _rope(q[..., MLA_NOPE:], pos[:, None, :])], axis=-1)
    kv = (_rmsnorm(ckv, kv_norm) @ w_ukv).reshape(B, S, H, MLA_NOPE + MLA_V).transpose(0, 2, 1, 3)
    k_nope, v = kv[..., :MLA_NOPE], kv[..., MLA_NOPE:]
    k_rope = jnp.broadcast_to(_rope(krope, pos)[:, None], (B, H, S, MLA_ROPE))
    k = jnp.concatenate([k_nope, k_rope], axis=-1)
    scale = 1.0 / math.sqrt(MLA_NOPE + MLA_ROPE)
    neg = jnp.finfo(jnp.float32).min
    outs = []
    for i in range(S // Q_BLOCK):
        s0, e = i * Q_BLOCK, (i + 1) * Q_BLOCK
        s = jnp.einsum('bhqd,bhkd->bhqk', q[:, :, s0:e], k[:, :, :e]).astype(jnp.float32) * scale
        mask = (s0 + jnp.arange(Q_BLOCK))[:, None] >= jnp.arange(e)[None, :]
        p = jax.nn.softmax(jnp.where(mask, s, neg), axis=-1).astype(v.dtype)
        outs.append(jnp.einsum('bhqk,bhkd->bhqd', p, v[:, :, :e]))
    o = jnp.concatenate(outs, axis=2)
    return o.transpose(0, 2, 1, 3).reshape(B, S, H * MLA_V)


def _gla(q, k, v, a_lr, r, w_a2, b_a, o_norm):
    B, S, _ = q.shape
    H, C = GLA_HEADS, GLA_CHUNK
    N = S // C
    f32 = jnp.float32

    def chunks(t, d):
        return t.reshape(B, N, C, H, d).transpose(0, 3, 1, 2, 4)

    qc = chunks(q.astype(f32), GLA_DK) * (GLA_DK ** -0.5)
    kc = chunks(k.astype(f32), GLA_DK)
    vc = chunks(v.astype(f32), GLA_DV)
    log_a = jax.nn.log_sigmoid((a_lr @ w_a2 + b_a).astype(f32)) / GLA_TAU
    bcum = jnp.cumsum(chunks(log_a, GLA_DK), axis=3)
    q_t = qc * jnp.exp(bcum)
    k_t = kc * jnp.exp(-bcum)
    att = jnp.einsum('bhncd,bhnjd->bhncj', q_t, k_t)
    tril = jnp.tril(jnp.ones((C, C), dtype=bool))
    o_intra = jnp.einsum('bhncj,bhnjv->bhncv', jnp.where(tril, att, 0.0), vc)
    b_last = bcum[:, :, :, -1:, :]
    dS = jnp.einsum('bhncd,bhncv->bhndv', kc * jnp.exp(b_last - bcum), vc)
    decay = jnp.exp(b_last[:, :, :, 0, :])

    def step(state, inp):
        dec, ds = inp
        return dec[..., None] * state + ds, state

    s0 = jnp.zeros((B, H, GLA_DK, GLA_DV), f32)
    _, s_prev = lax.scan(step, s0, (decay.transpose(2, 0, 1, 3), dS.transpose(2, 0, 1, 3, 4)))
    o_inter = jnp.einsum('bhncd,bhndv->bhncv', q_t, s_prev.transpose(1, 2, 0, 3, 4))
    o = (o_intra + o_inter).reshape(B, H, S, GLA_DV)
    o = _rmsnorm(o, o_norm)
    o = o.transpose(0, 2, 1, 3).reshape(B, S, H * GLA_DV).astype(r.dtype)
    return o * jax.nn.silu(r)


def _cross(xq, mem_n, w_kv):
    B, S, _ = xq.shape
    M = mem_n.shape[1]
    kv = (mem_n @ w_kv).reshape(B, M, 2, X_HEADS, X_DH)
    q = xq.reshape(B, S, X_HEADS, X_DH)
    s = jnp.einsum('bshd,bmhd->bhsm', q, kv[:, :, 0]).astype(jnp.float32) / math.sqrt(X_DH)
    p = jax.nn.softmax(s, axis=-1).astype(kv.dtype)
    o = jnp.einsum('bhsm,bmhd->bshd', p, kv[:, :, 1])
    return o.reshape(B, S, X_HEADS * X_DH)


def _mixer(h, mem, positions, mix_norm, mem_norm, w_in, gate_bias,
           mla_q_norm, mla_w_uq, mla_kv_norm, mla_w_ukv, mla_w_o,
           gla_w_a2, gla_b_a, gla_o_norm, gla_w_o, x_w_kv, x_w_o, w_out):
    B, S, D = h.shape
    u = _rmsnorm(h, mix_norm)
    z = u @ w_in
    cq, ckv, krope, gq, gk, gv, ga, gr, xq, gz = _split_cols(z, IN_WIDTHS)
    o_mla = _mla(cq, ckv, krope, positions, mla_q_norm, mla_w_uq, mla_kv_norm, mla_w_ukv) @ mla_w_o
    o_gla = _gla(gq, gk, gv, ga, gr, gla_w_a2, gla_b_a, gla_o_norm) @ gla_w_o
    o_x = _cross(xq, _rmsnorm(mem, mem_norm), x_w_kv) @ x_w_o
    gates = jax.nn.sigmoid(gz.reshape(B, S, N_BRANCH, D) + gate_bias)
    merged = gates[:, :, 0] * o_mla + gates[:, :, 1] * o_gla + gates[:, :, 2] * o_x
    return merged @ w_out


def setup_inputs(seed: int = 0) -> dict:
    key = jax.random.key(seed)
    ks = iter(jax.random.split(key, 40))
    L, D, F = DEPTH, D_MODEL, D_FF

    def w(shape, fan_in):
        return jax.random.normal(next(ks), shape, jnp.float32) * (fan_in ** -0.5)

    def gain(shape):
        return 1.0 + 0.05 * jax.random.normal(next(ks), shape, jnp.float32)

    def bias(shape, s=0.1):
        return s * jax.random.normal(next(ks), shape, jnp.float32)

    x = jax.random.normal(next(ks), (BATCH, SEQ, D), jnp.float32)
    mem = jax.random.normal(next(ks), (BATCH, N_MEM, D), jnp.float32)
    start = jax.random.randint(next(ks), (BATCH, 1), 0, 4096, dtype=jnp.int32)
    positions = (start + jnp.arange(SEQ, dtype=jnp.int32)[None, :]).astype(jnp.int32)
    return {
        "x": x, "mem": mem, "positions": positions,
        "ffn1_norm": gain((L, D)), "ffn1_wg": w((L, D, F), D), "ffn1_wu": w((L, D, F), D), "ffn1_wd": w((L, F, D), F),
        "mix_norm": gain((L, D)), "mem_norm": gain((L, D)),
        "w_in": w((L, D, D_IN), D), "gate_bias": bias((L, N_BRANCH, D)),
        "mla_q_norm": gain((L, MLA_Q_RANK)),
        "mla_w_uq": w((L, MLA_Q_RANK, MLA_HEADS * (MLA_NOPE + MLA_ROPE)), MLA_Q_RANK),
        "mla_kv_norm": gain((L, MLA_KV_RANK)),
        "mla_w_ukv": w((L, MLA_KV_RANK, MLA_HEADS * (MLA_NOPE + MLA_V)), MLA_KV_RANK),
        "mla_w_o": w((L, MLA_HEADS * MLA_V, D), MLA_HEADS * MLA_V),
        "gla_w_a2": w((L, GLA_GATE_RANK, GLA_HEADS * GLA_DK), GLA_GATE_RANK),
        "gla_b_a": bias((L, GLA_HEADS * GLA_DK)),
        "gla_o_norm": gain((L, GLA_DV)),
        "gla_w_o": w((L, GLA_HEADS * GLA_DV, D), GLA_HEADS * GLA_DV),
        "x_w_kv": w((L, D, 2 * X_HEADS * X_DH), D),
        "x_w_o": w((L, X_HEADS * X_DH, D), X_HEADS * X_DH),
        "w_out": w((L, D, D), D),
        "ffn2_norm": gain((L, D)), "ffn2_wg": w((L, D, F), D), "ffn2_wu": w((L, D, F), D), "ffn2_wd": w((L, F, D), F),
        "final_norm": gain((D,)),
    }


def reference(x, mem, positions, ffn1_norm, ffn1_wg, ffn1_wu, ffn1_wd, mix_norm, mem_norm, w_in, gate_bias,
              mla_q_norm, mla_w_uq, mla_kv_norm, mla_w_ukv, mla_w_o, gla_w_a2, gla_b_a, gla_o_norm, gla_w_o,
              x_w_kv, x_w_o, w_out, ffn2_norm, ffn2_wg, ffn2_wu, ffn2_wd, final_norm):
    h = x
    for l in range(DEPTH):
        h = h + 0.5 * _swiglu(_rmsnorm(h, ffn1_norm[l]), ffn1_wg[l], ffn1_wu[l], ffn1_wd[l])
        h = h + _mixer(h, mem, positions, mix_norm[l], mem_norm[l], w_in[l], gate_bias[l],
                       mla_q_norm[l], mla_w_uq[l], mla_kv_norm[l], mla_w_ukv[l], mla_w_o[l],
                       gla_w_a2[l], gla_b_a[l], gla_o_norm[l], gla_w_o[l], x_w_kv[l], x_w_o[l], w_out[l])
        h = h + 0.5 * _swiglu(_rmsnorm(h, ffn2_norm[l]), ffn2_wg[l], ffn2_wu[l], ffn2_wd[l])
    return _rmsnorm(h, final_norm)
```

```python
import functools
import math

import jax
import jax.numpy as jnp
from jax import lax
from jax.experimental import pallas as pl
from jax.experimental.pallas import tpu as pltpu

F32 = jnp.float32
BF16 = jnp.bfloat16

D_MODEL = 1024
N_MEM = 256
EPS = 1e-6
MLA_HEADS = 8
MLA_NOPE = 64
MLA_ROPE = 32
MLA_V = 64
MLA_Q_RANK = 384
MLA_KV_RANK = 256
ROPE_THETA = 10000.0
GLA_HEADS = 4
GLA_DK = 64
GLA_DV = 128
GLA_GATE_RANK = 16
GLA_TAU = 16.0
GLA_CHUNK = 64
X_HEADS = 4
X_DH = 128
D_FF = 2816
N_BRANCH = 3

LANE = 128
HEAD_PAD = 128
VMEM_LIMIT = 56 * 1024 * 1024

A_CQ, A_CKV, A_KR, A_GQ, A_GK, A_GV, A_GR, A_XQ, A_GA, A_END = (
    0, 384, 640, 768, 1024, 1280, 1792, 2304, 2816, 2944)

TM = 256
TQ = 256
TQX = 512
GLA_SB = 256


def _dot(a, b):
    return jnp.dot(a, b, preferred_element_type=F32)


def _dot_nt(a, b):
    return lax.dot_general(a, b, (((1,), (1,)), ((), ())), preferred_element_type=F32)


def _dot_tn(a, b):
    return lax.dot_general(a, b, (((0,), (0,)), ((), ())), preferred_element_type=F32)


def _rms(x, g):
    return x * lax.rsqrt(jnp.mean(x * x, axis=-1, keepdims=True) + EPS) * g


def _sigmoid(x):
    return 1.0 / (1.0 + jnp.exp(-x))


def _silu(x):
    return x * _sigmoid(x)


def _const_spec(shape):
    return pl.BlockSpec(shape, lambda *_: (0,) * len(shape), pipeline_mode=pl.Buffered(1))


def _swiglu_half(xn_bf16, wg_ref, wu_ref, wd_ref):
    g = _dot(xn_bf16, wg_ref[...])
    up = _dot(xn_bf16, wu_ref[...])
    a = (_silu(g) * up).astype(BF16)
    return _dot(a, wd_ref[...])


def _pre_kernel(x_ref, pos_ref, invf_ref, n1_ref, nm_ref, wg_ref, wu_ref, wd_ref, win_ref,
                qn_ref, wuq_ref, kvn_ref, wuk_ref, wuv_ref, wa2_ref, ba_ref,
                h1_ref, q_ref, k_ref, v_ref, gq_ref, gk_ref, gv_ref, gr_ref, xq_ref, la_ref):
    x = x_ref[...]
    u1 = _rms(x, n1_ref[...]).astype(BF16)
    h1 = x + 0.5 * _swiglu_half(u1, wg_ref, wu_ref, wd_ref)
    h1_ref[...] = h1
    u2 = _rms(h1, nm_ref[...]).astype(BF16)

    ang = pos_ref[...].astype(F32) * invf_ref[...]
    cos = jnp.cos(ang)
    sin = jnp.sin(ang)
    lane = lax.broadcasted_iota(jnp.int32, ang.shape, 1)
    sin_hi = jnp.where(lane >= 80, sin, 0.0)
    sin_lo = jnp.where(lane < 80, -sin, 0.0)

    def rope(t):
        return t * cos + pltpu.roll(t, 16, 1) * sin_hi + pltpu.roll(t, LANE - 16, 1) * sin_lo

    cqn = _rms(_dot(u2, win_ref[:, A_CQ:A_CKV]), qn_ref[...]).astype(BF16)
    for p in range(MLA_HEADS // 2):
        qp = _dot(cqn, wuq_ref[:, 2 * p * HEAD_PAD:(2 * p + 2) * HEAD_PAD])
        for j in range(2):
            h = 2 * p + j
            q_ref[:, h * HEAD_PAD:(h + 1) * HEAD_PAD] = rope(
                qp[:, j * HEAD_PAD:(j + 1) * HEAD_PAD]).astype(BF16)

    ckvn = _rms(_dot(u2, win_ref[:, A_CKV:A_KR]), kvn_ref[...]).astype(BF16)
    kr = rope(_dot(u2, win_ref[:, A_KR:A_GQ]))
    kr2 = jnp.concatenate([kr, kr], axis=1)
    for p in range(MLA_HEADS // 2):
        kn = _dot(ckvn, wuk_ref[:, 2 * p * HEAD_PAD:(2 * p + 2) * HEAD_PAD])
        k_ref[:, 2 * p * HEAD_PAD:(2 * p + 2) * HEAD_PAD] = (kn + kr2).astype(BF16)
    v_ref[...] = _dot(ckvn, wuv_ref[...]).astype(BF16)

    gq_ref[...] = _dot(u2, win_ref[:, A_GQ:A_GK])
    gk_ref[...] = _dot(u2, win_ref[:, A_GK:A_GV])
    gv_ref[...] = _dot(u2, win_ref[:, A_GV:A_GR]).astype(BF16)
    gr_ref[...] = _dot(u2, win_ref[:, A_GR:A_XQ]).astype(BF16)
    xq_ref[...] = _dot(u2, win_ref[:, A_XQ:A_GA]).astype(BF16)
    ga = _dot(u2, win_ref[:, A_GA:A_END]).astype(BF16)
    t = _dot(ga, wa2_ref[...]) + ba_ref[...]
    log_sig = jnp.minimum(t, 0.0) - jnp.log(1.0 + jnp.exp(-jnp.abs(t)))
    la_ref[...] = log_sig * (1.0 / GLA_TAU)


def _mla_kernel(q_ref, k_ref, v_ref, o_ref):
    qi = pl.program_id(2)
    scale = 1.0 / math.sqrt(MLA_NOPE + MLA_ROPE)
    tq = q_ref.shape[0]
    row = lax.broadcasted_iota(jnp.int32, (tq, tq), 0)
    col = lax.broadcasted_iota(jnp.int32, (tq, tq), 1)
    causal = col <= row

    for h in range(2):
        q = q_ref[:, h * HEAD_PAD:(h + 1) * HEAD_PAD]

        def step(start, carry, masked, h=h, q=q):
            m, l, acc = carry
            kb = k_ref[pl.ds(start, tq), h * HEAD_PAD:(h + 1) * HEAD_PAD]
            vb = v_ref[pl.ds(start, tq), h * MLA_V:(h + 1) * MLA_V]
            s = _dot_nt(q, kb) * scale
            if masked:
                s = jnp.where(causal, s, -1e30)
            m_new = jnp.maximum(m, jnp.max(s, axis=-1, keepdims=True))
            alpha = jnp.exp(m - m_new)
            p = jnp.exp(s - m_new)
            l = alpha * l + jnp.sum(p, axis=-1, keepdims=True)
            acc = alpha * acc + _dot(p.astype(BF16), vb)
            return m_new, l, acc

        init = (jnp.full((tq, 1), -1e30, F32), jnp.zeros((tq, 1), F32), jnp.zeros((tq, MLA_V), F32))
        carry = lax.fori_loop(
            0, qi, lambda j, c: step(pl.multiple_of(j * tq, tq), c, False), init)
        m, l, acc = step(pl.multiple_of(qi * tq, tq), carry, True)
        o_ref[:, h * MLA_V:(h + 1) * MLA_V] = (acc / l).astype(BF16)


def _memkv_kernel(mem_ref, n_ref, w_ref, o_ref):
    mn = _rms(mem_ref[...], n_ref[...]).astype(BF16)
    o_ref[...] = _dot(mn, w_ref[...]).astype(BF16)


def _xattn_kernel(q_ref, kv_ref, o_ref):
    scale = 1.0 / math.sqrt(X_DH)
    hw = X_HEADS * X_DH
    for h in range(X_HEADS):
        q = q_ref[:, h * X_DH:(h + 1) * X_DH]
        k = kv_ref[:, h * X_DH:(h + 1) * X_DH]
        v = kv_ref[:, hw + h * X_DH:hw + (h + 1) * X_DH]
        s = _dot_nt(q, k) * scale
        m = jnp.max(s, axis=-1, keepdims=True)
        p = jnp.exp(s - m)
        l = jnp.sum(p, axis=-1, keepdims=True)
        o = _dot(p.astype(BF16), v) / l
        o_ref[:, h * X_DH:(h + 1) * X_DH] = o.astype(BF16)


def _gla_kernel(q_ref, k_ref, la_ref, v_ref, r_ref, on_ref, o_ref, st_ref):
    sb, c = GLA_SB, GLA_CHUNK
    nsb = q_ref.shape[0] // sb
    row = lax.broadcasted_iota(jnp.int32, (sb, sb), 0)
    col = lax.broadcasted_iota(jnp.int32, (sb, sb), 1)
    tri = jnp.logical_and(row // c == col // c, col <= row)
    tri_bf = jnp.where(tri, 1.0, 0.0).astype(BF16)
    lane = lax.broadcasted_iota(jnp.int32, (1, LANE), 1)
    head_lanes = (lane < GLA_DK, lane >= GLA_DK)
    st_ref[...] = jnp.zeros_like(st_ref)

    def body(i, _):
        r0 = pl.multiple_of(i * sb, sb)
        la = la_ref[pl.ds(r0, sb), :]
        hi = la.astype(BF16)
        rem = la - hi.astype(F32)
        mid = rem.astype(BF16)
        lo = (rem - mid.astype(F32)).astype(BF16)
        bcum = _dot(tri_bf, hi) + _dot(tri_bf, mid) + _dot(tri_bf, lo)
        b_last = [bcum[j * c + c - 1:j * c + c, :] for j in range(sb // c)]
        b_last_rows = jnp.concatenate([jnp.broadcast_to(b, (c, LANE)) for b in b_last], axis=0)
        q_t = q_ref[pl.ds(r0, sb), :] * (GLA_DK ** -0.5) * jnp.exp(bcum)
        kk = k_ref[pl.ds(r0, sb), :]
        k_t = (kk * jnp.exp(-bcum)).astype(BF16)
        k_d = (kk * jnp.exp(b_last_rows - bcum)).astype(BF16)
        v = v_ref[pl.ds(r0, sb), :]

        qm = [jnp.where(head_lanes[h], q_t, 0.0).astype(BF16) for h in range(2)]
        o_intra = []
        for h in range(2):
            att = jnp.where(tri, _dot_nt(qm[h], k_t), 0.0).astype(BF16)
            o_intra.append(_dot(att, v[:, h * GLA_DV:(h + 1) * GLA_DV]))

        o_inter = [[], []]
        for j in range(sb // c):
            st = st_ref[...]
            st_bf = st.astype(BF16)
            for h in range(2):
                o_inter[h].append(_dot_nt(qm[h][j * c:(j + 1) * c, :],
                                          st_bf[h * GLA_DV:(h + 1) * GLA_DV, :]))
            d_st = _dot_tn(v[j * c:(j + 1) * c, :], k_d[j * c:(j + 1) * c, :])
            st_ref[...] = st * jnp.exp(b_last[j]) + d_st

        for h in range(2):
            o = o_intra[h] + jnp.concatenate(o_inter[h], axis=0)
            o = _rms(o, on_ref[...])
            r = r_ref[pl.ds(r0, sb), h * GLA_DV:(h + 1) * GLA_DV].astype(F32)
            o_ref[pl.ds(r0, sb), h * GLA_DV:(h + 1) * GLA_DV] = (o * _silu(r)).astype(BF16)
        return 0

    lax.fori_loop(0, nsb, body, 0)


def _post_kernel(h1_ref, om_ref, og_ref, ox_ref, nm_ref, wgz_ref, gb_ref, wom_ref, wog_ref, wox_ref,
                 wout_ref, n2_ref, wg_ref, wu_ref, wd_ref, nf_ref, o_ref):
    h1 = h1_ref[...]
    u = _rms(h1, nm_ref[...]).astype(BF16)
    d = D_MODEL
    merged = None
    for n, (o_br, w_br) in enumerate(((om_ref, wom_ref), (og_ref, wog_ref), (ox_ref, wox_ref))):
        gate = _sigmoid(_dot(u, wgz_ref[:, n * d:(n + 1) * d]) + gb_ref[:, n * d:(n + 1) * d])
        term = gate * _dot(o_br[...], w_br[...])
        merged = term if merged is None else merged + term
    h2 = h1 + _dot(merged.astype(BF16), wout_ref[...])
    u2 = _rms(h2, n2_ref[...]).astype(BF16)
    h3 = h2 + 0.5 * _swiglu_half(u2, wg_ref, wu_ref, wd_ref)
    o_ref[...] = _rms(h3, nf_ref[...])


def _cparams(sem, vmem=None):
    return pltpu.CompilerParams(dimension_semantics=sem, vmem_limit_bytes=vmem)


def kernel(x, mem, positions, ffn1_norm, ffn1_wg, ffn1_wu, ffn1_wd, mix_norm, mem_norm, w_in, gate_bias,
           mla_q_norm, mla_w_uq, mla_kv_norm, mla_w_ukv, mla_w_o, gla_w_a2, gla_b_a, gla_o_norm, gla_w_o,
           x_w_kv, x_w_o, w_out, ffn2_norm, ffn2_wg, ffn2_wu, ffn2_wd, final_norm):
    B, S, D = x.shape
    T = B * S
    M = mem.shape[1]
    F = D_FF
    H = MLA_HEADS
    bf = lambda a: a.astype(BF16)
    row = lambda a: a.reshape(1, -1)

    wi = w_in[0]
    c0 = 0
    cols = {}
    for name, w in (("cq", MLA_Q_RANK), ("ckv", MLA_KV_RANK), ("kr", MLA_ROPE), ("gq", GLA_HEADS * GLA_DK),
                    ("gk", GLA_HEADS * GLA_DK), ("gv", GLA_HEADS * GLA_DV), ("ga", GLA_GATE_RANK),
                    ("gr", GLA_HEADS * GLA_DV), ("xq", X_HEADS * X_DH), ("gz", N_BRANCH * D)):
        cols[name] = wi[:, c0:c0 + w]
        c0 += w
    zc = lambda n: jnp.zeros((D, n), F32)
    w_in_a = bf(jnp.concatenate(
        [cols["cq"], cols["ckv"], zc(MLA_NOPE), cols["kr"], zc(HEAD_PAD - MLA_NOPE - MLA_ROPE),
         cols["gq"], cols["gk"], cols["gv"], cols["gr"], cols["xq"],
         cols["ga"], zc(LANE - GLA_GATE_RANK)], axis=1))
    w_gz = bf(cols["gz"])
    w_uq = mla_w_uq[0].reshape(MLA_Q_RANK, H, MLA_NOPE + MLA_ROPE)
    w_uq = bf(jnp.pad(w_uq, ((0, 0), (0, 0), (0, HEAD_PAD - MLA_NOPE - MLA_ROPE))).reshape(MLA_Q_RANK, H * HEAD_PAD))
    w_ukv = mla_w_ukv[0].reshape(MLA_KV_RANK, H, MLA_NOPE + MLA_V)
    w_uk = bf(jnp.pad(w_ukv[:, :, :MLA_NOPE], ((0, 0), (0, 0), (0, HEAD_PAD - MLA_NOPE))).reshape(MLA_KV_RANK, H * HEAD_PAD))
    w_uv = bf(w_ukv[:, :, MLA_NOPE:].reshape(MLA_KV_RANK, H * MLA_V))
    w_a2 = bf(jnp.pad(gla_w_a2[0], ((0, LANE - GLA_GATE_RANK), (0, 0))))
    half = MLA_ROPE // 2
    inv_freq = ROPE_THETA ** (-jnp.arange(half, dtype=F32) / half)
    invf = jnp.concatenate([jnp.zeros((MLA_NOPE,), F32), inv_freq, inv_freq,
                            jnp.zeros((HEAD_PAD - MLA_NOPE - MLA_ROPE,), F32)]).reshape(1, HEAD_PAD)

    x2 = x.reshape(T, D)
    pos2 = positions.reshape(T, 1)
    nt = T // TM
    tok = lambda w: pl.BlockSpec((TM, w), lambda i: (i, 0))

    gdk, gdv, xw = GLA_HEADS * GLA_DK, GLA_HEADS * GLA_DV, X_HEADS * X_DH
    pre_out_shapes = (
        jax.ShapeDtypeStruct((T, D), F32),
        jax.ShapeDtypeStruct((T, H * HEAD_PAD), BF16),
        jax.ShapeDtypeStruct((T, H * HEAD_PAD), BF16),
        jax.ShapeDtypeStruct((T, H * MLA_V), BF16),
        jax.ShapeDtypeStruct((T, gdk), F32),
        jax.ShapeDtypeStruct((T, gdk), F32),
        jax.ShapeDtypeStruct((T, gdv), BF16),
        jax.ShapeDtypeStruct((T, gdv), BF16),
        jax.ShapeDtypeStruct((T, xw), BF16),
        jax.ShapeDtypeStruct((T, gdk), F32),
    )
    h1, q, k, v, gq, gk, gv, gr, xq, la = pl.pallas_call(
        _pre_kernel,
        out_shape=pre_out_shapes,
        grid=(nt,),
        in_specs=[tok(D), tok(1), _const_spec((1, HEAD_PAD)), _const_spec((1, D)), _const_spec((1, D)),
                  _const_spec((D, F)), _const_spec((D, F)), _const_spec((F, D)), _const_spec((D, A_END)),
                  _const_spec((1, MLA_Q_RANK)), _const_spec((MLA_Q_RANK, H * HEAD_PAD)),
                  _const_spec((1, MLA_KV_RANK)), _const_spec((MLA_KV_RANK, H * HEAD_PAD)),
                  _const_spec((MLA_KV_RANK, H * MLA_V)), _const_spec((LANE, gdk)), _const_spec((1, gdk))],
        out_specs=tuple(tok(s.shape[1]) for s in pre_out_shapes),
        compiler_params=_cparams(("parallel",), VMEM_LIMIT),
        name="pre",
    )(x2, pos2, invf, row(ffn1_norm[0]), row(mix_norm[0]), bf(ffn1_wg[0]), bf(ffn1_wu[0]), bf(ffn1_wd[0]),
      w_in_a, row(mla_q_norm[0]), w_uq, row(mla_kv_norm[0]), w_uk, w_uv, w_a2, row(gla_b_a[0]))

    nq = S // TQ
    o_mla = pl.pallas_call(
        _mla_kernel,
        out_shape=jax.ShapeDtypeStruct((T, H * MLA_V), BF16),
        grid=(B, H // 2, nq),
        in_specs=[pl.BlockSpec((TQ, 2 * HEAD_PAD), lambda b, p, i: (b * nq + i, p)),
                  pl.BlockSpec((S, 2 * HEAD_PAD), lambda b, p, i: (b, p)),
                  pl.BlockSpec((S, 2 * MLA_V), lambda b, p, i: (b, p))],
        out_specs=pl.BlockSpec((TQ, 2 * MLA_V), lambda b, p, i: (b * nq + i, p)),
        compiler_params=_cparams(("parallel", "parallel", "arbitrary")),
        name="mla",
    )(q, k, v)

    kvx = pl.pallas_call(
        _memkv_kernel,
        out_shape=jax.ShapeDtypeStruct((B * M, 2 * xw), BF16),
        grid=(B,),
        in_specs=[pl.BlockSpec((M, D), lambda b: (b, 0)), _const_spec((1, D)), _const_spec((D, 2 * xw))],
        out_specs=pl.BlockSpec((M, 2 * xw), lambda b: (b, 0)),
        compiler_params=_cparams(("parallel",)),
        name="memkv",
    )(mem.reshape(B * M, D), row(mem_norm[0]), bf(x_w_kv[0]))
    nqx = S // TQX
    o_x = pl.pallas_call(
        _xattn_kernel,
        out_shape=jax.ShapeDtypeStruct((T, xw), BF16),
        grid=(B, nqx),
        in_specs=[pl.BlockSpec((TQX, xw), lambda b, i: (b * nqx + i, 0)),
                  pl.BlockSpec((M, 2 * xw), lambda b, i: (b, 0))],
        out_specs=pl.BlockSpec((TQX, xw), lambda b, i: (b * nqx + i, 0)),
        compiler_params=_cparams(("parallel", "parallel")),
        name="xattn",
    )(xq, kvx)

    o_gla = pl.pallas_call(
        _gla_kernel,
        out_shape=jax.ShapeDtypeStruct((T, gdv), BF16),
        grid=(B, GLA_HEADS // 2),
        in_specs=[pl.BlockSpec((S, LANE), lambda b, p: (b, p)),
                  pl.BlockSpec((S, LANE), lambda b, p: (b, p)),
                  pl.BlockSpec((S, LANE), lambda b, p: (b, p)),
                  pl.BlockSpec((S, 2 * GLA_DV), lambda b, p: (b, p)),
                  pl.BlockSpec((S, 2 * GLA_DV), lambda b, p: (b, p)),
                  _const_spec((1, GLA_DV))],
        out_specs=pl.BlockSpec((S, 2 * GLA_DV), lambda b, p: (b, p)),
        scratch_shapes=[pltpu.VMEM((2 * GLA_DV, LANE), F32)],
        compiler_params=_cparams(("parallel", "parallel")),
        name="gla",
    )(gq, gk, la, gv, gr, row(gla_o_norm[0]))

    out = pl.pallas_call(
        _post_kernel,
        out_shape=jax.ShapeDtypeStruct((T, D), F32),
        grid=(nt,),
        in_specs=[tok(D), tok(H * MLA_V), tok(gdv), tok(xw), _const_spec((1, D)),
                  _const_spec((D, N_BRANCH * D)), _const_spec((1, N_BRANCH * D)),
                  _const_spec((H * MLA_V, D)), _const_spec((gdv, D)), _const_spec((xw, D)),
                  _const_spec((D, D)), _const_spec((1, D)),
                  _const_spec((D, F)), _const_spec((D, F)), _const_spec((F, D)), _const_spec((1, D))],
        out_specs=tok(D),
        compiler_params=_cparams(("parallel",), VMEM_LIMIT),
        name="post",
    )(h1, o_mla, o_gla, o_x, row(mix_norm[0]), w_gz, gate_bias[0].reshape(1, N_BRANCH * D),
      bf(mla_w_o[0]), bf(gla_w_o[0]), bf(x_w_o[0]), bf(w_out[0]), row(ffn2_norm[0]),
      bf(ffn2_wg[0]), bf(ffn2_wu[0]), bf(ffn2_wd[0]), row(final_norm))
    return out.reshape(B, S, D)
```

```python
import functools
import math

import jax
import jax.numpy as jnp
from jax import lax
from jax.experimental import pallas as pl
from jax.experimental.pallas import tpu as pltpu

F32 = jnp.float32
BF16 = jnp.bfloat16

D_MODEL = 1024
N_MEM = 256
EPS = 1e-6
MLA_HEADS = 8
MLA_NOPE = 64
MLA_ROPE = 32
MLA_V = 64
MLA_Q_RANK = 384
MLA_KV_RANK = 256
ROPE_THETA = 10000.0
GLA_HEADS = 4
GLA_DK = 64
GLA_DV = 128
GLA_GATE_RANK = 16
GLA_TAU = 16.0
GLA_CHUNK = 64
X_HEADS = 4
X_DH = 128
D_FF = 2816
N_BRANCH = 3

LANE = 128
HEAD_PAD = 128
VMEM_LIMIT = 56 * 1024 * 1024

A_CQ, A_CKV, A_KR, A_GQ, A_GK, A_GV, A_GR, A_XQ, A_GA, A_END = (
    0, 384, 640, 768, 1024, 1280, 1792, 2304, 2816, 2944)

TM = 256
TQ = 256
TQX = 512
GLA_SB = 256


def _dot(a, b):
    return jnp.dot(a, b, preferred_element_type=F32)


def _dot_nt(a, b):
    return lax.dot_general(a, b, (((1,), (1,)), ((), ())), preferred_element_type=F32)


def _dot_tn(a, b):
    return lax.dot_general(a, b, (((0,), (0,)), ((), ())), preferred_element_type=F32)


def _rms(x, g):
    return x * lax.rsqrt(jnp.mean(x * x, axis=-1, keepdims=True) + EPS) * g


def _sigmoid(x):
    return 1.0 / (1.0 + jnp.exp(-x))


def _silu(x):
    return x * _sigmoid(x)


def _const_spec(shape):
    return pl.BlockSpec(shape, lambda *_: (0,) * len(shape), pipeline_mode=pl.Buffered(1))


def _swiglu_half(xn_bf16, wg_ref, wu_ref, wd_ref):
    g = _dot(xn_bf16, wg_ref[...])
    up = _dot(xn_bf16, wu_ref[...])
    a = (_silu(g) * up).astype(BF16)
    return _dot(a, wd_ref[...])


def _pre_kernel(x_ref, pos_ref, invf_ref, n1_ref, nm_ref, wg_ref, wu_ref, wd_ref, win_ref,
                qn_ref, wuq_ref, kvn_ref, wuk_ref, wuv_ref, wa2_ref, ba_ref,
                h1_ref, q_ref, k_ref, v_ref, gq_ref, gk_ref, gv_ref, gr_ref, xq_ref, la_ref):
    x = x_ref[...]
    u1 = _rms(x, n1_ref[...]).astype(BF16)
    h1 = x + 0.5 * _swiglu_half(u1, wg_ref, wu_ref, wd_ref)
    h1_ref[...] = h1
    u2 = _rms(h1, nm_ref[...]).astype(BF16)

    ang = pos_ref[...].astype(F32) * invf_ref[...]
    cos = jnp.cos(ang)
    sin = jnp.sin(ang)
    lane = lax.broadcasted_iota(jnp.int32, ang.shape, 1)
    sin_hi = jnp.where(lane >= 80, sin, 0.0)
    sin_lo = jnp.where(lane < 80, -sin, 0.0)

    def rope(t):
        return t * cos + pltpu.roll(t, 16, 1) * sin_hi + pltpu.roll(t, LANE - 16, 1) * sin_lo

    cqn = _rms(_dot(u2, win_ref[:, A_CQ:A_CKV]), qn_ref[...]).astype(BF16)
    for p in range(MLA_HEADS // 2):
        qp = _dot(cqn, wuq_ref[:, 2 * p * HEAD_PAD:(2 * p + 2) * HEAD_PAD])
        for j in range(2):
            h = 2 * p + j
            q_ref[:, h * HEAD_PAD:(h + 1) * HEAD_PAD] = rope(
                qp[:, j * HEAD_PAD:(j + 1) * HEAD_PAD]).astype(BF16)

    ckvn = _rms(_dot(u2, win_ref[:, A_CKV:A_KR]), kvn_ref[...]).astype(BF16)
    kr = rope(_dot(u2, win_ref[:, A_KR:A_GQ]))
    kr2 = jnp.concatenate([kr, kr], axis=1)
    for p in range(MLA_HEADS // 2):
        kn = _dot(ckvn, wuk_ref[:, 2 * p * HEAD_PAD:(2 * p + 2) * HEAD_PAD])
        k_ref[:, 2 * p * HEAD_PAD:(2 * p + 2) * HEAD_PAD] = (kn + kr2).astype(BF16)
    v_ref[...] = _dot(ckvn, wuv_ref[...]).astype(BF16)

    gq_ref[...] = _dot(u2, win_ref[:, A_GQ:A_GK])
    gk_ref[...] = _dot(u2, win_ref[:, A_GK:A_GV])
    gv_ref[...] = _dot(u2, win_ref[:, A_GV:A_GR]).astype(BF16)
    gr_ref[...] = _dot(u2, win_ref[:, A_GR:A_XQ]).astype(BF16)
    xq_ref[...] = _dot(u2, win_ref[:, A_XQ:A_GA]).astype(BF16)
    ga = _dot(u2, win_ref[:, A_GA:A_END]).astype(BF16)
    t = _dot(ga, wa2_ref[...]) + ba_ref[...]
    log_sig = jnp.minimum(t, 0.0) - jnp.log(1.0 + jnp.exp(-jnp.abs(t)))
    la_ref[...] = log_sig * (1.0 / GLA_TAU)


def _mla_kernel(q_ref, k_ref, v_ref, o_ref):
    tq = TQ
    nq = q_ref.shape[0] // tq
    c_exp = (1.0 / math.sqrt(MLA_NOPE + MLA_ROPE)) * math.log2(math.e)
    row = lax.broadcasted_iota(jnp.int32, (tq, tq), 0)
    col = lax.broadcasted_iota(jnp.int32, (tq, tq), 1)
    causal = col <= row
    lane = lax.broadcasted_iota(jnp.int32, (tq, 2 * MLA_V), 1)

    for qi in range(nq):
        heads = []
        for h in range(2):
            q = q_ref[qi * tq:(qi + 1) * tq, h * HEAD_PAD:(h + 1) * HEAD_PAD]
            t = []
            for c in range(qi + 1):
                s = _dot_nt(q, k_ref[c * tq:(c + 1) * tq, h * HEAD_PAD:(h + 1) * HEAD_PAD]) * c_exp
                t.append(jnp.where(causal, s, -1e30) if c == qi else s)
            m = t[0]
            for s in t[1:]:
                m = jnp.maximum(m, s)
            m = jnp.max(m, axis=-1, keepdims=True)
            psum = None
            acc = None
            for c, s in enumerate(t):
                p = jnp.exp2(s - m)
                psum = p if psum is None else psum + p
                pv = _dot(p.astype(BF16), v_ref[c * tq:(c + 1) * tq, :])
                acc = pv if acc is None else acc + pv
            heads.append(acc / jnp.sum(psum, axis=-1, keepdims=True))
        o_ref[qi * tq:(qi + 1) * tq, :] = jnp.where(lane < MLA_V, heads[0], heads[1]).astype(BF16)


def _memkv_kernel(mem_ref, n_ref, w_ref, o_ref):
    mn = _rms(mem_ref[...], n_ref[...]).astype(BF16)
    o_ref[...] = _dot(mn, w_ref[...]).astype(BF16)


def _xattn_kernel(q_ref, kv_ref, o_ref):
    scale = 1.0 / math.sqrt(X_DH)
    hw = X_HEADS * X_DH
    for h in range(X_HEADS):
        q = q_ref[:, h * X_DH:(h + 1) * X_DH]
        k = kv_ref[:, h * X_DH:(h + 1) * X_DH]
        v = kv_ref[:, hw + h * X_DH:hw + (h + 1) * X_DH]
        s = _dot_nt(q, k) * scale
        m = jnp.max(s, axis=-1, keepdims=True)
        p = jnp.exp(s - m)
        l = jnp.sum(p, axis=-1, keepdims=True)
        o = _dot(p.astype(BF16), v) / l
        o_ref[:, h * X_DH:(h + 1) * X_DH] = o.astype(BF16)


def _gla_kernel(q_ref, k_ref, la_ref, v_ref, r_ref, on_ref, o_ref, st_ref):
    sb, c = GLA_SB, GLA_CHUNK
    nsb = q_ref.shape[0] // sb
    row = lax.broadcasted_iota(jnp.int32, (sb, sb), 0)
    col = lax.broadcasted_iota(jnp.int32, (sb, sb), 1)
    tri = jnp.logical_and(row // c == col // c, col <= row)
    tri_bf = jnp.where(tri, 1.0, 0.0).astype(BF16)
    lane = lax.broadcasted_iota(jnp.int32, (1, LANE), 1)
    head_lanes = (lane < GLA_DK, lane >= GLA_DK)
    st_ref[...] = jnp.zeros_like(st_ref)

    def body(i, _):
        r0 = pl.multiple_of(i * sb, sb)
        la = la_ref[pl.ds(r0, sb), :]
        hi = la.astype(BF16)
        rem = la - hi.astype(F32)
        mid = rem.astype(BF16)
        lo = (rem - mid.astype(F32)).astype(BF16)
        bcum = _dot(tri_bf, hi) + _dot(tri_bf, mid) + _dot(tri_bf, lo)
        b_last = [bcum[j * c + c - 1:j * c + c, :] for j in range(sb // c)]
        b_last_rows = jnp.concatenate([jnp.broadcast_to(b, (c, LANE)) for b in b_last], axis=0)
        q_t = q_ref[pl.ds(r0, sb), :] * (GLA_DK ** -0.5) * jnp.exp(bcum)
        kk = k_ref[pl.ds(r0, sb), :]
        k_t = (kk * jnp.exp(-bcum)).astype(BF16)
        k_d = (kk * jnp.exp(b_last_rows - bcum)).astype(BF16)
        v = v_ref[pl.ds(r0, sb), :]

        qm = [jnp.where(head_lanes[h], q_t, 0.0).astype(BF16) for h in range(2)]
        o_intra = []
        for h in range(2):
            att = jnp.where(tri, _dot_nt(qm[h], k_t), 0.0).astype(BF16)
            o_intra.append(_dot(att, v[:, h * GLA_DV:(h + 1) * GLA_DV]))

        o_inter = [[], []]
        for j in range(sb // c):
            st = st_ref[...]
            st_bf = st.astype(BF16)
            for h in range(2):
                o_inter[h].append(_dot_nt(qm[h][j * c:(j + 1) * c, :],
                                          st_bf[h * GLA_DV:(h + 1) * GLA_DV, :]))
            d_st = _dot_tn(v[j * c:(j + 1) * c, :], k_d[j * c:(j + 1) * c, :])
            st_ref[...] = st * jnp.exp(b_last[j]) + d_st

        for h in range(2):
            o = o_intra[h] + jnp.concatenate(o_inter[h], axis=0)
            o = _rms(o, on_ref[...])
            r = r_ref[pl.ds(r0, sb), h * GLA_DV:(h + 1) * GLA_DV].astype(F32)
            o_ref[pl.ds(r0, sb), h * GLA_DV:(h + 1) * GLA_DV] = (o * _silu(r)).astype(BF16)
        return 0

    lax.fori_loop(0, nsb, body, 0)


def _post_kernel(h1_ref, om_ref, og_ref, ox_ref, nm_ref, wgz_ref, gb_ref, wom_ref, wog_ref, wox_ref,
                 wout_ref, n2_ref, wg_ref, wu_ref, wd_ref, nf_ref, o_ref):
    h1 = h1_ref[...]
    u = _rms(h1, nm_ref[...]).astype(BF16)
    d = D_MODEL
    merged = None
    for n, (o_br, w_br) in enumerate(((om_ref, wom_ref), (og_ref, wog_ref), (ox_ref, wox_ref))):
        gate = _sigmoid(_dot(u, wgz_ref[:, n * d:(n + 1) * d]) + gb_ref[:, n * d:(n + 1) * d])
        term = gate * _dot(o_br[...], w_br[...])
        merged = term if merged is None else merged + term
    h2 = h1 + _dot(merged.astype(BF16), wout_ref[...])
    u2 = _rms(h2, n2_ref[...]).astype(BF16)
    h3 = h2 + 0.5 * _swiglu_half(u2, wg_ref, wu_ref, wd_ref)
    o_ref[...] = _rms(h3, nf_ref[...])


def _cparams(sem, vmem=None):
    return pltpu.CompilerParams(dimension_semantics=sem, vmem_limit_bytes=vmem)


def kernel(x, mem, positions, ffn1_norm, ffn1_wg, ffn1_wu, ffn1_wd, mix_norm, mem_norm, w_in, gate_bias,
           mla_q_norm, mla_w_uq, mla_kv_norm, mla_w_ukv, mla_w_o, gla_w_a2, gla_b_a, gla_o_norm, gla_w_o,
           x_w_kv, x_w_o, w_out, ffn2_norm, ffn2_wg, ffn2_wu, ffn2_wd, final_norm):
    B, S, D = x.shape
    T = B * S
    M = mem.shape[1]
    F = D_FF
    H = MLA_HEADS
    bf = lambda a: a.astype(BF16)
    row = lambda a: a.reshape(1, -1)

    wi = w_in[0]
    c0 = 0
    cols = {}
    for name, w in (("cq", MLA_Q_RANK), ("ckv", MLA_KV_RANK), ("kr", MLA_ROPE), ("gq", GLA_HEADS * GLA_DK),
                    ("gk", GLA_HEADS * GLA_DK), ("gv", GLA_HEADS * GLA_DV), ("ga", GLA_GATE_RANK),
                    ("gr", GLA_HEADS * GLA_DV), ("xq", X_HEADS * X_DH), ("gz", N_BRANCH * D)):
        cols[name] = wi[:, c0:c0 + w]
        c0 += w
    zc = lambda n: jnp.zeros((D, n), F32)
    w_in_a = bf(jnp.concatenate(
        [cols["cq"], cols["ckv"], zc(MLA_NOPE), cols["kr"], zc(HEAD_PAD - MLA_NOPE - MLA_ROPE),
         cols["gq"], cols["gk"], cols["gv"], cols["gr"], cols["xq"],
         cols["ga"], zc(LANE - GLA_GATE_RANK)], axis=1))
    w_gz = bf(cols["gz"])
    w_uq = mla_w_uq[0].reshape(MLA_Q_RANK, H, MLA_NOPE + MLA_ROPE)
    w_uq = bf(jnp.pad(w_uq, ((0, 0), (0, 0), (0, HEAD_PAD - MLA_NOPE - MLA_ROPE))).reshape(MLA_Q_RANK, H * HEAD_PAD))
    w_ukv = mla_w_ukv[0].reshape(MLA_KV_RANK, H, MLA_NOPE + MLA_V)
    w_uk = bf(jnp.pad(w_ukv[:, :, :MLA_NOPE], ((0, 0), (0, 0), (0, HEAD_PAD - MLA_NOPE))).reshape(MLA_KV_RANK, H * HEAD_PAD))
    w_uv = bf(w_ukv[:, :, MLA_NOPE:].reshape(MLA_KV_RANK, H * MLA_V))
    w_a2 = bf(jnp.pad(gla_w_a2[0], ((0, LANE - GLA_GATE_RANK), (0, 0))))
    half = MLA_ROPE // 2
    inv_freq = ROPE_THETA ** (-jnp.arange(half, dtype=F32) / half)
    invf = jnp.concatenate([jnp.zeros((MLA_NOPE,), F32), inv_freq, inv_freq,
                            jnp.zeros((HEAD_PAD - MLA_NOPE - MLA_ROPE,), F32)]).reshape(1, HEAD_PAD)

    x2 = x.reshape(T, D)
    pos2 = positions.reshape(T, 1)
    nt = T // TM
    tok = lambda w: pl.BlockSpec((TM, w), lambda i: (i, 0))

    gdk, gdv, xw = GLA_HEADS * GLA_DK, GLA_HEADS * GLA_DV, X_HEADS * X_DH
    pre_out_shapes = (
        jax.ShapeDtypeStruct((T, D), F32),
        jax.ShapeDtypeStruct((T, H * HEAD_PAD), BF16),
        jax.ShapeDtypeStruct((T, H * HEAD_PAD), BF16),
        jax.ShapeDtypeStruct((T, H * MLA_V), BF16),
        jax.ShapeDtypeStruct((T, gdk), F32),
        jax.ShapeDtypeStruct((T, gdk), F32),
        jax.ShapeDtypeStruct((T, gdv), BF16),
        jax.ShapeDtypeStruct((T, gdv), BF16),
        jax.ShapeDtypeStruct((T, xw), BF16),
        jax.ShapeDtypeStruct((T, gdk), F32),
    )
    h1, q, k, v, gq, gk, gv, gr, xq, la = pl.pallas_call(
        _pre_kernel,
        out_shape=pre_out_shapes,
        grid=(nt,),
        in_specs=[tok(D), tok(1), _const_spec((1, HEAD_PAD)), _const_spec((1, D)), _const_spec((1, D)),
                  _const_spec((D, F)), _const_spec((D, F)), _const_spec((F, D)), _const_spec((D, A_END)),
                  _const_spec((1, MLA_Q_RANK)), _const_spec((MLA_Q_RANK, H * HEAD_PAD)),
                  _const_spec((1, MLA_KV_RANK)), _const_spec((MLA_KV_RANK, H * HEAD_PAD)),
                  _const_spec((MLA_KV_RANK, H * MLA_V)), _const_spec((LANE, gdk)), _const_spec((1, gdk))],
        out_specs=tuple(tok(s.shape[1]) for s in pre_out_shapes),
        compiler_params=_cparams(("parallel",), VMEM_LIMIT),
        name="pre",
    )(x2, pos2, invf, row(ffn1_norm[0]), row(mix_norm[0]), bf(ffn1_wg[0]), bf(ffn1_wu[0]), bf(ffn1_wd[0]),
      w_in_a, row(mla_q_norm[0]), w_uq, row(mla_kv_norm[0]), w_uk, w_uv, w_a2, row(gla_b_a[0]))

    o_mla = pl.pallas_call(
        _mla_kernel,
        out_shape=jax.ShapeDtypeStruct((T, H * MLA_V), BF16),
        grid=(B, H // 2),
        in_specs=[pl.BlockSpec((S, 2 * HEAD_PAD), lambda b, p: (b, p)),
                  pl.BlockSpec((S, 2 * HEAD_PAD), lambda b, p: (b, p)),
                  pl.BlockSpec((S, 2 * MLA_V), lambda b, p: (b, p))],
        out_specs=pl.BlockSpec((S, 2 * MLA_V), lambda b, p: (b, p)),
        compiler_params=_cparams(("parallel", "parallel")),
        name="mla",
    )(q, k, v)

    kvx = pl.pallas_call(
        _memkv_kernel,
        out_shape=jax.ShapeDtypeStruct((B * M, 2 * xw), BF16),
        grid=(B,),
        in_specs=[pl.BlockSpec((M, D), lambda b: (b, 0)), _const_spec((1, D)), _const_spec((D, 2 * xw))],
        out_specs=pl.BlockSpec((M, 2 * xw), lambda b: (b, 0)),
        compiler_params=_cparams(("parallel",)),
        name="memkv",
    )(mem.reshape(B * M, D), row(mem_norm[0]), bf(x_w_kv[0]))
    nqx = S // TQX
    o_x = pl.pallas_call(
        _xattn_kernel,
        out_shape=jax.ShapeDtypeStruct((T, xw), BF16),
        grid=(B, nqx),
        in_specs=[pl.BlockSpec((TQX, xw), lambda b, i: (b * nqx + i, 0)),
                  pl.BlockSpec((M, 2 * xw), lambda b, i: (b, 0))],
        out_specs=pl.BlockSpec((TQX, xw), lambda b, i: (b * nqx + i, 0)),
        compiler_params=_cparams(("parallel", "parallel")),
        name="xattn",
    )(xq, kvx)

    o_gla = pl.pallas_call(
        _gla_kernel,
        out_shape=jax.ShapeDtypeStruct((T, gdv), BF16),
        grid=(B, GLA_HEADS // 2),
        in_specs=[pl.BlockSpec((S, LANE), lambda b, p: (b, p)),
                  pl.BlockSpec((S, LANE), lambda b, p: (b, p)),
                  pl.BlockSpec((S, LANE), lambda b, p: (b, p)),
                  pl.BlockSpec((S, 2 * GLA_DV), lambda b, p: (b, p)),
                  pl.BlockSpec((S, 2 * GLA_DV), lambda b, p: (b, p)),
                  _const_spec((1, GLA_DV))],
        out_specs=pl.BlockSpec((S, 2 * GLA_DV), lambda b, p: (b, p)),
        scratch_shapes=[pltpu.VMEM((2 * GLA_DV, LANE), F32)],
        compiler_params=_cparams(("parallel", "parallel")),
        name="gla",
    )(gq, gk, la, gv, gr, row(gla_o_norm[0]))

    out = pl.pallas_call(
        _post_kernel,
        out_shape=jax.ShapeDtypeStruct((T, D), F32),
        grid=(nt,),
        in_specs=[tok(D), tok(H * MLA_V), tok(gdv), tok(xw), _const_spec((1, D)),
                  _const_spec((D, N_BRANCH * D)), _const_spec((1, N_BRANCH * D)),
                  _const_spec((H * MLA_V, D)), _const_spec((gdv, D)), _const_spec((xw, D)),
                  _const_spec((D, D)), _const_spec((1, D)),
                  _const_spec((D, F)), _const_spec((D, F)), _const_spec((F, D)), _const_spec((1, D))],
        out_specs=tok(D),
        compiler_params=_cparams(("parallel",), VMEM_LIMIT),
        name="post",
    )(h1, o_mla, o_gla, o_x, row(mix_norm[0]), w_gz, gate_bias[0].reshape(1, N_BRANCH * D),
      bf(mla_w_o[0]), bf(gla_w_o[0]), bf(x_w_o[0]), bf(w_out[0]), row(ffn2_norm[0]),
      bf(ffn2_wg[0]), bf(ffn2_wu[0]), bf(ffn2_wd[0]), row(final_norm))
    return out.reshape(B, S, D)
```

```python
import functools
import math

import jax
import jax.numpy as jnp
from jax import lax
from jax.experimental import pallas as pl
from jax.experimental.pallas import tpu as pltpu

F32 = jnp.float32
BF16 = jnp.bfloat16

D_MODEL = 1024
N_MEM = 256
EPS = 1e-6
MLA_HEADS = 8
MLA_NOPE = 64
MLA_ROPE = 32
MLA_V = 64
MLA_Q_RANK = 384
MLA_KV_RANK = 256
ROPE_THETA = 10000.0
GLA_HEADS = 4
GLA_DK = 64
GLA_DV = 128
GLA_GATE_RANK = 16
GLA_TAU = 16.0
GLA_CHUNK = 64
X_HEADS = 4
X_DH = 128
D_FF = 2816
N_BRANCH = 3

LANE = 128
HEAD_PAD = 128
VMEM_LIMIT = 56 * 1024 * 1024

A_CQ, A_CKV, A_KR, A_GQ, A_GK, A_GV, A_GR, A_XQ, A_GA, A_END = (
    0, 384, 640, 768, 1024, 1280, 1792, 2304, 2816, 2944)

TM = 512
TQ = 256
TQX = 512
GLA_SB = 256


def _dot(a, b):
    return jnp.dot(a, b, preferred_element_type=F32)


def _dot_nt(a, b):
    return lax.dot_general(a, b, (((1,), (1,)), ((), ())), preferred_element_type=F32)


def _dot_tn(a, b):
    return lax.dot_general(a, b, (((0,), (0,)), ((), ())), preferred_element_type=F32)


def _rms(x, g):
    return x * lax.rsqrt(jnp.mean(x * x, axis=-1, keepdims=True) + EPS) * g


def _sigmoid(x):
    return 1.0 / (1.0 + jnp.exp(-x))


def _silu(x):
    return x * _sigmoid(x)


def _const_spec(shape):
    return pl.BlockSpec(shape, lambda *_: (0,) * len(shape), pipeline_mode=pl.Buffered(1))


def _swiglu_half(xn_bf16, wg_ref, wu_ref, wd_ref):
    g = _dot(xn_bf16, wg_ref[...])
    up = _dot(xn_bf16, wu_ref[...])
    a = (_silu(g) * up).astype(BF16)
    return _dot(a, wd_ref[...])


def _pre_kernel(x_ref, pos_ref, invf_ref, n1_ref, nm_ref, wg_ref, wu_ref, wd_ref, win_ref,
                qn_ref, wuq_ref, kvn_ref, wuk_ref, wuv_ref, wa2_ref, ba_ref,
                h1_ref, q_ref, k_ref, v_ref, gq_ref, gk_ref, gv_ref, gr_ref, xq_ref, la_ref):
    x = x_ref[...]
    u1 = _rms(x, n1_ref[...]).astype(BF16)
    h1 = x + 0.5 * _swiglu_half(u1, wg_ref, wu_ref, wd_ref)
    h1_ref[...] = h1
    u2 = _rms(h1, nm_ref[...]).astype(BF16)

    ang = pos_ref[...].astype(F32) * invf_ref[...]
    cos = jnp.cos(ang)
    sin = jnp.sin(ang)
    lane = lax.broadcasted_iota(jnp.int32, ang.shape, 1)
    sin_hi = jnp.where(lane >= 80, sin, 0.0)
    sin_lo = jnp.where(lane < 80, -sin, 0.0)

    def rope(t):
        return t * cos + pltpu.roll(t, 16, 1) * sin_hi + pltpu.roll(t, LANE - 16, 1) * sin_lo

    cqn = _rms(_dot(u2, win_ref[:, A_CQ:A_CKV]), qn_ref[...]).astype(BF16)
    for p in range(MLA_HEADS // 2):
        qp = _dot(cqn, wuq_ref[:, 2 * p * HEAD_PAD:(2 * p + 2) * HEAD_PAD])
        for j in range(2):
            h = 2 * p + j
            q_ref[:, h * HEAD_PAD:(h + 1) * HEAD_PAD] = rope(
                qp[:, j * HEAD_PAD:(j + 1) * HEAD_PAD]).astype(BF16)

    ckvn = _rms(_dot(u2, win_ref[:, A_CKV:A_KR]), kvn_ref[...]).astype(BF16)
    kr = rope(_dot(u2, win_ref[:, A_KR:A_GQ]))
    kr2 = jnp.concatenate([kr, kr], axis=1)
    for p in range(MLA_HEADS // 2):
        kn = _dot(ckvn, wuk_ref[:, 2 * p * HEAD_PAD:(2 * p + 2) * HEAD_PAD])
        k_ref[:, 2 * p * HEAD_PAD:(2 * p + 2) * HEAD_PAD] = (kn + kr2).astype(BF16)
    v_ref[...] = _dot(ckvn, wuv_ref[...]).astype(BF16)

    gq_ref[...] = _dot(u2, win_ref[:, A_GQ:A_GK])
    gk_ref[...] = _dot(u2, win_ref[:, A_GK:A_GV])
    gv_ref[...] = _dot(u2, win_ref[:, A_GV:A_GR]).astype(BF16)
    gr_ref[...] = _dot(u2, win_ref[:, A_GR:A_XQ]).astype(BF16)
    xq_ref[...] = _dot(u2, win_ref[:, A_XQ:A_GA]).astype(BF16)
    ga = _dot(u2, win_ref[:, A_GA:A_END]).astype(BF16)
    t = _dot(ga, wa2_ref[...]) + ba_ref[...]
    log_sig = jnp.minimum(t, 0.0) - jnp.log(1.0 + jnp.exp(-jnp.abs(t)))
    la_ref[...] = log_sig * (1.0 / GLA_TAU)


def _mla_kernel(q_ref, k_ref, v_ref, o_ref):
    tq = TQ
    nq = q_ref.shape[0] // tq
    c_exp = (1.0 / math.sqrt(MLA_NOPE + MLA_ROPE)) * math.log2(math.e)
    row = lax.broadcasted_iota(jnp.int32, (tq, tq), 0)
    col = lax.broadcasted_iota(jnp.int32, (tq, tq), 1)
    causal = col <= row
    lane = lax.broadcasted_iota(jnp.int32, (tq, 2 * MLA_V), 1)

    for qi in range(nq):
        heads = []
        for h in range(2):
            q = q_ref[qi * tq:(qi + 1) * tq, h * HEAD_PAD:(h + 1) * HEAD_PAD]
            t = []
            for c in range(qi + 1):
                s = _dot_nt(q, k_ref[c * tq:(c + 1) * tq, h * HEAD_PAD:(h + 1) * HEAD_PAD]) * c_exp
                t.append(jnp.where(causal, s, -1e30) if c == qi else s)
            m = t[0]
            for s in t[1:]:
                m = jnp.maximum(m, s)
            m = jnp.max(m, axis=-1, keepdims=True)
            psum = None
            acc = None
            for c, s in enumerate(t):
                p = jnp.exp2(s - m)
                psum = p if psum is None else psum + p
                pv = _dot(p.astype(BF16), v_ref[c * tq:(c + 1) * tq, :])
                acc = pv if acc is None else acc + pv
            heads.append(acc / jnp.sum(psum, axis=-1, keepdims=True))
        o_ref[qi * tq:(qi + 1) * tq, :] = jnp.where(lane < MLA_V, heads[0], heads[1]).astype(BF16)


def _memkv_kernel(mem_ref, n_ref, w_ref, o_ref):
    mn = _rms(mem_ref[...], n_ref[...]).astype(BF16)
    o_ref[...] = _dot(mn, w_ref[...]).astype(BF16)


def _xattn_kernel(q_ref, kv_ref, o_ref):
    scale = 1.0 / math.sqrt(X_DH)
    hw = X_HEADS * X_DH
    for h in range(X_HEADS):
        q = q_ref[:, h * X_DH:(h + 1) * X_DH]
        k = kv_ref[:, h * X_DH:(h + 1) * X_DH]
        v = kv_ref[:, hw + h * X_DH:hw + (h + 1) * X_DH]
        s = _dot_nt(q, k) * scale
        m = jnp.max(s, axis=-1, keepdims=True)
        p = jnp.exp(s - m)
        l = jnp.sum(p, axis=-1, keepdims=True)
        o = _dot(p.astype(BF16), v) / l
        o_ref[:, h * X_DH:(h + 1) * X_DH] = o.astype(BF16)


def _gla_kernel(q_ref, k_ref, la_ref, v_ref, r_ref, on_ref, o_ref, st_ref):
    sb, c = GLA_SB, GLA_CHUNK
    nsb = q_ref.shape[0] // sb
    row = lax.broadcasted_iota(jnp.int32, (sb, sb), 0)
    col = lax.broadcasted_iota(jnp.int32, (sb, sb), 1)
    tri = jnp.logical_and(row // c == col // c, col <= row)
    tri_bf = jnp.where(tri, 1.0, 0.0).astype(BF16)
    lane = lax.broadcasted_iota(jnp.int32, (1, LANE), 1)
    head_lanes = (lane < GLA_DK, lane >= GLA_DK)
    st_ref[...] = jnp.zeros_like(st_ref)

    def body(i, _):
        r0 = pl.multiple_of(i * sb, sb)
        la = la_ref[pl.ds(r0, sb), :]
        hi = la.astype(BF16)
        rem = la - hi.astype(F32)
        mid = rem.astype(BF16)
        lo = (rem - mid.astype(F32)).astype(BF16)
        bcum = _dot(tri_bf, hi) + _dot(tri_bf, mid) + _dot(tri_bf, lo)
        b_last = [bcum[j * c + c - 1:j * c + c, :] for j in range(sb // c)]
        b_last_rows = jnp.concatenate([jnp.broadcast_to(b, (c, LANE)) for b in b_last], axis=0)
        q_t = q_ref[pl.ds(r0, sb), :] * (GLA_DK ** -0.5) * jnp.exp(bcum)
        kk = k_ref[pl.ds(r0, sb), :]
        k_t = (kk * jnp.exp(-bcum)).astype(BF16)
        k_d = (kk * jnp.exp(b_last_rows - bcum)).astype(BF16)
        v = v_ref[pl.ds(r0, sb), :]

        qm = [jnp.where(head_lanes[h], q_t, 0.0).astype(BF16) for h in range(2)]
        o_intra = []
        for h in range(2):
            att = jnp.where(tri, _dot_nt(qm[h], k_t), 0.0).astype(BF16)
            o_intra.append(_dot(att, v[:, h * GLA_DV:(h + 1) * GLA_DV]))

        o_inter = [[], []]
        for j in range(sb // c):
            st = st_ref[...]
            st_bf = st.astype(BF16)
            for h in range(2):
                o_inter[h].append(_dot_nt(qm[h][j * c:(j + 1) * c, :],
                                          st_bf[h * GLA_DV:(h + 1) * GLA_DV, :]))
            d_st = _dot_tn(v[j * c:(j + 1) * c, :], k_d[j * c:(j + 1) * c, :])
            st_ref[...] = st * jnp.exp(b_last[j]) + d_st

        for h in range(2):
            o = o_intra[h] + jnp.concatenate(o_inter[h], axis=0)
            o = _rms(o, on_ref[...])
            r = r_ref[pl.ds(r0, sb), h * GLA_DV:(h + 1) * GLA_DV].astype(F32)
            o_ref[pl.ds(r0, sb), h * GLA_DV:(h + 1) * GLA_DV] = (o * _silu(r)).astype(BF16)
        return 0

    lax.fori_loop(0, nsb, body, 0)


def _post_kernel(h1_ref, om_ref, og_ref, ox_ref, nm_ref, wgz_ref, gb_ref, wom_ref, wog_ref, wox_ref,
                 wout_ref, n2_ref, wg_ref, wu_ref, wd_ref, nf_ref, o_ref):
    h1 = h1_ref[...]
    u = _rms(h1, nm_ref[...]).astype(BF16)
    d = D_MODEL
    merged = None
    for n, (o_br, w_br) in enumerate(((om_ref, wom_ref), (og_ref, wog_ref), (ox_ref, wox_ref))):
        gate = _sigmoid(_dot(u, wgz_ref[:, n * d:(n + 1) * d]) + gb_ref[:, n * d:(n + 1) * d])
        term = gate * _dot(o_br[...], w_br[...])
        merged = term if merged is None else merged + term
    h2 = h1 + _dot(merged.astype(BF16), wout_ref[...])
    u2 = _rms(h2, n2_ref[...]).astype(BF16)
    h3 = h2 + 0.5 * _swiglu_half(u2, wg_ref, wu_ref, wd_ref)
    o_ref[...] = _rms(h3, nf_ref[...])


def _cparams(sem, vmem=None):
    return pltpu.CompilerParams(dimension_semantics=sem, vmem_limit_bytes=vmem)


def kernel(x, mem, positions, ffn1_norm, ffn1_wg, ffn1_wu, ffn1_wd, mix_norm, mem_norm, w_in, gate_bias,
           mla_q_norm, mla_w_uq, mla_kv_norm, mla_w_ukv, mla_w_o, gla_w_a2, gla_b_a, gla_o_norm, gla_w_o,
           x_w_kv, x_w_o, w_out, ffn2_norm, ffn2_wg, ffn2_wu, ffn2_wd, final_norm):
    B, S, D = x.shape
    T = B * S
    M = mem.shape[1]
    F = D_FF
    H = MLA_HEADS
    bf = lambda a: a.astype(BF16)
    row = lambda a: a.reshape(1, -1)

    wi = w_in[0]
    c0 = 0
    cols = {}
    for name, w in (("cq", MLA_Q_RANK), ("ckv", MLA_KV_RANK), ("kr", MLA_ROPE), ("gq", GLA_HEADS * GLA_DK),
                    ("gk", GLA_HEADS * GLA_DK), ("gv", GLA_HEADS * GLA_DV), ("ga", GLA_GATE_RANK),
                    ("gr", GLA_HEADS * GLA_DV), ("xq", X_HEADS * X_DH), ("gz", N_BRANCH * D)):
        cols[name] = wi[:, c0:c0 + w]
        c0 += w
    zc = lambda n: jnp.zeros((D, n), F32)
    w_in_a = bf(jnp.concatenate(
        [cols["cq"], cols["ckv"], zc(MLA_NOPE), cols["kr"], zc(HEAD_PAD - MLA_NOPE - MLA_ROPE),
         cols["gq"], cols["gk"], cols["gv"], cols["gr"], cols["xq"],
         cols["ga"], zc(LANE - GLA_GATE_RANK)], axis=1))
    w_gz = bf(cols["gz"])
    w_uq = mla_w_uq[0].reshape(MLA_Q_RANK, H, MLA_NOPE + MLA_ROPE)
    w_uq = bf(jnp.pad(w_uq, ((0, 0), (0, 0), (0, HEAD_PAD - MLA_NOPE - MLA_ROPE))).reshape(MLA_Q_RANK, H * HEAD_PAD))
    w_ukv = mla_w_ukv[0].reshape(MLA_KV_RANK, H, MLA_NOPE + MLA_V)
    w_uk = bf(jnp.pad(w_ukv[:, :, :MLA_NOPE], ((0, 0), (0, 0), (0, HEAD_PAD - MLA_NOPE))).reshape(MLA_KV_RANK, H * HEAD_PAD))
    w_uv = bf(w_ukv[:, :, MLA_NOPE:].reshape(MLA_KV_RANK, H * MLA_V))
    w_a2 = bf(jnp.pad(gla_w_a2[0], ((0, LANE - GLA_GATE_RANK), (0, 0))))
    half = MLA_ROPE // 2
    inv_freq = ROPE_THETA ** (-jnp.arange(half, dtype=F32) / half)
    invf = jnp.concatenate([jnp.zeros((MLA_NOPE,), F32), inv_freq, inv_freq,
                            jnp.zeros((HEAD_PAD - MLA_NOPE - MLA_ROPE,), F32)]).reshape(1, HEAD_PAD)

    x2 = x.reshape(T, D)
    pos2 = positions.reshape(T, 1)
    nt = T // TM
    tok = lambda w: pl.BlockSpec((TM, w), lambda i: (i, 0))

    gdk, gdv, xw = GLA_HEADS * GLA_DK, GLA_HEADS * GLA_DV, X_HEADS * X_DH
    pre_out_shapes = (
        jax.ShapeDtypeStruct((T, D), F32),
        jax.ShapeDtypeStruct((T, H * HEAD_PAD), BF16),
        jax.ShapeDtypeStruct((T, H * HEAD_PAD), BF16),
        jax.ShapeDtypeStruct((T, H * MLA_V), BF16),
        jax.ShapeDtypeStruct((T, gdk), F32),
        jax.ShapeDtypeStruct((T, gdk), F32),
        jax.ShapeDtypeStruct((T, gdv), BF16),
        jax.ShapeDtypeStruct((T, gdv), BF16),
        jax.ShapeDtypeStruct((T, xw), BF16),
        jax.ShapeDtypeStruct((T, gdk), F32),
    )
    h1, q, k, v, gq, gk, gv, gr, xq, la = pl.pallas_call(
        _pre_kernel,
        out_shape=pre_out_shapes,
        grid=(nt,),
        in_specs=[tok(D), tok(1), _const_spec((1, HEAD_PAD)), _const_spec((1, D)), _const_spec((1, D)),
                  _const_spec((D, F)), _const_spec((D, F)), _const_spec((F, D)), _const_spec((D, A_END)),
                  _const_spec((1, MLA_Q_RANK)), _const_spec((MLA_Q_RANK, H * HEAD_PAD)),
                  _const_spec((1, MLA_KV_RANK)), _const_spec((MLA_KV_RANK, H * HEAD_PAD)),
                  _const_spec((MLA_KV_RANK, H * MLA_V)), _const_spec((LANE, gdk)), _const_spec((1, gdk))],
        out_specs=tuple(tok(s.shape[1]) for s in pre_out_shapes),
        compiler_params=_cparams(("parallel",), VMEM_LIMIT),
        name="pre",
    )(x2, pos2, invf, row(ffn1_norm[0]), row(mix_norm[0]), bf(ffn1_wg[0]), bf(ffn1_wu[0]), bf(ffn1_wd[0]),
      w_in_a, row(mla_q_norm[0]), w_uq, row(mla_kv_norm[0]), w_uk, w_uv, w_a2, row(gla_b_a[0]))

    o_mla = pl.pallas_call(
        _mla_kernel,
        out_shape=jax.ShapeDtypeStruct((T, H * MLA_V), BF16),
        grid=(B, H // 2),
        in_specs=[pl.BlockSpec((S, 2 * HEAD_PAD), lambda b, p: (b, p)),
                  pl.BlockSpec((S, 2 * HEAD_PAD), lambda b, p: (b, p)),
                  pl.BlockSpec((S, 2 * MLA_V), lambda b, p: (b, p))],
        out_specs=pl.BlockSpec((S, 2 * MLA_V), lambda b, p: (b, p)),
        compiler_params=_cparams(("parallel", "parallel")),
        name="mla",
    )(q, k, v)

    kvx = pl.pallas_call(
        _memkv_kernel,
        out_shape=jax.ShapeDtypeStruct((B * M, 2 * xw), BF16),
        grid=(B,),
        in_specs=[pl.BlockSpec((M, D), lambda b: (b, 0)), _const_spec((1, D)), _const_spec((D, 2 * xw))],
        out_specs=pl.BlockSpec((M, 2 * xw), lambda b: (b, 0)),
        compiler_params=_cparams(("parallel",)),
        name="memkv",
    )(mem.reshape(B * M, D), row(mem_norm[0]), bf(x_w_kv[0]))
    nqx = S // TQX
    o_x = pl.pallas_call(
        _xattn_kernel,
        out_shape=jax.ShapeDtypeStruct((T, xw), BF16),
        grid=(B, nqx),
        in_specs=[pl.BlockSpec((TQX, xw), lambda b, i: (b * nqx + i, 0)),
                  pl.BlockSpec((M, 2 * xw), lambda b, i: (b, 0))],
        out_specs=pl.BlockSpec((TQX, xw), lambda b, i: (b * nqx + i, 0)),
        compiler_params=_cparams(("parallel", "parallel")),
        name="xattn",
    )(xq, kvx)

    o_gla = pl.pallas_call(
        _gla_kernel,
        out_shape=jax.ShapeDtypeStruct((T, gdv), BF16),
        grid=(B, GLA_HEADS // 2),
        in_specs=[pl.BlockSpec((S, LANE), lambda b, p: (b, p)),
                  pl.BlockSpec((S, LANE), lambda b, p: (b, p)),
                  pl.BlockSpec((S, LANE), lambda b, p: (b, p)),
                  pl.BlockSpec((S, 2 * GLA_DV), lambda b, p: (b, p)),
                  pl.BlockSpec((S, 2 * GLA_DV), lambda b, p: (b, p)),
                  _const_spec((1, GLA_DV))],
        out_specs=pl.BlockSpec((S, 2 * GLA_DV), lambda b, p: (b, p)),
        scratch_shapes=[pltpu.VMEM((2 * GLA_DV, LANE), F32)],
        compiler_params=_cparams(("parallel", "parallel")),
        name="gla",
    )(gq, gk, la, gv, gr, row(gla_o_norm[0]))

    out = pl.pallas_call(
        _post_kernel,
        out_shape=jax.ShapeDtypeStruct((T, D), F32),
        grid=(nt,),
        in_specs=[tok(D), tok(H * MLA_V), tok(gdv), tok(xw), _const_spec((1, D)),
                  _const_spec((D, N_BRANCH * D)), _const_spec((1, N_BRANCH * D)),
                  _const_spec((H * MLA_V, D)), _const_spec((gdv, D)), _const_spec((xw, D)),
                  _const_spec((D, D)), _const_spec((1, D)),
                  _const_spec((D, F)), _const_spec((D, F)), _const_spec((F, D)), _const_spec((1, D))],
        out_specs=tok(D),
        compiler_params=_cparams(("parallel",), VMEM_LIMIT),
        name="post",
    )(h1, o_mla, o_gla, o_x, row(mix_norm[0]), w_gz, gate_bias[0].reshape(1, N_BRANCH * D),
      bf(mla_w_o[0]), bf(gla_w_o[0]), bf(x_w_o[0]), bf(w_out[0]), row(ffn2_norm[0]),
      bf(ffn2_wg[0]), bf(ffn2_wu[0]), bf(ffn2_wd[0]), row(final_norm))
    return out.reshape(B, S, D)
```

```python
import functools
import math

import jax
import jax.numpy as jnp
from jax import lax
from jax.experimental import pallas as pl
from jax.experimental.pallas import tpu as pltpu

F32 = jnp.float32
BF16 = jnp.bfloat16

D_MODEL = 1024
N_MEM = 256
EPS = 1e-6
MLA_HEADS = 8
MLA_NOPE = 64
MLA_ROPE = 32
MLA_V = 64
MLA_Q_RANK = 384
MLA_KV_RANK = 256
ROPE_THETA = 10000.0
GLA_HEADS = 4
GLA_DK = 64
GLA_DV = 128
GLA_GATE_RANK = 16
GLA_TAU = 16.0
GLA_CHUNK = 64
X_HEADS = 4
X_DH = 128
D_FF = 2816
N_BRANCH = 3

LANE = 128
HEAD_PAD = 128
VMEM_LIMIT = 56 * 1024 * 1024
MLA_EXP2_SCALE = math.log2(math.e) / math.sqrt(MLA_NOPE + MLA_ROPE)
MLA_VT_ROWS = MLA_V + 16
MLA_LOOKAHEAD = 3

A_CQ, A_CKV, A_KR, A_GQ, A_GK, A_GV, A_GR, A_XQ, A_GA, A_END = (
    0, 384, 640, 768, 1024, 1280, 1792, 2304, 2816, 2944)

TM = 512
TQ = 256
TQX = 512
GLA_SB = 256


def _dot(a, b):
    return jnp.dot(a, b, preferred_element_type=F32)


def _dot_nt(a, b):
    return lax.dot_general(a, b, (((1,), (1,)), ((), ())), preferred_element_type=F32)


def _dot_tn(a, b):
    return lax.dot_general(a, b, (((0,), (0,)), ((), ())), preferred_element_type=F32)


def _rms(x, g):
    return x * lax.rsqrt(jnp.mean(x * x, axis=-1, keepdims=True) + EPS) * g


def _sigmoid(x):
    return 1.0 / (1.0 + jnp.exp(-x))


def _silu(x):
    return x * _sigmoid(x)


def _const_spec(shape):
    return pl.BlockSpec(shape, lambda *_: (0,) * len(shape), pipeline_mode=pl.Buffered(1))


def _swiglu_half(xn_bf16, wg_ref, wu_ref, wd_ref):
    g = _dot(xn_bf16, wg_ref[...])
    up = _dot(xn_bf16, wu_ref[...])
    a = (_silu(g) * up).astype(BF16)
    return _dot(a, wd_ref[...])


def _pre_kernel(x_ref, pos_ref, invf_ref, n1_ref, nm_ref, wg_ref, wu_ref, wd_ref, win_ref,
                qn_ref, wuq_ref, kvn_ref, wuk_ref, wuv_ref, wa2_ref, ba_ref,
                h1_ref, q_ref, k_ref, v_ref, gq_ref, gk_ref, gv_ref, gr_ref, xq_ref, la_ref):
    x = x_ref[...]
    u1 = _rms(x, n1_ref[...]).astype(BF16)
    h1 = x + 0.5 * _swiglu_half(u1, wg_ref, wu_ref, wd_ref)
    h1_ref[...] = h1
    u2 = _rms(h1, nm_ref[...]).astype(BF16)

    ang = pos_ref[...].astype(F32) * invf_ref[...]
    cos = jnp.cos(ang)
    sin = jnp.sin(ang)
    lane = lax.broadcasted_iota(jnp.int32, ang.shape, 1)
    sin_hi = jnp.where(lane >= 80, sin, 0.0)
    sin_lo = jnp.where(lane < 80, -sin, 0.0)

    def rope(t):
        return t * cos + pltpu.roll(t, 16, 1) * sin_hi + pltpu.roll(t, LANE - 16, 1) * sin_lo

    cqn = _rms(_dot(u2, win_ref[:, A_CQ:A_CKV]), qn_ref[...]).astype(BF16)
    for p in range(MLA_HEADS // 2):
        qp = _dot(cqn, wuq_ref[:, 2 * p * HEAD_PAD:(2 * p + 2) * HEAD_PAD])
        for j in range(2):
            h = 2 * p + j
            q_ref[:, h * HEAD_PAD:(h + 1) * HEAD_PAD] = (
                rope(qp[:, j * HEAD_PAD:(j + 1) * HEAD_PAD]) * MLA_EXP2_SCALE).astype(BF16)

    ckvn = _rms(_dot(u2, win_ref[:, A_CKV:A_KR]), kvn_ref[...]).astype(BF16)
    kr = rope(_dot(u2, win_ref[:, A_KR:A_GQ]))
    kr2 = jnp.concatenate([kr, kr], axis=1)
    for p in range(MLA_HEADS // 2):
        kn = _dot(ckvn, wuk_ref[:, 2 * p * HEAD_PAD:(2 * p + 2) * HEAD_PAD])
        k_ref[:, 2 * p * HEAD_PAD:(2 * p + 2) * HEAD_PAD] = (kn + kr2).astype(BF16)
    v_ref[...] = _dot(ckvn, wuv_ref[...]).astype(BF16)

    gq_ref[...] = _dot(u2, win_ref[:, A_GQ:A_GK])
    gk_ref[...] = _dot(u2, win_ref[:, A_GK:A_GV])
    gv_ref[...] = _dot(u2, win_ref[:, A_GV:A_GR]).astype(BF16)
    gr_ref[...] = _dot(u2, win_ref[:, A_GR:A_XQ]).astype(BF16)
    xq_ref[...] = _dot(u2, win_ref[:, A_XQ:A_GA]).astype(BF16)
    ga = _dot(u2, win_ref[:, A_GA:A_END]).astype(BF16)
    t = _dot(ga, wa2_ref[...]) + ba_ref[...]
    log_sig = jnp.minimum(t, 0.0) - jnp.log(1.0 + jnp.exp(-jnp.abs(t)))
    la_ref[...] = log_sig * (1.0 / GLA_TAU)


def _mla_kernel(q_ref, k_ref, v_ref, o_ref, vt_ref, s_ref):
    tq = TQ
    nq = q_ref.shape[0] // tq
    key = lax.broadcasted_iota(jnp.int32, (tq, tq), 0)
    qry = lax.broadcasted_iota(jnp.int32, (tq, tq), 1)
    causal = key <= qry
    vt = v_ref[...].astype(F32).T.astype(BF16)
    ones_row = jnp.where(lax.broadcasted_iota(jnp.int32, (MLA_VT_ROWS - MLA_V, vt.shape[1]), 0) == 0,
                         1.0, 0.0).astype(BF16)
    for h in range(2):
        vt_ref[h * MLA_VT_ROWS:h * MLA_VT_ROWS + MLA_V, :] = vt[h * MLA_V:(h + 1) * MLA_V, :]
        vt_ref[h * MLA_VT_ROWS + MLA_V:(h + 1) * MLA_VT_ROWS, :] = ones_row

    def scores(qi, h, slot):
        q = q_ref[qi * tq:(qi + 1) * tq, h * HEAD_PAD:(h + 1) * HEAD_PAD]
        m = None
        for c in range(qi + 1):
            s = _dot_nt(k_ref[c * tq:(c + 1) * tq, h * HEAD_PAD:(h + 1) * HEAD_PAD], q)
            if c == qi:
                s = jnp.where(causal, s, -1e30)
            s_ref[slot, c * tq:(c + 1) * tq, :] = s
            cm = jnp.max(s, axis=0, keepdims=True)
            m = cm if m is None else jnp.maximum(m, cm)
        return m

    def values(qi, h, slot, m):
        acc = None
        for c in range(qi + 1):
            p = jnp.exp2(s_ref[slot, c * tq:(c + 1) * tq, :] - m).astype(BF16)
            pv = _dot(vt_ref[h * MLA_VT_ROWS:(h + 1) * MLA_VT_ROWS, c * tq:(c + 1) * tq], p)
            acc = pv if acc is None else acc + pv
        return acc[:MLA_V, :] / acc[MLA_V:MLA_V + 1, :]

    groups = [(qi, h) for qi in range(nq) for h in range(2)]
    nslot = s_ref.shape[0]
    maxes = {}
    for i in range(min(MLA_LOOKAHEAD, len(groups))):
        maxes[i] = scores(*groups[i], i % nslot)
    heads = {}
    for i, (qi, h) in enumerate(groups):
        j = i + MLA_LOOKAHEAD
        if j < len(groups):
            maxes[j] = scores(*groups[j], j % nslot)
        heads[h] = values(qi, h, i % nslot, maxes.pop(i))
        if h == 1:
            o_ref[qi * tq:(qi + 1) * tq, :] = jnp.concatenate([heads[0], heads[1]], axis=0).T.astype(BF16)


def _memkv_kernel(mem_ref, n_ref, w_ref, o_ref):
    mn = _rms(mem_ref[...], n_ref[...]).astype(BF16)
    o_ref[...] = _dot(mn, w_ref[...]).astype(BF16)


def _xattn_kernel(q_ref, kv_ref, o_ref):
    scale = 1.0 / math.sqrt(X_DH)
    hw = X_HEADS * X_DH
    for h in range(X_HEADS):
        q = q_ref[:, h * X_DH:(h + 1) * X_DH]
        k = kv_ref[:, h * X_DH:(h + 1) * X_DH]
        v = kv_ref[:, hw + h * X_DH:hw + (h + 1) * X_DH]
        s = _dot_nt(q, k) * scale
        m = jnp.max(s, axis=-1, keepdims=True)
        p = jnp.exp(s - m)
        l = jnp.sum(p, axis=-1, keepdims=True)
        o = _dot(p.astype(BF16), v) / l
        o_ref[:, h * X_DH:(h + 1) * X_DH] = o.astype(BF16)


def _gla_kernel(q_ref, k_ref, la_ref, v_ref, r_ref, on_ref, o_ref, st_ref):
    sb, c = GLA_SB, GLA_CHUNK
    nsb = q_ref.shape[0] // sb
    row = lax.broadcasted_iota(jnp.int32, (sb, sb), 0)
    col = lax.broadcasted_iota(jnp.int32, (sb, sb), 1)
    tri = jnp.logical_and(row // c == col // c, col <= row)
    tri_bf = jnp.where(tri, 1.0, 0.0).astype(BF16)
    lane = lax.broadcasted_iota(jnp.int32, (1, LANE), 1)
    head_lanes = (lane < GLA_DK, lane >= GLA_DK)
    st_ref[...] = jnp.zeros_like(st_ref)

    def body(i, _):
        r0 = pl.multiple_of(i * sb, sb)
        la = la_ref[pl.ds(r0, sb), :]
        hi = la.astype(BF16)
        rem = la - hi.astype(F32)
        mid = rem.astype(BF16)
        lo = (rem - mid.astype(F32)).astype(BF16)
        bcum = _dot(tri_bf, hi) + _dot(tri_bf, mid) + _dot(tri_bf, lo)
        b_last = [bcum[j * c + c - 1:j * c + c, :] for j in range(sb // c)]
        b_last_rows = jnp.concatenate([jnp.broadcast_to(b, (c, LANE)) for b in b_last], axis=0)
        q_t = q_ref[pl.ds(r0, sb), :] * (GLA_DK ** -0.5) * jnp.exp(bcum)
        kk = k_ref[pl.ds(r0, sb), :]
        k_t = (kk * jnp.exp(-bcum)).astype(BF16)
        k_d = (kk * jnp.exp(b_last_rows - bcum)).astype(BF16)
        v = v_ref[pl.ds(r0, sb), :]

        qm = [jnp.where(head_lanes[h], q_t, 0.0).astype(BF16) for h in range(2)]
        o_intra = []
        for h in range(2):
            att = jnp.where(tri, _dot_nt(qm[h], k_t), 0.0).astype(BF16)
            o_intra.append(_dot(att, v[:, h * GLA_DV:(h + 1) * GLA_DV]))

        o_inter = [[], []]
        for j in range(sb // c):
            st = st_ref[...]
            st_bf = st.astype(BF16)
            for h in range(2):
                o_inter[h].append(_dot_nt(qm[h][j * c:(j + 1) * c, :],
                                          st_bf[h * GLA_DV:(h + 1) * GLA_DV, :]))
            d_st = _dot_tn(v[j * c:(j + 1) * c, :], k_d[j * c:(j + 1) * c, :])
            st_ref[...] = st * jnp.exp(b_last[j]) + d_st

        for h in range(2):
            o = o_intra[h] + jnp.concatenate(o_inter[h], axis=0)
            o = _rms(o, on_ref[...])
            r = r_ref[pl.ds(r0, sb), h * GLA_DV:(h + 1) * GLA_DV].astype(F32)
            o_ref[pl.ds(r0, sb), h * GLA_DV:(h + 1) * GLA_DV] = (o * _silu(r)).astype(BF16)
        return 0

    lax.fori_loop(0, nsb, body, 0)


def _post_kernel(h1_ref, om_ref, og_ref, ox_ref, nm_ref, wgz_ref, gb_ref, wom_ref, wog_ref, wox_ref,
                 wout_ref, n2_ref, wg_ref, wu_ref, wd_ref, nf_ref, o_ref):
    h1 = h1_ref[...]
    u = _rms(h1, nm_ref[...]).astype(BF16)
    d = D_MODEL
    merged = None
    for n, (o_br, w_br) in enumerate(((om_ref, wom_ref), (og_ref, wog_ref), (ox_ref, wox_ref))):
        gate = _sigmoid(_dot(u, wgz_ref[:, n * d:(n + 1) * d]) + gb_ref[:, n * d:(n + 1) * d])
        term = gate * _dot(o_br[...], w_br[...])
        merged = term if merged is None else merged + term
    h2 = h1 + _dot(merged.astype(BF16), wout_ref[...])
    u2 = _rms(h2, n2_ref[...]).astype(BF16)
    h3 = h2 + 0.5 * _swiglu_half(u2, wg_ref, wu_ref, wd_ref)
    o_ref[...] = _rms(h3, nf_ref[...])


def _cparams(sem, vmem=None, flags=None):
    return pltpu.CompilerParams(dimension_semantics=sem, vmem_limit_bytes=vmem, flags=flags)


def kernel(x, mem, positions, ffn1_norm, ffn1_wg, ffn1_wu, ffn1_wd, mix_norm, mem_norm, w_in, gate_bias,
           mla_q_norm, mla_w_uq, mla_kv_norm, mla_w_ukv, mla_w_o, gla_w_a2, gla_b_a, gla_o_norm, gla_w_o,
           x_w_kv, x_w_o, w_out, ffn2_norm, ffn2_wg, ffn2_wu, ffn2_wd, final_norm):
    B, S, D = x.shape
    T = B * S
    M = mem.shape[1]
    F = D_FF
    H = MLA_HEADS
    bf = lambda a: a.astype(BF16)
    row = lambda a: a.reshape(1, -1)

    wi = w_in[0]
    c0 = 0
    cols = {}
    for name, w in (("cq", MLA_Q_RANK), ("ckv", MLA_KV_RANK), ("kr", MLA_ROPE), ("gq", GLA_HEADS * GLA_DK),
                    ("gk", GLA_HEADS * GLA_DK), ("gv", GLA_HEADS * GLA_DV), ("ga", GLA_GATE_RANK),
                    ("gr", GLA_HEADS * GLA_DV), ("xq", X_HEADS * X_DH), ("gz", N_BRANCH * D)):
        cols[name] = wi[:, c0:c0 + w]
        c0 += w
    zc = lambda n: jnp.zeros((D, n), F32)
    w_in_a = bf(jnp.concatenate(
        [cols["cq"], cols["ckv"], zc(MLA_NOPE), cols["kr"], zc(HEAD_PAD - MLA_NOPE - MLA_ROPE),
         cols["gq"], cols["gk"], cols["gv"], cols["gr"], cols["xq"],
         cols["ga"], zc(LANE - GLA_GATE_RANK)], axis=1))
    w_gz = bf(cols["gz"])
    w_uq = mla_w_uq[0].reshape(MLA_Q_RANK, H, MLA_NOPE + MLA_ROPE)
    w_uq = bf(jnp.pad(w_uq, ((0, 0), (0, 0), (0, HEAD_PAD - MLA_NOPE - MLA_ROPE))).reshape(MLA_Q_RANK, H * HEAD_PAD))
    w_ukv = mla_w_ukv[0].reshape(MLA_KV_RANK, H, MLA_NOPE + MLA_V)
    w_uk = bf(jnp.pad(w_ukv[:, :, :MLA_NOPE], ((0, 0), (0, 0), (0, HEAD_PAD - MLA_NOPE))).reshape(MLA_KV_RANK, H * HEAD_PAD))
    w_uv = bf(w_ukv[:, :, MLA_NOPE:].reshape(MLA_KV_RANK, H * MLA_V))
    w_a2 = bf(jnp.pad(gla_w_a2[0], ((0, LANE - GLA_GATE_RANK), (0, 0))))
    half = MLA_ROPE // 2
    inv_freq = ROPE_THETA ** (-jnp.arange(half, dtype=F32) / half)
    invf = jnp.concatenate([jnp.zeros((MLA_NOPE,), F32), inv_freq, inv_freq,
                            jnp.zeros((HEAD_PAD - MLA_NOPE - MLA_ROPE,), F32)]).reshape(1, HEAD_PAD)

    x2 = x.reshape(T, D)
    pos2 = jnp.broadcast_to(positions.reshape(T, 1), (T, HEAD_PAD))
    nt = T // TM
    tok = lambda w: pl.BlockSpec((TM, w), lambda i: (i, 0))

    gdk, gdv, xw = GLA_HEADS * GLA_DK, GLA_HEADS * GLA_DV, X_HEADS * X_DH
    pre_out_shapes = (
        jax.ShapeDtypeStruct((T, D), F32),
        jax.ShapeDtypeStruct((T, H * HEAD_PAD), BF16),
        jax.ShapeDtypeStruct((T, H * HEAD_PAD), BF16),
        jax.ShapeDtypeStruct((T, H * MLA_V), BF16),
        jax.ShapeDtypeStruct((T, gdk), F32),
        jax.ShapeDtypeStruct((T, gdk), F32),
        jax.ShapeDtypeStruct((T, gdv), BF16),
        jax.ShapeDtypeStruct((T, gdv), BF16),
        jax.ShapeDtypeStruct((T, xw), BF16),
        jax.ShapeDtypeStruct((T, gdk), F32),
    )
    h1, q, k, v, gq, gk, gv, gr, xq, la = pl.pallas_call(
        _pre_kernel,
        out_shape=pre_out_shapes,
        grid=(nt,),
        in_specs=[tok(D), tok(HEAD_PAD), _const_spec((1, HEAD_PAD)), _const_spec((1, D)), _const_spec((1, D)),
                  _const_spec((D, F)), _const_spec((D, F)), _const_spec((F, D)), _const_spec((D, A_END)),
                  _const_spec((1, MLA_Q_RANK)), _const_spec((MLA_Q_RANK, H * HEAD_PAD)),
                  _const_spec((1, MLA_KV_RANK)), _const_spec((MLA_KV_RANK, H * HEAD_PAD)),
                  _const_spec((MLA_KV_RANK, H * MLA_V)), _const_spec((LANE, gdk)), _const_spec((1, gdk))],
        out_specs=tuple(tok(s.shape[1]) for s in pre_out_shapes),
        compiler_params=_cparams(("parallel",), VMEM_LIMIT),
        name="pre",
    )(x2, pos2, invf, row(ffn1_norm[0]), row(mix_norm[0]), bf(ffn1_wg[0]), bf(ffn1_wu[0]), bf(ffn1_wd[0]),
      w_in_a, row(mla_q_norm[0]), w_uq, row(mla_kv_norm[0]), w_uk, w_uv, w_a2, row(gla_b_a[0]))

    o_mla = pl.pallas_call(
        _mla_kernel,
        out_shape=jax.ShapeDtypeStruct((T, H * MLA_V), BF16),
        grid=(B, H // 2),
        in_specs=[pl.BlockSpec((S, 2 * HEAD_PAD), lambda b, p: (b, p)),
                  pl.BlockSpec((S, 2 * HEAD_PAD), lambda b, p: (b, p)),
                  pl.BlockSpec((S, 2 * MLA_V), lambda b, p: (b, p))],
        out_specs=pl.BlockSpec((S, 2 * MLA_V), lambda b, p: (b, p)),
        scratch_shapes=[pltpu.VMEM((2 * MLA_VT_ROWS, S), BF16), pltpu.VMEM((MLA_LOOKAHEAD + 1, S, TQ), F32)],
        compiler_params=_cparams(("parallel", "parallel")),
        name="mla",
    )(q, k, v)

    kvx = pl.pallas_call(
        _memkv_kernel,
        out_shape=jax.ShapeDtypeStruct((B * M, 2 * xw), BF16),
        grid=(B,),
        in_specs=[pl.BlockSpec((M, D), lambda b: (b, 0)), _const_spec((1, D)), _const_spec((D, 2 * xw))],
        out_specs=pl.BlockSpec((M, 2 * xw), lambda b: (b, 0)),
        compiler_params=_cparams(("parallel",)),
        name="memkv",
    )(mem.reshape(B * M, D), row(mem_norm[0]), bf(x_w_kv[0]))
    nqx = S // TQX
    o_x = pl.pallas_call(
        _xattn_kernel,
        out_shape=jax.ShapeDtypeStruct((T, xw), BF16),
        grid=(B, nqx),
        in_specs=[pl.BlockSpec((TQX, xw), lambda b, i: (b * nqx + i, 0)),
                  pl.BlockSpec((M, 2 * xw), lambda b, i: (b, 0))],
        out_specs=pl.BlockSpec((TQX, xw), lambda b, i: (b * nqx + i, 0)),
        compiler_params=_cparams(("parallel", "parallel")),
        name="xattn",
    )(xq, kvx)

    o_gla = pl.pallas_call(
        _gla_kernel,
        out_shape=jax.ShapeDtypeStruct((T, gdv), BF16),
        grid=(B, GLA_HEADS // 2),
        in_specs=[pl.BlockSpec((S, LANE), lambda b, p: (b, p)),
                  pl.BlockSpec((S, LANE), lambda b, p: (b, p)),
                  pl.BlockSpec((S, LANE), lambda b, p: (b, p)),
                  pl.BlockSpec((S, 2 * GLA_DV), lambda b, p: (b, p)),
                  pl.BlockSpec((S, 2 * GLA_DV), lambda b, p: (b, p)),
                  _const_spec((1, GLA_DV))],
        out_specs=pl.BlockSpec((S, 2 * GLA_DV), lambda b, p: (b, p)),
        scratch_shapes=[pltpu.VMEM((2 * GLA_DV, LANE), F32)],
        compiler_params=_cparams(("parallel", "parallel")),
        name="gla",
    )(gq, gk, la, gv, gr, row(gla_o_norm[0]))

    out = pl.pallas_call(
        _post_kernel,
        out_shape=jax.ShapeDtypeStruct((T, D), F32),
        grid=(nt,),
        in_specs=[tok(D), tok(H * MLA_V), tok(gdv), tok(xw), _const_spec((1, D)),
                  _const_spec((D, N_BRANCH * D)), _const_spec((1, N_BRANCH * D)),
                  _const_spec((H * MLA_V, D)), _const_spec((gdv, D)), _const_spec((xw, D)),
                  _const_spec((D, D)), _const_spec((1, D)),
                  _const_spec((D, F)), _const_spec((D, F)), _const_spec((F, D)), _const_spec((1, D))],
        out_specs=tok(D),
        compiler_params=_cparams(("parallel",), VMEM_LIMIT),
        name="post",
    )(h1, o_mla, o_gla, o_x, row(mix_norm[0]), w_gz, gate_bias[0].reshape(1, N_BRANCH * D),
      bf(mla_w_o[0]), bf(gla_w_o[0]), bf(x_w_o[0]), bf(w_out[0]), row(ffn2_norm[0]),
      bf(ffn2_wg[0]), bf(ffn2_wu[0]), bf(ffn2_wd[0]), row(final_norm))
    return out.reshape(B, S, D)
```

```python
import functools
import math

import jax
import jax.numpy as jnp
from jax import lax
from jax.experimental import pallas as pl
from jax.experimental.pallas import tpu as pltpu

F32 = jnp.float32
BF16 = jnp.bfloat16

D_MODEL = 1024
N_MEM = 256
EPS = 1e-6
MLA_HEADS = 8
MLA_NOPE = 64
MLA_ROPE = 32
MLA_V = 64
MLA_Q_RANK = 384
MLA_KV_RANK = 256
ROPE_THETA = 10000.0
GLA_HEADS = 4
GLA_DK = 64
GLA_DV = 128
GLA_GATE_RANK = 16
GLA_TAU = 16.0
GLA_CHUNK = 64
X_HEADS = 4
X_DH = 128
D_FF = 2816
N_BRANCH = 3

LANE = 128
HEAD_PAD = 128
VMEM_LIMIT = 56 * 1024 * 1024
MLA_EXP2_SCALE = math.log2(math.e) / math.sqrt(MLA_NOPE + MLA_ROPE)
MLA_VT_ROWS = MLA_V + 16
MLA_LOOKAHEAD = 3

A_CQ, A_CKV, A_KR, A_GQ, A_GK, A_GV, A_GR, A_XQ, A_GA, A_END = (
    0, 384, 640, 768, 1024, 1280, 1792, 2304, 2816, 2944)

TM = 512
TQ = 256
TQX = 512
GLA_SB = 256


def _dot(a, b):
    return jnp.dot(a, b, preferred_element_type=F32)


def _dot_nt(a, b):
    return lax.dot_general(a, b, (((1,), (1,)), ((), ())), preferred_element_type=F32)


def _dot_tn(a, b):
    return lax.dot_general(a, b, (((0,), (0,)), ((), ())), preferred_element_type=F32)


def _rms(x, g):
    return x * lax.rsqrt(jnp.mean(x * x, axis=-1, keepdims=True) + EPS) * g


def _sigmoid(x):
    return 1.0 / (1.0 + jnp.exp(-x))


def _silu(x):
    return x * _sigmoid(x)


def _const_spec(shape):
    return pl.BlockSpec(shape, lambda *_: (0,) * len(shape), pipeline_mode=pl.Buffered(1))


def _swiglu_half(xn_bf16, wg_ref, wu_ref, wd_ref):
    g = _dot(xn_bf16, wg_ref[...])
    up = _dot(xn_bf16, wu_ref[...])
    a = (_silu(g) * up).astype(BF16)
    return _dot(a, wd_ref[...])


def _pre_kernel(x_ref, pos_ref, invf_ref, n1_ref, nm_ref, wg_ref, wu_ref, wd_ref, win_ref,
                qn_ref, wuq_ref, kvn_ref, wuk_ref, wuv_ref, wa2_ref, ba_ref,
                h1_ref, q_ref, k_ref, v_ref, gq_ref, gk_ref, gv_ref, gr_ref, xq_ref, la_ref):
    x = x_ref[...]
    u1 = _rms(x, n1_ref[...]).astype(BF16)
    h1 = x + 0.5 * _swiglu_half(u1, wg_ref, wu_ref, wd_ref)
    h1_ref[...] = h1
    u2 = _rms(h1, nm_ref[...]).astype(BF16)

    ang = pos_ref[...].astype(F32) * invf_ref[...]
    cos = jnp.cos(ang)
    sin = jnp.sin(ang)
    lane = lax.broadcasted_iota(jnp.int32, ang.shape, 1)
    sin_hi = jnp.where(lane >= 80, sin, 0.0)
    sin_lo = jnp.where(lane < 80, -sin, 0.0)

    def rope(t):
        return t * cos + pltpu.roll(t, 16, 1) * sin_hi + pltpu.roll(t, LANE - 16, 1) * sin_lo

    cqn = _rms(_dot(u2, win_ref[:, A_CQ:A_CKV]), qn_ref[...]).astype(BF16)
    for p in range(MLA_HEADS // 2):
        qp = _dot(cqn, wuq_ref[:, 2 * p * HEAD_PAD:(2 * p + 2) * HEAD_PAD])
        for j in range(2):
            h = 2 * p + j
            q_ref[:, h * HEAD_PAD:(h + 1) * HEAD_PAD] = (
                rope(qp[:, j * HEAD_PAD:(j + 1) * HEAD_PAD]) * MLA_EXP2_SCALE).astype(BF16)

    ckvn = _rms(_dot(u2, win_ref[:, A_CKV:A_KR]), kvn_ref[...]).astype(BF16)
    kr = rope(_dot(u2, win_ref[:, A_KR:A_GQ]))
    kr2 = jnp.concatenate([kr, kr], axis=1)
    for p in range(MLA_HEADS // 2):
        kn = _dot(ckvn, wuk_ref[:, 2 * p * HEAD_PAD:(2 * p + 2) * HEAD_PAD])
        k_ref[:, 2 * p * HEAD_PAD:(2 * p + 2) * HEAD_PAD] = (kn + kr2).astype(BF16)
    v_ref[...] = _dot(ckvn, wuv_ref[...]).astype(BF16)

    gq_ref[...] = _dot(u2, win_ref[:, A_GQ:A_GK])
    gk_ref[...] = _dot(u2, win_ref[:, A_GK:A_GV])
    gv_ref[...] = _dot(u2, win_ref[:, A_GV:A_GR]).astype(BF16)
    gr_ref[...] = _dot(u2, win_ref[:, A_GR:A_XQ]).astype(BF16)
    xq_ref[...] = _dot(u2, win_ref[:, A_XQ:A_GA]).astype(BF16)
    ga = _dot(u2, win_ref[:, A_GA:A_END]).astype(BF16)
    t = _dot(ga, wa2_ref[...]) + ba_ref[...]
    log_sig = jnp.minimum(t, 0.0) - jnp.log(1.0 + jnp.exp(-jnp.abs(t)))
    la_ref[...] = log_sig * (1.0 / GLA_TAU)


def _mla_kernel(q_ref, k_ref, v_ref, o_ref, vt_ref, s_ref):
    tq = TQ
    nq = q_ref.shape[0] // tq
    key = lax.broadcasted_iota(jnp.int32, (tq, tq), 0)
    qry = lax.broadcasted_iota(jnp.int32, (tq, tq), 1)
    causal = key <= qry
    vt = v_ref[...].astype(F32).T.astype(BF16)
    ones_row = jnp.where(lax.broadcasted_iota(jnp.int32, (MLA_VT_ROWS - MLA_V, vt.shape[1]), 0) == 0,
                         1.0, 0.0).astype(BF16)
    for h in range(2):
        vt_ref[h * MLA_VT_ROWS:h * MLA_VT_ROWS + MLA_V, :] = vt[h * MLA_V:(h + 1) * MLA_V, :]
        vt_ref[h * MLA_VT_ROWS + MLA_V:(h + 1) * MLA_VT_ROWS, :] = ones_row

    def scores(qi, h, slot):
        q = q_ref[qi * tq:(qi + 1) * tq, h * HEAD_PAD:(h + 1) * HEAD_PAD]
        m = None
        for c in range(qi + 1):
            s = _dot_nt(k_ref[c * tq:(c + 1) * tq, h * HEAD_PAD:(h + 1) * HEAD_PAD], q)
            if c == qi:
                s = jnp.where(causal, s, -1e30)
            s_ref[slot, c * tq:(c + 1) * tq, :] = s
            cm = jnp.max(s, axis=0, keepdims=True)
            m = cm if m is None else jnp.maximum(m, cm)
        return m

    def values(qi, h, slot, m):
        acc = None
        for c in range(qi + 1):
            p = jnp.exp2(s_ref[slot, c * tq:(c + 1) * tq, :] - m).astype(BF16)
            pv = _dot(vt_ref[h * MLA_VT_ROWS:(h + 1) * MLA_VT_ROWS, c * tq:(c + 1) * tq], p)
            acc = pv if acc is None else acc + pv
        return acc[:MLA_V, :] / acc[MLA_V:MLA_V + 1, :]

    groups = [(qi, h) for qi in range(nq) for h in range(2)]
    nslot = s_ref.shape[0]
    maxes = {}
    for i in range(min(MLA_LOOKAHEAD, len(groups))):
        maxes[i] = scores(*groups[i], i % nslot)
    heads = {}
    for i, (qi, h) in enumerate(groups):
        j = i + MLA_LOOKAHEAD
        if j < len(groups):
            maxes[j] = scores(*groups[j], j % nslot)
        heads[h] = values(qi, h, i % nslot, maxes.pop(i))
        if h == 1:
            o_ref[qi * tq:(qi + 1) * tq, :] = jnp.concatenate([heads[0], heads[1]], axis=0).T.astype(BF16)


def _xattn_kernel(q_ref, mem_ref, n_ref, w_ref, o_ref, kv_ref):
    @pl.when(pl.program_id(1) == 0)
    def _():
        mn = _rms(mem_ref[...], n_ref[...]).astype(BF16)
        kv_ref[...] = _dot(mn, w_ref[...]).astype(BF16)

    scale = 1.0 / math.sqrt(X_DH)
    hw = X_HEADS * X_DH
    for h in range(X_HEADS):
        q = q_ref[:, h * X_DH:(h + 1) * X_DH]
        k = kv_ref[:, h * X_DH:(h + 1) * X_DH]
        v = kv_ref[:, hw + h * X_DH:hw + (h + 1) * X_DH]
        s = _dot_nt(q, k) * scale
        m = jnp.max(s, axis=-1, keepdims=True)
        p = jnp.exp(s - m)
        l = jnp.sum(p, axis=-1, keepdims=True)
        o = _dot(p.astype(BF16), v) / l
        o_ref[:, h * X_DH:(h + 1) * X_DH] = o.astype(BF16)


def _gla_kernel(q_ref, k_ref, la_ref, v_ref, r_ref, on_ref, o_ref):
    sb, c = GLA_SB, GLA_CHUNK
    nsb = q_ref.shape[0] // sb
    nch = sb // c
    npair = GLA_HEADS // 2
    pw, vw = 2 * GLA_DK, 2 * GLA_DV
    row = lax.broadcasted_iota(jnp.int32, (sb, sb), 0)
    col = lax.broadcasted_iota(jnp.int32, (sb, sb), 1)
    tri = jnp.logical_and(row // c == col // c, col <= row)
    tri_bf = jnp.where(tri, 1.0, 0.0).astype(BF16)
    lane = lax.broadcasted_iota(jnp.int32, (1, pw), 1)
    head_lanes = (lane < GLA_DK, lane >= GLA_DK)
    srow = lax.broadcasted_iota(jnp.int32, (pw, vw), 0)
    scol = lax.broadcasted_iota(jnp.int32, (pw, vw), 1)
    same_head = (srow < GLA_DK) == (scol < GLA_DV)

    def local(i):
        r0 = i * sb
        la = la_ref[r0:r0 + sb, :]
        hi = la.astype(BF16)
        lo = (la - hi.astype(F32)).astype(BF16)
        bcum_all = _dot(tri_bf, hi) + _dot(tri_bf, lo)
        out = []
        for p in range(npair):
            bcum = bcum_all[:, p * pw:(p + 1) * pw]
            b_last = [bcum[j * c + c - 1:j * c + c, :] for j in range(nch)]
            b_last_rows = jnp.concatenate([jnp.broadcast_to(b, (c, pw)) for b in b_last], axis=0)
            q_t = q_ref[r0:r0 + sb, p * pw:(p + 1) * pw] * (GLA_DK ** -0.5) * jnp.exp(bcum)
            kk = k_ref[r0:r0 + sb, p * pw:(p + 1) * pw]
            k_t = (kk * jnp.exp(-bcum)).astype(BF16)
            k_d = (kk * jnp.exp(b_last_rows - bcum)).astype(BF16)
            v = v_ref[r0:r0 + sb, p * vw:(p + 1) * vw]
            o_intra = []
            for h in range(2):
                qm = jnp.where(head_lanes[h], q_t, 0.0).astype(BF16)
                att = jnp.where(tri, _dot_nt(qm, k_t), 0.0).astype(BF16)
                o_intra.append(_dot(att, v[:, h * GLA_DV:(h + 1) * GLA_DV]))
            d_st, decay = [], []
            for j in range(nch):
                d = _dot_tn(k_d[j * c:(j + 1) * c, :], v[j * c:(j + 1) * c, :])
                d_st.append(jnp.where(same_head, d, 0.0))
                dcol = jnp.broadcast_to(jnp.exp(b_last[j]), (pw, pw)).T
                decay.append(jnp.concatenate([dcol, dcol], axis=1))
            out.append((q_t.astype(BF16), o_intra, d_st, decay))
        return out

    def recurrent(i, loc, states):
        r0 = i * sb
        for p in range(npair):
            q_bf, o_intra, d_st, decay = loc[p]
            st = states[p]
            o_inter = []
            for j in range(nch):
                o_inter.append(_dot(q_bf[j * c:(j + 1) * c, :], st.astype(BF16)))
                st = decay[j] * st + d_st[j]
            states[p] = st
            o_inter = jnp.concatenate(o_inter, axis=0)
            for h in range(2):
                lo_, hi_ = p * vw + h * GLA_DV, p * vw + (h + 1) * GLA_DV
                o = _rms(o_intra[h] + o_inter[:, h * GLA_DV:(h + 1) * GLA_DV], on_ref[...])
                r = r_ref[r0:r0 + sb, lo_:hi_].astype(F32)
                o_ref[r0:r0 + sb, lo_:hi_] = (o * _silu(r)).astype(BF16)

    states = [jnp.zeros((pw, vw), F32) for _ in range(npair)]
    loc = local(0)
    for i in range(nsb):
        nxt = local(i + 1) if i + 1 < nsb else None
        recurrent(i, loc, states)
        loc = nxt


def _post_kernel(h1_ref, om_ref, og_ref, ox_ref, nm_ref, wgz_ref, gb_ref, wom_ref, wog_ref, wox_ref,
                 wout_ref, n2_ref, wg_ref, wu_ref, wd_ref, nf_ref, o_ref):
    h1 = h1_ref[...]
    u = _rms(h1, nm_ref[...]).astype(BF16)
    d = D_MODEL
    merged = None
    for n, (o_br, w_br) in enumerate(((om_ref, wom_ref), (og_ref, wog_ref), (ox_ref, wox_ref))):
        gate = _sigmoid(_dot(u, wgz_ref[:, n * d:(n + 1) * d]) + gb_ref[:, n * d:(n + 1) * d])
        term = gate * _dot(o_br[...], w_br[...])
        merged = term if merged is None else merged + term
    h2 = h1 + _dot(merged.astype(BF16), wout_ref[...])
    u2 = _rms(h2, n2_ref[...]).astype(BF16)
    h3 = h2 + 0.5 * _swiglu_half(u2, wg_ref, wu_ref, wd_ref)
    o_ref[...] = _rms(h3, nf_ref[...])


def _cparams(sem, vmem=None, flags=None):
    return pltpu.CompilerParams(dimension_semantics=sem, vmem_limit_bytes=vmem, flags=flags)


def kernel(x, mem, positions, ffn1_norm, ffn1_wg, ffn1_wu, ffn1_wd, mix_norm, mem_norm, w_in, gate_bias,
           mla_q_norm, mla_w_uq, mla_kv_norm, mla_w_ukv, mla_w_o, gla_w_a2, gla_b_a, gla_o_norm, gla_w_o,
           x_w_kv, x_w_o, w_out, ffn2_norm, ffn2_wg, ffn2_wu, ffn2_wd, final_norm):
    B, S, D = x.shape
    T = B * S
    M = mem.shape[1]
    F = D_FF
    H = MLA_HEADS
    bf = lambda a: a.astype(BF16)
    row = lambda a: a.reshape(1, -1)

    wi = w_in[0]
    c0 = 0
    cols = {}
    for name, w in (("cq", MLA_Q_RANK), ("ckv", MLA_KV_RANK), ("kr", MLA_ROPE), ("gq", GLA_HEADS * GLA_DK),
                    ("gk", GLA_HEADS * GLA_DK), ("gv", GLA_HEADS * GLA_DV), ("ga", GLA_GATE_RANK),
                    ("gr", GLA_HEADS * GLA_DV), ("xq", X_HEADS * X_DH), ("gz", N_BRANCH * D)):
        cols[name] = wi[:, c0:c0 + w]
        c0 += w
    zc = lambda n: jnp.zeros((D, n), F32)
    w_in_a = bf(jnp.concatenate(
        [cols["cq"], cols["ckv"], zc(MLA_NOPE), cols["kr"], zc(HEAD_PAD - MLA_NOPE - MLA_ROPE),
         cols["gq"], cols["gk"], cols["gv"], cols["gr"], cols["xq"],
         cols["ga"], zc(LANE - GLA_GATE_RANK)], axis=1))
    w_gz = bf(cols["gz"])
    w_uq = mla_w_uq[0].reshape(MLA_Q_RANK, H, MLA_NOPE + MLA_ROPE)
    w_uq = bf(jnp.pad(w_uq, ((0, 0), (0, 0), (0, HEAD_PAD - MLA_NOPE - MLA_ROPE))).reshape(MLA_Q_RANK, H * HEAD_PAD))
    w_ukv = mla_w_ukv[0].reshape(MLA_KV_RANK, H, MLA_NOPE + MLA_V)
    w_uk = bf(jnp.pad(w_ukv[:, :, :MLA_NOPE], ((0, 0), (0, 0), (0, HEAD_PAD - MLA_NOPE))).reshape(MLA_KV_RANK, H * HEAD_PAD))
    w_uv = bf(w_ukv[:, :, MLA_NOPE:].reshape(MLA_KV_RANK, H * MLA_V))
    w_a2 = bf(jnp.pad(gla_w_a2[0], ((0, LANE - GLA_GATE_RANK), (0, 0))))
    half = MLA_ROPE // 2
    inv_freq = ROPE_THETA ** (-jnp.arange(half, dtype=F32) / half)
    invf = jnp.concatenate([jnp.zeros((MLA_NOPE,), F32), inv_freq, inv_freq,
                            jnp.zeros((HEAD_PAD - MLA_NOPE - MLA_ROPE,), F32)]).reshape(1, HEAD_PAD)

    x2 = x.reshape(T, D)
    pos2 = jnp.broadcast_to(positions.reshape(T, 1), (T, HEAD_PAD))
    nt = T // TM
    tok = lambda w: pl.BlockSpec((TM, w), lambda i: (i, 0))

    gdk, gdv, xw = GLA_HEADS * GLA_DK, GLA_HEADS * GLA_DV, X_HEADS * X_DH
    pre_out_shapes = (
        jax.ShapeDtypeStruct((T, D), F32),
        jax.ShapeDtypeStruct((T, H * HEAD_PAD), BF16),
        jax.ShapeDtypeStruct((T, H * HEAD_PAD), BF16),
        jax.ShapeDtypeStruct((T, H * MLA_V), BF16),
        jax.ShapeDtypeStruct((T, gdk), F32),
        jax.ShapeDtypeStruct((T, gdk), F32),
        jax.ShapeDtypeStruct((T, gdv), BF16),
        jax.ShapeDtypeStruct((T, gdv), BF16),
        jax.ShapeDtypeStruct((T, xw), BF16),
        jax.ShapeDtypeStruct((T, gdk), F32),
    )
    h1, q, k, v, gq, gk, gv, gr, xq, la = pl.pallas_call(
        _pre_kernel,
        out_shape=pre_out_shapes,
        grid=(nt,),
        in_specs=[tok(D), tok(HEAD_PAD), _const_spec((1, HEAD_PAD)), _const_spec((1, D)), _const_spec((1, D)),
                  _const_spec((D, F)), _const_spec((D, F)), _const_spec((F, D)), _const_spec((D, A_END)),
                  _const_spec((1, MLA_Q_RANK)), _const_spec((MLA_Q_RANK, H * HEAD_PAD)),
                  _const_spec((1, MLA_KV_RANK)), _const_spec((MLA_KV_RANK, H * HEAD_PAD)),
                  _const_spec((MLA_KV_RANK, H * MLA_V)), _const_spec((LANE, gdk)), _const_spec((1, gdk))],
        out_specs=tuple(tok(s.shape[1]) for s in pre_out_shapes),
        compiler_params=_cparams(("parallel",), VMEM_LIMIT),
        name="pre",
    )(x2, pos2, invf, row(ffn1_norm[0]), row(mix_norm[0]), bf(ffn1_wg[0]), bf(ffn1_wu[0]), bf(ffn1_wd[0]),
      w_in_a, row(mla_q_norm[0]), w_uq, row(mla_kv_norm[0]), w_uk, w_uv, w_a2, row(gla_b_a[0]))

    o_mla = pl.pallas_call(
        _mla_kernel,
        out_shape=jax.ShapeDtypeStruct((T, H * MLA_V), BF16),
        grid=(B, H // 2),
        in_specs=[pl.BlockSpec((S, 2 * HEAD_PAD), lambda b, p: (b, p)),
                  pl.BlockSpec((S, 2 * HEAD_PAD), lambda b, p: (b, p)),
                  pl.BlockSpec((S, 2 * MLA_V), lambda b, p: (b, p))],
        out_specs=pl.BlockSpec((S, 2 * MLA_V), lambda b, p: (b, p)),
        scratch_shapes=[pltpu.VMEM((2 * MLA_VT_ROWS, S), BF16), pltpu.VMEM((MLA_LOOKAHEAD + 1, S, TQ), F32)],
        compiler_params=_cparams(("parallel", "parallel")),
        name="mla",
    )(q, k, v)

    nqx = S // TQX
    o_x = pl.pallas_call(
        _xattn_kernel,
        out_shape=jax.ShapeDtypeStruct((T, xw), BF16),
        grid=(B, nqx),
        in_specs=[pl.BlockSpec((TQX, xw), lambda b, i: (b * nqx + i, 0)),
                  pl.BlockSpec((M, D), lambda b, i: (b, 0)),
                  _const_spec((1, D)), _const_spec((D, 2 * xw))],
        out_specs=pl.BlockSpec((TQX, xw), lambda b, i: (b * nqx + i, 0)),
        scratch_shapes=[pltpu.VMEM((M, 2 * xw), BF16)],
        compiler_params=_cparams(("arbitrary", "arbitrary")),
        name="xattn",
    )(xq, mem.reshape(B * M, D), row(mem_norm[0]), bf(x_w_kv[0]))

    o_gla = pl.pallas_call(
        _gla_kernel,
        out_shape=jax.ShapeDtypeStruct((T, gdv), BF16),
        grid=(B,),
        in_specs=[pl.BlockSpec((S, gdk), lambda b: (b, 0)),
                  pl.BlockSpec((S, gdk), lambda b: (b, 0)),
                  pl.BlockSpec((S, gdk), lambda b: (b, 0)),
                  pl.BlockSpec((S, gdv), lambda b: (b, 0)),
                  pl.BlockSpec((S, gdv), lambda b: (b, 0)),
                  _const_spec((1, GLA_DV))],
        out_specs=pl.BlockSpec((S, gdv), lambda b: (b, 0)),
        compiler_params=_cparams(("parallel",), VMEM_LIMIT),
        name="gla",
    )(gq, gk, la, gv, gr, row(gla_o_norm[0]))

    out = pl.pallas_call(
        _post_kernel,
        out_shape=jax.ShapeDtypeStruct((T, D), F32),
        grid=(nt,),
        in_specs=[tok(D), tok(H * MLA_V), tok(gdv), tok(xw), _const_spec((1, D)),
                  _const_spec((D, N_BRANCH * D)), _const_spec((1, N_BRANCH * D)),
                  _const_spec((H * MLA_V, D)), _const_spec((gdv, D)), _const_spec((xw, D)),
                  _const_spec((D, D)), _const_spec((1, D)),
                  _const_spec((D, F)), _const_spec((D, F)), _const_spec((F, D)), _const_spec((1, D))],
        out_specs=tok(D),
        compiler_params=_cparams(("parallel",), VMEM_LIMIT),
        name="post",
    )(h1, o_mla, o_gla, o_x, row(mix_norm[0]), w_gz, gate_bias[0].reshape(1, N_BRANCH * D),
      bf(mla_w_o[0]), bf(gla_w_o[0]), bf(x_w_o[0]), bf(w_out[0]), row(ffn2_norm[0]),
      bf(ffn2_wg[0]), bf(ffn2_wu[0]), bf(ffn2_wd[0]), row(final_norm))
    return out.reshape(B, S, D)
```

```python
import functools
import math

import jax
import jax.numpy as jnp
from jax import lax
from jax.experimental import pallas as pl
from jax.experimental.pallas import tpu as pltpu

F32 = jnp.float32
BF16 = jnp.bfloat16

D_MODEL = 1024
N_MEM = 256
EPS = 1e-6
MLA_HEADS = 8
MLA_NOPE = 64
MLA_ROPE = 32
MLA_V = 64
MLA_Q_RANK = 384
MLA_KV_RANK = 256
ROPE_THETA = 10000.0
GLA_HEADS = 4
GLA_DK = 64
GLA_DV = 128
GLA_GATE_RANK = 16
GLA_TAU = 16.0
GLA_CHUNK = 64
X_HEADS = 4
X_DH = 128
D_FF = 2816
N_BRANCH = 3

LANE = 128
HEAD_PAD = 128
VMEM_LIMIT = 60 * 1024 * 1024
MLA_EXP2_SCALE = math.log2(math.e) / math.sqrt(MLA_NOPE + MLA_ROPE)
MLA_VT_ROWS = MLA_V + 16
MLA_LOOKAHEAD = 3

C_CQ, C_CKV, C_KR, C_GQ, C_GK, C_GV, C_GA, C_GR, C_XQ, C_GZ, C_END = (
    0, 384, 640, 672, 928, 1184, 1696, 1712, 2224, 2736, 5808)
W_STEPS = 32
A_CQ, A_CKV, A_KR, A_GQ, A_GK, A_GV, A_GR, A_XQ, A_GA, A_END = (
    0, 384, 640, 768, 1024, 1280, 1792, 2304, 2816, 2944)

TM = 512
TQ = 256
TQX = 512
GLA_SB = 256


def _dot(a, b):
    return jnp.dot(a, b, preferred_element_type=F32)


def _dot_nt(a, b):
    return lax.dot_general(a, b, (((1,), (1,)), ((), ())), preferred_element_type=F32)


def _dot_tn(a, b):
    return lax.dot_general(a, b, (((0,), (0,)), ((), ())), preferred_element_type=F32)


def _rms(x, g):
    return x * lax.rsqrt(jnp.mean(x * x, axis=-1, keepdims=True) + EPS) * g


def _sigmoid(x):
    return 1.0 / (1.0 + jnp.exp(-x))


def _silu(x):
    return x * _sigmoid(x)


def _const_spec(shape):
    return pl.BlockSpec(shape, lambda *_: (0,) * len(shape), pipeline_mode=pl.Buffered(1))


def _swiglu_half(xn_bf16, wg_ref, wu_ref, wd_ref):
    g = _dot(xn_bf16, wg_ref[...])
    up = _dot(xn_bf16, wu_ref[...])
    a = (_silu(g) * up).astype(BF16)
    return _dot(a, wd_ref[...])


def _convert_rows(dst_ref, src, step, rows):
    r0 = pl.multiple_of(step * rows, rows)
    dst_ref[pl.ds(r0, rows), :] = src.astype(BF16)


def _convert_ffn(i, wg32_ref, wu32_ref, wd32_ref, wg_ref, wu_ref, wd_ref):
    _convert_rows(wg_ref, wg32_ref[...], i, D_MODEL // W_STEPS)
    _convert_rows(wu_ref, wu32_ref[...], i, D_MODEL // W_STEPS)

    @pl.when(i % 2 == 0)
    def _():
        _convert_rows(wd_ref, wd32_ref[...], i // 2, D_FF // (W_STEPS // 2))


def _pre_kernel(x_ref, pos_ref, invf_ref, n1_ref, nm_ref, wg32_ref, wu32_ref, wd32_ref, win32_ref,
                qn_ref, wuq_ref, kvn_ref, wuk_ref, wuv_ref, wa2_ref, ba_ref,
                h1_ref, q_ref, k_ref, v_ref, gq_ref, gk_ref, gv_ref, gr_ref, xq_ref, la_ref,
                wg_ref, wu_ref, wd_ref, win_ref):
    i = pl.program_id(0)

    @pl.when(i < W_STEPS)
    def _():
        _convert_ffn(i, wg32_ref, wu32_ref, wd32_ref, wg_ref, wu_ref, wd_ref)
        z = win32_ref[...]
        zeros = lambda n: jnp.zeros((z.shape[0], n), F32)
        packed = jnp.concatenate(
            [z[:, :C_KR], zeros(MLA_NOPE), z[:, C_KR:C_GQ], zeros(HEAD_PAD - MLA_NOPE - MLA_ROPE),
             z[:, C_GQ:C_GA], z[:, C_GR:C_GZ], z[:, C_GA:C_GR], zeros(LANE - GLA_GATE_RANK)], axis=1)
        _convert_rows(win_ref, packed, i, D_MODEL // W_STEPS)

    @pl.when(i >= W_STEPS)
    def _():
        _pre_tokens(x_ref, pos_ref, invf_ref, n1_ref, nm_ref, wg_ref, wu_ref, wd_ref, win_ref,
                    qn_ref, wuq_ref, kvn_ref, wuk_ref, wuv_ref, wa2_ref, ba_ref,
                    h1_ref, q_ref, k_ref, v_ref, gq_ref, gk_ref, gv_ref, gr_ref, xq_ref, la_ref)


def _pre_tokens(x_ref, pos_ref, invf_ref, n1_ref, nm_ref, wg_ref, wu_ref, wd_ref, win_ref,
                qn_ref, wuq_ref, kvn_ref, wuk_ref, wuv_ref, wa2_ref, ba_ref,
                h1_ref, q_ref, k_ref, v_ref, gq_ref, gk_ref, gv_ref, gr_ref, xq_ref, la_ref):
    x = x_ref[...]
    u1 = _rms(x, n1_ref[...]).astype(BF16)
    h1 = x + 0.5 * _swiglu_half(u1, wg_ref, wu_ref, wd_ref)
    h1_ref[...] = h1
    u2 = _rms(h1, nm_ref[...]).astype(BF16)

    ang = pos_ref[...].astype(F32) * invf_ref[...]
    cos = jnp.cos(ang)
    sin = jnp.sin(ang)
    lane = lax.broadcasted_iota(jnp.int32, ang.shape, 1)
    sin_hi = jnp.where(lane >= 80, sin, 0.0)
    sin_lo = jnp.where(lane < 80, -sin, 0.0)

    def rope(t):
        return t * cos + pltpu.roll(t, 16, 1) * sin_hi + pltpu.roll(t, LANE - 16, 1) * sin_lo

    cqn = _rms(_dot(u2, win_ref[:, A_CQ:A_CKV]), qn_ref[...]).astype(BF16)
    for p in range(MLA_HEADS // 2):
        qp = _dot(cqn, wuq_ref[:, 2 * p * HEAD_PAD:(2 * p + 2) * HEAD_PAD])
        for j in range(2):
            h = 2 * p + j
            q_ref[:, h * HEAD_PAD:(h + 1) * HEAD_PAD] = (
                rope(qp[:, j * HEAD_PAD:(j + 1) * HEAD_PAD]) * MLA_EXP2_SCALE).astype(BF16)

    ckvn = _rms(_dot(u2, win_ref[:, A_CKV:A_KR]), kvn_ref[...]).astype(BF16)
    kr = rope(_dot(u2, win_ref[:, A_KR:A_GQ]))
    kr2 = jnp.concatenate([kr, kr], axis=1)
    for p in range(MLA_HEADS // 2):
        kn = _dot(ckvn, wuk_ref[:, 2 * p * HEAD_PAD:(2 * p + 2) * HEAD_PAD])
        k_ref[:, 2 * p * HEAD_PAD:(2 * p + 2) * HEAD_PAD] = (kn + kr2).astype(BF16)
    v_ref[...] = _dot(ckvn, wuv_ref[...]).astype(BF16)

    gq_ref[...] = _dot(u2, win_ref[:, A_GQ:A_GK])
    gk_ref[...] = _dot(u2, win_ref[:, A_GK:A_GV])
    gv_ref[...] = _dot(u2, win_ref[:, A_GV:A_GR]).astype(BF16)
    gr_ref[...] = _dot(u2, win_ref[:, A_GR:A_XQ]).astype(BF16)
    xq_ref[...] = _dot(u2, win_ref[:, A_XQ:A_GA]).astype(BF16)
    ga = _dot(u2, win_ref[:, A_GA:A_END]).astype(BF16)
    t = _dot(ga, wa2_ref[...]) + ba_ref[...]
    log_sig = jnp.minimum(t, 0.0) - jnp.log(1.0 + jnp.exp(-jnp.abs(t)))
    la_ref[...] = log_sig * (1.0 / GLA_TAU)


def _mla_kernel(q_ref, k_ref, v_ref, o_ref, vt_ref, s_ref):
    tq = TQ
    nq = q_ref.shape[0] // tq
    key = lax.broadcasted_iota(jnp.int32, (tq, tq), 0)
    qry = lax.broadcasted_iota(jnp.int32, (tq, tq), 1)
    causal = key <= qry
    vt = v_ref[...].astype(F32).T.astype(BF16)
    ones_row = jnp.where(lax.broadcasted_iota(jnp.int32, (MLA_VT_ROWS - MLA_V, vt.shape[1]), 0) == 0,
                         1.0, 0.0).astype(BF16)
    for h in range(2):
        vt_ref[h * MLA_VT_ROWS:h * MLA_VT_ROWS + MLA_V, :] = vt[h * MLA_V:(h + 1) * MLA_V, :]
        vt_ref[h * MLA_VT_ROWS + MLA_V:(h + 1) * MLA_VT_ROWS, :] = ones_row

    def scores(qi, h, slot):
        q = q_ref[qi * tq:(qi + 1) * tq, h * HEAD_PAD:(h + 1) * HEAD_PAD]
        m = None
        for c in range(qi + 1):
            s = _dot_nt(k_ref[c * tq:(c + 1) * tq, h * HEAD_PAD:(h + 1) * HEAD_PAD], q)
            if c == qi:
                s = jnp.where(causal, s, -1e30)
            s_ref[slot, c * tq:(c + 1) * tq, :] = s
            cm = jnp.max(s, axis=0, keepdims=True)
            m = cm if m is None else jnp.maximum(m, cm)
        return m

    def values(qi, h, slot, m):
        acc = None
        for c in range(qi + 1):
            p = jnp.exp2(s_ref[slot, c * tq:(c + 1) * tq, :] - m).astype(BF16)
            pv = _dot(vt_ref[h * MLA_VT_ROWS:(h + 1) * MLA_VT_ROWS, c * tq:(c + 1) * tq], p)
            acc = pv if acc is None else acc + pv
        return acc[:MLA_V, :] / acc[MLA_V:MLA_V + 1, :]

    groups = [(qi, h) for qi in range(nq) for h in range(2)]
    nslot = s_ref.shape[0]
    maxes = {}
    for i in range(min(MLA_LOOKAHEAD, len(groups))):
        maxes[i] = scores(*groups[i], i % nslot)
    heads = {}
    for i, (qi, h) in enumerate(groups):
        j = i + MLA_LOOKAHEAD
        if j < len(groups):
            maxes[j] = scores(*groups[j], j % nslot)
        heads[h] = values(qi, h, i % nslot, maxes.pop(i))
        if h == 1:
            o_ref[qi * tq:(qi + 1) * tq, :] = jnp.concatenate([heads[0], heads[1]], axis=0).T.astype(BF16)


def _xattn_kernel(q_ref, mem_ref, n_ref, w_ref, o_ref, kv_ref):
    @pl.when(pl.program_id(1) == 0)
    def _():
        mn = _rms(mem_ref[...], n_ref[...]).astype(BF16)
        kv_ref[...] = _dot(mn, w_ref[...]).astype(BF16)

    scale = 1.0 / math.sqrt(X_DH)
    hw = X_HEADS * X_DH
    for h in range(X_HEADS):
        q = q_ref[:, h * X_DH:(h + 1) * X_DH]
        k = kv_ref[:, h * X_DH:(h + 1) * X_DH]
        v = kv_ref[:, hw + h * X_DH:hw + (h + 1) * X_DH]
        s = _dot_nt(q, k) * scale
        m = jnp.max(s, axis=-1, keepdims=True)
        p = jnp.exp(s - m)
        l = jnp.sum(p, axis=-1, keepdims=True)
        o = _dot(p.astype(BF16), v) / l
        o_ref[:, h * X_DH:(h + 1) * X_DH] = o.astype(BF16)


def _gla_kernel(q_ref, k_ref, la_ref, v_ref, r_ref, on_ref, o_ref):
    sb, c = GLA_SB, GLA_CHUNK
    nsb = q_ref.shape[0] // sb
    nch = sb // c
    npair = GLA_HEADS // 2
    pw, vw = 2 * GLA_DK, 2 * GLA_DV
    row = lax.broadcasted_iota(jnp.int32, (sb, sb), 0)
    col = lax.broadcasted_iota(jnp.int32, (sb, sb), 1)
    tri = jnp.logical_and(row // c == col // c, col <= row)
    tri_bf = jnp.where(tri, 1.0, 0.0).astype(BF16)
    lane = lax.broadcasted_iota(jnp.int32, (1, pw), 1)
    head_lanes = (lane < GLA_DK, lane >= GLA_DK)
    srow = lax.broadcasted_iota(jnp.int32, (pw, vw), 0)
    scol = lax.broadcasted_iota(jnp.int32, (pw, vw), 1)
    same_head = (srow < GLA_DK) == (scol < GLA_DV)

    def local(i):
        r0 = i * sb
        la = la_ref[r0:r0 + sb, :]
        hi = la.astype(BF16)
        lo = (la - hi.astype(F32)).astype(BF16)
        bcum_all = _dot(tri_bf, hi) + _dot(tri_bf, lo)
        out = []
        for p in range(npair):
            bcum = bcum_all[:, p * pw:(p + 1) * pw]
            b_last = [bcum[j * c + c - 1:j * c + c, :] for j in range(nch)]
            b_last_rows = jnp.concatenate([jnp.broadcast_to(b, (c, pw)) for b in b_last], axis=0)
            q_t = q_ref[r0:r0 + sb, p * pw:(p + 1) * pw] * (GLA_DK ** -0.5) * jnp.exp(bcum)
            kk = k_ref[r0:r0 + sb, p * pw:(p + 1) * pw]
            k_t = (kk * jnp.exp(-bcum)).astype(BF16)
            k_d = (kk * jnp.exp(b_last_rows - bcum)).astype(BF16)
            v = v_ref[r0:r0 + sb, p * vw:(p + 1) * vw]
            o_intra = []
            for h in range(2):
                qm = jnp.where(head_lanes[h], q_t, 0.0).astype(BF16)
                att = jnp.where(tri, _dot_nt(qm, k_t), 0.0).astype(BF16)
                o_intra.append(_dot(att, v[:, h * GLA_DV:(h + 1) * GLA_DV]))
            d_st, decay = [], []
            for j in range(nch):
                d = _dot_tn(k_d[j * c:(j + 1) * c, :], v[j * c:(j + 1) * c, :])
                d_st.append(jnp.where(same_head, d, 0.0))
                dcol = jnp.broadcast_to(jnp.exp(b_last[j]), (pw, pw)).T
                decay.append(jnp.concatenate([dcol, dcol], axis=1))
            out.append((q_t.astype(BF16), o_intra, d_st, decay))
        return out

    def recurrent(i, loc, states):
        r0 = i * sb
        for p in range(npair):
            q_bf, o_intra, d_st, decay = loc[p]
            st = states[p]
            o_inter = []
            for j in range(nch):
                o_inter.append(_dot(q_bf[j * c:(j + 1) * c, :], st.astype(BF16)))
                st = decay[j] * st + d_st[j]
            states[p] = st
            o_inter = jnp.concatenate(o_inter, axis=0)
            for h in range(2):
                lo_, hi_ = p * vw + h * GLA_DV, p * vw + (h + 1) * GLA_DV
                o = _rms(o_intra[h] + o_inter[:, h * GLA_DV:(h + 1) * GLA_DV], on_ref[...])
                r = r_ref[r0:r0 + sb, lo_:hi_].astype(F32)
                o_ref[r0:r0 + sb, lo_:hi_] = (o * _silu(r)).astype(BF16)

    states = [jnp.zeros((pw, vw), F32) for _ in range(npair)]
    loc = local(0)
    for i in range(nsb):
        nxt = local(i + 1) if i + 1 < nsb else None
        recurrent(i, loc, states)
        loc = nxt


def _post_kernel(h1_ref, om_ref, og_ref, ox_ref, nm_ref, win32_ref, gb_ref, wom32_ref, wog32_ref, wox32_ref,
                 wout32_ref, n2_ref, wg32_ref, wu32_ref, wd32_ref, nf_ref, o_ref,
                 wgz_ref, wom_ref, wog_ref, wox_ref, wout_ref, wg_ref, wu_ref, wd_ref):
    i = pl.program_id(0)

    @pl.when(i < W_STEPS)
    def _():
        _convert_ffn(i, wg32_ref, wu32_ref, wd32_ref, wg_ref, wu_ref, wd_ref)
        _convert_rows(wgz_ref, win32_ref[...][:, C_GZ:C_END], i, D_MODEL // W_STEPS)
        for dst, src in ((wom_ref, wom32_ref), (wog_ref, wog32_ref), (wox_ref, wox32_ref)):
            _convert_rows(dst, src[...], i, dst.shape[0] // W_STEPS)
        _convert_rows(wout_ref, wout32_ref[...], i, D_MODEL // W_STEPS)

    @pl.when(i >= W_STEPS)
    def _():
        _post_tokens(h1_ref, om_ref, og_ref, ox_ref, nm_ref, wgz_ref, gb_ref, wom_ref, wog_ref, wox_ref,
                     wout_ref, n2_ref, wg_ref, wu_ref, wd_ref, nf_ref, o_ref)


def _post_tokens(h1_ref, om_ref, og_ref, ox_ref, nm_ref, wgz_ref, gb_ref, wom_ref, wog_ref, wox_ref,
                 wout_ref, n2_ref, wg_ref, wu_ref, wd_ref, nf_ref, o_ref):
    h1 = h1_ref[...]
    u = _rms(h1, nm_ref[...]).astype(BF16)
    d = D_MODEL
    merged = None
    for n, (o_br, w_br) in enumerate(((om_ref, wom_ref), (og_ref, wog_ref), (ox_ref, wox_ref))):
        gate = _sigmoid(_dot(u, wgz_ref[:, n * d:(n + 1) * d]) + gb_ref[:, n * d:(n + 1) * d])
        term = gate * _dot(o_br[...], w_br[...])
        merged = term if merged is None else merged + term
    h2 = h1 + _dot(merged.astype(BF16), wout_ref[...])
    u2 = _rms(h2, n2_ref[...]).astype(BF16)
    h3 = h2 + 0.5 * _swiglu_half(u2, wg_ref, wu_ref, wd_ref)
    o_ref[...] = _rms(h3, nf_ref[...])


def _cparams(sem, vmem=None, flags=None):
    return pltpu.CompilerParams(dimension_semantics=sem, vmem_limit_bytes=vmem, flags=flags)


def kernel(x, mem, positions, ffn1_norm, ffn1_wg, ffn1_wu, ffn1_wd, mix_norm, mem_norm, w_in, gate_bias,
           mla_q_norm, mla_w_uq, mla_kv_norm, mla_w_ukv, mla_w_o, gla_w_a2, gla_b_a, gla_o_norm, gla_w_o,
           x_w_kv, x_w_o, w_out, ffn2_norm, ffn2_wg, ffn2_wu, ffn2_wd, final_norm):
    B, S, D = x.shape
    T = B * S
    M = mem.shape[1]
    F = D_FF
    H = MLA_HEADS
    bf = lambda a: a.astype(BF16)
    row = lambda a: a.reshape(1, -1)

    w_uq = mla_w_uq[0].reshape(MLA_Q_RANK, H, MLA_NOPE + MLA_ROPE)
    w_uq = bf(jnp.pad(w_uq, ((0, 0), (0, 0), (0, HEAD_PAD - MLA_NOPE - MLA_ROPE))).reshape(MLA_Q_RANK, H * HEAD_PAD))
    w_ukv = mla_w_ukv[0].reshape(MLA_KV_RANK, H, MLA_NOPE + MLA_V)
    w_uk = bf(jnp.pad(w_ukv[:, :, :MLA_NOPE], ((0, 0), (0, 0), (0, HEAD_PAD - MLA_NOPE))).reshape(MLA_KV_RANK, H * HEAD_PAD))
    w_uv = bf(w_ukv[:, :, MLA_NOPE:].reshape(MLA_KV_RANK, H * MLA_V))
    w_a2 = bf(jnp.pad(gla_w_a2[0], ((0, LANE - GLA_GATE_RANK), (0, 0))))
    half = MLA_ROPE // 2
    inv_freq = ROPE_THETA ** (-jnp.arange(half, dtype=F32) / half)
    invf = jnp.concatenate([jnp.zeros((MLA_NOPE,), F32), inv_freq, inv_freq,
                            jnp.zeros((HEAD_PAD - MLA_NOPE - MLA_ROPE,), F32)]).reshape(1, HEAD_PAD)

    x2 = x.reshape(T, D)
    pos2 = jnp.broadcast_to(positions.reshape(T, 1), (T, HEAD_PAD))
    nt = T // TM
    tok = lambda w: pl.BlockSpec((TM, w), lambda i: (jnp.maximum(i - W_STEPS, 0), 0))
    wrows = lambda shape, every=1: pl.BlockSpec(
        (shape[0] * every // W_STEPS, shape[1]),
        lambda i: (jnp.minimum(i // every, W_STEPS // every - 1), 0))
    ffn_specs = [wrows((D, F)), wrows((D, F)), wrows((F, D), 2)]
    ffn_scratch = [pltpu.VMEM((D, F), BF16), pltpu.VMEM((D, F), BF16), pltpu.VMEM((F, D), BF16)]

    gdk, gdv, xw = GLA_HEADS * GLA_DK, GLA_HEADS * GLA_DV, X_HEADS * X_DH
    pre_out_shapes = (
        jax.ShapeDtypeStruct((T, D), F32),
        jax.ShapeDtypeStruct((T, H * HEAD_PAD), BF16),
        jax.ShapeDtypeStruct((T, H * HEAD_PAD), BF16),
        jax.ShapeDtypeStruct((T, H * MLA_V), BF16),
        jax.ShapeDtypeStruct((T, gdk), F32),
        jax.ShapeDtypeStruct((T, gdk), F32),
        jax.ShapeDtypeStruct((T, gdv), BF16),
        jax.ShapeDtypeStruct((T, gdv), BF16),
        jax.ShapeDtypeStruct((T, xw), BF16),
        jax.ShapeDtypeStruct((T, gdk), F32),
    )
    h1, q, k, v, gq, gk, gv, gr, xq, la = pl.pallas_call(
        _pre_kernel,
        out_shape=pre_out_shapes,
        grid=(W_STEPS + nt,),
        in_specs=[tok(D), tok(HEAD_PAD), _const_spec((1, HEAD_PAD)), _const_spec((1, D)), _const_spec((1, D)),
                  *ffn_specs, wrows((D, C_END)),
                  _const_spec((1, MLA_Q_RANK)), _const_spec((MLA_Q_RANK, H * HEAD_PAD)),
                  _const_spec((1, MLA_KV_RANK)), _const_spec((MLA_KV_RANK, H * HEAD_PAD)),
                  _const_spec((MLA_KV_RANK, H * MLA_V)), _const_spec((LANE, gdk)), _const_spec((1, gdk))],
        out_specs=tuple(tok(s.shape[1]) for s in pre_out_shapes),
        scratch_shapes=[*ffn_scratch, pltpu.VMEM((D, A_END), BF16)],
        compiler_params=_cparams(("arbitrary",), VMEM_LIMIT),
        name="pre",
    )(x2, pos2, invf, row(ffn1_norm[0]), row(mix_norm[0]), ffn1_wg[0], ffn1_wu[0], ffn1_wd[0],
      w_in[0], row(mla_q_norm[0]), w_uq, row(mla_kv_norm[0]), w_uk, w_uv, w_a2, row(gla_b_a[0]))

    o_mla = pl.pallas_call(
        _mla_kernel,
        out_shape=jax.ShapeDtypeStruct((T, H * MLA_V), BF16),
        grid=(B, H // 2),
        in_specs=[pl.BlockSpec((S, 2 * HEAD_PAD), lambda b, p: (b, p)),
                  pl.BlockSpec((S, 2 * HEAD_PAD), lambda b, p: (b, p)),
                  pl.BlockSpec((S, 2 * MLA_V), lambda b, p: (b, p))],
        out_specs=pl.BlockSpec((S, 2 * MLA_V), lambda b, p: (b, p)),
        scratch_shapes=[pltpu.VMEM((2 * MLA_VT_ROWS, S), BF16), pltpu.VMEM((MLA_LOOKAHEAD + 1, S, TQ), F32)],
        compiler_params=_cparams(("parallel", "parallel")),
        name="mla",
    )(q, k, v)

    nqx = S // TQX
    o_x = pl.pallas_call(
        _xattn_kernel,
        out_shape=jax.ShapeDtypeStruct((T, xw), BF16),
        grid=(B, nqx),
        in_specs=[pl.BlockSpec((TQX, xw), lambda b, i: (b * nqx + i, 0)),
                  pl.BlockSpec((M, D), lambda b, i: (b, 0)),
                  _const_spec((1, D)), _const_spec((D, 2 * xw))],
        out_specs=pl.BlockSpec((TQX, xw), lambda b, i: (b * nqx + i, 0)),
        scratch_shapes=[pltpu.VMEM((M, 2 * xw), BF16)],
        compiler_params=_cparams(("arbitrary", "arbitrary")),
        name="xattn",
    )(xq, mem.reshape(B * M, D), row(mem_norm[0]), bf(x_w_kv[0]))

    o_gla = pl.pallas_call(
        _gla_kernel,
        out_shape=jax.ShapeDtypeStruct((T, gdv), BF16),
        grid=(B,),
        in_specs=[pl.BlockSpec((S, gdk), lambda b: (b, 0)),
                  pl.BlockSpec((S, gdk), lambda b: (b, 0)),
                  pl.BlockSpec((S, gdk), lambda b: (b, 0)),
                  pl.BlockSpec((S, gdv), lambda b: (b, 0)),
                  pl.BlockSpec((S, gdv), lambda b: (b, 0)),
                  _const_spec((1, GLA_DV))],
        out_specs=pl.BlockSpec((S, gdv), lambda b: (b, 0)),
        compiler_params=_cparams(("parallel",), VMEM_LIMIT),
        name="gla",
    )(gq, gk, la, gv, gr, row(gla_o_norm[0]))

    out = pl.pallas_call(
        _post_kernel,
        out_shape=jax.ShapeDtypeStruct((T, D), F32),
        grid=(W_STEPS + nt,),
        in_specs=[tok(D), tok(H * MLA_V), tok(gdv), tok(xw), _const_spec((1, D)),
                  wrows((D, C_END)), _const_spec((1, N_BRANCH * D)),
                  wrows((H * MLA_V, D)), wrows((gdv, D)), wrows((xw, D)),
                  wrows((D, D)), _const_spec((1, D)),
                  *ffn_specs, _const_spec((1, D))],
        out_specs=tok(D),
        scratch_shapes=[pltpu.VMEM((D, N_BRANCH * D), BF16), pltpu.VMEM((H * MLA_V, D), BF16),
                        pltpu.VMEM((gdv, D), BF16), pltpu.VMEM((xw, D), BF16), pltpu.VMEM((D, D), BF16),
                        *ffn_scratch],
        compiler_params=_cparams(("arbitrary",), VMEM_LIMIT),
        name="post",
    )(h1, o_mla, o_gla, o_x, row(mix_norm[0]), w_in[0], gate_bias[0].reshape(1, N_BRANCH * D),
      mla_w_o[0], gla_w_o[0], x_w_o[0], w_out[0], row(ffn2_norm[0]),
      ffn2_wg[0], ffn2_wu[0], ffn2_wd[0], row(final_norm))
    return out.reshape(B, S, D)
```

```python
import functools
import math

import jax
import jax.numpy as jnp
from jax import lax
from jax.experimental import pallas as pl
from jax.experimental.pallas import tpu as pltpu

F32 = jnp.float32
BF16 = jnp.bfloat16

D_MODEL = 1024
N_MEM = 256
EPS = 1e-6
MLA_HEADS = 8
MLA_NOPE = 64
MLA_ROPE = 32
MLA_V = 64
MLA_Q_RANK = 384
MLA_KV_RANK = 256
ROPE_THETA = 10000.0
GLA_HEADS = 4
GLA_DK = 64
GLA_DV = 128
GLA_GATE_RANK = 16
GLA_TAU = 16.0
GLA_CHUNK = 64
X_HEADS = 4
X_DH = 128
D_FF = 2816
N_BRANCH = 3

LANE = 128
HEAD_PAD = 128
VMEM_LIMIT = 60 * 1024 * 1024
MLA_EXP2_SCALE = math.log2(math.e) / math.sqrt(MLA_NOPE + MLA_ROPE)
MLA_VT_ROWS = MLA_V + 16
MLA_LOOKAHEAD = 3

C_CQ, C_CKV, C_KR, C_GQ, C_GK, C_GV, C_GA, C_GR, C_XQ, C_GZ, C_END = (
    0, 384, 640, 672, 928, 1184, 1696, 1712, 2224, 2736, 5808)
W_STEPS = 16
A_CQ, A_CKV, A_KR, A_GQ, A_GK, A_GV, A_GR, A_XQ, A_GA, A_END = (
    0, 384, 640, 768, 1024, 1280, 1792, 2304, 2816, 2944)

TM = 512
TQ = 256
TQX = 512
GLA_SB = 256


def _dot(a, b):
    return jnp.dot(a, b, preferred_element_type=F32)


def _dot_nt(a, b):
    return lax.dot_general(a, b, (((1,), (1,)), ((), ())), preferred_element_type=F32)


def _dot_tn(a, b):
    return lax.dot_general(a, b, (((0,), (0,)), ((), ())), preferred_element_type=F32)


def _rms(x, g):
    return x * lax.rsqrt(jnp.mean(x * x, axis=-1, keepdims=True) + EPS) * g


def _sigmoid(x):
    return 1.0 / (1.0 + jnp.exp(-x))


def _silu(x):
    return x * _sigmoid(x)


def _const_spec(shape):
    return pl.BlockSpec(shape, lambda *_: (0,) * len(shape), pipeline_mode=pl.Buffered(1))


def _swiglu_half(xn_bf16, wg_ref, wu_ref, wd_ref):
    g = _dot(xn_bf16, wg_ref[...])
    up = _dot(xn_bf16, wu_ref[...])
    a = (_silu(g) * up).astype(BF16)
    return _dot(a, wd_ref[...])


def _convert_rows(dst_ref, src, step, rows):
    r0 = pl.multiple_of(step * rows, rows)
    dst_ref[pl.ds(r0, rows), :] = src.astype(BF16)


def _convert_ffn(i, wg32_ref, wu32_ref, wd32_ref, wg_ref, wu_ref, wd_ref):
    _convert_rows(wg_ref, wg32_ref[...], i, D_MODEL // W_STEPS)
    _convert_rows(wu_ref, wu32_ref[...], i, D_MODEL // W_STEPS)

    @pl.when(i % 2 == 0)
    def _():
        _convert_rows(wd_ref, wd32_ref[...], i // 2, D_FF // (W_STEPS // 2))


def _pre_kernel(x_ref, pos_ref, invf_ref, n1_ref, nm_ref, wg32_ref, wu32_ref, wd32_ref, win_ref,
                qn_ref, wuq_ref, kvn_ref, wuk_ref, wuv_ref, wa2_ref, ba_ref,
                h1_ref, q_ref, k_ref, v_ref, gq_ref, gk_ref, gv_ref, gr_ref, xq_ref, la_ref,
                wg_ref, wu_ref, wd_ref):
    i = pl.program_id(0)

    @pl.when(i < W_STEPS)
    def _():
        _convert_ffn(i, wg32_ref, wu32_ref, wd32_ref, wg_ref, wu_ref, wd_ref)

    @pl.when(i >= W_STEPS)
    def _():
        _pre_tokens(x_ref, pos_ref, invf_ref, n1_ref, nm_ref, wg_ref, wu_ref, wd_ref, win_ref,
                    qn_ref, wuq_ref, kvn_ref, wuk_ref, wuv_ref, wa2_ref, ba_ref,
                    h1_ref, q_ref, k_ref, v_ref, gq_ref, gk_ref, gv_ref, gr_ref, xq_ref, la_ref)


def _pre_tokens(x_ref, pos_ref, invf_ref, n1_ref, nm_ref, wg_ref, wu_ref, wd_ref, win_ref,
                qn_ref, wuq_ref, kvn_ref, wuk_ref, wuv_ref, wa2_ref, ba_ref,
                h1_ref, q_ref, k_ref, v_ref, gq_ref, gk_ref, gv_ref, gr_ref, xq_ref, la_ref):
    x = x_ref[...]
    u1 = _rms(x, n1_ref[...]).astype(BF16)
    h1 = x + 0.5 * _swiglu_half(u1, wg_ref, wu_ref, wd_ref)
    h1_ref[...] = h1
    u2 = _rms(h1, nm_ref[...]).astype(BF16)

    ang = pos_ref[...].astype(F32) * invf_ref[...]
    cos = jnp.cos(ang)
    sin = jnp.sin(ang)
    lane = lax.broadcasted_iota(jnp.int32, ang.shape, 1)
    sin_hi = jnp.where(lane >= 80, sin, 0.0)
    sin_lo = jnp.where(lane < 80, -sin, 0.0)

    def rope(t):
        return t * cos + pltpu.roll(t, 16, 1) * sin_hi + pltpu.roll(t, LANE - 16, 1) * sin_lo

    cqn = _rms(_dot_nt(u2, win_ref[A_CQ:A_CKV, :]), qn_ref[...]).astype(BF16)
    for p in range(MLA_HEADS // 2):
        qp = _dot(cqn, wuq_ref[:, 2 * p * HEAD_PAD:(2 * p + 2) * HEAD_PAD])
        for j in range(2):
            h = 2 * p + j
            q_ref[:, h * HEAD_PAD:(h + 1) * HEAD_PAD] = (
                rope(qp[:, j * HEAD_PAD:(j + 1) * HEAD_PAD]) * MLA_EXP2_SCALE).astype(BF16)

    ckvn = _rms(_dot_nt(u2, win_ref[A_CKV:A_KR, :]), kvn_ref[...]).astype(BF16)
    kr = rope(_dot_nt(u2, win_ref[A_KR:A_GQ, :]))
    kr2 = jnp.concatenate([kr, kr], axis=1)
    for p in range(MLA_HEADS // 2):
        kn = _dot(ckvn, wuk_ref[:, 2 * p * HEAD_PAD:(2 * p + 2) * HEAD_PAD])
        k_ref[:, 2 * p * HEAD_PAD:(2 * p + 2) * HEAD_PAD] = (kn + kr2).astype(BF16)
    v_ref[...] = _dot(ckvn, wuv_ref[...]).astype(BF16)

    gq_ref[...] = _dot_nt(u2, win_ref[A_GQ:A_GK, :])
    gk_ref[...] = _dot_nt(u2, win_ref[A_GK:A_GV, :])
    gv_ref[...] = _dot_nt(u2, win_ref[A_GV:A_GR, :]).astype(BF16)
    gr_ref[...] = _dot_nt(u2, win_ref[A_GR:A_XQ, :]).astype(BF16)
    xq_ref[...] = _dot_nt(u2, win_ref[A_XQ:A_GA, :]).astype(BF16)
    ga = _dot_nt(u2, win_ref[A_GA:A_END, :]).astype(BF16)
    t = _dot(ga, wa2_ref[...]) + ba_ref[...]
    log_sig = jnp.minimum(t, 0.0) - jnp.log(1.0 + jnp.exp(-jnp.abs(t)))
    la_ref[...] = log_sig * (1.0 / GLA_TAU)


def _mla_kernel(q_ref, k_ref, v_ref, o_ref, vt_ref, s_ref):
    tq = TQ
    nq = q_ref.shape[0] // tq
    key = lax.broadcasted_iota(jnp.int32, (tq, tq), 0)
    qry = lax.broadcasted_iota(jnp.int32, (tq, tq), 1)
    causal = key <= qry
    vt = v_ref[...].astype(F32).T.astype(BF16)
    ones_row = jnp.where(lax.broadcasted_iota(jnp.int32, (MLA_VT_ROWS - MLA_V, vt.shape[1]), 0) == 0,
                         1.0, 0.0).astype(BF16)
    for h in range(2):
        vt_ref[h * MLA_VT_ROWS:h * MLA_VT_ROWS + MLA_V, :] = vt[h * MLA_V:(h + 1) * MLA_V, :]
        vt_ref[h * MLA_VT_ROWS + MLA_V:(h + 1) * MLA_VT_ROWS, :] = ones_row

    def scores(qi, h, slot):
        q = q_ref[qi * tq:(qi + 1) * tq, h * HEAD_PAD:(h + 1) * HEAD_PAD]
        m = None
        for c in range(qi + 1):
            s = _dot_nt(k_ref[c * tq:(c + 1) * tq, h * HEAD_PAD:(h + 1) * HEAD_PAD], q)
            if c == qi:
                s = jnp.where(causal, s, -1e30)
            s_ref[slot, c * tq:(c + 1) * tq, :] = s
            cm = jnp.max(s, axis=0, keepdims=True)
            m = cm if m is None else jnp.maximum(m, cm)
        return m

    def values(qi, h, slot, m):
        acc = None
        for c in range(qi + 1):
            p = jnp.exp2(s_ref[slot, c * tq:(c + 1) * tq, :] - m).astype(BF16)
            pv = _dot(vt_ref[h * MLA_VT_ROWS:(h + 1) * MLA_VT_ROWS, c * tq:(c + 1) * tq], p)
            acc = pv if acc is None else acc + pv
        return acc[:MLA_V, :] / acc[MLA_V:MLA_V + 1, :]

    groups = [(qi, h) for qi in range(nq) for h in range(2)]
    nslot = s_ref.shape[0]
    maxes = {}
    for i in range(min(MLA_LOOKAHEAD, len(groups))):
        maxes[i] = scores(*groups[i], i % nslot)
    heads = {}
    for i, (qi, h) in enumerate(groups):
        j = i + MLA_LOOKAHEAD
        if j < len(groups):
            maxes[j] = scores(*groups[j], j % nslot)
        heads[h] = values(qi, h, i % nslot, maxes.pop(i))
        if h == 1:
            o_ref[qi * tq:(qi + 1) * tq, :] = jnp.concatenate([heads[0], heads[1]], axis=0).T.astype(BF16)


def _xattn_kernel(q_ref, mem_ref, n_ref, w_ref, o_ref, kv_ref):
    @pl.when(pl.program_id(1) == 0)
    def _():
        mn = _rms(mem_ref[...], n_ref[...]).astype(BF16)
        kv_ref[...] = _dot(mn, w_ref[...]).astype(BF16)

    scale = 1.0 / math.sqrt(X_DH)
    hw = X_HEADS * X_DH
    for h in range(X_HEADS):
        q = q_ref[:, h * X_DH:(h + 1) * X_DH]
        k = kv_ref[:, h * X_DH:(h + 1) * X_DH]
        v = kv_ref[:, hw + h * X_DH:hw + (h + 1) * X_DH]
        s = _dot_nt(q, k) * scale
        m = jnp.max(s, axis=-1, keepdims=True)
        p = jnp.exp(s - m)
        l = jnp.sum(p, axis=-1, keepdims=True)
        o = _dot(p.astype(BF16), v) / l
        o_ref[:, h * X_DH:(h + 1) * X_DH] = o.astype(BF16)


def _gla_kernel(q_ref, k_ref, la_ref, v_ref, r_ref, on_ref, o_ref):
    sb, c = GLA_SB, GLA_CHUNK
    nsb = q_ref.shape[0] // sb
    nch = sb // c
    npair = GLA_HEADS // 2
    pw, vw = 2 * GLA_DK, 2 * GLA_DV
    row = lax.broadcasted_iota(jnp.int32, (sb, sb), 0)
    col = lax.broadcasted_iota(jnp.int32, (sb, sb), 1)
    tri = jnp.logical_and(row // c == col // c, col <= row)
    tri_bf = jnp.where(tri, 1.0, 0.0).astype(BF16)
    lane = lax.broadcasted_iota(jnp.int32, (1, pw), 1)
    head_lanes = (lane < GLA_DK, lane >= GLA_DK)
    srow = lax.broadcasted_iota(jnp.int32, (pw, vw), 0)
    scol = lax.broadcasted_iota(jnp.int32, (pw, vw), 1)
    same_head = (srow < GLA_DK) == (scol < GLA_DV)

    def local(i):
        r0 = i * sb
        la = la_ref[r0:r0 + sb, :]
        hi = la.astype(BF16)
        lo = (la - hi.astype(F32)).astype(BF16)
        bcum_all = _dot(tri_bf, hi) + _dot(tri_bf, lo)
        out = []
        for p in range(npair):
            bcum = bcum_all[:, p * pw:(p + 1) * pw]
            b_last = [bcum[j * c + c - 1:j * c + c, :] for j in range(nch)]
            b_last_rows = jnp.concatenate([jnp.broadcast_to(b, (c, pw)) for b in b_last], axis=0)
            q_t = q_ref[r0:r0 + sb, p * pw:(p + 1) * pw] * (GLA_DK ** -0.5) * jnp.exp(bcum)
            kk = k_ref[r0:r0 + sb, p * pw:(p + 1) * pw]
            k_t = (kk * jnp.exp(-bcum)).astype(BF16)
            k_d = (kk * jnp.exp(b_last_rows - bcum)).astype(BF16)
            v = v_ref[r0:r0 + sb, p * vw:(p + 1) * vw]
            o_intra = []
            for h in range(2):
                qm = jnp.where(head_lanes[h], q_t, 0.0).astype(BF16)
                att = jnp.where(tri, _dot_nt(qm, k_t), 0.0).astype(BF16)
                o_intra.append(_dot(att, v[:, h * GLA_DV:(h + 1) * GLA_DV]))
            d_st, decay = [], []
            for j in range(nch):
                d = _dot_tn(k_d[j * c:(j + 1) * c, :], v[j * c:(j + 1) * c, :])
                d_st.append(jnp.where(same_head, d, 0.0))
                dcol = jnp.broadcast_to(jnp.exp(b_last[j]), (pw, pw)).T
                decay.append(jnp.concatenate([dcol, dcol], axis=1))
            out.append((q_t.astype(BF16), o_intra, d_st, decay))
        return out

    def recurrent(i, loc, states):
        r0 = i * sb
        for p in range(npair):
            q_bf, o_intra, d_st, decay = loc[p]
            st = states[p]
            o_inter = []
            for j in range(nch):
                o_inter.append(_dot(q_bf[j * c:(j + 1) * c, :], st.astype(BF16)))
                st = decay[j] * st + d_st[j]
            states[p] = st
            o_inter = jnp.concatenate(o_inter, axis=0)
            for h in range(2):
                lo_, hi_ = p * vw + h * GLA_DV, p * vw + (h + 1) * GLA_DV
                o = _rms(o_intra[h] + o_inter[:, h * GLA_DV:(h + 1) * GLA_DV], on_ref[...])
                r = r_ref[r0:r0 + sb, lo_:hi_].astype(F32)
                o_ref[r0:r0 + sb, lo_:hi_] = (o * _silu(r)).astype(BF16)

    states = [jnp.zeros((pw, vw), F32) for _ in range(npair)]
    loc = local(0)
    for i in range(nsb):
        nxt = local(i + 1) if i + 1 < nsb else None
        recurrent(i, loc, states)
        loc = nxt


def _post_kernel(h1_ref, om_ref, og_ref, ox_ref, nm_ref, wgz_ref, gb_ref, wom32_ref, wog32_ref, wox32_ref,
                 wout32_ref, n2_ref, wg32_ref, wu32_ref, wd32_ref, nf_ref, o_ref,
                 wom_ref, wog_ref, wox_ref, wout_ref, wg_ref, wu_ref, wd_ref):
    i = pl.program_id(0)

    @pl.when(i < W_STEPS)
    def _():
        _convert_ffn(i, wg32_ref, wu32_ref, wd32_ref, wg_ref, wu_ref, wd_ref)
        for dst, src in ((wom_ref, wom32_ref), (wog_ref, wog32_ref), (wox_ref, wox32_ref)):
            _convert_rows(dst, src[...], i, dst.shape[0] // W_STEPS)
        _convert_rows(wout_ref, wout32_ref[...], i, D_MODEL // W_STEPS)

    @pl.when(i >= W_STEPS)
    def _():
        _post_tokens(h1_ref, om_ref, og_ref, ox_ref, nm_ref, wgz_ref, gb_ref, wom_ref, wog_ref, wox_ref,
                     wout_ref, n2_ref, wg_ref, wu_ref, wd_ref, nf_ref, o_ref)


def _post_tokens(h1_ref, om_ref, og_ref, ox_ref, nm_ref, wgz_ref, gb_ref, wom_ref, wog_ref, wox_ref,
                 wout_ref, n2_ref, wg_ref, wu_ref, wd_ref, nf_ref, o_ref):
    h1 = h1_ref[...]
    u = _rms(h1, nm_ref[...]).astype(BF16)
    d = D_MODEL
    merged = None
    for n, (o_br, w_br) in enumerate(((om_ref, wom_ref), (og_ref, wog_ref), (ox_ref, wox_ref))):
        gate = _sigmoid(_dot_nt(u, wgz_ref[n * d:(n + 1) * d, :]) + gb_ref[:, n * d:(n + 1) * d])
        term = gate * _dot(o_br[...], w_br[...])
        merged = term if merged is None else merged + term
    h2 = h1 + _dot(merged.astype(BF16), wout_ref[...])
    u2 = _rms(h2, n2_ref[...]).astype(BF16)
    h3 = h2 + 0.5 * _swiglu_half(u2, wg_ref, wu_ref, wd_ref)
    o_ref[...] = _rms(h3, nf_ref[...])


def _cparams(sem, vmem=None, flags=None):
    return pltpu.CompilerParams(dimension_semantics=sem, vmem_limit_bytes=vmem, flags=flags)


def kernel(x, mem, positions, ffn1_norm, ffn1_wg, ffn1_wu, ffn1_wd, mix_norm, mem_norm, w_in, gate_bias,
           mla_q_norm, mla_w_uq, mla_kv_norm, mla_w_ukv, mla_w_o, gla_w_a2, gla_b_a, gla_o_norm, gla_w_o,
           x_w_kv, x_w_o, w_out, ffn2_norm, ffn2_wg, ffn2_wu, ffn2_wd, final_norm):
    B, S, D = x.shape
    T = B * S
    M = mem.shape[1]
    F = D_FF
    H = MLA_HEADS
    bf = lambda a: a.astype(BF16)
    row = lambda a: a.reshape(1, -1)

    wt = jnp.swapaxes(w_in[0], 0, 1)
    zr = lambda n: jnp.zeros((n, D), F32)
    w_in_at = bf(jnp.concatenate(
        [wt[:C_KR], zr(MLA_NOPE), wt[C_KR:C_GQ], zr(HEAD_PAD - MLA_NOPE - MLA_ROPE),
         wt[C_GQ:C_GA], wt[C_GR:C_GZ], wt[C_GA:C_GR], zr(LANE - GLA_GATE_RANK)], axis=0))
    w_gzt = bf(wt[C_GZ:])
    w_uq = mla_w_uq[0].reshape(MLA_Q_RANK, H, MLA_NOPE + MLA_ROPE)
    w_uq = bf(jnp.pad(w_uq, ((0, 0), (0, 0), (0, HEAD_PAD - MLA_NOPE - MLA_ROPE))).reshape(MLA_Q_RANK, H * HEAD_PAD))
    w_ukv = mla_w_ukv[0].reshape(MLA_KV_RANK, H, MLA_NOPE + MLA_V)
    w_uk = bf(jnp.pad(w_ukv[:, :, :MLA_NOPE], ((0, 0), (0, 0), (0, HEAD_PAD - MLA_NOPE))).reshape(MLA_KV_RANK, H * HEAD_PAD))
    w_uv = bf(w_ukv[:, :, MLA_NOPE:].reshape(MLA_KV_RANK, H * MLA_V))
    w_a2 = bf(jnp.pad(gla_w_a2[0], ((0, LANE - GLA_GATE_RANK), (0, 0))))
    half = MLA_ROPE // 2
    inv_freq = ROPE_THETA ** (-jnp.arange(half, dtype=F32) / half)
    invf = jnp.concatenate([jnp.zeros((MLA_NOPE,), F32), inv_freq, inv_freq,
                            jnp.zeros((HEAD_PAD - MLA_NOPE - MLA_ROPE,), F32)]).reshape(1, HEAD_PAD)

    x2 = x.reshape(T, D)
    pos2 = jnp.broadcast_to(positions.reshape(T, 1), (T, HEAD_PAD))
    nt = T // TM
    tok = lambda w: pl.BlockSpec((TM, w), lambda i: (jnp.maximum(i - W_STEPS, 0), 0))
    wrows = lambda shape, every=1: pl.BlockSpec(
        (shape[0] * every // W_STEPS, shape[1]),
        lambda i: (jnp.minimum(i // every, W_STEPS // every - 1), 0))
    ffn_specs = [wrows((D, F)), wrows((D, F)), wrows((F, D), 2)]
    ffn_scratch = [pltpu.VMEM((D, F), BF16), pltpu.VMEM((D, F), BF16), pltpu.VMEM((F, D), BF16)]

    gdk, gdv, xw = GLA_HEADS * GLA_DK, GLA_HEADS * GLA_DV, X_HEADS * X_DH
    pre_out_shapes = (
        jax.ShapeDtypeStruct((T, D), F32),
        jax.ShapeDtypeStruct((T, H * HEAD_PAD), BF16),
        jax.ShapeDtypeStruct((T, H * HEAD_PAD), BF16),
        jax.ShapeDtypeStruct((T, H * MLA_V), BF16),
        jax.ShapeDtypeStruct((T, gdk), F32),
        jax.ShapeDtypeStruct((T, gdk), F32),
        jax.ShapeDtypeStruct((T, gdv), BF16),
        jax.ShapeDtypeStruct((T, gdv), BF16),
        jax.ShapeDtypeStruct((T, xw), BF16),
        jax.ShapeDtypeStruct((T, gdk), F32),
    )
    h1, q, k, v, gq, gk, gv, gr, xq, la = pl.pallas_call(
        _pre_kernel,
        out_shape=pre_out_shapes,
        grid=(W_STEPS + nt,),
        in_specs=[tok(D), tok(HEAD_PAD), _const_spec((1, HEAD_PAD)), _const_spec((1, D)), _const_spec((1, D)),
                  *ffn_specs, _const_spec((A_END, D)),
                  _const_spec((1, MLA_Q_RANK)), _const_spec((MLA_Q_RANK, H * HEAD_PAD)),
                  _const_spec((1, MLA_KV_RANK)), _const_spec((MLA_KV_RANK, H * HEAD_PAD)),
                  _const_spec((MLA_KV_RANK, H * MLA_V)), _const_spec((LANE, gdk)), _const_spec((1, gdk))],
        out_specs=tuple(tok(s.shape[1]) for s in pre_out_shapes),
        scratch_shapes=ffn_scratch,
        compiler_params=_cparams(("arbitrary",), VMEM_LIMIT),
        name="pre",
    )(x2, pos2, invf, row(ffn1_norm[0]), row(mix_norm[0]), ffn1_wg[0], ffn1_wu[0], ffn1_wd[0],
      w_in_at, row(mla_q_norm[0]), w_uq, row(mla_kv_norm[0]), w_uk, w_uv, w_a2, row(gla_b_a[0]))

    o_mla = pl.pallas_call(
        _mla_kernel,
        out_shape=jax.ShapeDtypeStruct((T, H * MLA_V), BF16),
        grid=(B, H // 2),
        in_specs=[pl.BlockSpec((S, 2 * HEAD_PAD), lambda b, p: (b, p)),
                  pl.BlockSpec((S, 2 * HEAD_PAD), lambda b, p: (b, p)),
                  pl.BlockSpec((S, 2 * MLA_V), lambda b, p: (b, p))],
        out_specs=pl.BlockSpec((S, 2 * MLA_V), lambda b, p: (b, p)),
        scratch_shapes=[pltpu.VMEM((2 * MLA_VT_ROWS, S), BF16), pltpu.VMEM((MLA_LOOKAHEAD + 1, S, TQ), F32)],
        compiler_params=_cparams(("parallel", "parallel")),
        name="mla",
    )(q, k, v)

    nqx = S // TQX
    o_x = pl.pallas_call(
        _xattn_kernel,
        out_shape=jax.ShapeDtypeStruct((T, xw), BF16),
        grid=(B, nqx),
        in_specs=[pl.BlockSpec((TQX, xw), lambda b, i: (b * nqx + i, 0)),
                  pl.BlockSpec((M, D), lambda b, i: (b, 0)),
                  _const_spec((1, D)), _const_spec((D, 2 * xw))],
        out_specs=pl.BlockSpec((TQX, xw), lambda b, i: (b * nqx + i, 0)),
        scratch_shapes=[pltpu.VMEM((M, 2 * xw), BF16)],
        compiler_params=_cparams(("arbitrary", "arbitrary")),
        name="xattn",
    )(xq, mem.reshape(B * M, D), row(mem_norm[0]), bf(x_w_kv[0]))

    o_gla = pl.pallas_call(
        _gla_kernel,
        out_shape=jax.ShapeDtypeStruct((T, gdv), BF16),
        grid=(B,),
        in_specs=[pl.BlockSpec((S, gdk), lambda b: (b, 0)),
                  pl.BlockSpec((S, gdk), lambda b: (b, 0)),
                  pl.BlockSpec((S, gdk), lambda b: (b, 0)),
                  pl.BlockSpec((S, gdv), lambda b: (b, 0)),
                  pl.BlockSpec((S, gdv), lambda b: (b, 0)),
                  _const_spec((1, GLA_DV))],
        out_specs=pl.BlockSpec((S, gdv), lambda b: (b, 0)),
        compiler_params=_cparams(("parallel",), VMEM_LIMIT),
        name="gla",
    )(gq, gk, la, gv, gr, row(gla_o_norm[0]))

    out = pl.pallas_call(
        _post_kernel,
        out_shape=jax.ShapeDtypeStruct((T, D), F32),
        grid=(W_STEPS + nt,),
        in_specs=[tok(D), tok(H * MLA_V), tok(gdv), tok(xw), _const_spec((1, D)),
                  _const_spec((N_BRANCH * D, D)), _const_spec((1, N_BRANCH * D)),
                  wrows((H * MLA_V, D)), wrows((gdv, D)), wrows((xw, D)),
                  wrows((D, D)), _const_spec((1, D)),
                  *ffn_specs, _const_spec((1, D))],
        out_specs=tok(D),
        scratch_shapes=[pltpu.VMEM((H * MLA_V, D), BF16),
                        pltpu.VMEM((gdv, D), BF16), pltpu.VMEM((xw, D), BF16), pltpu.VMEM((D, D), BF16),
                        *ffn_scratch],
        compiler_params=_cparams(("arbitrary",), VMEM_LIMIT),
        name="post",
    )(h1, o_mla, o_gla, o_x, row(mix_norm[0]), w_gzt, gate_bias[0].reshape(1, N_BRANCH * D),
      mla_w_o[0], gla_w_o[0], x_w_o[0], w_out[0], row(ffn2_norm[0]),
      ffn2_wg[0], ffn2_wu[0], ffn2_wd[0], row(final_norm))
    return out.reshape(B, S, D)
```

```python
import functools
import math

import jax
import jax.numpy as jnp
from jax import lax
from jax.experimental import pallas as pl
from jax.experimental.pallas import tpu as pltpu

F32 = jnp.float32
BF16 = jnp.bfloat16

D_MODEL = 1024
N_MEM = 256
EPS = 1e-6
MLA_HEADS = 8
MLA_NOPE = 64
MLA_ROPE = 32
MLA_V = 64
MLA_Q_RANK = 384
MLA_KV_RANK = 256
ROPE_THETA = 10000.0
GLA_HEADS = 4
GLA_DK = 64
GLA_DV = 128
GLA_GATE_RANK = 16
GLA_TAU = 16.0
GLA_CHUNK = 64
X_HEADS = 4
X_DH = 128
D_FF = 2816
N_BRANCH = 3

LANE = 128
HEAD_PAD = 128
VMEM_LIMIT = 60 * 1024 * 1024
MLA_EXP2_SCALE = math.log2(math.e) / math.sqrt(MLA_NOPE + MLA_ROPE)
MLA_VT_ROWS = MLA_V + 16
MLA_LOOKAHEAD = 3

C_CQ, C_CKV, C_KR, C_GQ, C_GK, C_GV, C_GA, C_GR, C_XQ, C_GZ, C_END = (
    0, 384, 640, 672, 928, 1184, 1696, 1712, 2224, 2736, 5808)
W_STEPS = 16
A_CQ, A_CKV, A_KR, A_GQ, A_GK, A_GV, A_GR, A_XQ, A_GA, A_END = (
    0, 384, 640, 768, 1024, 1280, 1792, 2304, 2816, 2944)

TM = 512
TQ = 256
TQX = 512
GLA_SB = 256


def _dot(a, b):
    return jnp.dot(a, b, preferred_element_type=F32)


def _dot_nt(a, b):
    return lax.dot_general(a, b, (((1,), (1,)), ((), ())), preferred_element_type=F32)


def _dot_tn(a, b):
    return lax.dot_general(a, b, (((0,), (0,)), ((), ())), preferred_element_type=F32)


def _rms(x, g):
    return x * lax.rsqrt(jnp.mean(x * x, axis=-1, keepdims=True) + EPS) * g


def _sigmoid(x):
    return 1.0 / (1.0 + jnp.exp(-x))


def _silu(x):
    return x * _sigmoid(x)


def _const_spec(shape):
    return pl.BlockSpec(shape, lambda *_: (0,) * len(shape), pipeline_mode=pl.Buffered(1))


def _swiglu_half(xn_bf16, wg_ref, wu_ref, wd_ref):
    g = _dot(xn_bf16, wg_ref[...])
    up = _dot(xn_bf16, wu_ref[...])
    a = (_silu(g) * up).astype(BF16)
    return _dot(a, wd_ref[...])


def _convert_rows(dst_ref, src, step, rows):
    r0 = pl.multiple_of(step * rows, rows)
    dst_ref[pl.ds(r0, rows), :] = src.astype(BF16)


def _convert_ffn(i, wg32_ref, wu32_ref, wd32_ref, wg_ref, wu_ref, wd_ref):
    _convert_rows(wg_ref, wg32_ref[...], i, D_MODEL // W_STEPS)
    _convert_rows(wu_ref, wu32_ref[...], i, D_MODEL // W_STEPS)

    @pl.when(i % 2 == 0)
    def _():
        _convert_rows(wd_ref, wd32_ref[...], i // 2, D_FF // (W_STEPS // 2))


def _pre_kernel(x_ref, pos_ref, invf_ref, n1_ref, nm_ref, wg32_ref, wu32_ref, wd32_ref, win_ref,
                qn_ref, wuq_ref, kvn_ref, wuk_ref, wuv_ref, wa2_ref, ba_ref,
                h1_ref, q_ref, k_ref, v_ref, gq_ref, gk_ref, gv_ref, gr_ref, xq_ref, la_ref,
                wg_ref, wu_ref, wd_ref):
    i = pl.program_id(0)

    @pl.when(i < W_STEPS)
    def _():
        _convert_ffn(i, wg32_ref, wu32_ref, wd32_ref, wg_ref, wu_ref, wd_ref)

    @pl.when(i >= W_STEPS)
    def _():
        _pre_tokens(x_ref, pos_ref, invf_ref, n1_ref, nm_ref, wg_ref, wu_ref, wd_ref, win_ref,
                    qn_ref, wuq_ref, kvn_ref, wuk_ref, wuv_ref, wa2_ref, ba_ref,
                    h1_ref, q_ref, k_ref, v_ref, gq_ref, gk_ref, gv_ref, gr_ref, xq_ref, la_ref)


def _pre_tokens(x_ref, pos_ref, invf_ref, n1_ref, nm_ref, wg_ref, wu_ref, wd_ref, win_ref,
                qn_ref, wuq_ref, kvn_ref, wuk_ref, wuv_ref, wa2_ref, ba_ref,
                h1_ref, q_ref, k_ref, v_ref, gq_ref, gk_ref, gv_ref, gr_ref, xq_ref, la_ref):
    x = x_ref[...]
    u1 = _rms(x, n1_ref[...]).astype(BF16)
    h1 = x + 0.5 * _swiglu_half(u1, wg_ref, wu_ref, wd_ref)
    h1_ref[...] = h1
    u2 = _rms(h1, nm_ref[...]).astype(BF16)

    ang = pos_ref[...].astype(F32) * invf_ref[...]
    cos = jnp.cos(ang)
    sin = jnp.sin(ang)
    lane = lax.broadcasted_iota(jnp.int32, ang.shape, 1)
    sin_hi = jnp.where(lane >= 80, sin, 0.0)
    sin_lo = jnp.where(lane < 80, -sin, 0.0)

    def rope(t):
        return t * cos + pltpu.roll(t, 16, 1) * sin_hi + pltpu.roll(t, LANE - 16, 1) * sin_lo

    cqn = _rms(_dot_nt(u2, win_ref[A_CQ:A_CKV, :]), qn_ref[...]).astype(BF16)
    for p in range(MLA_HEADS // 2):
        qp = _dot(cqn, wuq_ref[:, 2 * p * HEAD_PAD:(2 * p + 2) * HEAD_PAD])
        for j in range(2):
            h = 2 * p + j
            q_ref[:, h * HEAD_PAD:(h + 1) * HEAD_PAD] = (
                rope(qp[:, j * HEAD_PAD:(j + 1) * HEAD_PAD]) * MLA_EXP2_SCALE).astype(BF16)

    ckvn = _rms(_dot_nt(u2, win_ref[A_CKV:A_KR, :]), kvn_ref[...]).astype(BF16)
    kr = rope(_dot_nt(u2, win_ref[A_KR:A_GQ, :]))
    kr2 = jnp.concatenate([kr, kr], axis=1)
    for p in range(MLA_HEADS // 2):
        kn = _dot(ckvn, wuk_ref[:, 2 * p * HEAD_PAD:(2 * p + 2) * HEAD_PAD])
        k_ref[:, 2 * p * HEAD_PAD:(2 * p + 2) * HEAD_PAD] = (kn + kr2).astype(BF16)
    v_ref[...] = _dot(ckvn, wuv_ref[...]).astype(BF16)

    gq_ref[...] = _dot_nt(u2, win_ref[A_GQ:A_GK, :])
    gk_ref[...] = _dot_nt(u2, win_ref[A_GK:A_GV, :])
    gv_ref[...] = _dot_nt(u2, win_ref[A_GV:A_GR, :]).astype(BF16)
    gr_ref[...] = _dot_nt(u2, win_ref[A_GR:A_XQ, :]).astype(BF16)
    xq_ref[...] = _dot_nt(u2, win_ref[A_XQ:A_GA, :]).astype(BF16)
    ga = _dot_nt(u2, win_ref[A_GA:A_END, :]).astype(BF16)
    t = _dot(ga, wa2_ref[...]) + ba_ref[...]
    log_sig = jnp.minimum(t, 0.0) - jnp.log(1.0 + jnp.exp(-jnp.abs(t)))
    la_ref[...] = log_sig * (1.0 / GLA_TAU)


def _mla_kernel(q_ref, k_ref, v_ref, o_ref, vt_ref, s_ref):
    tq = TQ
    nq = q_ref.shape[0] // tq
    key = lax.broadcasted_iota(jnp.int32, (tq, tq), 0)
    qry = lax.broadcasted_iota(jnp.int32, (tq, tq), 1)
    causal = key <= qry
    vt = v_ref[...].astype(F32).T.astype(BF16)
    ones_row = jnp.where(lax.broadcasted_iota(jnp.int32, (MLA_VT_ROWS - MLA_V, vt.shape[1]), 0) == 0,
                         1.0, 0.0).astype(BF16)
    for h in range(2):
        vt_ref[h * MLA_VT_ROWS:h * MLA_VT_ROWS + MLA_V, :] = vt[h * MLA_V:(h + 1) * MLA_V, :]
        vt_ref[h * MLA_VT_ROWS + MLA_V:(h + 1) * MLA_VT_ROWS, :] = ones_row

    def scores(qi, h, slot):
        q = q_ref[qi * tq:(qi + 1) * tq, h * HEAD_PAD:(h + 1) * HEAD_PAD]
        m = None
        for c in range(qi + 1):
            s = _dot_nt(k_ref[c * tq:(c + 1) * tq, h * HEAD_PAD:(h + 1) * HEAD_PAD], q)
            if c == qi:
                s = jnp.where(causal, s, -1e30)
            s_ref[slot, c * tq:(c + 1) * tq, :] = s
            cm = jnp.max(s, axis=0, keepdims=True)
            m = cm if m is None else jnp.maximum(m, cm)
        return m

    def values(qi, h, slot, m):
        acc = None
        for c in range(qi + 1):
            p = jnp.exp2((s_ref[slot, c * tq:(c + 1) * tq, :] - m).astype(BF16))
            pv = _dot(vt_ref[h * MLA_VT_ROWS:(h + 1) * MLA_VT_ROWS, c * tq:(c + 1) * tq], p)
            acc = pv if acc is None else acc + pv
        return acc[:MLA_V, :] / acc[MLA_V:MLA_V + 1, :]

    groups = [(qi, h) for qi in range(nq) for h in range(2)]
    nslot = s_ref.shape[0]
    maxes = {}
    for i in range(min(MLA_LOOKAHEAD, len(groups))):
        maxes[i] = scores(*groups[i], i % nslot)
    heads = {}
    for i, (qi, h) in enumerate(groups):
        j = i + MLA_LOOKAHEAD
        if j < len(groups):
            maxes[j] = scores(*groups[j], j % nslot)
        heads[h] = values(qi, h, i % nslot, maxes.pop(i))
        if h == 1:
            o_ref[qi * tq:(qi + 1) * tq, :] = jnp.concatenate([heads[0], heads[1]], axis=0).T.astype(BF16)


def _xattn_kernel(q_ref, mem_ref, n_ref, w_ref, o_ref, kv_ref):
    @pl.when(pl.program_id(1) == 0)
    def _():
        mn = _rms(mem_ref[...], n_ref[...]).astype(BF16)
        kv_ref[...] = _dot(mn, w_ref[...]).astype(BF16)

    scale = 1.0 / math.sqrt(X_DH)
    hw = X_HEADS * X_DH
    for h in range(X_HEADS):
        q = q_ref[:, h * X_DH:(h + 1) * X_DH]
        k = kv_ref[:, h * X_DH:(h + 1) * X_DH]
        v = kv_ref[:, hw + h * X_DH:hw + (h + 1) * X_DH]
        s = _dot_nt(q, k) * scale
        m = jnp.max(s, axis=-1, keepdims=True)
        p = jnp.exp(s - m)
        l = jnp.sum(p, axis=-1, keepdims=True)
        o = _dot(p.astype(BF16), v) / l
        o_ref[:, h * X_DH:(h + 1) * X_DH] = o.astype(BF16)


def _gla_kernel(q_ref, k_ref, la_ref, v_ref, r_ref, on_ref, o_ref):
    sb, c = GLA_SB, GLA_CHUNK
    nsb = q_ref.shape[0] // sb
    nch = sb // c
    npair = GLA_HEADS // 2
    pw, vw = 2 * GLA_DK, 2 * GLA_DV
    row = lax.broadcasted_iota(jnp.int32, (sb, sb), 0)
    col = lax.broadcasted_iota(jnp.int32, (sb, sb), 1)
    tri = jnp.logical_and(row // c == col // c, col <= row)
    tri_bf = jnp.where(tri, 1.0, 0.0).astype(BF16)
    lane = lax.broadcasted_iota(jnp.int32, (1, pw), 1)
    head_lanes = (lane < GLA_DK, lane >= GLA_DK)
    srow = lax.broadcasted_iota(jnp.int32, (pw, vw), 0)
    scol = lax.broadcasted_iota(jnp.int32, (pw, vw), 1)
    same_head = (srow < GLA_DK) == (scol < GLA_DV)

    def local(i):
        r0 = i * sb
        la = la_ref[r0:r0 + sb, :]
        hi = la.astype(BF16)
        lo = (la - hi.astype(F32)).astype(BF16)
        bcum_all = _dot(tri_bf, hi) + _dot(tri_bf, lo)
        out = []
        for p in range(npair):
            bcum = bcum_all[:, p * pw:(p + 1) * pw]
            b_last = [bcum[j * c + c - 1:j * c + c, :] for j in range(nch)]
            b_last_rows = jnp.concatenate([jnp.broadcast_to(b, (c, pw)) for b in b_last], axis=0)
            q_t = q_ref[r0:r0 + sb, p * pw:(p + 1) * pw] * (GLA_DK ** -0.5) * jnp.exp(bcum)
            kk = k_ref[r0:r0 + sb, p * pw:(p + 1) * pw]
            k_t = (kk * jnp.exp(-bcum)).astype(BF16)
            k_d = (kk * jnp.exp(b_last_rows - bcum)).astype(BF16)
            v = v_ref[r0:r0 + sb, p * vw:(p + 1) * vw]
            o_intra = []
            for h in range(2):
                qm = jnp.where(head_lanes[h], q_t, 0.0).astype(BF16)
                att = jnp.where(tri, _dot_nt(qm, k_t), 0.0).astype(BF16)
                o_intra.append(_dot(att, v[:, h * GLA_DV:(h + 1) * GLA_DV]))
            d_st, decay = [], []
            for j in range(nch):
                d = _dot_tn(k_d[j * c:(j + 1) * c, :], v[j * c:(j + 1) * c, :])
                d_st.append(jnp.where(same_head, d, 0.0))
                dcol = jnp.broadcast_to(jnp.exp(b_last[j]), (pw, pw)).T
                decay.append(jnp.concatenate([dcol, dcol], axis=1))
            out.append((q_t.astype(BF16), o_intra, d_st, decay))
        return out

    def recurrent(i, loc, states):
        r0 = i * sb
        for p in range(npair):
            q_bf, o_intra, d_st, decay = loc[p]
            st = states[p]
            o_inter = []
            for j in range(nch):
                o_inter.append(_dot(q_bf[j * c:(j + 1) * c, :], st.astype(BF16)))
                st = decay[j] * st + d_st[j]
            states[p] = st
            o_inter = jnp.concatenate(o_inter, axis=0)
            for h in range(2):
                lo_, hi_ = p * vw + h * GLA_DV, p * vw + (h + 1) * GLA_DV
                o = _rms(o_intra[h] + o_inter[:, h * GLA_DV:(h + 1) * GLA_DV], on_ref[...])
                r = r_ref[r0:r0 + sb, lo_:hi_].astype(F32)
                o_ref[r0:r0 + sb, lo_:hi_] = (o * _silu(r)).astype(BF16)

    states = [jnp.zeros((pw, vw), F32) for _ in range(npair)]
    loc = local(0)
    for i in range(nsb):
        nxt = local(i + 1) if i + 1 < nsb else None
        recurrent(i, loc, states)
        loc = nxt


def _post_kernel(h1_ref, om_ref, og_ref, ox_ref, nm_ref, wgz_ref, gb_ref, wom32_ref, wog32_ref, wox32_ref,
                 wout32_ref, n2_ref, wg32_ref, wu32_ref, wd32_ref, nf_ref, o_ref,
                 wom_ref, wog_ref, wox_ref, wout_ref, wg_ref, wu_ref, wd_ref):
    i = pl.program_id(0)

    @pl.when(i < W_STEPS)
    def _():
        _convert_ffn(i, wg32_ref, wu32_ref, wd32_ref, wg_ref, wu_ref, wd_ref)
        for dst, src in ((wom_ref, wom32_ref), (wog_ref, wog32_ref), (wox_ref, wox32_ref)):
            _convert_rows(dst, src[...], i, dst.shape[0] // W_STEPS)
        _convert_rows(wout_ref, wout32_ref[...], i, D_MODEL // W_STEPS)

    @pl.when(i >= W_STEPS)
    def _():
        _post_tokens(h1_ref, om_ref, og_ref, ox_ref, nm_ref, wgz_ref, gb_ref, wom_ref, wog_ref, wox_ref,
                     wout_ref, n2_ref, wg_ref, wu_ref, wd_ref, nf_ref, o_ref)


def _post_tokens(h1_ref, om_ref, og_ref, ox_ref, nm_ref, wgz_ref, gb_ref, wom_ref, wog_ref, wox_ref,
                 wout_ref, n2_ref, wg_ref, wu_ref, wd_ref, nf_ref, o_ref):
    h1 = h1_ref[...]
    u = _rms(h1, nm_ref[...]).astype(BF16)
    d = D_MODEL
    merged = None
    for n, (o_br, w_br) in enumerate(((om_ref, wom_ref), (og_ref, wog_ref), (ox_ref, wox_ref))):
        gate = _sigmoid(_dot_nt(u, wgz_ref[n * d:(n + 1) * d, :]) + gb_ref[:, n * d:(n + 1) * d])
        term = gate * _dot(o_br[...], w_br[...])
        merged = term if merged is None else merged + term
    h2 = h1 + _dot(merged.astype(BF16), wout_ref[...])
    u2 = _rms(h2, n2_ref[...]).astype(BF16)
    h3 = h2 + 0.5 * _swiglu_half(u2, wg_ref, wu_ref, wd_ref)
    o_ref[...] = _rms(h3, nf_ref[...])


def _cparams(sem, vmem=None, flags=None):
    return pltpu.CompilerParams(dimension_semantics=sem, vmem_limit_bytes=vmem, flags=flags)


def kernel(x, mem, positions, ffn1_norm, ffn1_wg, ffn1_wu, ffn1_wd, mix_norm, mem_norm, w_in, gate_bias,
           mla_q_norm, mla_w_uq, mla_kv_norm, mla_w_ukv, mla_w_o, gla_w_a2, gla_b_a, gla_o_norm, gla_w_o,
           x_w_kv, x_w_o, w_out, ffn2_norm, ffn2_wg, ffn2_wu, ffn2_wd, final_norm):
    B, S, D = x.shape
    T = B * S
    M = mem.shape[1]
    F = D_FF
    H = MLA_HEADS
    bf = lambda a: a.astype(BF16)
    row = lambda a: a.reshape(1, -1)

    wt = jnp.swapaxes(w_in[0], 0, 1)
    zr = lambda n: jnp.zeros((n, D), F32)
    w_in_at = bf(jnp.concatenate(
        [wt[:C_KR], zr(MLA_NOPE), wt[C_KR:C_GQ], zr(HEAD_PAD - MLA_NOPE - MLA_ROPE),
         wt[C_GQ:C_GA], wt[C_GR:C_GZ], wt[C_GA:C_GR], zr(LANE - GLA_GATE_RANK)], axis=0))
    w_gzt = bf(wt[C_GZ:])
    w_uq = mla_w_uq[0].reshape(MLA_Q_RANK, H, MLA_NOPE + MLA_ROPE)
    w_uq = bf(jnp.pad(w_uq, ((0, 0), (0, 0), (0, HEAD_PAD - MLA_NOPE - MLA_ROPE))).reshape(MLA_Q_RANK, H * HEAD_PAD))
    w_ukv = mla_w_ukv[0].reshape(MLA_KV_RANK, H, MLA_NOPE + MLA_V)
    w_uk = bf(jnp.pad(w_ukv[:, :, :MLA_NOPE], ((0, 0), (0, 0), (0, HEAD_PAD - MLA_NOPE))).reshape(MLA_KV_RANK, H * HEAD_PAD))
    w_uv = bf(w_ukv[:, :, MLA_NOPE:].reshape(MLA_KV_RANK, H * MLA_V))
    w_a2 = bf(jnp.pad(gla_w_a2[0], ((0, LANE - GLA_GATE_RANK), (0, 0))))
    half = MLA_ROPE // 2
    inv_freq = ROPE_THETA ** (-jnp.arange(half, dtype=F32) / half)
    invf = jnp.concatenate([jnp.zeros((MLA_NOPE,), F32), inv_freq, inv_freq,
                            jnp.zeros((HEAD_PAD - MLA_NOPE - MLA_ROPE,), F32)]).reshape(1, HEAD_PAD)

    x2 = x.reshape(T, D)
    pos2 = jnp.broadcast_to(positions.reshape(T, 1), (T, HEAD_PAD))
    nt = T // TM
    tok = lambda w: pl.BlockSpec((TM, w), lambda i: (jnp.maximum(i - W_STEPS, 0), 0))
    wrows = lambda shape, every=1: pl.BlockSpec(
        (shape[0] * every // W_STEPS, shape[1]),
        lambda i: (jnp.minimum(i // every, W_STEPS // every - 1), 0))
    ffn_specs = [wrows((D, F)), wrows((D, F)), wrows((F, D), 2)]
    ffn_scratch = [pltpu.VMEM((D, F), BF16), pltpu.VMEM((D, F), BF16), pltpu.VMEM((F, D), BF16)]

    gdk, gdv, xw = GLA_HEADS * GLA_DK, GLA_HEADS * GLA_DV, X_HEADS * X_DH
    pre_out_shapes = (
        jax.ShapeDtypeStruct((T, D), F32),
        jax.ShapeDtypeStruct((T, H * HEAD_PAD), BF16),
        jax.ShapeDtypeStruct((T, H * HEAD_PAD), BF16),
        jax.ShapeDtypeStruct((T, H * MLA_V), BF16),
        jax.ShapeDtypeStruct((T, gdk), F32),
        jax.ShapeDtypeStruct((T, gdk), F32),
        jax.ShapeDtypeStruct((T, gdv), BF16),
        jax.ShapeDtypeStruct((T, gdv), BF16),
        jax.ShapeDtypeStruct((T, xw), BF16),
        jax.ShapeDtypeStruct((T, gdk), F32),
    )
    h1, q, k, v, gq, gk, gv, gr, xq, la = pl.pallas_call(
        _pre_kernel,
        out_shape=pre_out_shapes,
        grid=(W_STEPS + nt,),
        in_specs=[tok(D), tok(HEAD_PAD), _const_spec((1, HEAD_PAD)), _const_spec((1, D)), _const_spec((1, D)),
                  *ffn_specs, _const_spec((A_END, D)),
                  _const_spec((1, MLA_Q_RANK)), _const_spec((MLA_Q_RANK, H * HEAD_PAD)),
                  _const_spec((1, MLA_KV_RANK)), _const_spec((MLA_KV_RANK, H * HEAD_PAD)),
                  _const_spec((MLA_KV_RANK, H * MLA_V)), _const_spec((LANE, gdk)), _const_spec((1, gdk))],
        out_specs=tuple(tok(s.shape[1]) for s in pre_out_shapes),
        scratch_shapes=ffn_scratch,
        compiler_params=_cparams(("arbitrary",), VMEM_LIMIT),
        name="pre",
    )(x2, pos2, invf, row(ffn1_norm[0]), row(mix_norm[0]), ffn1_wg[0], ffn1_wu[0], ffn1_wd[0],
      w_in_at, row(mla_q_norm[0]), w_uq, row(mla_kv_norm[0]), w_uk, w_uv, w_a2, row(gla_b_a[0]))

    o_mla = pl.pallas_call(
        _mla_kernel,
        out_shape=jax.ShapeDtypeStruct((T, H * MLA_V), BF16),
        grid=(B, H // 2),
        in_specs=[pl.BlockSpec((S, 2 * HEAD_PAD), lambda b, p: (b, p)),
                  pl.BlockSpec((S, 2 * HEAD_PAD), lambda b, p: (b, p)),
                  pl.BlockSpec((S, 2 * MLA_V), lambda b, p: (b, p))],
        out_specs=pl.BlockSpec((S, 2 * MLA_V), lambda b, p: (b, p)),
        scratch_shapes=[pltpu.VMEM((2 * MLA_VT_ROWS, S), BF16), pltpu.VMEM((MLA_LOOKAHEAD + 1, S, TQ), F32)],
        compiler_params=_cparams(("parallel", "parallel")),
        name="mla",
    )(q, k, v)

    nqx = S // TQX
    o_x = pl.pallas_call(
        _xattn_kernel,
        out_shape=jax.ShapeDtypeStruct((T, xw), BF16),
        grid=(B, nqx),
        in_specs=[pl.BlockSpec((TQX, xw), lambda b, i: (b * nqx + i, 0)),
                  pl.BlockSpec((M, D), lambda b, i: (b, 0)),
                  _const_spec((1, D)), _const_spec((D, 2 * xw))],
        out_specs=pl.BlockSpec((TQX, xw), lambda b, i: (b * nqx + i, 0)),
        scratch_shapes=[pltpu.VMEM((M, 2 * xw), BF16)],
        compiler_params=_cparams(("arbitrary", "arbitrary")),
        name="xattn",
    )(xq, mem.reshape(B * M, D), row(mem_norm[0]), bf(x_w_kv[0]))

    o_gla = pl.pallas_call(
        _gla_kernel,
        out_shape=jax.ShapeDtypeStruct((T, gdv), BF16),
        grid=(B,),
        in_specs=[pl.BlockSpec((S, gdk), lambda b: (b, 0)),
                  pl.BlockSpec((S, gdk), lambda b: (b, 0)),
                  pl.BlockSpec((S, gdk), lambda b: (b, 0)),
                  pl.BlockSpec((S, gdv), lambda b: (b, 0)),
                  pl.BlockSpec((S, gdv), lambda b: (b, 0)),
                  _const_spec((1, GLA_DV))],
        out_specs=pl.BlockSpec((S, gdv), lambda b: (b, 0)),
        compiler_params=_cparams(("parallel",), VMEM_LIMIT),
        name="gla",
    )(gq, gk, la, gv, gr, row(gla_o_norm[0]))

    out = pl.pallas_call(
        _post_kernel,
        out_shape=jax.ShapeDtypeStruct((T, D), F32),
        grid=(W_STEPS + nt,),
        in_specs=[tok(D), tok(H * MLA_V), tok(gdv), tok(xw), _const_spec((1, D)),
                  _const_spec((N_BRANCH * D, D)), _const_spec((1, N_BRANCH * D)),
                  wrows((H * MLA_V, D)), wrows((gdv, D)), wrows((xw, D)),
                  wrows((D, D)), _const_spec((1, D)),
                  *ffn_specs, _const_spec((1, D))],
        out_specs=tok(D),
        scratch_shapes=[pltpu.VMEM((H * MLA_V, D), BF16),
                        pltpu.VMEM((gdv, D), BF16), pltpu.VMEM((xw, D), BF16), pltpu.VMEM((D, D), BF16),
                        *ffn_scratch],
        compiler_params=_cparams(("arbitrary",), VMEM_LIMIT),
        name="post",
    )(h1, o_mla, o_gla, o_x, row(mix_norm[0]), w_gzt, gate_bias[0].reshape(1, N_BRANCH * D),
      mla_w_o[0], gla_w_o[0], x_w_o[0], w_out[0], row(ffn2_norm[0]),
      ffn2_wg[0], ffn2_wu[0], ffn2_wd[0], row(final_norm))
    return out.reshape(B, S, D)
```

```python
import functools
import math

import jax
import jax.numpy as jnp
from jax import lax
from jax.experimental import pallas as pl
from jax.experimental.pallas import tpu as pltpu

F32 = jnp.float32
BF16 = jnp.bfloat16

D_MODEL = 1024
N_MEM = 256
EPS = 1e-6
MLA_HEADS = 8
MLA_NOPE = 64
MLA_ROPE = 32
MLA_V = 64
MLA_Q_RANK = 384
MLA_KV_RANK = 256
ROPE_THETA = 10000.0
GLA_HEADS = 4
GLA_DK = 64
GLA_DV = 128
GLA_GATE_RANK = 16
GLA_TAU = 16.0
GLA_CHUNK = 64
X_HEADS = 4
X_DH = 128
D_FF = 2816
N_BRANCH = 3

LANE = 128
HEAD_PAD = 128
VMEM_LIMIT = 60 * 1024 * 1024
MLA_EXP2_SCALE = math.log2(math.e) / math.sqrt(MLA_NOPE + MLA_ROPE)
MLA_VT_ROWS = MLA_V + 16
MLA_LOOKAHEAD = 3

C_CQ, C_CKV, C_KR, C_GQ, C_GK, C_GV, C_GA, C_GR, C_XQ, C_GZ, C_END = (
    0, 384, 640, 672, 928, 1184, 1696, 1712, 2224, 2736, 5808)
W_STEPS = 16
A_CQ, A_CKV, A_KR, A_GQ, A_GK, A_GV, A_GR, A_XQ, A_GA, A_END = (
    0, 384, 640, 768, 1024, 1280, 1792, 2304, 2816, 2944)

TM = 512
TQ = 256
TQX = 2048
GLA_SB = 256


def _dot(a, b):
    return jnp.dot(a, b, preferred_element_type=F32)


def _dot_nt(a, b):
    return lax.dot_general(a, b, (((1,), (1,)), ((), ())), preferred_element_type=F32)


def _dot_tn(a, b):
    return lax.dot_general(a, b, (((0,), (0,)), ((), ())), preferred_element_type=F32)


def _rms(x, g):
    return x * lax.rsqrt(jnp.mean(x * x, axis=-1, keepdims=True) + EPS) * g


def _sigmoid(x):
    return 1.0 / (1.0 + jnp.exp(-x))


def _silu(x):
    return x * _sigmoid(x)


def _const_spec(shape):
    return pl.BlockSpec(shape, lambda *_: (0,) * len(shape), pipeline_mode=pl.Buffered(1))


def _swiglu_half(xn_bf16, wg_ref, wu_ref, wd_ref):
    g = _dot(xn_bf16, wg_ref[...])
    up = _dot(xn_bf16, wu_ref[...])
    a = (_silu(g) * up).astype(BF16)
    return _dot(a, wd_ref[...])


def _convert_rows(dst_ref, src, step, rows):
    r0 = pl.multiple_of(step * rows, rows)
    dst_ref[pl.ds(r0, rows), :] = src.astype(BF16)


def _convert_ffn(i, wg32_ref, wu32_ref, wd32_ref, wg_ref, wu_ref, wd_ref):
    _convert_rows(wg_ref, wg32_ref[...], i, D_MODEL // W_STEPS)
    _convert_rows(wu_ref, wu32_ref[...], i, D_MODEL // W_STEPS)

    @pl.when(i % 2 == 0)
    def _():
        _convert_rows(wd_ref, wd32_ref[...], i // 2, D_FF // (W_STEPS // 2))


def _pre_kernel(x_ref, pos_ref, invf_ref, n1_ref, nm_ref, wg32_ref, wu32_ref, wd32_ref, win_ref,
                qn_ref, wuq_ref, kvn_ref, wuk_ref, wuv_ref, wa2_ref, ba_ref,
                h1_ref, q_ref, k_ref, v_ref, gq_ref, gk_ref, gv_ref, gr_ref, xq_ref, la_ref,
                wg_ref, wu_ref, wd_ref):
    i = pl.program_id(0)

    @pl.when(i < W_STEPS)
    def _():
        _convert_ffn(i, wg32_ref, wu32_ref, wd32_ref, wg_ref, wu_ref, wd_ref)

    @pl.when(i >= W_STEPS)
    def _():
        _pre_tokens(x_ref, pos_ref, invf_ref, n1_ref, nm_ref, wg_ref, wu_ref, wd_ref, win_ref,
                    qn_ref, wuq_ref, kvn_ref, wuk_ref, wuv_ref, wa2_ref, ba_ref,
                    h1_ref, q_ref, k_ref, v_ref, gq_ref, gk_ref, gv_ref, gr_ref, xq_ref, la_ref)


def _pre_tokens(x_ref, pos_ref, invf_ref, n1_ref, nm_ref, wg_ref, wu_ref, wd_ref, win_ref,
                qn_ref, wuq_ref, kvn_ref, wuk_ref, wuv_ref, wa2_ref, ba_ref,
                h1_ref, q_ref, k_ref, v_ref, gq_ref, gk_ref, gv_ref, gr_ref, xq_ref, la_ref):
    x = x_ref[...]
    u1 = _rms(x, n1_ref[...]).astype(BF16)
    h1 = x + 0.5 * _swiglu_half(u1, wg_ref, wu_ref, wd_ref)
    h1_ref[...] = h1
    u2 = _rms(h1, nm_ref[...]).astype(BF16)

    ang = pos_ref[...].astype(F32) * invf_ref[...]
    cos = jnp.cos(ang)
    sin = jnp.sin(ang)
    lane = lax.broadcasted_iota(jnp.int32, ang.shape, 1)
    sin_hi = jnp.where(lane >= 80, sin, 0.0)
    sin_lo = jnp.where(lane < 80, -sin, 0.0)

    def rope(t):
        return t * cos + pltpu.roll(t, 16, 1) * sin_hi + pltpu.roll(t, LANE - 16, 1) * sin_lo

    cqn = _rms(_dot_nt(u2, win_ref[A_CQ:A_CKV, :]), qn_ref[...]).astype(BF16)
    for p in range(MLA_HEADS // 2):
        qp = _dot(cqn, wuq_ref[:, 2 * p * HEAD_PAD:(2 * p + 2) * HEAD_PAD])
        for j in range(2):
            h = 2 * p + j
            q_ref[p, :, j * HEAD_PAD:(j + 1) * HEAD_PAD] = (
                rope(qp[:, j * HEAD_PAD:(j + 1) * HEAD_PAD]) * MLA_EXP2_SCALE).astype(BF16)

    ckvn = _rms(_dot_nt(u2, win_ref[A_CKV:A_KR, :]), kvn_ref[...]).astype(BF16)
    kr = rope(_dot_nt(u2, win_ref[A_KR:A_GQ, :]))
    kr2 = jnp.concatenate([kr, kr], axis=1)
    for p in range(MLA_HEADS // 2):
        kn = _dot(ckvn, wuk_ref[:, 2 * p * HEAD_PAD:(2 * p + 2) * HEAD_PAD])
        k_ref[p] = (kn + kr2).astype(BF16)
    vv = _dot(ckvn, wuv_ref[...]).astype(BF16)
    for p in range(MLA_HEADS // 2):
        v_ref[p] = vv[:, 2 * p * MLA_V:(2 * p + 2) * MLA_V]

    gq_ref[...] = _dot_nt(u2, win_ref[A_GQ:A_GK, :])
    gk_ref[...] = _dot_nt(u2, win_ref[A_GK:A_GV, :])
    gv_ref[...] = _dot_nt(u2, win_ref[A_GV:A_GR, :]).astype(BF16)
    gr_ref[...] = _dot_nt(u2, win_ref[A_GR:A_XQ, :]).astype(BF16)
    xq_ref[...] = _dot_nt(u2, win_ref[A_XQ:A_GA, :]).astype(BF16)
    ga = _dot_nt(u2, win_ref[A_GA:A_END, :]).astype(BF16)
    t = _dot(ga, wa2_ref[...]) + ba_ref[...]
    log_sig = jnp.minimum(t, 0.0) - jnp.log(1.0 + jnp.exp(-jnp.abs(t)))
    la_ref[...] = log_sig * (1.0 / GLA_TAU)


def _mla_kernel(q_ref, k_ref, v_ref, o_ref, vt_ref, s_ref):
    def pair(p, carry):
        _mla_pair(q_ref.at[p], k_ref.at[p], v_ref.at[p], o_ref.at[p], vt_ref, s_ref)
        return carry

    lax.fori_loop(0, q_ref.shape[0], pair, 0)


def _mla_pair(q_ref, k_ref, v_ref, o_ref, vt_ref, s_ref):
    tq = TQ
    nq = q_ref.shape[0] // tq
    key = lax.broadcasted_iota(jnp.int32, (tq, tq), 0)
    qry = lax.broadcasted_iota(jnp.int32, (tq, tq), 1)
    causal = key <= qry
    vt = v_ref[...].astype(F32).T.astype(BF16)
    ones_row = jnp.where(lax.broadcasted_iota(jnp.int32, (MLA_VT_ROWS - MLA_V, vt.shape[1]), 0) == 0,
                         1.0, 0.0).astype(BF16)
    for h in range(2):
        vt_ref[h * MLA_VT_ROWS:h * MLA_VT_ROWS + MLA_V, :] = vt[h * MLA_V:(h + 1) * MLA_V, :]
        vt_ref[h * MLA_VT_ROWS + MLA_V:(h + 1) * MLA_VT_ROWS, :] = ones_row

    def scores(qi, h, slot):
        q = q_ref[qi * tq:(qi + 1) * tq, h * HEAD_PAD:(h + 1) * HEAD_PAD]
        m = None
        for c in range(qi + 1):
            s = _dot_nt(k_ref[c * tq:(c + 1) * tq, h * HEAD_PAD:(h + 1) * HEAD_PAD], q)
            if c == qi:
                s = jnp.where(causal, s, -1e30)
            s_ref[slot, c * tq:(c + 1) * tq, :] = s
            cm = jnp.max(s, axis=0, keepdims=True)
            m = cm if m is None else jnp.maximum(m, cm)
        return m

    def values(qi, h, slot, m):
        acc = None
        for c in range(qi + 1):
            p = jnp.exp2(s_ref[slot, c * tq:(c + 1) * tq, :] - m).astype(BF16)
            pv = _dot(vt_ref[h * MLA_VT_ROWS:(h + 1) * MLA_VT_ROWS, c * tq:(c + 1) * tq], p)
            acc = pv if acc is None else acc + pv
        return acc[:MLA_V, :] / acc[MLA_V:MLA_V + 1, :]

    groups = [(qi, h) for qi in range(nq) for h in range(2)]
    nslot = s_ref.shape[0]
    maxes = {}
    for i in range(min(MLA_LOOKAHEAD, len(groups))):
        maxes[i] = scores(*groups[i], i % nslot)
    heads = {}
    for i, (qi, h) in enumerate(groups):
        j = i + MLA_LOOKAHEAD
        if j < len(groups):
            maxes[j] = scores(*groups[j], j % nslot)
        heads[h] = values(qi, h, i % nslot, maxes.pop(i))
        if h == 1:
            o_ref[qi * tq:(qi + 1) * tq, :] = jnp.concatenate([heads[0], heads[1]], axis=0).T.astype(BF16)


def _xattn_kernel(q_ref, mem_ref, n_ref, w_ref, o_ref, kv_ref):
    @pl.when(pl.program_id(1) == 0)
    def _():
        mn = _rms(mem_ref[...], n_ref[...]).astype(BF16)
        kv_ref[...] = _dot(mn, w_ref[...]).astype(BF16)

    scale = 1.0 / math.sqrt(X_DH)
    hw = X_HEADS * X_DH
    for h in range(X_HEADS):
        q = q_ref[:, h * X_DH:(h + 1) * X_DH]
        k = kv_ref[:, h * X_DH:(h + 1) * X_DH]
        v = kv_ref[:, hw + h * X_DH:hw + (h + 1) * X_DH]
        s = _dot_nt(q, k) * scale
        m = jnp.max(s, axis=-1, keepdims=True)
        p = jnp.exp(s - m)
        l = jnp.sum(p, axis=-1, keepdims=True)
        o = _dot(p.astype(BF16), v) / l
        o_ref[:, h * X_DH:(h + 1) * X_DH] = o.astype(BF16)


def _gla_kernel(q_ref, k_ref, la_ref, v_ref, r_ref, on_ref, o_ref):
    sb, c = GLA_SB, GLA_CHUNK
    nsb = q_ref.shape[0] // sb
    nch = sb // c
    npair = GLA_HEADS // 2
    pw, vw = 2 * GLA_DK, 2 * GLA_DV
    row = lax.broadcasted_iota(jnp.int32, (sb, sb), 0)
    col = lax.broadcasted_iota(jnp.int32, (sb, sb), 1)
    tri = jnp.logical_and(row // c == col // c, col <= row)
    tri_bf = jnp.where(tri, 1.0, 0.0).astype(BF16)
    lane = lax.broadcasted_iota(jnp.int32, (1, pw), 1)
    head_lanes = (lane < GLA_DK, lane >= GLA_DK)
    srow = lax.broadcasted_iota(jnp.int32, (pw, vw), 0)
    scol = lax.broadcasted_iota(jnp.int32, (pw, vw), 1)
    same_head = (srow < GLA_DK) == (scol < GLA_DV)

    def local(i):
        r0 = i * sb
        la = la_ref[r0:r0 + sb, :]
        hi = la.astype(BF16)
        lo = (la - hi.astype(F32)).astype(BF16)
        bcum_all = _dot(tri_bf, hi) + _dot(tri_bf, lo)
        out = []
        for p in range(npair):
            bcum = bcum_all[:, p * pw:(p + 1) * pw]
            b_last = [bcum[j * c + c - 1:j * c + c, :] for j in range(nch)]
            b_last_rows = jnp.concatenate([jnp.broadcast_to(b, (c, pw)) for b in b_last], axis=0)
            q_t = q_ref[r0:r0 + sb, p * pw:(p + 1) * pw] * (GLA_DK ** -0.5) * jnp.exp(bcum)
            kk = k_ref[r0:r0 + sb, p * pw:(p + 1) * pw]
            k_t = (kk * jnp.exp(-bcum)).astype(BF16)
            k_d = (kk * jnp.exp(b_last_rows - bcum)).astype(BF16)
            v = v_ref[r0:r0 + sb, p * vw:(p + 1) * vw]
            o_intra = []
            for h in range(2):
                qm = jnp.where(head_lanes[h], q_t, 0.0).astype(BF16)
                att = jnp.where(tri, _dot_nt(qm, k_t), 0.0).astype(BF16)
                o_intra.append(_dot(att, v[:, h * GLA_DV:(h + 1) * GLA_DV]))
            d_st, decay = [], []
            for j in range(nch):
                d = _dot_tn(k_d[j * c:(j + 1) * c, :], v[j * c:(j + 1) * c, :])
                d_st.append(jnp.where(same_head, d, 0.0))
                dcol = jnp.broadcast_to(jnp.exp(b_last[j]), (pw, pw)).T
                decay.append(jnp.concatenate([dcol, dcol], axis=1))
            out.append((q_t.astype(BF16), o_intra, d_st, decay))
        return out

    def recurrent(i, loc, states):
        r0 = i * sb
        for p in range(npair):
            q_bf, o_intra, d_st, decay = loc[p]
            st = states[p]
            o_inter = []
            for j in range(nch):
                o_inter.append(_dot(q_bf[j * c:(j + 1) * c, :], st.astype(BF16)))
                st = decay[j] * st + d_st[j]
            states[p] = st
            o_inter = jnp.concatenate(o_inter, axis=0)
            for h in range(2):
                lo_, hi_ = p * vw + h * GLA_DV, p * vw + (h + 1) * GLA_DV
                o = _rms(o_intra[h] + o_inter[:, h * GLA_DV:(h + 1) * GLA_DV], on_ref[...])
                r = r_ref[r0:r0 + sb, lo_:hi_].astype(F32)
                o_ref[r0:r0 + sb, lo_:hi_] = (o * _silu(r)).astype(BF16)

    states = [jnp.zeros((pw, vw), F32) for _ in range(npair)]
    loc = local(0)
    for i in range(nsb):
        nxt = local(i + 1) if i + 1 < nsb else None
        recurrent(i, loc, states)
        loc = nxt


def _post_kernel(h1_ref, om_ref, og_ref, ox_ref, nm_ref, wgz_ref, gb_ref, wom32_ref, wog32_ref, wox32_ref,
                 wout32_ref, n2_ref, wg32_ref, wu32_ref, wd32_ref, nf_ref, o_ref,
                 wom_ref, wog_ref, wox_ref, wout_ref, wg_ref, wu_ref, wd_ref):
    i = pl.program_id(0)

    @pl.when(i < W_STEPS)
    def _():
        _convert_ffn(i, wg32_ref, wu32_ref, wd32_ref, wg_ref, wu_ref, wd_ref)
        for dst, src in ((wom_ref, wom32_ref), (wog_ref, wog32_ref), (wox_ref, wox32_ref)):
            _convert_rows(dst, src[...], i, dst.shape[0] // W_STEPS)
        _convert_rows(wout_ref, wout32_ref[...], i, D_MODEL // W_STEPS)

    @pl.when(i >= W_STEPS)
    def _():
        _post_tokens(h1_ref, om_ref, og_ref, ox_ref, nm_ref, wgz_ref, gb_ref, wom_ref, wog_ref, wox_ref,
                     wout_ref, n2_ref, wg_ref, wu_ref, wd_ref, nf_ref, o_ref)


def _post_tokens(h1_ref, om_ref, og_ref, ox_ref, nm_ref, wgz_ref, gb_ref, wom_ref, wog_ref, wox_ref,
                 wout_ref, n2_ref, wg_ref, wu_ref, wd_ref, nf_ref, o_ref):
    h1 = h1_ref[...]
    u = _rms(h1, nm_ref[...]).astype(BF16)
    d = D_MODEL
    merged = None
    for n, (o_br, w_br) in enumerate(((om_ref, wom_ref), (og_ref, wog_ref), (ox_ref, wox_ref))):
        gate = _sigmoid(_dot_nt(u, wgz_ref[n * d:(n + 1) * d, :]) + gb_ref[:, n * d:(n + 1) * d])
        o_in = jnp.concatenate([o_br[p] for p in range(o_br.shape[0])], axis=1) if n == 0 else o_br[...]
        term = gate * _dot(o_in, w_br[...])
        merged = term if merged is None else merged + term
    h2 = h1 + _dot(merged.astype(BF16), wout_ref[...])
    u2 = _rms(h2, n2_ref[...]).astype(BF16)
    h3 = h2 + 0.5 * _swiglu_half(u2, wg_ref, wu_ref, wd_ref)
    o_ref[...] = _rms(h3, nf_ref[...])


def _cparams(sem, vmem=None, flags=None):
    return pltpu.CompilerParams(dimension_semantics=sem, vmem_limit_bytes=vmem, flags=flags)


def kernel(x, mem, positions, ffn1_norm, ffn1_wg, ffn1_wu, ffn1_wd, mix_norm, mem_norm, w_in, gate_bias,
           mla_q_norm, mla_w_uq, mla_kv_norm, mla_w_ukv, mla_w_o, gla_w_a2, gla_b_a, gla_o_norm, gla_w_o,
           x_w_kv, x_w_o, w_out, ffn2_norm, ffn2_wg, ffn2_wu, ffn2_wd, final_norm):
    B, S, D = x.shape
    T = B * S
    M = mem.shape[1]
    F = D_FF
    H = MLA_HEADS
    bf = lambda a: a.astype(BF16)
    row = lambda a: a.reshape(1, -1)

    wt = jnp.swapaxes(w_in[0], 0, 1)
    zr = lambda n: jnp.zeros((n, D), F32)
    w_in_at = bf(jnp.concatenate(
        [wt[:C_KR], zr(MLA_NOPE), wt[C_KR:C_GQ], zr(HEAD_PAD - MLA_NOPE - MLA_ROPE),
         wt[C_GQ:C_GA], wt[C_GR:C_GZ], wt[C_GA:C_GR], zr(LANE - GLA_GATE_RANK)], axis=0))
    w_gzt = bf(wt[C_GZ:])
    w_uq = mla_w_uq[0].reshape(MLA_Q_RANK, H, MLA_NOPE + MLA_ROPE)
    w_uq = bf(jnp.pad(w_uq, ((0, 0), (0, 0), (0, HEAD_PAD - MLA_NOPE - MLA_ROPE))).reshape(MLA_Q_RANK, H * HEAD_PAD))
    w_ukv = mla_w_ukv[0].reshape(MLA_KV_RANK, H, MLA_NOPE + MLA_V)
    w_uk = bf(jnp.pad(w_ukv[:, :, :MLA_NOPE], ((0, 0), (0, 0), (0, HEAD_PAD - MLA_NOPE))).reshape(MLA_KV_RANK, H * HEAD_PAD))
    w_uv = bf(w_ukv[:, :, MLA_NOPE:].reshape(MLA_KV_RANK, H * MLA_V))
    w_a2 = bf(jnp.pad(gla_w_a2[0], ((0, LANE - GLA_GATE_RANK), (0, 0))))
    half = MLA_ROPE // 2
    inv_freq = ROPE_THETA ** (-jnp.arange(half, dtype=F32) / half)
    invf = jnp.concatenate([jnp.zeros((MLA_NOPE,), F32), inv_freq, inv_freq,
                            jnp.zeros((HEAD_PAD - MLA_NOPE - MLA_ROPE,), F32)]).reshape(1, HEAD_PAD)

    x2 = x.reshape(T, D)
    pos2 = jnp.broadcast_to(positions.reshape(T, 1), (T, HEAD_PAD))
    nt = T // TM
    tok = lambda w: pl.BlockSpec((TM, w), lambda i: (jnp.maximum(i - W_STEPS, 0), 0))
    npair = H // 2
    tok_pairs = lambda w: pl.BlockSpec((npair, TM, w), lambda i: (0, jnp.maximum(i - W_STEPS, 0), 0))
    wrows = lambda shape, every=1: pl.BlockSpec(
        (shape[0] * every // W_STEPS, shape[1]),
        lambda i: (jnp.minimum(i // every, W_STEPS // every - 1), 0))
    ffn_specs = [wrows((D, F)), wrows((D, F)), wrows((F, D), 2)]
    ffn_scratch = [pltpu.VMEM((D, F), BF16), pltpu.VMEM((D, F), BF16), pltpu.VMEM((F, D), BF16)]

    gdk, gdv, xw = GLA_HEADS * GLA_DK, GLA_HEADS * GLA_DV, X_HEADS * X_DH
    pre_out_shapes = (
        jax.ShapeDtypeStruct((T, D), F32),
        jax.ShapeDtypeStruct((npair, T, 2 * HEAD_PAD), BF16),
        jax.ShapeDtypeStruct((npair, T, 2 * HEAD_PAD), BF16),
        jax.ShapeDtypeStruct((npair, T, 2 * MLA_V), BF16),
        jax.ShapeDtypeStruct((T, gdk), F32),
        jax.ShapeDtypeStruct((T, gdk), F32),
        jax.ShapeDtypeStruct((T, gdv), BF16),
        jax.ShapeDtypeStruct((T, gdv), BF16),
        jax.ShapeDtypeStruct((T, xw), BF16),
        jax.ShapeDtypeStruct((T, gdk), F32),
    )
    h1, q, k, v, gq, gk, gv, gr, xq, la = pl.pallas_call(
        _pre_kernel,
        out_shape=pre_out_shapes,
        grid=(W_STEPS + nt,),
        in_specs=[tok(D), tok(HEAD_PAD), _const_spec((1, HEAD_PAD)), _const_spec((1, D)), _const_spec((1, D)),
                  *ffn_specs, _const_spec((A_END, D)),
                  _const_spec((1, MLA_Q_RANK)), _const_spec((MLA_Q_RANK, H * HEAD_PAD)),
                  _const_spec((1, MLA_KV_RANK)), _const_spec((MLA_KV_RANK, H * HEAD_PAD)),
                  _const_spec((MLA_KV_RANK, H * MLA_V)), _const_spec((LANE, gdk)), _const_spec((1, gdk))],
        out_specs=tuple(tok(s.shape[1]) if len(s.shape) == 2 else tok_pairs(s.shape[2])
                        for s in pre_out_shapes),
        scratch_shapes=ffn_scratch,
        compiler_params=_cparams(("arbitrary",), VMEM_LIMIT),
        name="pre",
    )(x2, pos2, invf, row(ffn1_norm[0]), row(mix_norm[0]), ffn1_wg[0], ffn1_wu[0], ffn1_wd[0],
      w_in_at, row(mla_q_norm[0]), w_uq, row(mla_kv_norm[0]), w_uk, w_uv, w_a2, row(gla_b_a[0]))

    seq_pairs = lambda w: pl.BlockSpec((npair, S, w), lambda b: (0, b, 0))
    o_mla = pl.pallas_call(
        _mla_kernel,
        out_shape=jax.ShapeDtypeStruct((npair, T, 2 * MLA_V), BF16),
        grid=(B,),
        in_specs=[seq_pairs(2 * HEAD_PAD), seq_pairs(2 * HEAD_PAD), seq_pairs(2 * MLA_V)],
        out_specs=seq_pairs(2 * MLA_V),
        scratch_shapes=[pltpu.VMEM((2 * MLA_VT_ROWS, S), BF16), pltpu.VMEM((MLA_LOOKAHEAD + 1, S, TQ), F32)],
        compiler_params=_cparams(("parallel",), VMEM_LIMIT),
        name="mla",
    )(q, k, v)

    nqx = S // TQX
    o_x = pl.pallas_call(
        _xattn_kernel,
        out_shape=jax.ShapeDtypeStruct((T, xw), BF16),
        grid=(B, nqx),
        in_specs=[pl.BlockSpec((TQX, xw), lambda b, i: (b * nqx + i, 0)),
                  pl.BlockSpec((M, D), lambda b, i: (b, 0)),
                  _const_spec((1, D)), _const_spec((D, 2 * xw))],
        out_specs=pl.BlockSpec((TQX, xw), lambda b, i: (b * nqx + i, 0)),
        scratch_shapes=[pltpu.VMEM((M, 2 * xw), BF16)],
        compiler_params=_cparams(("arbitrary", "arbitrary")),
        name="xattn",
    )(xq, mem.reshape(B * M, D), row(mem_norm[0]), bf(x_w_kv[0]))

    o_gla = pl.pallas_call(
        _gla_kernel,
        out_shape=jax.ShapeDtypeStruct((T, gdv), BF16),
        grid=(B,),
        in_specs=[pl.BlockSpec((S, gdk), lambda b: (b, 0)),
                  pl.BlockSpec((S, gdk), lambda b: (b, 0)),
                  pl.BlockSpec((S, gdk), lambda b: (b, 0)),
                  pl.BlockSpec((S, gdv), lambda b: (b, 0)),
                  pl.BlockSpec((S, gdv), lambda b: (b, 0)),
                  _const_spec((1, GLA_DV))],
        out_specs=pl.BlockSpec((S, gdv), lambda b: (b, 0)),
        compiler_params=_cparams(("parallel",), VMEM_LIMIT),
        name="gla",
    )(gq, gk, la, gv, gr, row(gla_o_norm[0]))

    out = pl.pallas_call(
        _post_kernel,
        out_shape=jax.ShapeDtypeStruct((T, D), F32),
        grid=(W_STEPS + nt,),
        in_specs=[tok(D), tok_pairs(2 * MLA_V), tok(gdv), tok(xw), _const_spec((1, D)),
                  _const_spec((N_BRANCH * D, D)), _const_spec((1, N_BRANCH * D)),
                  wrows((H * MLA_V, D)), wrows((gdv, D)), wrows((xw, D)),
                  wrows((D, D)), _const_spec((1, D)),
                  *ffn_specs, _const_spec((1, D))],
        out_specs=tok(D),
        scratch_shapes=[pltpu.VMEM((H * MLA_V, D), BF16),
                        pltpu.VMEM((gdv, D), BF16), pltpu.VMEM((xw, D), BF16), pltpu.VMEM((D, D), BF16),
                        *ffn_scratch],
        compiler_params=_cparams(("arbitrary",), VMEM_LIMIT),
        name="post",
    )(h1, o_mla, o_gla, o_x, row(mix_norm[0]), w_gzt, gate_bias[0].reshape(1, N_BRANCH * D),
      mla_w_o[0], gla_w_o[0], x_w_o[0], w_out[0], row(ffn2_norm[0]),
      ffn2_wg[0], ffn2_wu[0], ffn2_wd[0], row(final_norm))
    return out.reshape(B, S, D)
```

```python
import functools
import math

import jax
import jax.numpy as jnp
from jax import lax
from jax.experimental import pallas as pl
from jax.experimental.pallas import tpu as pltpu

F32 = jnp.float32
BF16 = jnp.bfloat16

D_MODEL = 1024
N_MEM = 256
EPS = 1e-6
MLA_HEADS = 8
MLA_NOPE = 64
MLA_ROPE = 32
MLA_V = 64
MLA_Q_RANK = 384
MLA_KV_RANK = 256
ROPE_THETA = 10000.0
GLA_HEADS = 4
GLA_DK = 64
GLA_DV = 128
GLA_GATE_RANK = 16
GLA_TAU = 16.0
GLA_CHUNK = 64
X_HEADS = 4
X_DH = 128
D_FF = 2816
N_BRANCH = 3

LANE = 128
HEAD_PAD = 128
VMEM_LIMIT = 60 * 1024 * 1024
MLA_EXP2_SCALE = math.log2(math.e) / math.sqrt(MLA_NOPE + MLA_ROPE)
MLA_VT_ROWS = MLA_V + 16
MLA_LOOKAHEAD = 3

C_CQ, C_CKV, C_KR, C_GQ, C_GK, C_GV, C_GA, C_GR, C_XQ, C_GZ, C_END = (
    0, 384, 640, 672, 928, 1184, 1696, 1712, 2224, 2736, 5808)
W_STEPS = 16
A_CQ, A_CKV, A_KR, A_GQ, A_GK, A_GV, A_GR, A_XQ, A_GA, A_END = (
    0, 384, 640, 768, 1024, 1280, 1792, 2304, 2816, 2944)

TM = 512
MERGE_TILE = 256
TQ = 256
TQX = 2048
GLA_SB = 256


def _dot(a, b):
    return jnp.dot(a, b, preferred_element_type=F32)


def _dot_nt(a, b):
    return lax.dot_general(a, b, (((1,), (1,)), ((), ())), preferred_element_type=F32)


def _dot_tn(a, b):
    return lax.dot_general(a, b, (((0,), (0,)), ((), ())), preferred_element_type=F32)


def _rms(x, g):
    return x * lax.rsqrt(jnp.mean(x * x, axis=-1, keepdims=True) + EPS) * g


def _sigmoid(x):
    return 1.0 / (1.0 + jnp.exp(-x))


def _silu(x):
    return x * _sigmoid(x)


def _const_spec(shape):
    return pl.BlockSpec(shape, lambda *_: (0,) * len(shape), pipeline_mode=pl.Buffered(1))


def _swiglu_half(xn_bf16, wgu_ref, wd_ref):
    tiles = []
    for f0 in range(0, 2 * D_FF, 2 * LANE):
        gu = _dot(xn_bf16, wgu_ref[:, f0:f0 + 2 * LANE])
        tiles.append((_silu(gu[:, :LANE]) * gu[:, LANE:]).astype(BF16))
    return _dot(jnp.concatenate(tiles, axis=1), wd_ref[...])


def _convert_rows(dst_ref, src, step, rows):
    r0 = pl.multiple_of(step * rows, rows)
    dst_ref[pl.ds(r0, rows), :] = src.astype(BF16)


def _convert_ffn(i, wg32_ref, wu32_ref, wd32_ref, wgu_ref, wd_ref):
    rows = D_MODEL // W_STEPS
    r0 = pl.multiple_of(i * rows, rows)
    wg = wg32_ref[...].astype(BF16)
    wu = wu32_ref[...].astype(BF16)
    for j in range(D_FF // LANE):
        wgu_ref[pl.ds(r0, rows), 2 * j * LANE:(2 * j + 1) * LANE] = wg[:, j * LANE:(j + 1) * LANE]
        wgu_ref[pl.ds(r0, rows), (2 * j + 1) * LANE:(2 * j + 2) * LANE] = wu[:, j * LANE:(j + 1) * LANE]

    @pl.when(i % 2 == 0)
    def _():
        _convert_rows(wd_ref, wd32_ref[...], i // 2, D_FF // (W_STEPS // 2))


def _pre_kernel(x_ref, pos_ref, invf_ref, n1_ref, nm_ref, wg32_ref, wu32_ref, wd32_ref, win_ref,
                qn_ref, wuq_ref, kvn_ref, wuk_ref, wuv_ref, wa2_ref, ba_ref,
                h1_ref, q_ref, k_ref, v_ref, gq_ref, gk_ref, gv_ref, gr_ref, xq_ref, la_ref,
                wgu_ref, wd_ref):
    i = pl.program_id(0)

    @pl.when(i < W_STEPS)
    def _():
        _convert_ffn(i, wg32_ref, wu32_ref, wd32_ref, wgu_ref, wd_ref)

    @pl.when(i >= W_STEPS)
    def _():
        _pre_tokens(x_ref, pos_ref, invf_ref, n1_ref, nm_ref, wgu_ref, wd_ref, win_ref,
                    qn_ref, wuq_ref, kvn_ref, wuk_ref, wuv_ref, wa2_ref, ba_ref,
                    h1_ref, q_ref, k_ref, v_ref, gq_ref, gk_ref, gv_ref, gr_ref, xq_ref, la_ref)


def _pre_tokens(x_ref, pos_ref, invf_ref, n1_ref, nm_ref, wgu_ref, wd_ref, win_ref,
                qn_ref, wuq_ref, kvn_ref, wuk_ref, wuv_ref, wa2_ref, ba_ref,
                h1_ref, q_ref, k_ref, v_ref, gq_ref, gk_ref, gv_ref, gr_ref, xq_ref, la_ref):
    x = x_ref[...]
    u1 = _rms(x, n1_ref[...]).astype(BF16)
    h1 = x + 0.5 * _swiglu_half(u1, wgu_ref, wd_ref)
    h1_ref[...] = h1
    u2 = _rms(h1, nm_ref[...]).astype(BF16)

    ang = pos_ref[...].astype(F32) * invf_ref[...]
    cos = jnp.cos(ang)
    sin = jnp.sin(ang)
    lane = lax.broadcasted_iota(jnp.int32, ang.shape, 1)
    sin_hi = jnp.where(lane >= 80, sin, 0.0)
    sin_lo = jnp.where(lane < 80, -sin, 0.0)

    def rope(t):
        return t * cos + pltpu.roll(t, 16, 1) * sin_hi + pltpu.roll(t, LANE - 16, 1) * sin_lo

    cqn = _rms(_dot_nt(u2, win_ref[A_CQ:A_CKV, :]), qn_ref[...]).astype(BF16)
    for p in range(MLA_HEADS // 2):
        qp = _dot(cqn, wuq_ref[:, 2 * p * HEAD_PAD:(2 * p + 2) * HEAD_PAD])
        for j in range(2):
            h = 2 * p + j
            q_ref[p, :, j * HEAD_PAD:(j + 1) * HEAD_PAD] = (
                rope(qp[:, j * HEAD_PAD:(j + 1) * HEAD_PAD]) * MLA_EXP2_SCALE).astype(BF16)

    ckvn = _rms(_dot_nt(u2, win_ref[A_CKV:A_KR, :]), kvn_ref[...]).astype(BF16)
    kr = rope(_dot_nt(u2, win_ref[A_KR:A_GQ, :]))
    kr2 = jnp.concatenate([kr, kr], axis=1)
    for p in range(MLA_HEADS // 2):
        kn = _dot(ckvn, wuk_ref[:, 2 * p * HEAD_PAD:(2 * p + 2) * HEAD_PAD])
        k_ref[p] = (kn + kr2).astype(BF16)
    vv = _dot(ckvn, wuv_ref[...]).astype(BF16)
    for p in range(MLA_HEADS // 2):
        v_ref[p] = vv[:, 2 * p * MLA_V:(2 * p + 2) * MLA_V]

    gq_ref[...] = _dot_nt(u2, win_ref[A_GQ:A_GK, :])
    gk_ref[...] = _dot_nt(u2, win_ref[A_GK:A_GV, :])
    gv_ref[...] = _dot_nt(u2, win_ref[A_GV:A_GR, :]).astype(BF16)
    gr_ref[...] = _dot_nt(u2, win_ref[A_GR:A_XQ, :]).astype(BF16)
    xq_ref[...] = _dot_nt(u2, win_ref[A_XQ:A_GA, :]).astype(BF16)
    ga = _dot_nt(u2, win_ref[A_GA:A_END, :]).astype(BF16)
    t = _dot(ga, wa2_ref[...]) + ba_ref[...]
    log_sig = jnp.minimum(t, 0.0) - jnp.log(1.0 + jnp.exp(-jnp.abs(t)))
    la_ref[...] = log_sig * (1.0 / GLA_TAU)


def _mla_kernel(q_ref, k_ref, v_ref, o_ref, vt_ref, s_ref):
    def pair(p, carry):
        _mla_pair(q_ref.at[p], k_ref.at[p], v_ref.at[p], o_ref.at[p], vt_ref, s_ref)
        return carry

    lax.fori_loop(0, q_ref.shape[0], pair, 0)


def _mla_pair(q_ref, k_ref, v_ref, o_ref, vt_ref, s_ref):
    tq = TQ
    nq = q_ref.shape[0] // tq
    key = lax.broadcasted_iota(jnp.int32, (tq, tq), 0)
    qry = lax.broadcasted_iota(jnp.int32, (tq, tq), 1)
    causal = key <= qry
    vt = v_ref[...].astype(F32).T.astype(BF16)
    ones_row = jnp.where(lax.broadcasted_iota(jnp.int32, (MLA_VT_ROWS - MLA_V, vt.shape[1]), 0) == 0,
                         1.0, 0.0).astype(BF16)
    for h in range(2):
        vt_ref[h * MLA_VT_ROWS:h * MLA_VT_ROWS + MLA_V, :] = vt[h * MLA_V:(h + 1) * MLA_V, :]
        vt_ref[h * MLA_VT_ROWS + MLA_V:(h + 1) * MLA_VT_ROWS, :] = ones_row

    def scores(qi, h, slot):
        q = q_ref[qi * tq:(qi + 1) * tq, h * HEAD_PAD:(h + 1) * HEAD_PAD]
        m = None
        for c in range(qi + 1):
            s = _dot_nt(k_ref[c * tq:(c + 1) * tq, h * HEAD_PAD:(h + 1) * HEAD_PAD], q)
            if c == qi:
                s = jnp.where(causal, s, -1e30)
            s_ref[slot, c * tq:(c + 1) * tq, :] = s
            cm = jnp.max(s, axis=0, keepdims=True)
            m = cm if m is None else jnp.maximum(m, cm)
        return m

    def values(qi, h, slot, m):
        acc = None
        for c in range(qi + 1):
            p = jnp.exp2(s_ref[slot, c * tq:(c + 1) * tq, :] - m).astype(BF16)
            pv = _dot(vt_ref[h * MLA_VT_ROWS:(h + 1) * MLA_VT_ROWS, c * tq:(c + 1) * tq], p)
            acc = pv if acc is None else acc + pv
        return acc[:MLA_V, :] / acc[MLA_V:MLA_V + 1, :]

    groups = [(qi, h) for qi in range(nq) for h in range(2)]
    nslot = s_ref.shape[0]
    maxes = {}
    for i in range(min(MLA_LOOKAHEAD, len(groups))):
        maxes[i] = scores(*groups[i], i % nslot)
    heads = {}
    for i, (qi, h) in enumerate(groups):
        j = i + MLA_LOOKAHEAD
        if j < len(groups):
            maxes[j] = scores(*groups[j], j % nslot)
        heads[h] = values(qi, h, i % nslot, maxes.pop(i))
        if h == 1:
            o_ref[qi * tq:(qi + 1) * tq, :] = jnp.concatenate([heads[0], heads[1]], axis=0).T.astype(BF16)


def _xattn_kernel(q_ref, mem_ref, n_ref, w_ref, o_ref, kv_ref):
    @pl.when(pl.program_id(1) == 0)
    def _():
        mn = _rms(mem_ref[...], n_ref[...]).astype(BF16)
        kv_ref[...] = _dot(mn, w_ref[...]).astype(BF16)

    scale = 1.0 / math.sqrt(X_DH)
    hw = X_HEADS * X_DH
    for h in range(X_HEADS):
        q = q_ref[:, h * X_DH:(h + 1) * X_DH]
        k = kv_ref[:, h * X_DH:(h + 1) * X_DH]
        v = kv_ref[:, hw + h * X_DH:hw + (h + 1) * X_DH]
        s = _dot_nt(q, k) * scale
        m = jnp.max(s, axis=-1, keepdims=True)
        p = jnp.exp(s - m)
        l = jnp.sum(p, axis=-1, keepdims=True)
        o = _dot(p.astype(BF16), v) / l
        o_ref[:, h * X_DH:(h + 1) * X_DH] = o.astype(BF16)


def _gla_kernel(q_ref, k_ref, la_ref, v_ref, r_ref, on_ref, o_ref):
    sb, c = GLA_SB, GLA_CHUNK
    nsb = q_ref.shape[0] // sb
    nch = sb // c
    npair = GLA_HEADS // 2
    pw, vw = 2 * GLA_DK, 2 * GLA_DV
    row = lax.broadcasted_iota(jnp.int32, (sb, sb), 0)
    col = lax.broadcasted_iota(jnp.int32, (sb, sb), 1)
    tri = jnp.logical_and(row // c == col // c, col <= row)
    tri_bf = jnp.where(tri, 1.0, 0.0).astype(BF16)
    lane = lax.broadcasted_iota(jnp.int32, (1, pw), 1)
    head_lanes = (lane < GLA_DK, lane >= GLA_DK)
    srow = lax.broadcasted_iota(jnp.int32, (pw, vw), 0)
    scol = lax.broadcasted_iota(jnp.int32, (pw, vw), 1)
    same_head = (srow < GLA_DK) == (scol < GLA_DV)

    def local(i):
        r0 = i * sb
        la = la_ref[r0:r0 + sb, :]
        hi = la.astype(BF16)
        lo = (la - hi.astype(F32)).astype(BF16)
        bcum_all = _dot(tri_bf, hi) + _dot(tri_bf, lo)
        out = []
        for p in range(npair):
            bcum = bcum_all[:, p * pw:(p + 1) * pw]
            b_last = [bcum[j * c + c - 1:j * c + c, :] for j in range(nch)]
            b_last_rows = jnp.concatenate([jnp.broadcast_to(b, (c, pw)) for b in b_last], axis=0)
            q_t = q_ref[r0:r0 + sb, p * pw:(p + 1) * pw] * (GLA_DK ** -0.5) * jnp.exp(bcum)
            kk = k_ref[r0:r0 + sb, p * pw:(p + 1) * pw]
            k_t = (kk * jnp.exp(-bcum)).astype(BF16)
            k_d = (kk * jnp.exp(b_last_rows - bcum)).astype(BF16)
            v = v_ref[r0:r0 + sb, p * vw:(p + 1) * vw]
            o_intra = []
            for h in range(2):
                qm = jnp.where(head_lanes[h], q_t, 0.0).astype(BF16)
                att = jnp.where(tri, _dot_nt(qm, k_t), 0.0).astype(BF16)
                o_intra.append(_dot(att, v[:, h * GLA_DV:(h + 1) * GLA_DV]))
            d_st, decay = [], []
            for j in range(nch):
                d = _dot_tn(k_d[j * c:(j + 1) * c, :], v[j * c:(j + 1) * c, :])
                d_st.append(jnp.where(same_head, d, 0.0))
                dcol = jnp.broadcast_to(jnp.exp(b_last[j]), (pw, pw)).T
                decay.append(jnp.concatenate([dcol, dcol], axis=1))
            out.append((q_t.astype(BF16), o_intra, d_st, decay))
        return out

    def recurrent(i, loc, states):
        r0 = i * sb
        for p in range(npair):
            q_bf, o_intra, d_st, decay = loc[p]
            st = states[p]
            o_inter = []
            for j in range(nch):
                o_inter.append(_dot(q_bf[j * c:(j + 1) * c, :], st.astype(BF16)))
                st = decay[j] * st + d_st[j]
            states[p] = st
            o_inter = jnp.concatenate(o_inter, axis=0)
            for h in range(2):
                lo_, hi_ = p * vw + h * GLA_DV, p * vw + (h + 1) * GLA_DV
                o = _rms(o_intra[h] + o_inter[:, h * GLA_DV:(h + 1) * GLA_DV], on_ref[...])
                r = r_ref[r0:r0 + sb, lo_:hi_].astype(F32)
                o_ref[r0:r0 + sb, lo_:hi_] = (o * _silu(r)).astype(BF16)

    states = [jnp.zeros((pw, vw), F32) for _ in range(npair)]
    loc = local(0)
    for i in range(nsb):
        nxt = local(i + 1) if i + 1 < nsb else None
        recurrent(i, loc, states)
        loc = nxt


def _post_kernel(h1_ref, om_ref, og_ref, ox_ref, nm_ref, wgz_ref, gb_ref, wom32_ref, wog32_ref, wox32_ref,
                 wout32_ref, n2_ref, wg32_ref, wu32_ref, wd32_ref, nf_ref, o_ref,
                 wom_ref, wog_ref, wox_ref, wout_ref, wgu_ref, wd_ref):
    i = pl.program_id(0)

    @pl.when(i < W_STEPS)
    def _():
        _convert_ffn(i, wg32_ref, wu32_ref, wd32_ref, wgu_ref, wd_ref)
        for dst, src in ((wom_ref, wom32_ref), (wog_ref, wog32_ref), (wox_ref, wox32_ref)):
            _convert_rows(dst, src[...], i, dst.shape[0] // W_STEPS)
        _convert_rows(wout_ref, wout32_ref[...], i, D_MODEL // W_STEPS)

    @pl.when(i >= W_STEPS)
    def _():
        _post_tokens(h1_ref, om_ref, og_ref, ox_ref, nm_ref, wgz_ref, gb_ref, wom_ref, wog_ref, wox_ref,
                     wout_ref, n2_ref, wgu_ref, wd_ref, nf_ref, o_ref)


def _post_tokens(h1_ref, om_ref, og_ref, ox_ref, nm_ref, wgz_ref, gb_ref, wom_ref, wog_ref, wox_ref,
                 wout_ref, n2_ref, wgu_ref, wd_ref, nf_ref, o_ref):
    h1 = h1_ref[...]
    u = _rms(h1, nm_ref[...]).astype(BF16)
    d = D_MODEL
    branches = ((jnp.concatenate([om_ref[p] for p in range(om_ref.shape[0])], axis=1), wom_ref),
                (og_ref[...], wog_ref), (ox_ref[...], wox_ref))
    tiles = []
    for c0 in range(0, d, MERGE_TILE):
        acc = None
        for n, (o_in, w_br) in enumerate(branches):
            gate = _sigmoid(_dot_nt(u, wgz_ref[n * d + c0:n * d + c0 + MERGE_TILE, :])
                            + gb_ref[:, n * d + c0:n * d + c0 + MERGE_TILE])
            term = gate * _dot(o_in, w_br[:, c0:c0 + MERGE_TILE])
            acc = term if acc is None else acc + term
        tiles.append(acc.astype(BF16))
    h2 = h1 + _dot(jnp.concatenate(tiles, axis=1), wout_ref[...])
    u2 = _rms(h2, n2_ref[...]).astype(BF16)
    h3 = h2 + 0.5 * _swiglu_half(u2, wgu_ref, wd_ref)
    o_ref[...] = _rms(h3, nf_ref[...])


def _cparams(sem, vmem=None, flags=None):
    return pltpu.CompilerParams(dimension_semantics=sem, vmem_limit_bytes=vmem, flags=flags)


def kernel(x, mem, positions, ffn1_norm, ffn1_wg, ffn1_wu, ffn1_wd, mix_norm, mem_norm, w_in, gate_bias,
           mla_q_norm, mla_w_uq, mla_kv_norm, mla_w_ukv, mla_w_o, gla_w_a2, gla_b_a, gla_o_norm, gla_w_o,
           x_w_kv, x_w_o, w_out, ffn2_norm, ffn2_wg, ffn2_wu, ffn2_wd, final_norm):
    B, S, D = x.shape
    T = B * S
    M = mem.shape[1]
    F = D_FF
    H = MLA_HEADS
    bf = lambda a: a.astype(BF16)
    row = lambda a: a.reshape(1, -1)

    wt = jnp.swapaxes(w_in[0], 0, 1)
    zr = lambda n: jnp.zeros((n, D), F32)
    w_in_at = bf(jnp.concatenate(
        [wt[:C_KR], zr(MLA_NOPE), wt[C_KR:C_GQ], zr(HEAD_PAD - MLA_NOPE - MLA_ROPE),
         wt[C_GQ:C_GA], wt[C_GR:C_GZ], wt[C_GA:C_GR], zr(LANE - GLA_GATE_RANK)], axis=0))
    w_gzt = bf(wt[C_GZ:])
    w_uq = mla_w_uq[0].reshape(MLA_Q_RANK, H, MLA_NOPE + MLA_ROPE)
    w_uq = bf(jnp.pad(w_uq, ((0, 0), (0, 0), (0, HEAD_PAD - MLA_NOPE - MLA_ROPE))).reshape(MLA_Q_RANK, H * HEAD_PAD))
    w_ukv = mla_w_ukv[0].reshape(MLA_KV_RANK, H, MLA_NOPE + MLA_V)
    w_uk = bf(jnp.pad(w_ukv[:, :, :MLA_NOPE], ((0, 0), (0, 0), (0, HEAD_PAD - MLA_NOPE))).reshape(MLA_KV_RANK, H * HEAD_PAD))
    w_uv = bf(w_ukv[:, :, MLA_NOPE:].reshape(MLA_KV_RANK, H * MLA_V))
    w_a2 = bf(jnp.pad(gla_w_a2[0], ((0, LANE - GLA_GATE_RANK), (0, 0))))
    half = MLA_ROPE // 2
    inv_freq = ROPE_THETA ** (-jnp.arange(half, dtype=F32) / half)
    invf = jnp.concatenate([jnp.zeros((MLA_NOPE,), F32), inv_freq, inv_freq,
                            jnp.zeros((HEAD_PAD - MLA_NOPE - MLA_ROPE,), F32)]).reshape(1, HEAD_PAD)

    x2 = x.reshape(T, D)
    pos2 = jnp.broadcast_to(positions.reshape(T, 1), (T, HEAD_PAD))
    nt = T // TM
    tok = lambda w: pl.BlockSpec((TM, w), lambda i: (jnp.maximum(i - W_STEPS, 0), 0))
    npair = H // 2
    tok_pairs = lambda w: pl.BlockSpec((npair, TM, w), lambda i: (0, jnp.maximum(i - W_STEPS, 0), 0))
    wrows = lambda shape, every=1: pl.BlockSpec(
        (shape[0] * every // W_STEPS, shape[1]),
        lambda i: (jnp.minimum(i // every, W_STEPS // every - 1), 0))
    ffn_specs = [wrows((D, F)), wrows((D, F)), wrows((F, D), 2)]
    ffn_scratch = [pltpu.VMEM((D, 2 * F), BF16), pltpu.VMEM((F, D), BF16)]

    gdk, gdv, xw = GLA_HEADS * GLA_DK, GLA_HEADS * GLA_DV, X_HEADS * X_DH
    pre_out_shapes = (
        jax.ShapeDtypeStruct((T, D), F32),
        jax.ShapeDtypeStruct((npair, T, 2 * HEAD_PAD), BF16),
        jax.ShapeDtypeStruct((npair, T, 2 * HEAD_PAD), BF16),
        jax.ShapeDtypeStruct((npair, T, 2 * MLA_V), BF16),
        jax.ShapeDtypeStruct((T, gdk), F32),
        jax.ShapeDtypeStruct((T, gdk), F32),
        jax.ShapeDtypeStruct((T, gdv), BF16),
        jax.ShapeDtypeStruct((T, gdv), BF16),
        jax.ShapeDtypeStruct((T, xw), BF16),
        jax.ShapeDtypeStruct((T, gdk), F32),
    )
    h1, q, k, v, gq, gk, gv, gr, xq, la = pl.pallas_call(
        _pre_kernel,
        out_shape=pre_out_shapes,
        grid=(W_STEPS + nt,),
        in_specs=[tok(D), tok(HEAD_PAD), _const_spec((1, HEAD_PAD)), _const_spec((1, D)), _const_spec((1, D)),
                  *ffn_specs, _const_spec((A_END, D)),
                  _const_spec((1, MLA_Q_RANK)), _const_spec((MLA_Q_RANK, H * HEAD_PAD)),
                  _const_spec((1, MLA_KV_RANK)), _const_spec((MLA_KV_RANK, H * HEAD_PAD)),
                  _const_spec((MLA_KV_RANK, H * MLA_V)), _const_spec((LANE, gdk)), _const_spec((1, gdk))],
        out_specs=tuple(tok(s.shape[1]) if len(s.shape) == 2 else tok_pairs(s.shape[2])
                        for s in pre_out_shapes),
        scratch_shapes=ffn_scratch,
        compiler_params=_cparams(("arbitrary",), VMEM_LIMIT),
        name="pre",
    )(x2, pos2, invf, row(ffn1_norm[0]), row(mix_norm[0]), ffn1_wg[0], ffn1_wu[0], ffn1_wd[0],
      w_in_at, row(mla_q_norm[0]), w_uq, row(mla_kv_norm[0]), w_uk, w_uv, w_a2, row(gla_b_a[0]))

    seq_pairs = lambda w: pl.BlockSpec((npair, S, w), lambda b: (0, b, 0))
    o_mla = pl.pallas_call(
        _mla_kernel,
        out_shape=jax.ShapeDtypeStruct((npair, T, 2 * MLA_V), BF16),
        grid=(B,),
        in_specs=[seq_pairs(2 * HEAD_PAD), seq_pairs(2 * HEAD_PAD), seq_pairs(2 * MLA_V)],
        out_specs=seq_pairs(2 * MLA_V),
        scratch_shapes=[pltpu.VMEM((2 * MLA_VT_ROWS, S), BF16), pltpu.VMEM((MLA_LOOKAHEAD + 1, S, TQ), F32)],
        compiler_params=_cparams(("parallel",), VMEM_LIMIT),
        name="mla",
    )(q, k, v)

    nqx = S // TQX
    o_x = pl.pallas_call(
        _xattn_kernel,
        out_shape=jax.ShapeDtypeStruct((T, xw), BF16),
        grid=(B, nqx),
        in_specs=[pl.BlockSpec((TQX, xw), lambda b, i: (b * nqx + i, 0)),
                  pl.BlockSpec((M, D), lambda b, i: (b, 0)),
                  _const_spec((1, D)), _const_spec((D, 2 * xw))],
        out_specs=pl.BlockSpec((TQX, xw), lambda b, i: (b * nqx + i, 0)),
        scratch_shapes=[pltpu.VMEM((M, 2 * xw), BF16)],
        compiler_params=_cparams(("arbitrary", "arbitrary")),
        name="xattn",
    )(xq, mem.reshape(B * M, D), row(mem_norm[0]), bf(x_w_kv[0]))

    o_gla = pl.pallas_call(
        _gla_kernel,
        out_shape=jax.ShapeDtypeStruct((T, gdv), BF16),
        grid=(B,),
        in_specs=[pl.BlockSpec((S, gdk), lambda b: (b, 0)),
                  pl.BlockSpec((S, gdk), lambda b: (b, 0)),
                  pl.BlockSpec((S, gdk), lambda b: (b, 0)),
                  pl.BlockSpec((S, gdv), lambda b: (b, 0)),
                  pl.BlockSpec((S, gdv), lambda b: (b, 0)),
                  _const_spec((1, GLA_DV))],
        out_specs=pl.BlockSpec((S, gdv), lambda b: (b, 0)),
        compiler_params=_cparams(("parallel",), VMEM_LIMIT),
        name="gla",
    )(gq, gk, la, gv, gr, row(gla_o_norm[0]))

    out = pl.pallas_call(
        _post_kernel,
        out_shape=jax.ShapeDtypeStruct((T, D), F32),
        grid=(W_STEPS + nt,),
        in_specs=[tok(D), tok_pairs(2 * MLA_V), tok(gdv), tok(xw), _const_spec((1, D)),
                  _const_spec((N_BRANCH * D, D)), _const_spec((1, N_BRANCH * D)),
                  wrows((H * MLA_V, D)), wrows((gdv, D)), wrows((xw, D)),
                  wrows((D, D)), _const_spec((1, D)),
                  *ffn_specs, _const_spec((1, D))],
        out_specs=tok(D),
        scratch_shapes=[pltpu.VMEM((H * MLA_V, D), BF16),
                        pltpu.VMEM((gdv, D), BF16), pltpu.VMEM((xw, D), BF16), pltpu.VMEM((D, D), BF16),
                        *ffn_scratch],
        compiler_params=_cparams(("arbitrary",), VMEM_LIMIT),
        name="post",
    )(h1, o_mla, o_gla, o_x, row(mix_norm[0]), w_gzt, gate_bias[0].reshape(1, N_BRANCH * D),
      mla_w_o[0], gla_w_o[0], x_w_o[0], w_out[0], row(ffn2_norm[0]),
      ffn2_wg[0], ffn2_wu[0], ffn2_wd[0], row(final_norm))
    return out.reshape(B, S, D)
```

```python
import functools
import math

import jax
import jax.numpy as jnp
from jax import lax
from jax.experimental import pallas as pl
from jax.experimental.pallas import tpu as pltpu

F32 = jnp.float32
BF16 = jnp.bfloat16

D_MODEL = 1024
N_MEM = 256
EPS = 1e-6
MLA_HEADS = 8
MLA_NOPE = 64
MLA_ROPE = 32
MLA_V = 64
MLA_Q_RANK = 384
MLA_KV_RANK = 256
ROPE_THETA = 10000.0
GLA_HEADS = 4
GLA_DK = 64
GLA_DV = 128
GLA_GATE_RANK = 16
GLA_TAU = 16.0
GLA_CHUNK = 64
X_HEADS = 4
X_DH = 128
D_FF = 2816
N_BRANCH = 3

LANE = 128
HEAD_PAD = 128
VMEM_LIMIT = 60 * 1024 * 1024
MLA_EXP2_SCALE = math.log2(math.e) / math.sqrt(MLA_NOPE + MLA_ROPE)
MLA_VT_ROWS = MLA_V + 16
MLA_LOOKAHEAD = 3

C_CQ, C_CKV, C_KR, C_GQ, C_GK, C_GV, C_GA, C_GR, C_XQ, C_GZ, C_END = (
    0, 384, 640, 672, 928, 1184, 1696, 1712, 2224, 2736, 5808)
W_STEPS = 16

TM = 512
MERGE_TILE = 256
TQ = 256
TQX = 2048
GLA_SB = 256


def _dot(a, b):
    return jnp.dot(a, b, preferred_element_type=F32)


def _dot_nt(a, b):
    return lax.dot_general(a, b, (((1,), (1,)), ((), ())), preferred_element_type=F32)


def _dot_tn(a, b):
    return lax.dot_general(a, b, (((0,), (0,)), ((), ())), preferred_element_type=F32)


def _rms(x, g):
    return x * lax.rsqrt(jnp.mean(x * x, axis=-1, keepdims=True) + EPS) * g


def _sigmoid(x):
    return 1.0 / (1.0 + jnp.exp(-x))


def _silu(x):
    return x * _sigmoid(x)


def _const_spec(shape):
    return pl.BlockSpec(shape, lambda *_: (0,) * len(shape), pipeline_mode=pl.Buffered(1))


def _swiglu_half(xn_bf16, wgu_ref, wd_ref):
    tiles = []
    for f0 in range(0, 2 * D_FF, 2 * LANE):
        gu = _dot(xn_bf16, wgu_ref[:, f0:f0 + 2 * LANE])
        tiles.append((_silu(gu[:, :LANE]) * gu[:, LANE:]).astype(BF16))
    return _dot(jnp.concatenate(tiles, axis=1), wd_ref[...])


def _convert_rows(dst_ref, src, step, rows):
    r0 = pl.multiple_of(step * rows, rows)
    dst_ref[pl.ds(r0, rows), :] = src.astype(BF16)


def _convert_ffn(i, wg32_ref, wu32_ref, wd32_ref, wgu_ref, wd_ref):
    rows = D_MODEL // W_STEPS
    r0 = pl.multiple_of(i * rows, rows)
    wg = wg32_ref[...].astype(BF16)
    wu = wu32_ref[...].astype(BF16)
    for j in range(D_FF // LANE):
        wgu_ref[pl.ds(r0, rows), 2 * j * LANE:(2 * j + 1) * LANE] = wg[:, j * LANE:(j + 1) * LANE]
        wgu_ref[pl.ds(r0, rows), (2 * j + 1) * LANE:(2 * j + 2) * LANE] = wu[:, j * LANE:(j + 1) * LANE]

    @pl.when(i % 2 == 0)
    def _():
        _convert_rows(wd_ref, wd32_ref[...], i // 2, D_FF // (W_STEPS // 2))


def _pre_kernel(x_ref, pos_ref, invf_ref, n1_ref, nm_ref, wg32_ref, wu32_ref, wd32_ref, win_ref,
                qn_ref, wuq_ref, kvn_ref, wuk_ref, wuv_ref, wa2_ref, ba_ref,
                h1_ref, q_ref, k_ref, v_ref, gq_ref, gk_ref, gv_ref, gr_ref, xq_ref, la_ref,
                wgu_ref, wd_ref):
    i = pl.program_id(0)

    @pl.when(i < W_STEPS)
    def _():
        _convert_ffn(i, wg32_ref, wu32_ref, wd32_ref, wgu_ref, wd_ref)

    @pl.when(i >= W_STEPS)
    def _():
        _pre_tokens(x_ref, pos_ref, invf_ref, n1_ref, nm_ref, wgu_ref, wd_ref, win_ref,
                    qn_ref, wuq_ref, kvn_ref, wuk_ref, wuv_ref, wa2_ref, ba_ref,
                    h1_ref, q_ref, k_ref, v_ref, gq_ref, gk_ref, gv_ref, gr_ref, xq_ref, la_ref)


def _pre_tokens(x_ref, pos_ref, invf_ref, n1_ref, nm_ref, wgu_ref, wd_ref, win_ref,
                qn_ref, wuq_ref, kvn_ref, wuk_ref, wuv_ref, wa2_ref, ba_ref,
                h1_ref, q_ref, k_ref, v_ref, gq_ref, gk_ref, gv_ref, gr_ref, xq_ref, la_ref):
    x = x_ref[...]
    u1 = _rms(x, n1_ref[...]).astype(BF16)
    h1 = x + 0.5 * _swiglu_half(u1, wgu_ref, wd_ref)
    h1_ref[...] = h1
    u2 = _rms(h1, nm_ref[...]).astype(BF16)

    ang = pos_ref[...].astype(F32) * invf_ref[...]
    cos = jnp.cos(ang)
    sin = jnp.sin(ang)
    lane = lax.broadcasted_iota(jnp.int32, ang.shape, 1)
    sin_hi = jnp.where(lane >= 80, sin, 0.0)
    sin_lo = jnp.where(lane < 80, -sin, 0.0)

    def rope(t):
        return t * cos + pltpu.roll(t, 16, 1) * sin_hi + pltpu.roll(t, LANE - 16, 1) * sin_lo

    cqn = _rms(_dot_nt(u2, win_ref[C_CQ:C_CKV, :]), qn_ref[...]).astype(BF16)
    for p in range(MLA_HEADS // 2):
        qp = _dot(cqn, wuq_ref[:, 2 * p * HEAD_PAD:(2 * p + 2) * HEAD_PAD])
        for j in range(2):
            h = 2 * p + j
            q_ref[p, :, j * HEAD_PAD:(j + 1) * HEAD_PAD] = (
                rope(qp[:, j * HEAD_PAD:(j + 1) * HEAD_PAD]) * MLA_EXP2_SCALE).astype(BF16)

    ckvn = _rms(_dot_nt(u2, win_ref[C_CKV:C_KR, :]), kvn_ref[...]).astype(BF16)
    kr_raw = _dot_nt(u2, win_ref[C_KR:C_GQ, :])
    kr = rope(jnp.concatenate([jnp.zeros((kr_raw.shape[0], MLA_NOPE), F32), kr_raw,
                               jnp.zeros((kr_raw.shape[0], HEAD_PAD - MLA_NOPE - MLA_ROPE), F32)], axis=1))
    kr2 = jnp.concatenate([kr, kr], axis=1)
    for p in range(MLA_HEADS // 2):
        kn = _dot(ckvn, wuk_ref[:, 2 * p * HEAD_PAD:(2 * p + 2) * HEAD_PAD])
        k_ref[p] = (kn + kr2).astype(BF16)
    vv = _dot(ckvn, wuv_ref[...]).astype(BF16)
    for p in range(MLA_HEADS // 2):
        v_ref[p] = vv[:, 2 * p * MLA_V:(2 * p + 2) * MLA_V]

    ga = _dot_nt(u2, win_ref[C_GA:C_GR, :]).astype(BF16)
    gq_ref[...] = _dot_nt(u2, win_ref[C_GQ:C_GK, :])
    gk_ref[...] = _dot_nt(u2, win_ref[C_GK:C_GV, :])
    gv_ref[...] = _dot_nt(u2, win_ref[C_GV:C_GA, :]).astype(BF16)
    t = _dot(ga, wa2_ref[...]) + ba_ref[...]
    log_sig = jnp.minimum(t, 0.0) - jnp.log(1.0 + jnp.exp(-jnp.abs(t)))
    la_ref[...] = log_sig * (1.0 / GLA_TAU)
    gr_ref[...] = _dot_nt(u2, win_ref[C_GR:C_XQ, :]).astype(BF16)
    xq_ref[...] = _dot_nt(u2, win_ref[C_XQ:C_GZ, :]).astype(BF16)


def _mla_kernel(q_ref, k_ref, v_ref, o_ref, vt_ref, s_ref):
    def pair(p, carry):
        _mla_pair(q_ref.at[p], k_ref.at[p], v_ref.at[p], o_ref.at[p], vt_ref, s_ref)
        return carry

    lax.fori_loop(0, q_ref.shape[0], pair, 0)


def _mla_pair(q_ref, k_ref, v_ref, o_ref, vt_ref, s_ref):
    tq = TQ
    nq = q_ref.shape[0] // tq
    key = lax.broadcasted_iota(jnp.int32, (tq, tq), 0)
    qry = lax.broadcasted_iota(jnp.int32, (tq, tq), 1)
    causal = key <= qry
    vt = v_ref[...].astype(F32).T.astype(BF16)
    ones_row = jnp.where(lax.broadcasted_iota(jnp.int32, (MLA_VT_ROWS - MLA_V, vt.shape[1]), 0) == 0,
                         1.0, 0.0).astype(BF16)
    for h in range(2):
        vt_ref[h * MLA_VT_ROWS:h * MLA_VT_ROWS + MLA_V, :] = vt[h * MLA_V:(h + 1) * MLA_V, :]
        vt_ref[h * MLA_VT_ROWS + MLA_V:(h + 1) * MLA_VT_ROWS, :] = ones_row

    def scores(qi, h, slot):
        q = q_ref[qi * tq:(qi + 1) * tq, h * HEAD_PAD:(h + 1) * HEAD_PAD]
        m = None
        for c in range(qi + 1):
            s = _dot_nt(k_ref[c * tq:(c + 1) * tq, h * HEAD_PAD:(h + 1) * HEAD_PAD], q)
            if c == qi:
                s = jnp.where(causal, s, -1e30)
            s_ref[slot, c * tq:(c + 1) * tq, :] = s
            cm = jnp.max(s, axis=0, keepdims=True)
            m = cm if m is None else jnp.maximum(m, cm)
        return m

    def values(qi, h, slot, m):
        acc = None
        for c in range(qi + 1):
            p = jnp.exp2(s_ref[slot, c * tq:(c + 1) * tq, :] - m).astype(BF16)
            pv = _dot(vt_ref[h * MLA_VT_ROWS:(h + 1) * MLA_VT_ROWS, c * tq:(c + 1) * tq], p)
            acc = pv if acc is None else acc + pv
        return acc[:MLA_V, :] / acc[MLA_V:MLA_V + 1, :]

    groups = [(qi, h) for qi in range(nq) for h in range(2)]
    nslot = s_ref.shape[0]
    maxes = {}
    for i in range(min(MLA_LOOKAHEAD, len(groups))):
        maxes[i] = scores(*groups[i], i % nslot)
    heads = {}
    for i, (qi, h) in enumerate(groups):
        j = i + MLA_LOOKAHEAD
        if j < len(groups):
            maxes[j] = scores(*groups[j], j % nslot)
        heads[h] = values(qi, h, i % nslot, maxes.pop(i))
        if h == 1:
            o_ref[qi * tq:(qi + 1) * tq, :] = jnp.concatenate([heads[0], heads[1]], axis=0).T.astype(BF16)


def _xattn_kernel(q_ref, mem_ref, n_ref, w_ref, o_ref, kv_ref):
    @pl.when(pl.program_id(1) == 0)
    def _():
        mn = _rms(mem_ref[...], n_ref[...]).astype(BF16)
        kv_ref[...] = _dot(mn, w_ref[...]).astype(BF16)

    scale = 1.0 / math.sqrt(X_DH)
    hw = X_HEADS * X_DH
    for h in range(X_HEADS):
        q = q_ref[:, h * X_DH:(h + 1) * X_DH]
        k = kv_ref[:, h * X_DH:(h + 1) * X_DH]
        v = kv_ref[:, hw + h * X_DH:hw + (h + 1) * X_DH]
        s = _dot_nt(q, k) * scale
        m = jnp.max(s, axis=-1, keepdims=True)
        p = jnp.exp(s - m)
        l = jnp.sum(p, axis=-1, keepdims=True)
        o = _dot(p.astype(BF16), v) / l
        o_ref[:, h * X_DH:(h + 1) * X_DH] = o.astype(BF16)


def _gla_kernel(q_ref, k_ref, la_ref, v_ref, r_ref, on_ref, o_ref):
    sb, c = GLA_SB, GLA_CHUNK
    nsb = q_ref.shape[0] // sb
    nch = sb // c
    npair = GLA_HEADS // 2
    pw, vw = 2 * GLA_DK, 2 * GLA_DV
    row = lax.broadcasted_iota(jnp.int32, (sb, sb), 0)
    col = lax.broadcasted_iota(jnp.int32, (sb, sb), 1)
    tri = jnp.logical_and(row // c == col // c, col <= row)
    tri_bf = jnp.where(tri, 1.0, 0.0).astype(BF16)
    lane = lax.broadcasted_iota(jnp.int32, (1, pw), 1)
    head_lanes = (lane < GLA_DK, lane >= GLA_DK)
    srow = lax.broadcasted_iota(jnp.int32, (pw, vw), 0)
    scol = lax.broadcasted_iota(jnp.int32, (pw, vw), 1)
    same_head = (srow < GLA_DK) == (scol < GLA_DV)

    def local(i):
        r0 = i * sb
        la = la_ref[r0:r0 + sb, :]
        hi = la.astype(BF16)
        lo = (la - hi.astype(F32)).astype(BF16)
        bcum_all = _dot(tri_bf, hi) + _dot(tri_bf, lo)
        out = []
        for p in range(npair):
            bcum = bcum_all[:, p * pw:(p + 1) * pw]
            b_last = [bcum[j * c + c - 1:j * c + c, :] for j in range(nch)]
            b_last_rows = jnp.concatenate([jnp.broadcast_to(b, (c, pw)) for b in b_last], axis=0)
            q_t = q_ref[r0:r0 + sb, p * pw:(p + 1) * pw] * (GLA_DK ** -0.5) * jnp.exp(bcum)
            kk = k_ref[r0:r0 + sb, p * pw:(p + 1) * pw]
            k_t = (kk * jnp.exp(-bcum)).astype(BF16)
            k_d = (kk * jnp.exp(b_last_rows - bcum)).astype(BF16)
            v = v_ref[r0:r0 + sb, p * vw:(p + 1) * vw]
            o_intra = []
            for h in range(2):
                qm = jnp.where(head_lanes[h], q_t, 0.0).astype(BF16)
                att = jnp.where(tri, _dot_nt(qm, k_t), 0.0).astype(BF16)
                o_intra.append(_dot(att, v[:, h * GLA_DV:(h + 1) * GLA_DV]))
            d_st, decay = [], []
            for j in range(nch):
                d = _dot_tn(k_d[j * c:(j + 1) * c, :], v[j * c:(j + 1) * c, :])
                d_st.append(jnp.where(same_head, d, 0.0))
                dcol = jnp.broadcast_to(jnp.exp(b_last[j]), (pw, pw)).T
                decay.append(jnp.concatenate([dcol, dcol], axis=1))
            out.append((q_t.astype(BF16), o_intra, d_st, decay))
        return out

    def recurrent(i, loc, states):
        r0 = i * sb
        for p in range(npair):
            q_bf, o_intra, d_st, decay = loc[p]
            st = states[p]
            o_inter = []
            for j in range(nch):
                o_inter.append(_dot(q_bf[j * c:(j + 1) * c, :], st.astype(BF16)))
                st = decay[j] * st + d_st[j]
            states[p] = st
            o_inter = jnp.concatenate(o_inter, axis=0)
            for h in range(2):
                lo_, hi_ = p * vw + h * GLA_DV, p * vw + (h + 1) * GLA_DV
                o = _rms(o_intra[h] + o_inter[:, h * GLA_DV:(h + 1) * GLA_DV], on_ref[...])
                r = r_ref[r0:r0 + sb, lo_:hi_].astype(F32)
                o_ref[r0:r0 + sb, lo_:hi_] = (o * _silu(r)).astype(BF16)

    states = [jnp.zeros((pw, vw), F32) for _ in range(npair)]
    loc = local(0)
    for i in range(nsb):
        nxt = local(i + 1) if i + 1 < nsb else None
        recurrent(i, loc, states)
        loc = nxt


def _post_kernel(h1_ref, om_ref, og_ref, ox_ref, nm_ref, wgz_ref, gb_ref, wom32_ref, wog32_ref, wox32_ref,
                 wout32_ref, n2_ref, wg32_ref, wu32_ref, wd32_ref, nf_ref, o_ref,
                 wom_ref, wog_ref, wox_ref, wout_ref, wgu_ref, wd_ref):
    i = pl.program_id(0)

    @pl.when(i < W_STEPS)
    def _():
        _convert_ffn(i, wg32_ref, wu32_ref, wd32_ref, wgu_ref, wd_ref)
        for dst, src in ((wom_ref, wom32_ref), (wog_ref, wog32_ref), (wox_ref, wox32_ref)):
            _convert_rows(dst, src[...], i, dst.shape[0] // W_STEPS)
        _convert_rows(wout_ref, wout32_ref[...], i, D_MODEL // W_STEPS)

    @pl.when(i >= W_STEPS)
    def _():
        _post_tokens(h1_ref, om_ref, og_ref, ox_ref, nm_ref, wgz_ref, gb_ref, wom_ref, wog_ref, wox_ref,
                     wout_ref, n2_ref, wgu_ref, wd_ref, nf_ref, o_ref)


def _post_tokens(h1_ref, om_ref, og_ref, ox_ref, nm_ref, wgz_ref, gb_ref, wom_ref, wog_ref, wox_ref,
                 wout_ref, n2_ref, wgu_ref, wd_ref, nf_ref, o_ref):
    h1 = h1_ref[...]
    u = _rms(h1, nm_ref[...]).astype(BF16)
    d = D_MODEL
    branches = ((jnp.concatenate([om_ref[p] for p in range(om_ref.shape[0])], axis=1), wom_ref),
                (og_ref[...], wog_ref), (ox_ref[...], wox_ref))
    tiles = []
    for c0 in range(0, d, MERGE_TILE):
        acc = None
        for n, (o_in, w_br) in enumerate(branches):
            gate = _sigmoid(_dot_nt(u, wgz_ref[n * d + c0:n * d + c0 + MERGE_TILE, :])
                            + gb_ref[:, n * d + c0:n * d + c0 + MERGE_TILE])
            term = gate * _dot(o_in, w_br[:, c0:c0 + MERGE_TILE])
            acc = term if acc is None else acc + term
        tiles.append(acc.astype(BF16))
    h2 = h1 + _dot(jnp.concatenate(tiles, axis=1), wout_ref[...])
    u2 = _rms(h2, n2_ref[...]).astype(BF16)
    h3 = h2 + 0.5 * _swiglu_half(u2, wgu_ref, wd_ref)
    o_ref[...] = _rms(h3, nf_ref[...])


def _cparams(sem, vmem=None, flags=None):
    return pltpu.CompilerParams(dimension_semantics=sem, vmem_limit_bytes=vmem, flags=flags)


def kernel(x, mem, positions, ffn1_norm, ffn1_wg, ffn1_wu, ffn1_wd, mix_norm, mem_norm, w_in, gate_bias,
           mla_q_norm, mla_w_uq, mla_kv_norm, mla_w_ukv, mla_w_o, gla_w_a2, gla_b_a, gla_o_norm, gla_w_o,
           x_w_kv, x_w_o, w_out, ffn2_norm, ffn2_wg, ffn2_wu, ffn2_wd, final_norm):
    B, S, D = x.shape
    T = B * S
    M = mem.shape[1]
    F = D_FF
    H = MLA_HEADS
    bf = lambda a: a.astype(BF16)
    row = lambda a: a.reshape(1, -1)

    w_in_t = bf(jnp.swapaxes(w_in[0], 0, 1))
    w_uq = mla_w_uq[0].reshape(MLA_Q_RANK, H, MLA_NOPE + MLA_ROPE)
    w_uq = bf(jnp.pad(w_uq, ((0, 0), (0, 0), (0, HEAD_PAD - MLA_NOPE - MLA_ROPE))).reshape(MLA_Q_RANK, H * HEAD_PAD))
    w_ukv = mla_w_ukv[0].reshape(MLA_KV_RANK, H, MLA_NOPE + MLA_V)
    w_uk = bf(jnp.pad(w_ukv[:, :, :MLA_NOPE], ((0, 0), (0, 0), (0, HEAD_PAD - MLA_NOPE))).reshape(MLA_KV_RANK, H * HEAD_PAD))
    w_uv = bf(w_ukv[:, :, MLA_NOPE:].reshape(MLA_KV_RANK, H * MLA_V))
    w_a2 = bf(gla_w_a2[0])
    half = MLA_ROPE // 2
    inv_freq = ROPE_THETA ** (-jnp.arange(half, dtype=F32) / half)
    invf = jnp.concatenate([jnp.zeros((MLA_NOPE,), F32), inv_freq, inv_freq,
                            jnp.zeros((HEAD_PAD - MLA_NOPE - MLA_ROPE,), F32)]).reshape(1, HEAD_PAD)

    x2 = x.reshape(T, D)
    pos2 = jnp.broadcast_to(positions.reshape(T, 1), (T, HEAD_PAD))
    nt = T // TM
    tok = lambda w: pl.BlockSpec((TM, w), lambda i: (jnp.maximum(i - W_STEPS, 0), 0))
    npair = H // 2
    tok_pairs = lambda w: pl.BlockSpec((npair, TM, w), lambda i: (0, jnp.maximum(i - W_STEPS, 0), 0))
    wrows = lambda shape, every=1: pl.BlockSpec(
        (shape[0] * every // W_STEPS, shape[1]),
        lambda i: (jnp.minimum(i // every, W_STEPS // every - 1), 0))
    ffn_specs = [wrows((D, F)), wrows((D, F)), wrows((F, D), 2)]
    ffn_scratch = [pltpu.VMEM((D, 2 * F), BF16), pltpu.VMEM((F, D), BF16)]

    gdk, gdv, xw = GLA_HEADS * GLA_DK, GLA_HEADS * GLA_DV, X_HEADS * X_DH
    pre_out_shapes = (
        jax.ShapeDtypeStruct((T, D), F32),
        jax.ShapeDtypeStruct((npair, T, 2 * HEAD_PAD), BF16),
        jax.ShapeDtypeStruct((npair, T, 2 * HEAD_PAD), BF16),
        jax.ShapeDtypeStruct((npair, T, 2 * MLA_V), BF16),
        jax.ShapeDtypeStruct((T, gdk), F32),
        jax.ShapeDtypeStruct((T, gdk), F32),
        jax.ShapeDtypeStruct((T, gdv), BF16),
        jax.ShapeDtypeStruct((T, gdv), BF16),
        jax.ShapeDtypeStruct((T, xw), BF16),
        jax.ShapeDtypeStruct((T, gdk), F32),
    )
    h1, q, k, v, gq, gk, gv, gr, xq, la = pl.pallas_call(
        _pre_kernel,
        out_shape=pre_out_shapes,
        grid=(W_STEPS + nt,),
        in_specs=[tok(D), tok(HEAD_PAD), _const_spec((1, HEAD_PAD)), _const_spec((1, D)), _const_spec((1, D)),
                  *ffn_specs, _const_spec((C_GZ, D)),
                  _const_spec((1, MLA_Q_RANK)), _const_spec((MLA_Q_RANK, H * HEAD_PAD)),
                  _const_spec((1, MLA_KV_RANK)), _const_spec((MLA_KV_RANK, H * HEAD_PAD)),
                  _const_spec((MLA_KV_RANK, H * MLA_V)), _const_spec((GLA_GATE_RANK, gdk)), _const_spec((1, gdk))],
        out_specs=tuple(tok(s.shape[1]) if len(s.shape) == 2 else tok_pairs(s.shape[2])
                        for s in pre_out_shapes),
        scratch_shapes=ffn_scratch,
        compiler_params=_cparams(("arbitrary",), VMEM_LIMIT),
        name="pre",
    )(x2, pos2, invf, row(ffn1_norm[0]), row(mix_norm[0]), ffn1_wg[0], ffn1_wu[0], ffn1_wd[0],
      w_in_t, row(mla_q_norm[0]), w_uq, row(mla_kv_norm[0]), w_uk, w_uv, w_a2, row(gla_b_a[0]))

    seq_pairs = lambda w: pl.BlockSpec((npair, S, w), lambda b: (0, b, 0))
    o_mla = pl.pallas_call(
        _mla_kernel,
        out_shape=jax.ShapeDtypeStruct((npair, T, 2 * MLA_V), BF16),
        grid=(B,),
        in_specs=[seq_pairs(2 * HEAD_PAD), seq_pairs(2 * HEAD_PAD), seq_pairs(2 * MLA_V)],
        out_specs=seq_pairs(2 * MLA_V),
        scratch_shapes=[pltpu.VMEM((2 * MLA_VT_ROWS, S), BF16), pltpu.VMEM((MLA_LOOKAHEAD + 1, S, TQ), F32)],
        compiler_params=_cparams(("parallel",), VMEM_LIMIT),
        name="mla",
    )(q, k, v)

    nqx = S // TQX
    o_x = pl.pallas_call(
        _xattn_kernel,
        out_shape=jax.ShapeDtypeStruct((T, xw), BF16),
        grid=(B, nqx),
        in_specs=[pl.BlockSpec((TQX, xw), lambda b, i: (b * nqx + i, 0)),
                  pl.BlockSpec((M, D), lambda b, i: (b, 0)),
                  _const_spec((1, D)), _const_spec((D, 2 * xw))],
        out_specs=pl.BlockSpec((TQX, xw), lambda b, i: (b * nqx + i, 0)),
        scratch_shapes=[pltpu.VMEM((M, 2 * xw), BF16)],
        compiler_params=_cparams(("arbitrary", "arbitrary")),
        name="xattn",
    )(xq, mem.reshape(B * M, D), row(mem_norm[0]), bf(x_w_kv[0]))

    o_gla = pl.pallas_call(
        _gla_kernel,
        out_shape=jax.ShapeDtypeStruct((T, gdv), BF16),
        grid=(B,),
        in_specs=[pl.BlockSpec((S, gdk), lambda b: (b, 0)),
                  pl.BlockSpec((S, gdk), lambda b: (b, 0)),
                  pl.BlockSpec((S, gdk), lambda b: (b, 0)),
                  pl.BlockSpec((S, gdv), lambda b: (b, 0)),
                  pl.BlockSpec((S, gdv), lambda b: (b, 0)),
                  _const_spec((1, GLA_DV))],
        out_specs=pl.BlockSpec((S, gdv), lambda b: (b, 0)),
        compiler_params=_cparams(("parallel",), VMEM_LIMIT),
        name="gla",
    )(gq, gk, la, gv, gr, row(gla_o_norm[0]))

    out = pl.pallas_call(
        _post_kernel,
        out_shape=jax.ShapeDtypeStruct((T, D), F32),
        grid=(W_STEPS + nt,),
        in_specs=[tok(D), tok_pairs(2 * MLA_V), tok(gdv), tok(xw), _const_spec((1, D)),
                  pl.BlockSpec((pl.Element(N_BRANCH * D), pl.Element(D)), lambda i: (C_GZ, 0),
                               pipeline_mode=pl.Buffered(1)),
                  _const_spec((1, N_BRANCH * D)),
                  wrows((H * MLA_V, D)), wrows((gdv, D)), wrows((xw, D)),
                  wrows((D, D)), _const_spec((1, D)),
                  *ffn_specs, _const_spec((1, D))],
        out_specs=tok(D),
        scratch_shapes=[pltpu.VMEM((H * MLA_V, D), BF16),
                        pltpu.VMEM((gdv, D), BF16), pltpu.VMEM((xw, D), BF16), pltpu.VMEM((D, D), BF16),
                        *ffn_scratch],
        compiler_params=_cparams(("arbitrary",), VMEM_LIMIT),
        name="post",
    )(h1, o_mla, o_gla, o_x, row(mix_norm[0]), w_in_t, gate_bias[0].reshape(1, N_BRANCH * D),
      mla_w_o[0], gla_w_o[0], x_w_o[0], w_out[0], row(ffn2_norm[0]),
      ffn2_wg[0], ffn2_wu[0], ffn2_wd[0], row(final_norm))
    return out.reshape(B, S, D)
```

```python
import functools
import math

import jax
import jax.numpy as jnp
from jax import lax
from jax.experimental import pallas as pl
from jax.experimental.pallas import tpu as pltpu

F32 = jnp.float32
BF16 = jnp.bfloat16

D_MODEL = 1024
N_MEM = 256
EPS = 1e-6
MLA_HEADS = 8
MLA_NOPE = 64
MLA_ROPE = 32
MLA_V = 64
MLA_Q_RANK = 384
MLA_KV_RANK = 256
ROPE_THETA = 10000.0
GLA_HEADS = 4
GLA_DK = 64
GLA_DV = 128
GLA_GATE_RANK = 16
GLA_TAU = 16.0
GLA_CHUNK = 64
X_HEADS = 4
X_DH = 128
D_FF = 2816
N_BRANCH = 3

LANE = 128
HEAD_PAD = 128
VMEM_LIMIT = 60 * 1024 * 1024
MLA_EXP2_SCALE = math.log2(math.e) / math.sqrt(MLA_NOPE + MLA_ROPE)
MLA_VT_ROWS = MLA_V + 16
MLA_LOOKAHEAD = 3

C_CQ, C_CKV, C_KR, C_GQ, C_GK, C_GV, C_GA, C_GR, C_XQ, C_GZ, C_END = (
    0, 384, 640, 672, 928, 1184, 1696, 1712, 2224, 2736, 5808)
W_STEPS = 16

TM = 512
MERGE_TILE = 256
TQ = 256
TQX = 2048
GLA_SB = 256


def _dot(a, b):
    return jnp.dot(a, b, preferred_element_type=F32)


def _dot_nt(a, b):
    return lax.dot_general(a, b, (((1,), (1,)), ((), ())), preferred_element_type=F32)


def _dot_tn(a, b):
    return lax.dot_general(a, b, (((0,), (0,)), ((), ())), preferred_element_type=F32)


def _rms(x, g):
    return x * lax.rsqrt(jnp.mean(x * x, axis=-1, keepdims=True) + EPS) * g


def _sigmoid(x):
    return 1.0 / (1.0 + jnp.exp(-x))


def _silu(x):
    return x * _sigmoid(x)


def _const_spec(shape):
    return pl.BlockSpec(shape, lambda *_: (0,) * len(shape), pipeline_mode=pl.Buffered(1))


def _swiglu_half(xn_bf16, wgu_ref, wd_ref):
    tiles = []
    for f0 in range(0, 2 * D_FF, 2 * LANE):
        gu = _dot(xn_bf16, wgu_ref[:, f0:f0 + 2 * LANE])
        tiles.append((_silu(gu[:, :LANE]) * gu[:, LANE:]).astype(BF16))
    return _dot(jnp.concatenate(tiles, axis=1), wd_ref[...])


def _convert_rows(dst_ref, src, step, rows):
    r0 = pl.multiple_of(step * rows, rows)
    dst_ref[pl.ds(r0, rows), :] = src.astype(BF16)


def _convert_ffn(i, wg32_ref, wu32_ref, wd32_ref, wgu_ref, wd_ref):
    rows = D_MODEL // W_STEPS
    r0 = pl.multiple_of(i * rows, rows)
    wg = wg32_ref[...].astype(BF16)
    wu = wu32_ref[...].astype(BF16)
    for j in range(D_FF // LANE):
        wgu_ref[pl.ds(r0, rows), 2 * j * LANE:(2 * j + 1) * LANE] = wg[:, j * LANE:(j + 1) * LANE]
        wgu_ref[pl.ds(r0, rows), (2 * j + 1) * LANE:(2 * j + 2) * LANE] = wu[:, j * LANE:(j + 1) * LANE]

    @pl.when(i % 2 == 0)
    def _():
        _convert_rows(wd_ref, wd32_ref[...], i // 2, D_FF // (W_STEPS // 2))


def _pre_kernel(x_ref, pos_ref, invf_ref, n1_ref, nm_ref, wg32_ref, wu32_ref, wd32_ref, win_ref,
                qn_ref, wuq_ref, kvn_ref, wuk_ref, wuv_ref, wa2_ref, ba_ref,
                h1_ref, q_ref, k_ref, v_ref, gq_ref, gk_ref, gv_ref, gr_ref, xq_ref, la_ref,
                wgu_ref, wd_ref):
    i = pl.program_id(0)

    @pl.when(i < W_STEPS)
    def _():
        _convert_ffn(i, wg32_ref, wu32_ref, wd32_ref, wgu_ref, wd_ref)

    @pl.when(i >= W_STEPS)
    def _():
        _pre_tokens(x_ref, pos_ref, invf_ref, n1_ref, nm_ref, wgu_ref, wd_ref, win_ref,
                    qn_ref, wuq_ref, kvn_ref, wuk_ref, wuv_ref, wa2_ref, ba_ref,
                    h1_ref, q_ref, k_ref, v_ref, gq_ref, gk_ref, gv_ref, gr_ref, xq_ref, la_ref)


def _pre_tokens(x_ref, pos_ref, invf_ref, n1_ref, nm_ref, wgu_ref, wd_ref, win_ref,
                qn_ref, wuq_ref, kvn_ref, wuk_ref, wuv_ref, wa2_ref, ba_ref,
                h1_ref, q_ref, k_ref, v_ref, gq_ref, gk_ref, gv_ref, gr_ref, xq_ref, la_ref):
    x = x_ref[...]
    u1 = _rms(x, n1_ref[...]).astype(BF16)
    h1 = x + 0.5 * _swiglu_half(u1, wgu_ref, wd_ref)
    h1_ref[...] = h1
    u2 = _rms(h1, nm_ref[...]).astype(BF16)

    ang = pos_ref[...].astype(F32) * invf_ref[...]
    cos = jnp.cos(ang)
    sin = jnp.sin(ang)
    lane = lax.broadcasted_iota(jnp.int32, ang.shape, 1)
    sin_hi = jnp.where(lane >= 80, sin, 0.0)
    sin_lo = jnp.where(lane < 80, -sin, 0.0)

    def rope(t):
        return t * cos + pltpu.roll(t, 16, 1) * sin_hi + pltpu.roll(t, LANE - 16, 1) * sin_lo

    cqn = _rms(_dot_nt(u2, win_ref[C_CQ:C_CKV, :]), qn_ref[...]).astype(BF16)
    for p in range(MLA_HEADS // 2):
        qp = _dot(cqn, wuq_ref[:, 2 * p * HEAD_PAD:(2 * p + 2) * HEAD_PAD])
        for j in range(2):
            h = 2 * p + j
            q_ref[p, :, j * HEAD_PAD:(j + 1) * HEAD_PAD] = (
                rope(qp[:, j * HEAD_PAD:(j + 1) * HEAD_PAD]) * MLA_EXP2_SCALE).astype(BF16)

    ckvn = _rms(_dot_nt(u2, win_ref[C_CKV:C_KR, :]), kvn_ref[...]).astype(BF16)
    kr_raw = _dot_nt(u2, win_ref[C_KR:C_GQ, :])
    kr = rope(jnp.concatenate([jnp.zeros((kr_raw.shape[0], MLA_NOPE), F32), kr_raw,
                               jnp.zeros((kr_raw.shape[0], HEAD_PAD - MLA_NOPE - MLA_ROPE), F32)], axis=1))
    kr2 = jnp.concatenate([kr, kr], axis=1)
    for p in range(MLA_HEADS // 2):
        kn = _dot(ckvn, wuk_ref[:, 2 * p * HEAD_PAD:(2 * p + 2) * HEAD_PAD])
        k_ref[p] = (kn + kr2).astype(BF16)
    vv = _dot(ckvn, wuv_ref[...]).astype(BF16)
    for p in range(MLA_HEADS // 2):
        v_ref[p] = vv[:, 2 * p * MLA_V:(2 * p + 2) * MLA_V]

    ga = _dot_nt(u2, win_ref[C_GA:C_GR, :]).astype(BF16)
    gq_ref[...] = _dot_nt(u2, win_ref[C_GQ:C_GK, :])
    gk_ref[...] = _dot_nt(u2, win_ref[C_GK:C_GV, :])
    gv_ref[...] = _dot_nt(u2, win_ref[C_GV:C_GA, :]).astype(BF16)
    t = _dot(ga, wa2_ref[...]) + ba_ref[...]
    log_sig = jnp.minimum(t, 0.0) - jnp.log(1.0 + jnp.exp(-jnp.abs(t)))
    la_ref[...] = log_sig * (1.0 / GLA_TAU)
    gr_ref[...] = _dot_nt(u2, win_ref[C_GR:C_XQ, :]).astype(BF16)
    xq_ref[...] = _dot_nt(u2, win_ref[C_XQ:C_GZ, :]).astype(BF16)


def _mla_kernel(q_ref, k_ref, v_ref, o_ref, vt_ref):
    def pair(p, carry):
        _mla_pair(q_ref.at[p], k_ref.at[p], v_ref.at[p], o_ref.at[p], vt_ref)
        return carry

    lax.fori_loop(0, q_ref.shape[0], pair, 0)


def _mla_pair(q_ref, k_ref, v_ref, o_ref, vt_ref):
    tq = TQ
    nq = q_ref.shape[0] // tq
    key = lax.broadcasted_iota(jnp.int32, (tq, tq), 0)
    qry = lax.broadcasted_iota(jnp.int32, (tq, tq), 1)
    causal = key <= qry
    vt = v_ref[...].astype(F32).T.astype(BF16)
    ones_row = jnp.where(lax.broadcasted_iota(jnp.int32, (MLA_VT_ROWS - MLA_V, vt.shape[1]), 0) == 0,
                         1.0, 0.0).astype(BF16)
    for h in range(2):
        vt_ref[h * MLA_VT_ROWS:h * MLA_VT_ROWS + MLA_V, :] = vt[h * MLA_V:(h + 1) * MLA_V, :]
        vt_ref[h * MLA_VT_ROWS + MLA_V:(h + 1) * MLA_VT_ROWS, :] = ones_row

    items = [(qi, c) for qi in range(nq) for c in range(qi + 1)]
    pending = [{}, {}]
    state = [{}, {}]
    done = {}

    def score(h, i):
        qi, c = items[i]
        q = q_ref[qi * tq:(qi + 1) * tq, h * HEAD_PAD:(h + 1) * HEAD_PAD]
        s = _dot_nt(k_ref[c * tq:(c + 1) * tq, h * HEAD_PAD:(h + 1) * HEAD_PAD], q)
        pending[h][i] = jnp.where(causal, s, -1e30) if c == qi else s

    def consume(h, i):
        qi, c = items[i]
        st = state[h]
        s = pending[h].pop(i)
        cm = jnp.max(s, axis=0, keepdims=True)
        m_new = cm if c == 0 else jnp.maximum(st["m"], cm)
        p = jnp.exp2(s - m_new).astype(BF16)
        pv = _dot(vt_ref[h * MLA_VT_ROWS:(h + 1) * MLA_VT_ROWS, c * tq:(c + 1) * tq], p)
        st["acc"] = pv if c == 0 else st["acc"] * jnp.exp2(st["m"] - m_new) + pv
        st["m"] = m_new
        if c == qi:
            done[(qi, h)] = st["acc"][:MLA_V, :] / st["acc"][MLA_V:MLA_V + 1, :]
            if (qi, 1 - h) in done:
                o_ref[qi * tq:(qi + 1) * tq, :] = jnp.concatenate(
                    [done.pop((qi, 0)), done.pop((qi, 1))], axis=0).T.astype(BF16)

    for t in range(len(items) + MLA_LOOKAHEAD):
        for h in range(2):
            if t < len(items):
                score(h, t)
            if t >= MLA_LOOKAHEAD:
                consume(h, t - MLA_LOOKAHEAD)


def _xattn_kernel(q_ref, mem_ref, n_ref, w_ref, o_ref, kv_ref):
    @pl.when(pl.program_id(1) == 0)
    def _():
        mn = _rms(mem_ref[...], n_ref[...]).astype(BF16)
        kv_ref[...] = _dot(mn, w_ref[...]).astype(BF16)

    scale = 1.0 / math.sqrt(X_DH)
    hw = X_HEADS * X_DH
    for h in range(X_HEADS):
        q = q_ref[:, h * X_DH:(h + 1) * X_DH]
        k = kv_ref[:, h * X_DH:(h + 1) * X_DH]
        v = kv_ref[:, hw + h * X_DH:hw + (h + 1) * X_DH]
        s = _dot_nt(q, k) * scale
        m = jnp.max(s, axis=-1, keepdims=True)
        p = jnp.exp(s - m)
        l = jnp.sum(p, axis=-1, keepdims=True)
        o = _dot(p.astype(BF16), v) / l
        o_ref[:, h * X_DH:(h + 1) * X_DH] = o.astype(BF16)


def _gla_kernel(q_ref, k_ref, la_ref, v_ref, r_ref, on_ref, o_ref):
    sb, c = GLA_SB, GLA_CHUNK
    nsb = q_ref.shape[0] // sb
    nch = sb // c
    npair = GLA_HEADS // 2
    pw, vw = 2 * GLA_DK, 2 * GLA_DV
    row = lax.broadcasted_iota(jnp.int32, (sb, sb), 0)
    col = lax.broadcasted_iota(jnp.int32, (sb, sb), 1)
    tri = jnp.logical_and(row // c == col // c, col <= row)
    tri_bf = jnp.where(tri, 1.0, 0.0).astype(BF16)
    lane = lax.broadcasted_iota(jnp.int32, (1, pw), 1)
    head_lanes = (lane < GLA_DK, lane >= GLA_DK)
    srow = lax.broadcasted_iota(jnp.int32, (pw, vw), 0)
    scol = lax.broadcasted_iota(jnp.int32, (pw, vw), 1)
    same_head = (srow < GLA_DK) == (scol < GLA_DV)

    def local(i):
        r0 = i * sb
        la = la_ref[r0:r0 + sb, :]
        hi = la.astype(BF16)
        lo = (la - hi.astype(F32)).astype(BF16)
        bcum_all = _dot(tri_bf, hi) + _dot(tri_bf, lo)
        out = []
        for p in range(npair):
            bcum = bcum_all[:, p * pw:(p + 1) * pw]
            b_last = [bcum[j * c + c - 1:j * c + c, :] for j in range(nch)]
            b_last_rows = jnp.concatenate([jnp.broadcast_to(b, (c, pw)) for b in b_last], axis=0)
            q_t = q_ref[r0:r0 + sb, p * pw:(p + 1) * pw] * (GLA_DK ** -0.5) * jnp.exp(bcum)
            kk = k_ref[r0:r0 + sb, p * pw:(p + 1) * pw]
            k_t = (kk * jnp.exp(-bcum)).astype(BF16)
            k_d = (kk * jnp.exp(b_last_rows - bcum)).astype(BF16)
            v = v_ref[r0:r0 + sb, p * vw:(p + 1) * vw]
            o_intra = []
            for h in range(2):
                qm = jnp.where(head_lanes[h], q_t, 0.0).astype(BF16)
                att = jnp.where(tri, _dot_nt(qm, k_t), 0.0).astype(BF16)
                o_intra.append(_dot(att, v[:, h * GLA_DV:(h + 1) * GLA_DV]))
            d_st, decay = [], []
            for j in range(nch):
                d = _dot_tn(k_d[j * c:(j + 1) * c, :], v[j * c:(j + 1) * c, :])
                d_st.append(jnp.where(same_head, d, 0.0))
                dcol = jnp.broadcast_to(jnp.exp(b_last[j]), (pw, pw)).T
                decay.append(jnp.concatenate([dcol, dcol], axis=1))
            out.append((q_t.astype(BF16), o_intra, d_st, decay))
        return out

    def recurrent(i, loc, states):
        r0 = i * sb
        for p in range(npair):
            q_bf, o_intra, d_st, decay = loc[p]
            st = states[p]
            o_inter = []
            for j in range(nch):
                o_inter.append(_dot(q_bf[j * c:(j + 1) * c, :], st.astype(BF16)))
                st = decay[j] * st + d_st[j]
            states[p] = st
            o_inter = jnp.concatenate(o_inter, axis=0)
            for h in range(2):
                lo_, hi_ = p * vw + h * GLA_DV, p * vw + (h + 1) * GLA_DV
                o = _rms(o_intra[h] + o_inter[:, h * GLA_DV:(h + 1) * GLA_DV], on_ref[...])
                r = r_ref[r0:r0 + sb, lo_:hi_].astype(F32)
                o_ref[r0:r0 + sb, lo_:hi_] = (o * _silu(r)).astype(BF16)

    states = [jnp.zeros((pw, vw), F32) for _ in range(npair)]
    loc = local(0)
    for i in range(nsb):
        nxt = local(i + 1) if i + 1 < nsb else None
        recurrent(i, loc, states)
        loc = nxt


def _post_kernel(h1_ref, om_ref, og_ref, ox_ref, nm_ref, wgz_ref, gb_ref, wom32_ref, wog32_ref, wox32_ref,
                 wout32_ref, n2_ref, wg32_ref, wu32_ref, wd32_ref, nf_ref, o_ref,
                 wom_ref, wog_ref, wox_ref, wout_ref, wgu_ref, wd_ref):
    i = pl.program_id(0)

    @pl.when(i < W_STEPS)
    def _():
        _convert_ffn(i, wg32_ref, wu32_ref, wd32_ref, wgu_ref, wd_ref)
        for dst, src in ((wom_ref, wom32_ref), (wog_ref, wog32_ref), (wox_ref, wox32_ref)):
            _convert_rows(dst, src[...], i, dst.shape[0] // W_STEPS)
        _convert_rows(wout_ref, wout32_ref[...], i, D_MODEL // W_STEPS)

    @pl.when(i >= W_STEPS)
    def _():
        _post_tokens(h1_ref, om_ref, og_ref, ox_ref, nm_ref, wgz_ref, gb_ref, wom_ref, wog_ref, wox_ref,
                     wout_ref, n2_ref, wgu_ref, wd_ref, nf_ref, o_ref)


def _post_tokens(h1_ref, om_ref, og_ref, ox_ref, nm_ref, wgz_ref, gb_ref, wom_ref, wog_ref, wox_ref,
                 wout_ref, n2_ref, wgu_ref, wd_ref, nf_ref, o_ref):
    h1 = h1_ref[...]
    u = _rms(h1, nm_ref[...]).astype(BF16)
    d = D_MODEL
    branches = ((jnp.concatenate([om_ref[p] for p in range(om_ref.shape[0])], axis=1), wom_ref),
                (og_ref[...], wog_ref), (ox_ref[...], wox_ref))
    tiles = []
    for c0 in range(0, d, MERGE_TILE):
        acc = None
        for n, (o_in, w_br) in enumerate(branches):
            gate = _sigmoid(_dot_nt(u, wgz_ref[n * d + c0:n * d + c0 + MERGE_TILE, :])
                            + gb_ref[:, n * d + c0:n * d + c0 + MERGE_TILE])
            term = gate * _dot(o_in, w_br[:, c0:c0 + MERGE_TILE])
            acc = term if acc is None else acc + term
        tiles.append(acc.astype(BF16))
    h2 = h1 + _dot(jnp.concatenate(tiles, axis=1), wout_ref[...])
    u2 = _rms(h2, n2_ref[...]).astype(BF16)
    h3 = h2 + 0.5 * _swiglu_half(u2, wgu_ref, wd_ref)
    o_ref[...] = _rms(h3, nf_ref[...])


def _cparams(sem, vmem=None, flags=None):
    return pltpu.CompilerParams(dimension_semantics=sem, vmem_limit_bytes=vmem, flags=flags)


def kernel(x, mem, positions, ffn1_norm, ffn1_wg, ffn1_wu, ffn1_wd, mix_norm, mem_norm, w_in, gate_bias,
           mla_q_norm, mla_w_uq, mla_kv_norm, mla_w_ukv, mla_w_o, gla_w_a2, gla_b_a, gla_o_norm, gla_w_o,
           x_w_kv, x_w_o, w_out, ffn2_norm, ffn2_wg, ffn2_wu, ffn2_wd, final_norm):
    B, S, D = x.shape
    T = B * S
    M = mem.shape[1]
    F = D_FF
    H = MLA_HEADS
    bf = lambda a: a.astype(BF16)
    row = lambda a: a.reshape(1, -1)

    w_in_t = bf(jnp.swapaxes(w_in[0], 0, 1))
    w_uq = mla_w_uq[0].reshape(MLA_Q_RANK, H, MLA_NOPE + MLA_ROPE)
    w_uq = bf(jnp.pad(w_uq, ((0, 0), (0, 0), (0, HEAD_PAD - MLA_NOPE - MLA_ROPE))).reshape(MLA_Q_RANK, H * HEAD_PAD))
    w_ukv = mla_w_ukv[0].reshape(MLA_KV_RANK, H, MLA_NOPE + MLA_V)
    w_uk = bf(jnp.pad(w_ukv[:, :, :MLA_NOPE], ((0, 0), (0, 0), (0, HEAD_PAD - MLA_NOPE))).reshape(MLA_KV_RANK, H * HEAD_PAD))
    w_uv = bf(w_ukv[:, :, MLA_NOPE:].reshape(MLA_KV_RANK, H * MLA_V))
    w_a2 = bf(gla_w_a2[0])
    half = MLA_ROPE // 2
    inv_freq = ROPE_THETA ** (-jnp.arange(half, dtype=F32) / half)
    invf = jnp.concatenate([jnp.zeros((MLA_NOPE,), F32), inv_freq, inv_freq,
                            jnp.zeros((HEAD_PAD - MLA_NOPE - MLA_ROPE,), F32)]).reshape(1, HEAD_PAD)

    x2 = x.reshape(T, D)
    pos2 = jnp.broadcast_to(positions.reshape(T, 1), (T, HEAD_PAD))
    nt = T // TM
    tok = lambda w: pl.BlockSpec((TM, w), lambda i: (jnp.maximum(i - W_STEPS, 0), 0))
    npair = H // 2
    tok_pairs = lambda w: pl.BlockSpec((npair, TM, w), lambda i: (0, jnp.maximum(i - W_STEPS, 0), 0))
    wrows = lambda shape, every=1: pl.BlockSpec(
        (shape[0] * every // W_STEPS, shape[1]),
        lambda i: (jnp.minimum(i // every, W_STEPS // every - 1), 0))
    ffn_specs = [wrows((D, F)), wrows((D, F)), wrows((F, D), 2)]
    ffn_scratch = [pltpu.VMEM((D, 2 * F), BF16), pltpu.VMEM((F, D), BF16)]

    gdk, gdv, xw = GLA_HEADS * GLA_DK, GLA_HEADS * GLA_DV, X_HEADS * X_DH
    pre_out_shapes = (
        jax.ShapeDtypeStruct((T, D), F32),
        jax.ShapeDtypeStruct((npair, T, 2 * HEAD_PAD), BF16),
        jax.ShapeDtypeStruct((npair, T, 2 * HEAD_PAD), BF16),
        jax.ShapeDtypeStruct((npair, T, 2 * MLA_V), BF16),
        jax.ShapeDtypeStruct((T, gdk), F32),
        jax.ShapeDtypeStruct((T, gdk), F32),
        jax.ShapeDtypeStruct((T, gdv), BF16),
        jax.ShapeDtypeStruct((T, gdv), BF16),
        jax.ShapeDtypeStruct((T, xw), BF16),
        jax.ShapeDtypeStruct((T, gdk), F32),
    )
    h1, q, k, v, gq, gk, gv, gr, xq, la = pl.pallas_call(
        _pre_kernel,
        out_shape=pre_out_shapes,
        grid=(W_STEPS + nt,),
        in_specs=[tok(D), tok(HEAD_PAD), _const_spec((1, HEAD_PAD)), _const_spec((1, D)), _const_spec((1, D)),
                  *ffn_specs, _const_spec((C_GZ, D)),
                  _const_spec((1, MLA_Q_RANK)), _const_spec((MLA_Q_RANK, H * HEAD_PAD)),
                  _const_spec((1, MLA_KV_RANK)), _const_spec((MLA_KV_RANK, H * HEAD_PAD)),
                  _const_spec((MLA_KV_RANK, H * MLA_V)), _const_spec((GLA_GATE_RANK, gdk)), _const_spec((1, gdk))],
        out_specs=tuple(tok(s.shape[1]) if len(s.shape) == 2 else tok_pairs(s.shape[2])
                        for s in pre_out_shapes),
        scratch_shapes=ffn_scratch,
        compiler_params=_cparams(("arbitrary",), VMEM_LIMIT),
        name="pre",
    )(x2, pos2, invf, row(ffn1_norm[0]), row(mix_norm[0]), ffn1_wg[0], ffn1_wu[0], ffn1_wd[0],
      w_in_t, row(mla_q_norm[0]), w_uq, row(mla_kv_norm[0]), w_uk, w_uv, w_a2, row(gla_b_a[0]))

    seq_pairs = lambda w: pl.BlockSpec((npair, S, w), lambda b: (0, b, 0))
    o_mla = pl.pallas_call(
        _mla_kernel,
        out_shape=jax.ShapeDtypeStruct((npair, T, 2 * MLA_V), BF16),
        grid=(B,),
        in_specs=[seq_pairs(2 * HEAD_PAD), seq_pairs(2 * HEAD_PAD), seq_pairs(2 * MLA_V)],
        out_specs=seq_pairs(2 * MLA_V),
        scratch_shapes=[pltpu.VMEM((2 * MLA_VT_ROWS, S), BF16)],
        compiler_params=_cparams(("parallel",), VMEM_LIMIT),
        name="mla",
    )(q, k, v)

    nqx = S // TQX
    o_x = pl.pallas_call(
        _xattn_kernel,
        out_shape=jax.ShapeDtypeStruct((T, xw), BF16),
        grid=(B, nqx),
        in_specs=[pl.BlockSpec((TQX, xw), lambda b, i: (b * nqx + i, 0)),
                  pl.BlockSpec((M, D), lambda b, i: (b, 0)),
                  _const_spec((1, D)), _const_spec((D, 2 * xw))],
        out_specs=pl.BlockSpec((TQX, xw), lambda b, i: (b * nqx + i, 0)),
        scratch_shapes=[pltpu.VMEM((M, 2 * xw), BF16)],
        compiler_params=_cparams(("arbitrary", "arbitrary")),
        name="xattn",
    )(xq, mem.reshape(B * M, D), row(mem_norm[0]), bf(x_w_kv[0]))

    o_gla = pl.pallas_call(
        _gla_kernel,
        out_shape=jax.ShapeDtypeStruct((T, gdv), BF16),
        grid=(B,),
        in_specs=[pl.BlockSpec((S, gdk), lambda b: (b, 0)),
                  pl.BlockSpec((S, gdk), lambda b: (b, 0)),
                  pl.BlockSpec((S, gdk), lambda b: (b, 0)),
                  pl.BlockSpec((S, gdv), lambda b: (b, 0)),
                  pl.BlockSpec((S, gdv), lambda b: (b, 0)),
                  _const_spec((1, GLA_DV))],
        out_specs=pl.BlockSpec((S, gdv), lambda b: (b, 0)),
        compiler_params=_cparams(("parallel",), VMEM_LIMIT),
        name="gla",
    )(gq, gk, la, gv, gr, row(gla_o_norm[0]))

    out = pl.pallas_call(
        _post_kernel,
        out_shape=jax.ShapeDtypeStruct((T, D), F32),
        grid=(W_STEPS + nt,),
        in_specs=[tok(D), tok_pairs(2 * MLA_V), tok(gdv), tok(xw), _const_spec((1, D)),
                  pl.BlockSpec((pl.Element(N_BRANCH * D), pl.Element(D)), lambda i: (C_GZ, 0),
                               pipeline_mode=pl.Buffered(1)),
                  _const_spec((1, N_BRANCH * D)),
                  wrows((H * MLA_V, D)), wrows((gdv, D)), wrows((xw, D)),
                  wrows((D, D)), _const_spec((1, D)),
                  *ffn_specs, _const_spec((1, D))],
        out_specs=tok(D),
        scratch_shapes=[pltpu.VMEM((H * MLA_V, D), BF16),
                        pltpu.VMEM((gdv, D), BF16), pltpu.VMEM((xw, D), BF16), pltpu.VMEM((D, D), BF16),
                        *ffn_scratch],
        compiler_params=_cparams(("arbitrary",), VMEM_LIMIT),
        name="post",
    )(h1, o_mla, o_gla, o_x, row(mix_norm[0]), w_in_t, gate_bias[0].reshape(1, N_BRANCH * D),
      mla_w_o[0], gla_w_o[0], x_w_o[0], w_out[0], row(ffn2_norm[0]),
      ffn2_wg[0], ffn2_wu[0], ffn2_wd[0], row(final_norm))
    return out.reshape(B, S, D)
```

```python
import functools
import math

import jax
import jax.numpy as jnp
from jax import lax
from jax.experimental import pallas as pl
from jax.experimental.pallas import tpu as pltpu

F32 = jnp.float32
BF16 = jnp.bfloat16

D_MODEL = 1024
N_MEM = 256
EPS = 1e-6
MLA_HEADS = 8
MLA_NOPE = 64
MLA_ROPE = 32
MLA_V = 64
MLA_Q_RANK = 384
MLA_KV_RANK = 256
ROPE_THETA = 10000.0
GLA_HEADS = 4
GLA_DK = 64
GLA_DV = 128
GLA_GATE_RANK = 16
GLA_TAU = 16.0
GLA_CHUNK = 64
X_HEADS = 4
X_DH = 128
D_FF = 2816
N_BRANCH = 3

LANE = 128
HEAD_PAD = 128
VMEM_LIMIT = 60 * 1024 * 1024
MLA_EXP2_SCALE = math.log2(math.e) / math.sqrt(MLA_NOPE + MLA_ROPE)
MLA_VT_ROWS = MLA_V + 16
MLA_LOOKAHEAD = 3

C_CQ, C_CKV, C_KR, C_GQ, C_GK, C_GV, C_GA, C_GR, C_XQ, C_GZ, C_END = (
    0, 384, 640, 672, 928, 1184, 1696, 1712, 2224, 2736, 5808)
W_STEPS = 8

TM = 512
MERGE_TILE = 256
TQ = 256
TQX = 2048
GLA_SB = 256


def _dot(a, b):
    return jnp.dot(a, b, preferred_element_type=F32)


def _dot_nt(a, b):
    return lax.dot_general(a, b, (((1,), (1,)), ((), ())), preferred_element_type=F32)


def _dot_tn(a, b):
    return lax.dot_general(a, b, (((0,), (0,)), ((), ())), preferred_element_type=F32)


def _rms(x, g):
    return x * lax.rsqrt(jnp.mean(x * x, axis=-1, keepdims=True) + EPS) * g


def _sigmoid(x):
    return 1.0 / (1.0 + jnp.exp(-x))


def _silu(x):
    return x * _sigmoid(x)


def _const_spec(shape):
    return pl.BlockSpec(shape, lambda *_: (0,) * len(shape), pipeline_mode=pl.Buffered(1))


def _swiglu_half(xn_bf16, wgu_ref, wd_ref):
    tiles = []
    for f0 in range(0, 2 * D_FF, 2 * LANE):
        gu = _dot(xn_bf16, wgu_ref[:, f0:f0 + 2 * LANE])
        tiles.append((_silu(gu[:, :LANE]) * gu[:, LANE:]).astype(BF16))
    return _dot(jnp.concatenate(tiles, axis=1), wd_ref[...])


def _convert_rows(dst_ref, src, step, rows):
    r0 = pl.multiple_of(step * rows, rows)
    dst_ref[pl.ds(r0, rows), :] = src.astype(BF16)


def _convert_ffn(i, wg32_ref, wu32_ref, wd32_ref, wgu_ref, wd_ref):
    rows = D_MODEL // W_STEPS
    r0 = pl.multiple_of(i * rows, rows)
    wg = wg32_ref[...].astype(BF16)
    wu = wu32_ref[...].astype(BF16)
    for j in range(D_FF // LANE):
        wgu_ref[pl.ds(r0, rows), 2 * j * LANE:(2 * j + 1) * LANE] = wg[:, j * LANE:(j + 1) * LANE]
        wgu_ref[pl.ds(r0, rows), (2 * j + 1) * LANE:(2 * j + 2) * LANE] = wu[:, j * LANE:(j + 1) * LANE]
    _convert_rows(wd_ref, wd32_ref[...], i, D_FF // W_STEPS)


def _pre_kernel(x_ref, pos_ref, invf_ref, n1_ref, nm_ref, wg32_ref, wu32_ref, wd32_ref, win_ref,
                qn_ref, wuq_ref, kvn_ref, wuk_ref, wuv_ref, wa2_ref, ba_ref,
                h1_ref, q_ref, k_ref, v_ref, gq_ref, gk_ref, gv_ref, gr_ref, xq_ref, la_ref,
                wgu_ref, wd_ref):
    i = pl.program_id(0)

    @pl.when(i < W_STEPS)
    def _():
        _convert_ffn(i, wg32_ref, wu32_ref, wd32_ref, wgu_ref, wd_ref)

    @pl.when(i >= W_STEPS)
    def _():
        _pre_tokens(x_ref, pos_ref, invf_ref, n1_ref, nm_ref, wgu_ref, wd_ref, win_ref,
                    qn_ref, wuq_ref, kvn_ref, wuk_ref, wuv_ref, wa2_ref, ba_ref,
                    h1_ref, q_ref, k_ref, v_ref, gq_ref, gk_ref, gv_ref, gr_ref, xq_ref, la_ref)


def _pre_tokens(x_ref, pos_ref, invf_ref, n1_ref, nm_ref, wgu_ref, wd_ref, win_ref,
                qn_ref, wuq_ref, kvn_ref, wuk_ref, wuv_ref, wa2_ref, ba_ref,
                h1_ref, q_ref, k_ref, v_ref, gq_ref, gk_ref, gv_ref, gr_ref, xq_ref, la_ref):
    x = x_ref[...]
    u1 = _rms(x, n1_ref[...]).astype(BF16)
    h1 = x + 0.5 * _swiglu_half(u1, wgu_ref, wd_ref)
    h1_ref[...] = h1
    u2 = _rms(h1, nm_ref[...]).astype(BF16)

    ang = pos_ref[...].astype(F32) * invf_ref[...]
    cos = jnp.cos(ang)
    sin = jnp.sin(ang)
    lane = lax.broadcasted_iota(jnp.int32, ang.shape, 1)
    sin_hi = jnp.where(lane >= 80, sin, 0.0)
    sin_lo = jnp.where(lane < 80, -sin, 0.0)

    def rope(t):
        return t * cos + pltpu.roll(t, 16, 1) * sin_hi + pltpu.roll(t, LANE - 16, 1) * sin_lo

    cqn = _rms(_dot_nt(u2, win_ref[C_CQ:C_CKV, :]), qn_ref[...]).astype(BF16)
    for p in range(MLA_HEADS // 2):
        qp = _dot(cqn, wuq_ref[:, 2 * p * HEAD_PAD:(2 * p + 2) * HEAD_PAD])
        for j in range(2):
            h = 2 * p + j
            q_ref[p, :, j * HEAD_PAD:(j + 1) * HEAD_PAD] = (
                rope(qp[:, j * HEAD_PAD:(j + 1) * HEAD_PAD]) * MLA_EXP2_SCALE).astype(BF16)

    ckvn = _rms(_dot_nt(u2, win_ref[C_CKV:C_KR, :]), kvn_ref[...]).astype(BF16)
    kr_raw = _dot_nt(u2, win_ref[C_KR:C_GQ, :])
    kr = rope(jnp.concatenate([jnp.zeros((kr_raw.shape[0], MLA_NOPE), F32), kr_raw,
                               jnp.zeros((kr_raw.shape[0], HEAD_PAD - MLA_NOPE - MLA_ROPE), F32)], axis=1))
    kr2 = jnp.concatenate([kr, kr], axis=1)
    for p in range(MLA_HEADS // 2):
        kn = _dot(ckvn, wuk_ref[:, 2 * p * HEAD_PAD:(2 * p + 2) * HEAD_PAD])
        k_ref[p] = (kn + kr2).astype(BF16)
    vv = _dot(ckvn, wuv_ref[...]).astype(BF16)
    for p in range(MLA_HEADS // 2):
        v_ref[p] = vv[:, 2 * p * MLA_V:(2 * p + 2) * MLA_V]

    ga = _dot_nt(u2, win_ref[C_GA:C_GR, :]).astype(BF16)
    gq_ref[...] = _dot_nt(u2, win_ref[C_GQ:C_GK, :])
    gk_ref[...] = _dot_nt(u2, win_ref[C_GK:C_GV, :])
    gv_ref[...] = _dot_nt(u2, win_ref[C_GV:C_GA, :]).astype(BF16)
    t = _dot(ga, wa2_ref[...]) + ba_ref[...]
    log_sig = jnp.minimum(t, 0.0) - jnp.log(1.0 + jnp.exp(-jnp.abs(t)))
    la_ref[...] = log_sig * (1.0 / GLA_TAU)
    gr_ref[...] = _dot_nt(u2, win_ref[C_GR:C_XQ, :]).astype(BF16)
    xq_ref[...] = _dot_nt(u2, win_ref[C_XQ:C_GZ, :]).astype(BF16)


def _mla_kernel(q_ref, k_ref, v_ref, o_ref, vt_ref):
    def pair(p, carry):
        _mla_pair(q_ref.at[p], k_ref.at[p], v_ref.at[p], o_ref.at[p], vt_ref)
        return carry

    lax.fori_loop(0, q_ref.shape[0], pair, 0)


def _mla_pair(q_ref, k_ref, v_ref, o_ref, vt_ref):
    tq = TQ
    nq = q_ref.shape[0] // tq
    key = lax.broadcasted_iota(jnp.int32, (tq, tq), 0)
    qry = lax.broadcasted_iota(jnp.int32, (tq, tq), 1)
    causal = key <= qry
    vt = v_ref[...].astype(F32).T.astype(BF16)
    ones_row = jnp.where(lax.broadcasted_iota(jnp.int32, (MLA_VT_ROWS - MLA_V, vt.shape[1]), 0) == 0,
                         1.0, 0.0).astype(BF16)
    for h in range(2):
        vt_ref[h * MLA_VT_ROWS:h * MLA_VT_ROWS + MLA_V, :] = vt[h * MLA_V:(h + 1) * MLA_V, :]
        vt_ref[h * MLA_VT_ROWS + MLA_V:(h + 1) * MLA_VT_ROWS, :] = ones_row

    items = [(qi, c) for qi in range(nq) for c in range(qi + 1)]
    pending = [{}, {}]
    state = [{}, {}]
    done = {}

    def score(h, i):
        qi, c = items[i]
        q = q_ref[qi * tq:(qi + 1) * tq, h * HEAD_PAD:(h + 1) * HEAD_PAD]
        s = _dot_nt(k_ref[c * tq:(c + 1) * tq, h * HEAD_PAD:(h + 1) * HEAD_PAD], q)
        pending[h][i] = jnp.where(causal, s, -1e30) if c == qi else s

    def consume(h, i):
        qi, c = items[i]
        st = state[h]
        s = pending[h].pop(i)
        cm = jnp.max(s, axis=0, keepdims=True)
        m_new = cm if c == 0 else jnp.maximum(st["m"], cm)
        p = jnp.exp2(s - m_new).astype(BF16)
        pv = _dot(vt_ref[h * MLA_VT_ROWS:(h + 1) * MLA_VT_ROWS, c * tq:(c + 1) * tq], p)
        st["acc"] = pv if c == 0 else st["acc"] * jnp.exp2(st["m"] - m_new) + pv
        st["m"] = m_new
        if c == qi:
            done[(qi, h)] = st["acc"][:MLA_V, :] / st["acc"][MLA_V:MLA_V + 1, :]
            if (qi, 1 - h) in done:
                o_ref[qi * tq:(qi + 1) * tq, :] = jnp.concatenate(
                    [done.pop((qi, 0)), done.pop((qi, 1))], axis=0).T.astype(BF16)

    for t in range(len(items) + MLA_LOOKAHEAD):
        for h in range(2):
            if t < len(items):
                score(h, t)
            if t >= MLA_LOOKAHEAD:
                consume(h, t - MLA_LOOKAHEAD)


def _xattn_kernel(q_ref, mem_ref, n_ref, w_ref, o_ref, kv_ref):
    @pl.when(pl.program_id(1) == 0)
    def _():
        mn = _rms(mem_ref[...], n_ref[...]).astype(BF16)
        kv_ref[...] = _dot(mn, w_ref[...]).astype(BF16)

    scale = 1.0 / math.sqrt(X_DH)
    hw = X_HEADS * X_DH
    for h in range(X_HEADS):
        q = q_ref[:, h * X_DH:(h + 1) * X_DH]
        k = kv_ref[:, h * X_DH:(h + 1) * X_DH]
        v = kv_ref[:, hw + h * X_DH:hw + (h + 1) * X_DH]
        s = _dot_nt(q, k) * scale
        m = jnp.max(s, axis=-1, keepdims=True)
        p = jnp.exp(s - m)
        l = jnp.sum(p, axis=-1, keepdims=True)
        o = _dot(p.astype(BF16), v) / l
        o_ref[:, h * X_DH:(h + 1) * X_DH] = o.astype(BF16)


def _gla_kernel(q_ref, k_ref, la_ref, v_ref, r_ref, on_ref, o_ref):
    sb, c = GLA_SB, GLA_CHUNK
    nsb = q_ref.shape[0] // sb
    nch = sb // c
    npair = GLA_HEADS // 2
    pw, vw = 2 * GLA_DK, 2 * GLA_DV
    row = lax.broadcasted_iota(jnp.int32, (sb, sb), 0)
    col = lax.broadcasted_iota(jnp.int32, (sb, sb), 1)
    tri = jnp.logical_and(row // c == col // c, col <= row)
    tri_bf = jnp.where(tri, 1.0, 0.0).astype(BF16)
    lane = lax.broadcasted_iota(jnp.int32, (1, pw), 1)
    head_lanes = (lane < GLA_DK, lane >= GLA_DK)
    srow = lax.broadcasted_iota(jnp.int32, (pw, vw), 0)
    scol = lax.broadcasted_iota(jnp.int32, (pw, vw), 1)
    same_head = (srow < GLA_DK) == (scol < GLA_DV)

    def local(i):
        r0 = i * sb
        la = la_ref[r0:r0 + sb, :]
        hi = la.astype(BF16)
        lo = (la - hi.astype(F32)).astype(BF16)
        bcum_all = _dot(tri_bf, hi) + _dot(tri_bf, lo)
        out = []
        for p in range(npair):
            bcum = bcum_all[:, p * pw:(p + 1) * pw]
            b_last = [bcum[j * c + c - 1:j * c + c, :] for j in range(nch)]
            b_last_rows = jnp.concatenate([jnp.broadcast_to(b, (c, pw)) for b in b_last], axis=0)
            q_t = q_ref[r0:r0 + sb, p * pw:(p + 1) * pw] * (GLA_DK ** -0.5) * jnp.exp(bcum)
            kk = k_ref[r0:r0 + sb, p * pw:(p + 1) * pw]
            k_t = (kk * jnp.exp(-bcum)).astype(BF16)
            k_d = (kk * jnp.exp(b_last_rows - bcum)).astype(BF16)
            v = v_ref[r0:r0 + sb, p * vw:(p + 1) * vw]
            o_intra = []
            for h in range(2):
                qm = jnp.where(head_lanes[h], q_t, 0.0).astype(BF16)
                att = jnp.where(tri, _dot_nt(qm, k_t), 0.0).astype(BF16)
                o_intra.append(_dot(att, v[:, h * GLA_DV:(h + 1) * GLA_DV]))
            d_st, decay = [], []
            for j in range(nch):
                d = _dot_tn(k_d[j * c:(j + 1) * c, :], v[j * c:(j + 1) * c, :])
                d_st.append(jnp.where(same_head, d, 0.0))
                dcol = jnp.broadcast_to(jnp.exp(b_last[j]), (pw, pw)).T
                decay.append(jnp.concatenate([dcol, dcol], axis=1))
            out.append((q_t.astype(BF16), o_intra, d_st, decay))
        return out

    def recurrent(i, loc, states):
        r0 = i * sb
        for p in range(npair):
            q_bf, o_intra, d_st, decay = loc[p]
            st = states[p]
            o_inter = []
            for j in range(nch):
                o_inter.append(_dot(q_bf[j * c:(j + 1) * c, :], st.astype(BF16)))
                st = decay[j] * st + d_st[j]
            states[p] = st
            o_inter = jnp.concatenate(o_inter, axis=0)
            for h in range(2):
                lo_, hi_ = p * vw + h * GLA_DV, p * vw + (h + 1) * GLA_DV
                o = _rms(o_intra[h] + o_inter[:, h * GLA_DV:(h + 1) * GLA_DV], on_ref[...])
                r = r_ref[r0:r0 + sb, lo_:hi_].astype(F32)
                o_ref[r0:r0 + sb, lo_:hi_] = (o * _silu(r)).astype(BF16)

    states = [jnp.zeros((pw, vw), F32) for _ in range(npair)]
    loc = local(0)
    for i in range(nsb):
        nxt = local(i + 1) if i + 1 < nsb else None
        recurrent(i, loc, states)
        loc = nxt


def _post_kernel(h1_ref, om_ref, og_ref, ox_ref, nm_ref, wgz_ref, gb_ref, wom32_ref, wog32_ref, wox32_ref,
                 wout32_ref, n2_ref, wg32_ref, wu32_ref, wd32_ref, nf_ref, o_ref,
                 wom_ref, wog_ref, wox_ref, wout_ref, wgu_ref, wd_ref):
    i = pl.program_id(0)

    @pl.when(i < W_STEPS)
    def _():
        _convert_ffn(i, wg32_ref, wu32_ref, wd32_ref, wgu_ref, wd_ref)
        for dst, src in ((wom_ref, wom32_ref), (wog_ref, wog32_ref), (wox_ref, wox32_ref)):
            _convert_rows(dst, src[...], i, dst.shape[0] // W_STEPS)
        _convert_rows(wout_ref, wout32_ref[...], i, D_MODEL // W_STEPS)

    @pl.when(i >= W_STEPS)
    def _():
        _post_tokens(h1_ref, om_ref, og_ref, ox_ref, nm_ref, wgz_ref, gb_ref, wom_ref, wog_ref, wox_ref,
                     wout_ref, n2_ref, wgu_ref, wd_ref, nf_ref, o_ref)


def _post_tokens(h1_ref, om_ref, og_ref, ox_ref, nm_ref, wgz_ref, gb_ref, wom_ref, wog_ref, wox_ref,
                 wout_ref, n2_ref, wgu_ref, wd_ref, nf_ref, o_ref):
    h1 = h1_ref[...]
    u = _rms(h1, nm_ref[...]).astype(BF16)
    d = D_MODEL
    branches = ((jnp.concatenate([om_ref[p] for p in range(om_ref.shape[0])], axis=1), wom_ref),
                (og_ref[...], wog_ref), (ox_ref[...], wox_ref))
    tiles = []
    for c0 in range(0, d, MERGE_TILE):
        acc = None
        for n, (o_in, w_br) in enumerate(branches):
            gate = _sigmoid(_dot_nt(u, wgz_ref[n * d + c0:n * d + c0 + MERGE_TILE, :])
                            + gb_ref[:, n * d + c0:n * d + c0 + MERGE_TILE])
            term = gate * _dot(o_in, w_br[:, c0:c0 + MERGE_TILE])
            acc = term if acc is None else acc + term
        tiles.append(acc.astype(BF16))
    h2 = h1 + _dot(jnp.concatenate(tiles, axis=1), wout_ref[...])
    u2 = _rms(h2, n2_ref[...]).astype(BF16)
    h3 = h2 + 0.5 * _swiglu_half(u2, wgu_ref, wd_ref)
    o_ref[...] = _rms(h3, nf_ref[...])


def _cparams(sem, vmem=None, flags=None):
    return pltpu.CompilerParams(dimension_semantics=sem, vmem_limit_bytes=vmem, flags=flags)


def kernel(x, mem, positions, ffn1_norm, ffn1_wg, ffn1_wu, ffn1_wd, mix_norm, mem_norm, w_in, gate_bias,
           mla_q_norm, mla_w_uq, mla_kv_norm, mla_w_ukv, mla_w_o, gla_w_a2, gla_b_a, gla_o_norm, gla_w_o,
           x_w_kv, x_w_o, w_out, ffn2_norm, ffn2_wg, ffn2_wu, ffn2_wd, final_norm):
    B, S, D = x.shape
    T = B * S
    M = mem.shape[1]
    F = D_FF
    H = MLA_HEADS
    bf = lambda a: a.astype(BF16)
    row = lambda a: a.reshape(1, -1)

    w_in_t = bf(jnp.swapaxes(w_in[0], 0, 1))
    w_uq = mla_w_uq[0].reshape(MLA_Q_RANK, H, MLA_NOPE + MLA_ROPE)
    w_uq = bf(jnp.pad(w_uq, ((0, 0), (0, 0), (0, HEAD_PAD - MLA_NOPE - MLA_ROPE))).reshape(MLA_Q_RANK, H * HEAD_PAD))
    w_ukv = mla_w_ukv[0].reshape(MLA_KV_RANK, H, MLA_NOPE + MLA_V)
    w_uk = bf(jnp.pad(w_ukv[:, :, :MLA_NOPE], ((0, 0), (0, 0), (0, HEAD_PAD - MLA_NOPE))).reshape(MLA_KV_RANK, H * HEAD_PAD))
    w_uv = bf(w_ukv[:, :, MLA_NOPE:].reshape(MLA_KV_RANK, H * MLA_V))
    w_a2 = bf(gla_w_a2[0])
    half = MLA_ROPE // 2
    inv_freq = ROPE_THETA ** (-jnp.arange(half, dtype=F32) / half)
    invf = jnp.concatenate([jnp.zeros((MLA_NOPE,), F32), inv_freq, inv_freq,
                            jnp.zeros((HEAD_PAD - MLA_NOPE - MLA_ROPE,), F32)]).reshape(1, HEAD_PAD)

    x2 = x.reshape(T, D)
    pos2 = jnp.broadcast_to(positions.reshape(T, 1), (T, HEAD_PAD))
    nt = T // TM
    tok = lambda w: pl.BlockSpec((TM, w), lambda i: (jnp.maximum(i - W_STEPS, 0), 0))
    npair = H // 2
    tok_pairs = lambda w: pl.BlockSpec((npair, TM, w), lambda i: (0, jnp.maximum(i - W_STEPS, 0), 0))
    wrows = lambda shape: pl.BlockSpec(
        (shape[0] // W_STEPS, shape[1]), lambda i: (jnp.minimum(i, W_STEPS - 1), 0))
    ffn_specs = [wrows((D, F)), wrows((D, F)), wrows((F, D))]
    ffn_scratch = [pltpu.VMEM((D, 2 * F), BF16), pltpu.VMEM((F, D), BF16)]

    gdk, gdv, xw = GLA_HEADS * GLA_DK, GLA_HEADS * GLA_DV, X_HEADS * X_DH
    pre_out_shapes = (
        jax.ShapeDtypeStruct((T, D), F32),
        jax.ShapeDtypeStruct((npair, T, 2 * HEAD_PAD), BF16),
        jax.ShapeDtypeStruct((npair, T, 2 * HEAD_PAD), BF16),
        jax.ShapeDtypeStruct((npair, T, 2 * MLA_V), BF16),
        jax.ShapeDtypeStruct((T, gdk), F32),
        jax.ShapeDtypeStruct((T, gdk), F32),
        jax.ShapeDtypeStruct((T, gdv), BF16),
        jax.ShapeDtypeStruct((T, gdv), BF16),
        jax.ShapeDtypeStruct((T, xw), BF16),
        jax.ShapeDtypeStruct((T, gdk), F32),
    )
    h1, q, k, v, gq, gk, gv, gr, xq, la = pl.pallas_call(
        _pre_kernel,
        out_shape=pre_out_shapes,
        grid=(W_STEPS + nt,),
        in_specs=[tok(D), tok(HEAD_PAD), _const_spec((1, HEAD_PAD)), _const_spec((1, D)), _const_spec((1, D)),
                  *ffn_specs, _const_spec((C_GZ, D)),
                  _const_spec((1, MLA_Q_RANK)), _const_spec((MLA_Q_RANK, H * HEAD_PAD)),
                  _const_spec((1, MLA_KV_RANK)), _const_spec((MLA_KV_RANK, H * HEAD_PAD)),
                  _const_spec((MLA_KV_RANK, H * MLA_V)), _const_spec((GLA_GATE_RANK, gdk)), _const_spec((1, gdk))],
        out_specs=tuple(tok(s.shape[1]) if len(s.shape) == 2 else tok_pairs(s.shape[2])
                        for s in pre_out_shapes),
        scratch_shapes=ffn_scratch,
        compiler_params=_cparams(("arbitrary",), VMEM_LIMIT),
        name="pre",
    )(x2, pos2, invf, row(ffn1_norm[0]), row(mix_norm[0]), ffn1_wg[0], ffn1_wu[0], ffn1_wd[0],
      w_in_t, row(mla_q_norm[0]), w_uq, row(mla_kv_norm[0]), w_uk, w_uv, w_a2, row(gla_b_a[0]))

    seq_pairs = lambda w: pl.BlockSpec((npair, S, w), lambda b: (0, b, 0))
    o_mla = pl.pallas_call(
        _mla_kernel,
        out_shape=jax.ShapeDtypeStruct((npair, T, 2 * MLA_V), BF16),
        grid=(B,),
        in_specs=[seq_pairs(2 * HEAD_PAD), seq_pairs(2 * HEAD_PAD), seq_pairs(2 * MLA_V)],
        out_specs=seq_pairs(2 * MLA_V),
        scratch_shapes=[pltpu.VMEM((2 * MLA_VT_ROWS, S), BF16)],
        compiler_params=_cparams(("parallel",), VMEM_LIMIT),
        name="mla",
    )(q, k, v)

    nqx = S // TQX
    o_x = pl.pallas_call(
        _xattn_kernel,
        out_shape=jax.ShapeDtypeStruct((T, xw), BF16),
        grid=(B, nqx),
        in_specs=[pl.BlockSpec((TQX, xw), lambda b, i: (b * nqx + i, 0)),
                  pl.BlockSpec((M, D), lambda b, i: (b, 0)),
                  _const_spec((1, D)), _const_spec((D, 2 * xw))],
        out_specs=pl.BlockSpec((TQX, xw), lambda b, i: (b * nqx + i, 0)),
        scratch_shapes=[pltpu.VMEM((M, 2 * xw), BF16)],
        compiler_params=_cparams(("arbitrary", "arbitrary")),
        name="xattn",
    )(xq, mem.reshape(B * M, D), row(mem_norm[0]), bf(x_w_kv[0]))

    o_gla = pl.pallas_call(
        _gla_kernel,
        out_shape=jax.ShapeDtypeStruct((T, gdv), BF16),
        grid=(B,),
        in_specs=[pl.BlockSpec((S, gdk), lambda b: (b, 0)),
                  pl.BlockSpec((S, gdk), lambda b: (b, 0)),
                  pl.BlockSpec((S, gdk), lambda b: (b, 0)),
                  pl.BlockSpec((S, gdv), lambda b: (b, 0)),
                  pl.BlockSpec((S, gdv), lambda b: (b, 0)),
                  _const_spec((1, GLA_DV))],
        out_specs=pl.BlockSpec((S, gdv), lambda b: (b, 0)),
        compiler_params=_cparams(("parallel",), VMEM_LIMIT),
        name="gla",
    )(gq, gk, la, gv, gr, row(gla_o_norm[0]))

    out = pl.pallas_call(
        _post_kernel,
        out_shape=jax.ShapeDtypeStruct((T, D), F32),
        grid=(W_STEPS + nt,),
        in_specs=[tok(D), tok_pairs(2 * MLA_V), tok(gdv), tok(xw), _const_spec((1, D)),
                  pl.BlockSpec((pl.Element(N_BRANCH * D), pl.Element(D)), lambda i: (C_GZ, 0),
                               pipeline_mode=pl.Buffered(1)),
                  _const_spec((1, N_BRANCH * D)),
                  wrows((H * MLA_V, D)), wrows((gdv, D)), wrows((xw, D)),
                  wrows((D, D)), _const_spec((1, D)),
                  *ffn_specs, _const_spec((1, D))],
        out_specs=tok(D),
        scratch_shapes=[pltpu.VMEM((H * MLA_V, D), BF16),
                        pltpu.VMEM((gdv, D), BF16), pltpu.VMEM((xw, D), BF16), pltpu.VMEM((D, D), BF16),
                        *ffn_scratch],
        compiler_params=_cparams(("arbitrary",), VMEM_LIMIT),
        name="post",
    )(h1, o_mla, o_gla, o_x, row(mix_norm[0]), w_in_t, gate_bias[0].reshape(1, N_BRANCH * D),
      mla_w_o[0], gla_w_o[0], x_w_o[0], w_out[0], row(ffn2_norm[0]),
      ffn2_wg[0], ffn2_wu[0], ffn2_wd[0], row(final_norm))
    return out.reshape(B, S, D)
```

```python
import math

import jax
import jax.numpy as jnp
from jax import lax
from jax.experimental import pallas as pl
from jax.experimental.pallas import tpu as pltpu

F32 = jnp.float32
BF16 = jnp.bfloat16

D_MODEL = 1024
EPS = 1e-6
MLA_HEADS = 8
MLA_NOPE = 64
MLA_ROPE = 32
MLA_V = 64
MLA_Q_RANK = 384
MLA_KV_RANK = 256
ROPE_THETA = 10000.0
GLA_HEADS = 4
GLA_DK = 64
GLA_DV = 128
GLA_GATE_RANK = 16
GLA_TAU = 16.0
GLA_CHUNK = 64
X_HEADS = 4
X_DH = 128
D_FF = 2816
N_BRANCH = 3

LANE = 128
HEAD_PAD = 128
VMEM_LIMIT = 60 * 1024 * 1024
MLA_EXP2_SCALE = math.log2(math.e) / math.sqrt(MLA_NOPE + MLA_ROPE)
MLA_VT_ROWS = MLA_V + 16
MLA_LOOKAHEAD = 3

C_CQ, C_CKV, C_KR, C_GQ, C_GK, C_GV, C_GA, C_GR, C_XQ, C_GZ, C_END = (
    0, 384, 640, 672, 928, 1184, 1696, 1712, 2224, 2736, 5808)
W_STEPS = 8

TM = 512
MERGE_TILE = 256
TQ = 256
TQX = 2048
GLA_SB = 256


def _dot(a, b):
    return jnp.dot(a, b, preferred_element_type=F32)


def _dot_nt(a, b):
    return lax.dot_general(a, b, (((1,), (1,)), ((), ())), preferred_element_type=F32)


def _dot_tn(a, b):
    return lax.dot_general(a, b, (((0,), (0,)), ((), ())), preferred_element_type=F32)


def _rms(x, g):
    return x * lax.rsqrt(jnp.mean(x * x, axis=-1, keepdims=True) + EPS) * g


def _sigmoid(x):
    return 1.0 / (1.0 + jnp.exp(-x))


def _silu(x):
    return x * _sigmoid(x)


def _const_spec(shape):
    return pl.BlockSpec(shape, lambda *_: (0,) * len(shape), pipeline_mode=pl.Buffered(1))


def _swiglu_half(xn_bf16, wgu_ref, wd_ref):
    tiles = []
    for f0 in range(0, 2 * D_FF, 2 * LANE):
        gu = _dot(xn_bf16, wgu_ref[:, f0:f0 + 2 * LANE])
        tiles.append((_silu(gu[:, :LANE]) * gu[:, LANE:]).astype(BF16))
    return _dot(jnp.concatenate(tiles, axis=1), wd_ref[...])


def _convert_rows(dst_ref, src, step, rows):
    r0 = pl.multiple_of(step * rows, rows)
    dst_ref[pl.ds(r0, rows), :] = src.astype(BF16)


def _convert_ffn(i, wg32_ref, wu32_ref, wd32_ref, wgu_ref, wd_ref):
    rows = D_MODEL // W_STEPS
    r0 = pl.multiple_of(i * rows, rows)
    wg = wg32_ref[...].astype(BF16)
    wu = wu32_ref[...].astype(BF16)
    for j in range(D_FF // LANE):
        wgu_ref[pl.ds(r0, rows), 2 * j * LANE:(2 * j + 1) * LANE] = wg[:, j * LANE:(j + 1) * LANE]
        wgu_ref[pl.ds(r0, rows), (2 * j + 1) * LANE:(2 * j + 2) * LANE] = wu[:, j * LANE:(j + 1) * LANE]
    _convert_rows(wd_ref, wd32_ref[...], i, D_FF // W_STEPS)


def _pre_kernel(x_ref, pos_ref, invf_ref, n1_ref, nm_ref, wg32_ref, wu32_ref, wd32_ref, win_ref,
                qn_ref, wuq_ref, kvn_ref, wuk_ref, wuv_ref, wa2_ref, ba_ref,
                h1_ref, q_ref, k_ref, v_ref, gf_ref, gvr_ref, xq_ref,
                wgu_ref, wd_ref):
    i = pl.program_id(0)

    @pl.when(i < W_STEPS)
    def _():
        _convert_ffn(i, wg32_ref, wu32_ref, wd32_ref, wgu_ref, wd_ref)

    @pl.when(i >= W_STEPS)
    def _():
        _pre_tokens(x_ref, pos_ref, invf_ref, n1_ref, nm_ref, wgu_ref, wd_ref, win_ref,
                    qn_ref, wuq_ref, kvn_ref, wuk_ref, wuv_ref, wa2_ref, ba_ref,
                    h1_ref, q_ref, k_ref, v_ref, gf_ref, gvr_ref, xq_ref)


def _pre_tokens(x_ref, pos_ref, invf_ref, n1_ref, nm_ref, wgu_ref, wd_ref, win_ref,
                qn_ref, wuq_ref, kvn_ref, wuk_ref, wuv_ref, wa2_ref, ba_ref,
                h1_ref, q_ref, k_ref, v_ref, gf_ref, gvr_ref, xq_ref):
    x = x_ref[...]
    u1 = _rms(x, n1_ref[...]).astype(BF16)
    h1 = x + 0.5 * _swiglu_half(u1, wgu_ref, wd_ref)
    h1_ref[...] = h1
    u2 = _rms(h1, nm_ref[...]).astype(BF16)

    ang = pos_ref[...].astype(F32) * invf_ref[...]
    cos = jnp.cos(ang)
    sin = jnp.sin(ang)
    lane = lax.broadcasted_iota(jnp.int32, ang.shape, 1)
    sin_hi = jnp.where(lane >= 80, sin, 0.0)
    sin_lo = jnp.where(lane < 80, -sin, 0.0)

    def rope(t):
        return t * cos + pltpu.roll(t, 16, 1) * sin_hi + pltpu.roll(t, LANE - 16, 1) * sin_lo

    cqn = _rms(_dot_nt(u2, win_ref[C_CQ:C_CKV, :]), qn_ref[...]).astype(BF16)
    for p in range(MLA_HEADS // 2):
        qp = _dot(cqn, wuq_ref[:, 2 * p * HEAD_PAD:(2 * p + 2) * HEAD_PAD])
        for j in range(2):
            h = 2 * p + j
            q_ref[p, :, j * HEAD_PAD:(j + 1) * HEAD_PAD] = (
                rope(qp[:, j * HEAD_PAD:(j + 1) * HEAD_PAD]) * MLA_EXP2_SCALE).astype(BF16)

    ckvn = _rms(_dot_nt(u2, win_ref[C_CKV:C_KR, :]), kvn_ref[...]).astype(BF16)
    kr_raw = _dot_nt(u2, win_ref[C_KR:C_GQ, :])
    kr = rope(jnp.concatenate([jnp.zeros((kr_raw.shape[0], MLA_NOPE), F32), kr_raw,
                               jnp.zeros((kr_raw.shape[0], HEAD_PAD - MLA_NOPE - MLA_ROPE), F32)], axis=1))
    kr2 = jnp.concatenate([kr, kr], axis=1)
    for p in range(MLA_HEADS // 2):
        kn = _dot(ckvn, wuk_ref[:, 2 * p * HEAD_PAD:(2 * p + 2) * HEAD_PAD])
        k_ref[p] = (kn + kr2).astype(BF16)
    vv = _dot(ckvn, wuv_ref[...]).astype(BF16)
    for p in range(MLA_HEADS // 2):
        v_ref[p] = vv[:, 2 * p * MLA_V:(2 * p + 2) * MLA_V]

    ga = _dot_nt(u2, win_ref[C_GA:C_GR, :]).astype(BF16)
    gdk, gdv = GLA_HEADS * GLA_DK, GLA_HEADS * GLA_DV
    gf_ref[:, :gdk] = _dot_nt(u2, win_ref[C_GQ:C_GK, :])
    gf_ref[:, gdk:2 * gdk] = _dot_nt(u2, win_ref[C_GK:C_GV, :])
    gvr_ref[:, :gdv] = _dot_nt(u2, win_ref[C_GV:C_GA, :]).astype(BF16)
    t = _dot(ga, wa2_ref[...]) + ba_ref[...]
    log_sig = jnp.minimum(t, 0.0) - jnp.log(1.0 + jnp.exp(-jnp.abs(t)))
    gf_ref[:, 2 * gdk:] = log_sig * (1.0 / GLA_TAU)
    gvr_ref[:, gdv:] = _dot_nt(u2, win_ref[C_GR:C_XQ, :]).astype(BF16)
    xq_ref[...] = _dot_nt(u2, win_ref[C_XQ:C_GZ, :]).astype(BF16)


def _mla_kernel(q_ref, k_ref, v_ref, o_ref, vt_ref):
    def pair(p, carry):
        _mla_pair(q_ref.at[p], k_ref.at[p], v_ref.at[p], o_ref.at[p], vt_ref)
        return carry

    lax.fori_loop(0, q_ref.shape[0], pair, 0)


def _mla_pair(q_ref, k_ref, v_ref, o_ref, vt_ref):
    tq = TQ
    nq = q_ref.shape[0] // tq
    key = lax.broadcasted_iota(jnp.int32, (tq, tq), 0)
    qry = lax.broadcasted_iota(jnp.int32, (tq, tq), 1)
    causal = key <= qry
    vt = v_ref[...].astype(F32).T.astype(BF16)
    ones_row = jnp.where(lax.broadcasted_iota(jnp.int32, (MLA_VT_ROWS - MLA_V, vt.shape[1]), 0) == 0,
                         1.0, 0.0).astype(BF16)
    for h in range(2):
        vt_ref[h * MLA_VT_ROWS:h * MLA_VT_ROWS + MLA_V, :] = vt[h * MLA_V:(h + 1) * MLA_V, :]
        vt_ref[h * MLA_VT_ROWS + MLA_V:(h + 1) * MLA_VT_ROWS, :] = ones_row

    items = [(qi, c) for qi in range(nq) for c in range(qi + 1)]
    pending = [{}, {}]
    state = [{}, {}]
    done = {}

    def score(h, i):
        qi, c = items[i]
        q = q_ref[qi * tq:(qi + 1) * tq, h * HEAD_PAD:(h + 1) * HEAD_PAD]
        s = _dot_nt(k_ref[c * tq:(c + 1) * tq, h * HEAD_PAD:(h + 1) * HEAD_PAD], q)
        pending[h][i] = jnp.where(causal, s, -1e30) if c == qi else s

    def consume(h, i):
        qi, c = items[i]
        st = state[h]
        s = pending[h].pop(i)
        cm = jnp.max(s, axis=0, keepdims=True)
        m_new = cm if c == 0 else jnp.maximum(st["m"], cm)
        p = jnp.exp2(s - m_new).astype(BF16)
        pv = _dot(vt_ref[h * MLA_VT_ROWS:(h + 1) * MLA_VT_ROWS, c * tq:(c + 1) * tq], p)
        st["acc"] = pv if c == 0 else st["acc"] * jnp.exp2(st["m"] - m_new) + pv
        st["m"] = m_new
        if c == qi:
            done[(qi, h)] = st["acc"][:MLA_V, :] / st["acc"][MLA_V:MLA_V + 1, :]
            if (qi, 1 - h) in done:
                o_ref[qi * tq:(qi + 1) * tq, :] = jnp.concatenate(
                    [done.pop((qi, 0)), done.pop((qi, 1))], axis=0).T.astype(BF16)

    for t in range(len(items) + MLA_LOOKAHEAD):
        for h in range(2):
            if t < len(items):
                score(h, t)
            if t >= MLA_LOOKAHEAD:
                consume(h, t - MLA_LOOKAHEAD)


def _xattn_kernel(q_ref, mem_ref, n_ref, w_ref, o_ref, kv_ref):
    @pl.when(pl.program_id(1) == 0)
    def _():
        mn = _rms(mem_ref[...], n_ref[...]).astype(BF16)
        kv_ref[...] = _dot(mn, w_ref[...]).astype(BF16)

    scale = 1.0 / math.sqrt(X_DH)
    hw = X_HEADS * X_DH
    for h in range(X_HEADS):
        q = q_ref[:, h * X_DH:(h + 1) * X_DH]
        k = kv_ref[:, h * X_DH:(h + 1) * X_DH]
        v = kv_ref[:, hw + h * X_DH:hw + (h + 1) * X_DH]
        s = _dot_nt(q, k) * scale
        m = jnp.max(s, axis=-1, keepdims=True)
        p = jnp.exp(s - m)
        l = jnp.sum(p, axis=-1, keepdims=True)
        o = _dot(p.astype(BF16), v) / l
        o_ref[:, h * X_DH:(h + 1) * X_DH] = o.astype(BF16)


def _gla_kernel(gf_ref, gvr_ref, on_ref, o_ref):
    sb, c = GLA_SB, GLA_CHUNK
    nsb = gf_ref.shape[0] // sb
    nch = sb // c
    npair = GLA_HEADS // 2
    pw, vw = 2 * GLA_DK, 2 * GLA_DV
    gdk, gdv = GLA_HEADS * GLA_DK, GLA_HEADS * GLA_DV
    row = lax.broadcasted_iota(jnp.int32, (sb, sb), 0)
    col = lax.broadcasted_iota(jnp.int32, (sb, sb), 1)
    tri = jnp.logical_and(row // c == col // c, col <= row)
    tri_bf = jnp.where(tri, 1.0, 0.0).astype(BF16)
    lane = lax.broadcasted_iota(jnp.int32, (1, pw), 1)
    head_lanes = (lane < GLA_DK, lane >= GLA_DK)
    srow = lax.broadcasted_iota(jnp.int32, (pw, vw), 0)
    scol = lax.broadcasted_iota(jnp.int32, (pw, vw), 1)
    same_head = (srow < GLA_DK) == (scol < GLA_DV)

    def local(i):
        r0 = i * sb
        la = gf_ref[r0:r0 + sb, 2 * gdk:]
        hi = la.astype(BF16)
        lo = (la - hi.astype(F32)).astype(BF16)
        bcum_all = _dot(tri_bf, hi) + _dot(tri_bf, lo)
        out = []
        for p in range(npair):
            bcum = bcum_all[:, p * pw:(p + 1) * pw]
            b_last = [bcum[j * c + c - 1:j * c + c, :] for j in range(nch)]
            b_last_rows = jnp.concatenate([jnp.broadcast_to(b, (c, pw)) for b in b_last], axis=0)
            q_t = gf_ref[r0:r0 + sb, p * pw:(p + 1) * pw] * (GLA_DK ** -0.5) * jnp.exp(bcum)
            kk = gf_ref[r0:r0 + sb, gdk + p * pw:gdk + (p + 1) * pw]
            k_t = (kk * jnp.exp(-bcum)).astype(BF16)
            k_d = (kk * jnp.exp(b_last_rows - bcum)).astype(BF16)
            v = gvr_ref[r0:r0 + sb, p * vw:(p + 1) * vw]
            o_intra = []
            for h in range(2):
                qm = jnp.where(head_lanes[h], q_t, 0.0).astype(BF16)
                att = jnp.where(tri, _dot_nt(qm, k_t), 0.0).astype(BF16)
                o_intra.append(_dot(att, v[:, h * GLA_DV:(h + 1) * GLA_DV]))
            d_st, decay = [], []
            for j in range(nch):
                d = _dot_tn(k_d[j * c:(j + 1) * c, :], v[j * c:(j + 1) * c, :])
                d_st.append(jnp.where(same_head, d, 0.0))
                dcol = jnp.broadcast_to(jnp.exp(b_last[j]), (pw, pw)).T
                decay.append(jnp.concatenate([dcol, dcol], axis=1))
            out.append((q_t.astype(BF16), o_intra, d_st, decay))
        return out

    def recurrent(i, loc, states):
        r0 = i * sb
        for p in range(npair):
            q_bf, o_intra, d_st, decay = loc[p]
            st = states[p]
            o_inter = []
            for j in range(nch):
                o_inter.append(_dot(q_bf[j * c:(j + 1) * c, :], st.astype(BF16)))
                st = decay[j] * st + d_st[j]
            states[p] = st
            o_inter = jnp.concatenate(o_inter, axis=0)
            for h in range(2):
                lo_, hi_ = p * vw + h * GLA_DV, p * vw + (h + 1) * GLA_DV
                o = _rms(o_intra[h] + o_inter[:, h * GLA_DV:(h + 1) * GLA_DV], on_ref[...])
                r = gvr_ref[r0:r0 + sb, gdv + lo_:gdv + hi_].astype(F32)
                o_ref[r0:r0 + sb, lo_:hi_] = (o * _silu(r)).astype(BF16)

    states = [jnp.zeros((pw, vw), F32) for _ in range(npair)]
    loc = local(0)
    for i in range(nsb):
        nxt = local(i + 1) if i + 1 < nsb else None
        recurrent(i, loc, states)
        loc = nxt


def _post_kernel(h1_ref, om_ref, og_ref, ox_ref, nm_ref, wgz_ref, gb_ref, wom32_ref, wog32_ref, wox32_ref,
                 wout32_ref, n2_ref, wg32_ref, wu32_ref, wd32_ref, nf_ref, o_ref,
                 wom_ref, wog_ref, wox_ref, wout_ref, wgu_ref, wd_ref):
    i = pl.program_id(0)

    @pl.when(i < W_STEPS)
    def _():
        _convert_ffn(i, wg32_ref, wu32_ref, wd32_ref, wgu_ref, wd_ref)
        for dst, src in ((wom_ref, wom32_ref), (wog_ref, wog32_ref), (wox_ref, wox32_ref)):
            _convert_rows(dst, src[...], i, dst.shape[0] // W_STEPS)
        _convert_rows(wout_ref, wout32_ref[...], i, D_MODEL // W_STEPS)

    @pl.when(i >= W_STEPS)
    def _():
        _post_tokens(h1_ref, om_ref, og_ref, ox_ref, nm_ref, wgz_ref, gb_ref, wom_ref, wog_ref, wox_ref,
                     wout_ref, n2_ref, wgu_ref, wd_ref, nf_ref, o_ref)


def _post_tokens(h1_ref, om_ref, og_ref, ox_ref, nm_ref, wgz_ref, gb_ref, wom_ref, wog_ref, wox_ref,
                 wout_ref, n2_ref, wgu_ref, wd_ref, nf_ref, o_ref):
    h1 = h1_ref[...]
    u = _rms(h1, nm_ref[...]).astype(BF16)
    d = D_MODEL
    branches = ((jnp.concatenate([om_ref[p] for p in range(om_ref.shape[0])], axis=1), wom_ref),
                (og_ref[...], wog_ref), (ox_ref[...], wox_ref))
    tiles = []
    for c0 in range(0, d, MERGE_TILE):
        acc = None
        for n, (o_in, w_br) in enumerate(branches):
            gate = _sigmoid(_dot_nt(u, wgz_ref[n * d + c0:n * d + c0 + MERGE_TILE, :])
                            + gb_ref[:, n * d + c0:n * d + c0 + MERGE_TILE])
            term = gate * _dot(o_in, w_br[:, c0:c0 + MERGE_TILE])
            acc = term if acc is None else acc + term
        tiles.append(acc.astype(BF16))
    h2 = h1 + _dot(jnp.concatenate(tiles, axis=1), wout_ref[...])
    u2 = _rms(h2, n2_ref[...]).astype(BF16)
    h3 = h2 + 0.5 * _swiglu_half(u2, wgu_ref, wd_ref)
    o_ref[...] = _rms(h3, nf_ref[...])


def _cparams(sem, vmem=None, flags=None):
    return pltpu.CompilerParams(dimension_semantics=sem, vmem_limit_bytes=vmem, flags=flags)


def kernel(x, mem, positions, ffn1_norm, ffn1_wg, ffn1_wu, ffn1_wd, mix_norm, mem_norm, w_in, gate_bias,
           mla_q_norm, mla_w_uq, mla_kv_norm, mla_w_ukv, mla_w_o, gla_w_a2, gla_b_a, gla_o_norm, gla_w_o,
           x_w_kv, x_w_o, w_out, ffn2_norm, ffn2_wg, ffn2_wu, ffn2_wd, final_norm):
    B, S, D = x.shape
    T = B * S
    M = mem.shape[1]
    F = D_FF
    H = MLA_HEADS
    bf = lambda a: a.astype(BF16)
    row = lambda a: a.reshape(1, -1)

    w_in_t = bf(jnp.swapaxes(w_in[0], 0, 1))
    w_uq = mla_w_uq[0].reshape(MLA_Q_RANK, H, MLA_NOPE + MLA_ROPE)
    w_uq = bf(jnp.pad(w_uq, ((0, 0), (0, 0), (0, HEAD_PAD - MLA_NOPE - MLA_ROPE))).reshape(MLA_Q_RANK, H * HEAD_PAD))
    w_ukv = mla_w_ukv[0].reshape(MLA_KV_RANK, H, MLA_NOPE + MLA_V)
    w_uk = bf(jnp.pad(w_ukv[:, :, :MLA_NOPE], ((0, 0), (0, 0), (0, HEAD_PAD - MLA_NOPE))).reshape(MLA_KV_RANK, H * HEAD_PAD))
    w_uv = bf(w_ukv[:, :, MLA_NOPE:].reshape(MLA_KV_RANK, H * MLA_V))
    w_a2 = bf(gla_w_a2[0])
    half = MLA_ROPE // 2
    inv_freq = ROPE_THETA ** (-jnp.arange(half, dtype=F32) / half)
    invf = jnp.concatenate([jnp.zeros((MLA_NOPE,), F32), inv_freq, inv_freq,
                            jnp.zeros((HEAD_PAD - MLA_NOPE - MLA_ROPE,), F32)]).reshape(1, HEAD_PAD)

    x2 = x.reshape(T, D)
    pos2 = jnp.broadcast_to(positions.reshape(T, 1), (T, HEAD_PAD))
    nt = T // TM
    tok = lambda w: pl.BlockSpec((TM, w), lambda i: (jnp.maximum(i - W_STEPS, 0), 0))
    npair = H // 2
    tok_pairs = lambda w: pl.BlockSpec((npair, TM, w), lambda i: (0, jnp.maximum(i - W_STEPS, 0), 0))
    wrows = lambda shape: pl.BlockSpec(
        (shape[0] // W_STEPS, shape[1]), lambda i: (jnp.minimum(i, W_STEPS - 1), 0))
    ffn_specs = [wrows((D, F)), wrows((D, F)), wrows((F, D))]
    ffn_scratch = [pltpu.VMEM((D, 2 * F), BF16), pltpu.VMEM((F, D), BF16)]

    gdk, gdv, xw = GLA_HEADS * GLA_DK, GLA_HEADS * GLA_DV, X_HEADS * X_DH
    pre_out_shapes = (
        jax.ShapeDtypeStruct((T, D), F32),
        jax.ShapeDtypeStruct((npair, T, 2 * HEAD_PAD), BF16),
        jax.ShapeDtypeStruct((npair, T, 2 * HEAD_PAD), BF16),
        jax.ShapeDtypeStruct((npair, T, 2 * MLA_V), BF16),
        jax.ShapeDtypeStruct((T, 3 * gdk), F32),
        jax.ShapeDtypeStruct((T, 2 * gdv), BF16),
        jax.ShapeDtypeStruct((T, xw), BF16),
    )
    h1, q, k, v, gf, gvr, xq = pl.pallas_call(
        _pre_kernel,
        out_shape=pre_out_shapes,
        grid=(W_STEPS + nt,),
        in_specs=[tok(D), tok(HEAD_PAD), _const_spec((1, HEAD_PAD)), _const_spec((1, D)), _const_spec((1, D)),
                  *ffn_specs, _const_spec((C_GZ, D)),
                  _const_spec((1, MLA_Q_RANK)), _const_spec((MLA_Q_RANK, H * HEAD_PAD)),
                  _const_spec((1, MLA_KV_RANK)), _const_spec((MLA_KV_RANK, H * HEAD_PAD)),
                  _const_spec((MLA_KV_RANK, H * MLA_V)), _const_spec((GLA_GATE_RANK, gdk)), _const_spec((1, gdk))],
        out_specs=tuple(tok(s.shape[1]) if len(s.shape) == 2 else tok_pairs(s.shape[2])
                        for s in pre_out_shapes),
        scratch_shapes=ffn_scratch,
        compiler_params=_cparams(("arbitrary",), VMEM_LIMIT),
        name="pre",
    )(x2, pos2, invf, row(ffn1_norm[0]), row(mix_norm[0]), ffn1_wg[0], ffn1_wu[0], ffn1_wd[0],
      w_in_t, row(mla_q_norm[0]), w_uq, row(mla_kv_norm[0]), w_uk, w_uv, w_a2, row(gla_b_a[0]))

    seq_pairs = lambda w: pl.BlockSpec((npair, S, w), lambda b: (0, b, 0))
    o_mla = pl.pallas_call(
        _mla_kernel,
        out_shape=jax.ShapeDtypeStruct((npair, T, 2 * MLA_V), BF16),
        grid=(B,),
        in_specs=[seq_pairs(2 * HEAD_PAD), seq_pairs(2 * HEAD_PAD), seq_pairs(2 * MLA_V)],
        out_specs=seq_pairs(2 * MLA_V),
        scratch_shapes=[pltpu.VMEM((2 * MLA_VT_ROWS, S), BF16)],
        compiler_params=_cparams(("parallel",), VMEM_LIMIT),
        name="mla",
    )(q, k, v)

    nqx = S // TQX
    o_x = pl.pallas_call(
        _xattn_kernel,
        out_shape=jax.ShapeDtypeStruct((T, xw), BF16),
        grid=(B, nqx),
        in_specs=[pl.BlockSpec((TQX, xw), lambda b, i: (b * nqx + i, 0)),
                  pl.BlockSpec((M, D), lambda b, i: (b, 0)),
                  _const_spec((1, D)), _const_spec((D, 2 * xw))],
        out_specs=pl.BlockSpec((TQX, xw), lambda b, i: (b * nqx + i, 0)),
        scratch_shapes=[pltpu.VMEM((M, 2 * xw), BF16)],
        compiler_params=_cparams(("arbitrary", "arbitrary")),
        name="xattn",
    )(xq, mem.reshape(B * M, D), row(mem_norm[0]), bf(x_w_kv[0]))

    o_gla = pl.pallas_call(
        _gla_kernel,
        out_shape=jax.ShapeDtypeStruct((T, gdv), BF16),
        grid=(B,),
        in_specs=[pl.BlockSpec((S, 3 * gdk), lambda b: (b, 0)),
                  pl.BlockSpec((S, 2 * gdv), lambda b: (b, 0)),
                  _const_spec((1, GLA_DV))],
        out_specs=pl.BlockSpec((S, gdv), lambda b: (b, 0)),
        compiler_params=_cparams(("parallel",), VMEM_LIMIT),
        name="gla",
    )(gf, gvr, row(gla_o_norm[0]))

    out = pl.pallas_call(
        _post_kernel,
        out_shape=jax.ShapeDtypeStruct((T, D), F32),
        grid=(W_STEPS + nt,),
        in_specs=[tok(D), tok_pairs(2 * MLA_V), tok(gdv), tok(xw), _const_spec((1, D)),
                  pl.BlockSpec((pl.Element(N_BRANCH * D), pl.Element(D)), lambda i: (C_GZ, 0),
                               pipeline_mode=pl.Buffered(1)),
                  _const_spec((1, N_BRANCH * D)),
                  wrows((H * MLA_V, D)), wrows((gdv, D)), wrows((xw, D)),
                  wrows((D, D)), _const_spec((1, D)),
                  *ffn_specs, _const_spec((1, D))],
        out_specs=tok(D),
        scratch_shapes=[pltpu.VMEM((H * MLA_V, D), BF16),
                        pltpu.VMEM((gdv, D), BF16), pltpu.VMEM((xw, D), BF16), pltpu.VMEM((D, D), BF16),
                        *ffn_scratch],
        compiler_params=_cparams(("arbitrary",), VMEM_LIMIT),
        name="post",
    )(h1, o_mla, o_gla, o_x, row(mix_norm[0]), w_in_t, gate_bias[0].reshape(1, N_BRANCH * D),
      mla_w_o[0], gla_w_o[0], x_w_o[0], w_out[0], row(ffn2_norm[0]),
      ffn2_wg[0], ffn2_wu[0], ffn2_wd[0], row(final_norm))
    return out.reshape(B, S, D)
```

```python
import math

import jax
import jax.numpy as jnp
from jax import lax
from jax.experimental import pallas as pl
from jax.experimental.pallas import tpu as pltpu

F32 = jnp.float32
BF16 = jnp.bfloat16

D_MODEL = 1024
EPS = 1e-6
MLA_HEADS = 8
MLA_NOPE = 64
MLA_ROPE = 32
MLA_V = 64
MLA_Q_RANK = 384
MLA_KV_RANK = 256
ROPE_THETA = 10000.0
GLA_HEADS = 4
GLA_DK = 64
GLA_DV = 128
GLA_GATE_RANK = 16
GLA_TAU = 16.0
GLA_CHUNK = 64
X_HEADS = 4
X_DH = 128
D_FF = 2816
N_BRANCH = 3

LANE = 128
HEAD_PAD = 128
VMEM_LIMIT = 60 * 1024 * 1024
MLA_EXP2_SCALE = math.log2(math.e) / math.sqrt(MLA_NOPE + MLA_ROPE)
MLA_VT_ROWS = MLA_V + 16
MLA_GROUP = 2
MLA_LOOKAHEAD = 3

C_CQ, C_CKV, C_KR, C_GQ, C_GK, C_GV, C_GA, C_GR, C_XQ, C_GZ, C_END = (
    0, 384, 640, 672, 928, 1184, 1696, 1712, 2224, 2736, 5808)
W_STEPS = 8

TM = 512
MERGE_TILE = 256
TQ = 256
TQX = 2048
GLA_SB = 256


def _dot(a, b):
    return jnp.dot(a, b, preferred_element_type=F32)


def _dot_nt(a, b):
    return lax.dot_general(a, b, (((1,), (1,)), ((), ())), preferred_element_type=F32)


def _dot_tn(a, b):
    return lax.dot_general(a, b, (((0,), (0,)), ((), ())), preferred_element_type=F32)


def _rms(x, g):
    return x * lax.rsqrt(jnp.mean(x * x, axis=-1, keepdims=True) + EPS) * g


def _sigmoid(x):
    return 1.0 / (1.0 + jnp.exp(-x))


def _silu(x):
    return x * _sigmoid(x)


def _const_spec(shape):
    return pl.BlockSpec(shape, lambda *_: (0,) * len(shape), pipeline_mode=pl.Buffered(1))


def _swiglu_half(xn_bf16, wgu_ref, wd_ref):
    tiles = []
    for f0 in range(0, 2 * D_FF, 2 * LANE):
        gu = _dot(xn_bf16, wgu_ref[:, f0:f0 + 2 * LANE])
        tiles.append((_silu(gu[:, :LANE]) * gu[:, LANE:]).astype(BF16))
    return _dot(jnp.concatenate(tiles, axis=1), wd_ref[...])


def _convert_rows(dst_ref, src, step, rows):
    r0 = pl.multiple_of(step * rows, rows)
    dst_ref[pl.ds(r0, rows), :] = src.astype(BF16)


def _convert_ffn(i, wg32_ref, wu32_ref, wd32_ref, wgu_ref, wd_ref):
    rows = D_MODEL // W_STEPS
    r0 = pl.multiple_of(i * rows, rows)
    wg = wg32_ref[...].astype(BF16)
    wu = wu32_ref[...].astype(BF16)
    for j in range(D_FF // LANE):
        wgu_ref[pl.ds(r0, rows), 2 * j * LANE:(2 * j + 1) * LANE] = wg[:, j * LANE:(j + 1) * LANE]
        wgu_ref[pl.ds(r0, rows), (2 * j + 1) * LANE:(2 * j + 2) * LANE] = wu[:, j * LANE:(j + 1) * LANE]
    _convert_rows(wd_ref, wd32_ref[...], i, D_FF // W_STEPS)


def _pre_kernel(x_ref, pos_ref, invf_ref, n1_ref, nm_ref, wg32_ref, wu32_ref, wd32_ref, win_ref,
                qn_ref, wuq_ref, kvn_ref, wuk_ref, wuv_ref, wa2_ref, ba_ref,
                h1_ref, q_ref, k_ref, v_ref, gf_ref, gvr_ref, xq_ref,
                wgu_ref, wd_ref):
    i = pl.program_id(0)

    @pl.when(i < W_STEPS)
    def _():
        _convert_ffn(i, wg32_ref, wu32_ref, wd32_ref, wgu_ref, wd_ref)

    @pl.when(i >= W_STEPS)
    def _():
        _pre_tokens(x_ref, pos_ref, invf_ref, n1_ref, nm_ref, wgu_ref, wd_ref, win_ref,
                    qn_ref, wuq_ref, kvn_ref, wuk_ref, wuv_ref, wa2_ref, ba_ref,
                    h1_ref, q_ref, k_ref, v_ref, gf_ref, gvr_ref, xq_ref)


def _pre_tokens(x_ref, pos_ref, invf_ref, n1_ref, nm_ref, wgu_ref, wd_ref, win_ref,
                qn_ref, wuq_ref, kvn_ref, wuk_ref, wuv_ref, wa2_ref, ba_ref,
                h1_ref, q_ref, k_ref, v_ref, gf_ref, gvr_ref, xq_ref):
    x = x_ref[...]
    u1 = _rms(x, n1_ref[...]).astype(BF16)
    h1 = x + 0.5 * _swiglu_half(u1, wgu_ref, wd_ref)
    h1_ref[...] = h1
    u2 = _rms(h1, nm_ref[...]).astype(BF16)

    ang = pos_ref[...].astype(F32) * invf_ref[...]
    cos = jnp.cos(ang)
    sin = jnp.sin(ang)
    lane = lax.broadcasted_iota(jnp.int32, ang.shape, 1)
    sin_hi = jnp.where(lane >= 80, sin, 0.0)
    sin_lo = jnp.where(lane < 80, -sin, 0.0)

    def rope(t):
        return t * cos + pltpu.roll(t, 16, 1) * sin_hi + pltpu.roll(t, LANE - 16, 1) * sin_lo

    cqn = _rms(_dot_nt(u2, win_ref[C_CQ:C_CKV, :]), qn_ref[...]).astype(BF16)
    for p in range(MLA_HEADS // 2):
        qp = _dot(cqn, wuq_ref[:, 2 * p * HEAD_PAD:(2 * p + 2) * HEAD_PAD])
        for j in range(2):
            h = 2 * p + j
            q_ref[p, :, j * HEAD_PAD:(j + 1) * HEAD_PAD] = (
                rope(qp[:, j * HEAD_PAD:(j + 1) * HEAD_PAD]) * MLA_EXP2_SCALE).astype(BF16)

    ckvn = _rms(_dot_nt(u2, win_ref[C_CKV:C_KR, :]), kvn_ref[...]).astype(BF16)
    kr_raw = _dot_nt(u2, win_ref[C_KR:C_GQ, :])
    kr = rope(jnp.concatenate([jnp.zeros((kr_raw.shape[0], MLA_NOPE), F32), kr_raw,
                               jnp.zeros((kr_raw.shape[0], HEAD_PAD - MLA_NOPE - MLA_ROPE), F32)], axis=1))
    kr2 = jnp.concatenate([kr, kr], axis=1)
    for p in range(MLA_HEADS // 2):
        kn = _dot(ckvn, wuk_ref[:, 2 * p * HEAD_PAD:(2 * p + 2) * HEAD_PAD])
        k_ref[p] = (kn + kr2).astype(BF16)
    vv = _dot(ckvn, wuv_ref[...]).astype(BF16)
    for p in range(MLA_HEADS // 2):
        v_ref[p] = vv[:, 2 * p * MLA_V:(2 * p + 2) * MLA_V]

    ga = _dot_nt(u2, win_ref[C_GA:C_GR, :]).astype(BF16)
    gdk, gdv = GLA_HEADS * GLA_DK, GLA_HEADS * GLA_DV
    gf_ref[:, :gdk] = _dot_nt(u2, win_ref[C_GQ:C_GK, :])
    gf_ref[:, gdk:2 * gdk] = _dot_nt(u2, win_ref[C_GK:C_GV, :])
    gvr_ref[:, :gdv] = _dot_nt(u2, win_ref[C_GV:C_GA, :]).astype(BF16)
    t = _dot(ga, wa2_ref[...]) + ba_ref[...]
    log_sig = jnp.minimum(t, 0.0) - jnp.log(1.0 + jnp.exp(-jnp.abs(t)))
    gf_ref[:, 2 * gdk:] = log_sig * (1.0 / GLA_TAU)
    gvr_ref[:, gdv:] = _dot_nt(u2, win_ref[C_GR:C_XQ, :]).astype(BF16)
    xq_ref[...] = _dot_nt(u2, win_ref[C_XQ:C_GZ, :]).astype(BF16)


def _mla_kernel(q_ref, k_ref, v_ref, o_ref, vt_ref):
    def group(g, carry):
        sl = pl.ds(g * MLA_GROUP, MLA_GROUP)
        _mla_pairs(q_ref.at[sl], k_ref.at[sl], v_ref.at[sl], o_ref.at[sl], vt_ref)
        return carry

    lax.fori_loop(0, q_ref.shape[0] // MLA_GROUP, group, 0)


def _mla_pairs(q_ref, k_ref, v_ref, o_ref, vt_ref):
    npair = q_ref.shape[0]
    tq = TQ
    nq = q_ref.shape[1] // tq
    key = lax.broadcasted_iota(jnp.int32, (tq, tq), 0)
    qry = lax.broadcasted_iota(jnp.int32, (tq, tq), 1)
    causal = key <= qry
    ones_row = jnp.where(lax.broadcasted_iota(jnp.int32, (MLA_VT_ROWS - MLA_V, q_ref.shape[1]), 0) == 0,
                         1.0, 0.0).astype(BF16)

    def prepare(p):
        vt = v_ref[p].astype(F32).T.astype(BF16)
        for h in range(2):
            vt_ref[p, h * MLA_VT_ROWS:h * MLA_VT_ROWS + MLA_V, :] = vt[h * MLA_V:(h + 1) * MLA_V, :]
            vt_ref[p, h * MLA_VT_ROWS + MLA_V:(h + 1) * MLA_VT_ROWS, :] = ones_row

    items = [(p, qi, c) for p in range(npair) for qi in range(nq) for c in range(qi + 1)]
    pending = [{}, {}]
    state = [{}, {}]
    done = {}

    def score(h, i):
        p, qi, c = items[i]
        q = q_ref[p, qi * tq:(qi + 1) * tq, h * HEAD_PAD:(h + 1) * HEAD_PAD]
        s = _dot_nt(k_ref[p, c * tq:(c + 1) * tq, h * HEAD_PAD:(h + 1) * HEAD_PAD], q)
        pending[h][i] = jnp.where(causal, s, -1e30) if c == qi else s

    def consume(h, i):
        p, qi, c = items[i]
        st = state[h]
        s = pending[h].pop(i)
        cm = jnp.max(s, axis=0, keepdims=True)
        m_new = cm if c == 0 else jnp.maximum(st["m"], cm)
        pr = jnp.exp2(s - m_new).astype(BF16)
        pv = _dot(vt_ref[p, h * MLA_VT_ROWS:(h + 1) * MLA_VT_ROWS, c * tq:(c + 1) * tq], pr)
        st["acc"] = pv if c == 0 else st["acc"] * jnp.exp2(st["m"] - m_new) + pv
        st["m"] = m_new
        if c == qi:
            done[(p, qi, h)] = st["acc"][:MLA_V, :] / st["acc"][MLA_V:MLA_V + 1, :]
            if (p, qi, 1 - h) in done:
                o_ref[p, qi * tq:(qi + 1) * tq, :] = jnp.concatenate(
                    [done.pop((p, qi, 0)), done.pop((p, qi, 1))], axis=0).T.astype(BF16)

    for t in range(len(items) + MLA_LOOKAHEAD):
        if t < len(items) and items[t][1:] == (0, 0):
            prepare(items[t][0])
        for h in range(2):
            if t < len(items):
                score(h, t)
            if t >= MLA_LOOKAHEAD:
                consume(h, t - MLA_LOOKAHEAD)


def _xattn_kernel(q_ref, mem_ref, n_ref, w_ref, o_ref, kv_ref):
    @pl.when(pl.program_id(1) == 0)
    def _():
        mn = _rms(mem_ref[...], n_ref[...]).astype(BF16)
        kv_ref[...] = _dot(mn, w_ref[...]).astype(BF16)

    scale = 1.0 / math.sqrt(X_DH)
    hw = X_HEADS * X_DH
    for h in range(X_HEADS):
        q = q_ref[:, h * X_DH:(h + 1) * X_DH]
        k = kv_ref[:, h * X_DH:(h + 1) * X_DH]
        v = kv_ref[:, hw + h * X_DH:hw + (h + 1) * X_DH]
        s = _dot_nt(q, k) * scale
        m = jnp.max(s, axis=-1, keepdims=True)
        p = jnp.exp(s - m)
        l = jnp.sum(p, axis=-1, keepdims=True)
        o = _dot(p.astype(BF16), v) / l
        o_ref[:, h * X_DH:(h + 1) * X_DH] = o.astype(BF16)


def _gla_kernel(gf_ref, gvr_ref, on_ref, o_ref):
    sb, c = GLA_SB, GLA_CHUNK
    nsb = gf_ref.shape[0] // sb
    nch = sb // c
    npair = GLA_HEADS // 2
    pw, vw = 2 * GLA_DK, 2 * GLA_DV
    gdk, gdv = GLA_HEADS * GLA_DK, GLA_HEADS * GLA_DV
    row = lax.broadcasted_iota(jnp.int32, (sb, sb), 0)
    col = lax.broadcasted_iota(jnp.int32, (sb, sb), 1)
    tri = jnp.logical_and(row // c == col // c, col <= row)
    tri_bf = jnp.where(tri, 1.0, 0.0).astype(BF16)
    lane = lax.broadcasted_iota(jnp.int32, (1, pw), 1)
    head_lanes = (lane < GLA_DK, lane >= GLA_DK)
    srow = lax.broadcasted_iota(jnp.int32, (pw, vw), 0)
    scol = lax.broadcasted_iota(jnp.int32, (pw, vw), 1)
    same_head = (srow < GLA_DK) == (scol < GLA_DV)

    def local(i):
        r0 = i * sb
        la = gf_ref[r0:r0 + sb, 2 * gdk:]
        hi = la.astype(BF16)
        lo = (la - hi.astype(F32)).astype(BF16)
        bcum_all = _dot(tri_bf, hi) + _dot(tri_bf, lo)
        out = []
        for p in range(npair):
            bcum = bcum_all[:, p * pw:(p + 1) * pw]
            e_last = [jnp.exp(bcum[j * c + c - 1:j * c + c, :]) for j in range(nch)]
            e_last_rows = jnp.concatenate([jnp.broadcast_to(e, (c, pw)) for e in e_last], axis=0)
            q_t = gf_ref[r0:r0 + sb, p * pw:(p + 1) * pw] * (GLA_DK ** -0.5) * jnp.exp(bcum)
            kk = gf_ref[r0:r0 + sb, gdk + p * pw:gdk + (p + 1) * pw]
            k_undecayed = kk * jnp.exp(-bcum)
            k_t = k_undecayed.astype(BF16)
            k_d = (k_undecayed * e_last_rows).astype(BF16)
            v = gvr_ref[r0:r0 + sb, p * vw:(p + 1) * vw]
            o_intra = []
            for h in range(2):
                qm = jnp.where(head_lanes[h], q_t, 0.0).astype(BF16)
                att = jnp.where(tri, _dot_nt(qm, k_t), 0.0).astype(BF16)
                o_intra.append(_dot(att, v[:, h * GLA_DV:(h + 1) * GLA_DV]))
            d_st, decay = [], []
            for j in range(nch):
                d = _dot_tn(k_d[j * c:(j + 1) * c, :], v[j * c:(j + 1) * c, :])
                d_st.append(jnp.where(same_head, d, 0.0))
                dcol = jnp.broadcast_to(e_last[j], (pw, pw)).T
                decay.append(jnp.concatenate([dcol, dcol], axis=1))
            out.append((q_t.astype(BF16), o_intra, d_st, decay))
        return out

    def recurrent(i, loc, states):
        r0 = i * sb
        for p in range(npair):
            q_bf, o_intra, d_st, decay = loc[p]
            st = states[p]
            o_inter = []
            for j in range(nch):
                o_inter.append(_dot(q_bf[j * c:(j + 1) * c, :], st.astype(BF16)))
                st = decay[j] * st + d_st[j]
            states[p] = st
            o_inter = jnp.concatenate(o_inter, axis=0)
            for h in range(2):
                lo_, hi_ = p * vw + h * GLA_DV, p * vw + (h + 1) * GLA_DV
                o = _rms(o_intra[h] + o_inter[:, h * GLA_DV:(h + 1) * GLA_DV], on_ref[...])
                r = gvr_ref[r0:r0 + sb, gdv + lo_:gdv + hi_].astype(F32)
                o_ref[r0:r0 + sb, lo_:hi_] = (o * _silu(r)).astype(BF16)

    states = [jnp.zeros((pw, vw), F32) for _ in range(npair)]
    loc = local(0)
    for i in range(nsb):
        nxt = local(i + 1) if i + 1 < nsb else None
        recurrent(i, loc, states)
        loc = nxt


def _post_kernel(h1_ref, om_ref, og_ref, ox_ref, nm_ref, wgz_ref, gb_ref, wom32_ref, wog32_ref, wox32_ref,
                 wout32_ref, n2_ref, wg32_ref, wu32_ref, wd32_ref, nf_ref, o_ref,
                 wom_ref, wog_ref, wox_ref, wout_ref, wgu_ref, wd_ref):
    i = pl.program_id(0)

    @pl.when(i < W_STEPS)
    def _():
        _convert_ffn(i, wg32_ref, wu32_ref, wd32_ref, wgu_ref, wd_ref)
        for dst, src in ((wom_ref, wom32_ref), (wog_ref, wog32_ref), (wox_ref, wox32_ref)):
            _convert_rows(dst, src[...], i, dst.shape[0] // W_STEPS)
        _convert_rows(wout_ref, wout32_ref[...], i, D_MODEL // W_STEPS)

    @pl.when(i >= W_STEPS)
    def _():
        _post_tokens(h1_ref, om_ref, og_ref, ox_ref, nm_ref, wgz_ref, gb_ref, wom_ref, wog_ref, wox_ref,
                     wout_ref, n2_ref, wgu_ref, wd_ref, nf_ref, o_ref)


def _post_tokens(h1_ref, om_ref, og_ref, ox_ref, nm_ref, wgz_ref, gb_ref, wom_ref, wog_ref, wox_ref,
                 wout_ref, n2_ref, wgu_ref, wd_ref, nf_ref, o_ref):
    h1 = h1_ref[...]
    u = _rms(h1, nm_ref[...]).astype(BF16)
    d = D_MODEL
    branches = ((jnp.concatenate([om_ref[p] for p in range(om_ref.shape[0])], axis=1), wom_ref),
                (og_ref[...], wog_ref), (ox_ref[...], wox_ref))
    tiles = []
    for c0 in range(0, d, MERGE_TILE):
        acc = None
        for n, (o_in, w_br) in enumerate(branches):
            gate = _sigmoid(_dot_nt(u, wgz_ref[n * d + c0:n * d + c0 + MERGE_TILE, :])
                            + gb_ref[:, n * d + c0:n * d + c0 + MERGE_TILE])
            term = gate * _dot(o_in, w_br[:, c0:c0 + MERGE_TILE])
            acc = term if acc is None else acc + term
        tiles.append(acc.astype(BF16))
    h2 = h1 + _dot(jnp.concatenate(tiles, axis=1), wout_ref[...])
    u2 = _rms(h2, n2_ref[...]).astype(BF16)
    h3 = h2 + 0.5 * _swiglu_half(u2, wgu_ref, wd_ref)
    o_ref[...] = _rms(h3, nf_ref[...])


def _cparams(sem, vmem=None, flags=None):
    return pltpu.CompilerParams(dimension_semantics=sem, vmem_limit_bytes=vmem, flags=flags)


def kernel(x, mem, positions, ffn1_norm, ffn1_wg, ffn1_wu, ffn1_wd, mix_norm, mem_norm, w_in, gate_bias,
           mla_q_norm, mla_w_uq, mla_kv_norm, mla_w_ukv, mla_w_o, gla_w_a2, gla_b_a, gla_o_norm, gla_w_o,
           x_w_kv, x_w_o, w_out, ffn2_norm, ffn2_wg, ffn2_wu, ffn2_wd, final_norm):
    B, S, D = x.shape
    T = B * S
    M = mem.shape[1]
    F = D_FF
    H = MLA_HEADS
    bf = lambda a: a.astype(BF16)
    row = lambda a: a.reshape(1, -1)

    w_in_t = bf(jnp.swapaxes(w_in[0], 0, 1))
    w_uq = mla_w_uq[0].reshape(MLA_Q_RANK, H, MLA_NOPE + MLA_ROPE)
    w_uq = bf(jnp.pad(w_uq, ((0, 0), (0, 0), (0, HEAD_PAD - MLA_NOPE - MLA_ROPE))).reshape(MLA_Q_RANK, H * HEAD_PAD))
    w_ukv = mla_w_ukv[0].reshape(MLA_KV_RANK, H, MLA_NOPE + MLA_V)
    w_uk = bf(jnp.pad(w_ukv[:, :, :MLA_NOPE], ((0, 0), (0, 0), (0, HEAD_PAD - MLA_NOPE))).reshape(MLA_KV_RANK, H * HEAD_PAD))
    w_uv = bf(w_ukv[:, :, MLA_NOPE:].reshape(MLA_KV_RANK, H * MLA_V))
    w_a2 = bf(gla_w_a2[0])
    half = MLA_ROPE // 2
    inv_freq = ROPE_THETA ** (-jnp.arange(half, dtype=F32) / half)
    invf = jnp.concatenate([jnp.zeros((MLA_NOPE,), F32), inv_freq, inv_freq,
                            jnp.zeros((HEAD_PAD - MLA_NOPE - MLA_ROPE,), F32)]).reshape(1, HEAD_PAD)

    x2 = x.reshape(T, D)
    pos2 = jnp.broadcast_to(positions.reshape(T, 1), (T, HEAD_PAD))
    nt = T // TM
    tok = lambda w: pl.BlockSpec((TM, w), lambda i: (jnp.maximum(i - W_STEPS, 0), 0))
    npair = H // 2
    tok_pairs = lambda w: pl.BlockSpec((npair, TM, w), lambda i: (0, jnp.maximum(i - W_STEPS, 0), 0))
    wrows = lambda shape: pl.BlockSpec(
        (shape[0] // W_STEPS, shape[1]), lambda i: (jnp.minimum(i, W_STEPS - 1), 0))
    ffn_specs = [wrows((D, F)), wrows((D, F)), wrows((F, D))]
    ffn_scratch = [pltpu.VMEM((D, 2 * F), BF16), pltpu.VMEM((F, D), BF16)]

    gdk, gdv, xw = GLA_HEADS * GLA_DK, GLA_HEADS * GLA_DV, X_HEADS * X_DH
    pre_out_shapes = (
        jax.ShapeDtypeStruct((T, D), F32),
        jax.ShapeDtypeStruct((npair, T, 2 * HEAD_PAD), BF16),
        jax.ShapeDtypeStruct((npair, T, 2 * HEAD_PAD), BF16),
        jax.ShapeDtypeStruct((npair, T, 2 * MLA_V), BF16),
        jax.ShapeDtypeStruct((T, 3 * gdk), F32),
        jax.ShapeDtypeStruct((T, 2 * gdv), BF16),
        jax.ShapeDtypeStruct((T, xw), BF16),
    )
    h1, q, k, v, gf, gvr, xq = pl.pallas_call(
        _pre_kernel,
        out_shape=pre_out_shapes,
        grid=(W_STEPS + nt,),
        in_specs=[tok(D), tok(HEAD_PAD), _const_spec((1, HEAD_PAD)), _const_spec((1, D)), _const_spec((1, D)),
                  *ffn_specs, _const_spec((C_GZ, D)),
                  _const_spec((1, MLA_Q_RANK)), _const_spec((MLA_Q_RANK, H * HEAD_PAD)),
                  _const_spec((1, MLA_KV_RANK)), _const_spec((MLA_KV_RANK, H * HEAD_PAD)),
                  _const_spec((MLA_KV_RANK, H * MLA_V)), _const_spec((GLA_GATE_RANK, gdk)), _const_spec((1, gdk))],
        out_specs=tuple(tok(s.shape[1]) if len(s.shape) == 2 else tok_pairs(s.shape[2])
                        for s in pre_out_shapes),
        scratch_shapes=ffn_scratch,
        compiler_params=_cparams(("arbitrary",), VMEM_LIMIT),
        name="pre",
    )(x2, pos2, invf, row(ffn1_norm[0]), row(mix_norm[0]), ffn1_wg[0], ffn1_wu[0], ffn1_wd[0],
      w_in_t, row(mla_q_norm[0]), w_uq, row(mla_kv_norm[0]), w_uk, w_uv, w_a2, row(gla_b_a[0]))

    seq_pairs = lambda w: pl.BlockSpec((npair, S, w), lambda b: (0, b, 0))
    o_mla = pl.pallas_call(
        _mla_kernel,
        out_shape=jax.ShapeDtypeStruct((npair, T, 2 * MLA_V), BF16),
        grid=(B,),
        in_specs=[seq_pairs(2 * HEAD_PAD), seq_pairs(2 * HEAD_PAD), seq_pairs(2 * MLA_V)],
        out_specs=seq_pairs(2 * MLA_V),
        scratch_shapes=[pltpu.VMEM((MLA_GROUP, 2 * MLA_VT_ROWS, S), BF16)],
        compiler_params=_cparams(("parallel",), VMEM_LIMIT),
        name="mla",
    )(q, k, v)

    nqx = S // TQX
    o_x = pl.pallas_call(
        _xattn_kernel,
        out_shape=jax.ShapeDtypeStruct((T, xw), BF16),
        grid=(B, nqx),
        in_specs=[pl.BlockSpec((TQX, xw), lambda b, i: (b * nqx + i, 0)),
                  pl.BlockSpec((M, D), lambda b, i: (b, 0)),
                  _const_spec((1, D)), _const_spec((D, 2 * xw))],
        out_specs=pl.BlockSpec((TQX, xw), lambda b, i: (b * nqx + i, 0)),
        scratch_shapes=[pltpu.VMEM((M, 2 * xw), BF16)],
        compiler_params=_cparams(("arbitrary", "arbitrary")),
        name="xattn",
    )(xq, mem.reshape(B * M, D), row(mem_norm[0]), bf(x_w_kv[0]))

    o_gla = pl.pallas_call(
        _gla_kernel,
        out_shape=jax.ShapeDtypeStruct((T, gdv), BF16),
        grid=(B,),
        in_specs=[pl.BlockSpec((S, 3 * gdk), lambda b: (b, 0)),
                  pl.BlockSpec((S, 2 * gdv), lambda b: (b, 0)),
                  _const_spec((1, GLA_DV))],
        out_specs=pl.BlockSpec((S, gdv), lambda b: (b, 0)),
        compiler_params=_cparams(("parallel",), VMEM_LIMIT),
        name="gla",
    )(gf, gvr, row(gla_o_norm[0]))

    out = pl.pallas_call(
        _post_kernel,
        out_shape=jax.ShapeDtypeStruct((T, D), F32),
        grid=(W_STEPS + nt,),
        in_specs=[tok(D), tok_pairs(2 * MLA_V), tok(gdv), tok(xw), _const_spec((1, D)),
                  pl.BlockSpec((pl.Element(N_BRANCH * D), pl.Element(D)), lambda i: (C_GZ, 0),
                               pipeline_mode=pl.Buffered(1)),
                  _const_spec((1, N_BRANCH * D)),
                  wrows((H * MLA_V, D)), wrows((gdv, D)), wrows((xw, D)),
                  wrows((D, D)), _const_spec((1, D)),
                  *ffn_specs, _const_spec((1, D))],
        out_specs=tok(D),
        scratch_shapes=[pltpu.VMEM((H * MLA_V, D), BF16),
                        pltpu.VMEM((gdv, D), BF16), pltpu.VMEM((xw, D), BF16), pltpu.VMEM((D, D), BF16),
                        *ffn_scratch],
        compiler_params=_cparams(("arbitrary",), VMEM_LIMIT),
        name="post",
    )(h1, o_mla, o_gla, o_x, row(mix_norm[0]), w_in_t, gate_bias[0].reshape(1, N_BRANCH * D),
      mla_w_o[0], gla_w_o[0], x_w_o[0], w_out[0], row(ffn2_norm[0]),
      ffn2_wg[0], ffn2_wu[0], ffn2_wd[0], row(final_norm))
    return out.reshape(B, S, D)
```

```python
import functools
import math

import jax
import jax.numpy as jnp
from jax import lax
from jax.experimental import pallas as pl
from jax.experimental.pallas import tpu as pltpu

F32 = jnp.float32
BF16 = jnp.bfloat16

D_MODEL = 1024
EPS = 1e-6
MLA_HEADS = 8
MLA_NOPE = 64
MLA_ROPE = 32
MLA_V = 64
MLA_Q_RANK = 384
MLA_KV_RANK = 256
ROPE_THETA = 10000.0
GLA_HEADS = 4
GLA_DK = 64
GLA_DV = 128
GLA_GATE_RANK = 16
GLA_TAU = 16.0
GLA_CHUNK = 64
X_HEADS = 4
X_DH = 128
D_FF = 2816
N_BRANCH = 3

LANE = 128
HEAD_PAD = 128
VMEM_LIMIT = 60 * 1024 * 1024
MLA_EXP2_SCALE = math.log2(math.e) / math.sqrt(MLA_NOPE + MLA_ROPE)
MLA_VT_ROWS = MLA_V + 16
MLA_GROUP = 2
MLA_LOOKAHEAD = 3

C_CQ, C_CKV, C_KR, C_GQ, C_GK, C_GV, C_GA, C_GR, C_XQ, C_GZ, C_END = (
    0, 384, 640, 672, 928, 1184, 1696, 1712, 2224, 2736, 5808)
W_STEPS = 8

TM_FFN = 1024
TM = 1024
MERGE_TILE = 256
TQ = 256
TQX = 2048
GLA_SB = 256


def _dot(a, b):
    return jnp.dot(a, b, preferred_element_type=F32)


def _dot_nt(a, b):
    return lax.dot_general(a, b, (((1,), (1,)), ((), ())), preferred_element_type=F32)


def _dot_tn(a, b):
    return lax.dot_general(a, b, (((0,), (0,)), ((), ())), preferred_element_type=F32)


def _rms(x, g):
    return x * lax.rsqrt(jnp.mean(x * x, axis=-1, keepdims=True) + EPS) * g


def _sigmoid(x):
    return 1.0 / (1.0 + jnp.exp(-x))


def _silu(x):
    return x * _sigmoid(x)


def _const_spec(shape):
    return pl.BlockSpec(shape, lambda *_: (0,) * len(shape), pipeline_mode=pl.Buffered(1))


def _swiglu_half(xn_bf16, wgu_ref, wd_ref):
    tiles = []
    for f0 in range(0, 2 * D_FF, 2 * LANE):
        gu = _dot(xn_bf16, wgu_ref[:, f0:f0 + 2 * LANE])
        tiles.append((_silu(gu[:, :LANE]) * gu[:, LANE:]).astype(BF16))
    return _dot(jnp.concatenate(tiles, axis=1), wd_ref[...])


def _convert_rows(dst_ref, src, step, rows):
    r0 = pl.multiple_of(step * rows, rows)
    dst_ref[pl.ds(r0, rows), :] = src.astype(BF16)


def _convert_ffn(i, wg32_ref, wu32_ref, wd32_ref, wgu_ref, wd_ref):
    rows = D_MODEL // W_STEPS
    r0 = pl.multiple_of(i * rows, rows)
    wg = wg32_ref[...].astype(BF16)
    wu = wu32_ref[...].astype(BF16)
    for j in range(D_FF // LANE):
        wgu_ref[pl.ds(r0, rows), 2 * j * LANE:(2 * j + 1) * LANE] = wg[:, j * LANE:(j + 1) * LANE]
        wgu_ref[pl.ds(r0, rows), (2 * j + 1) * LANE:(2 * j + 2) * LANE] = wu[:, j * LANE:(j + 1) * LANE]
    _convert_rows(wd_ref, wd32_ref[...], i, D_FF // W_STEPS)


def _ffn_kernel(x_ref, n_ref, wg32_ref, wu32_ref, wd32_ref, nf_ref, o_ref, wgu_ref, wd_ref, *, final_norm):
    i = pl.program_id(0)

    @pl.when(i < W_STEPS)
    def _():
        _convert_ffn(i, wg32_ref, wu32_ref, wd32_ref, wgu_ref, wd_ref)

    @pl.when(i >= W_STEPS)
    def _():
        x = x_ref[...]
        y = x + 0.5 * _swiglu_half(_rms(x, n_ref[...]).astype(BF16), wgu_ref, wd_ref)
        o_ref[...] = _rms(y, nf_ref[...]) if final_norm else y


def _proj_kernel(h1_ref, pos_ref, invf_ref, nm_ref, win_ref,
                 qn_ref, wuq_ref, kvn_ref, wuk_ref, wuv_ref, wa2_ref, ba_ref,
                 q_ref, k_ref, v_ref, gf_ref, gvr_ref, xq_ref):
    u2 = _rms(h1_ref[...], nm_ref[...]).astype(BF16)

    ang = pos_ref[...].astype(F32) * invf_ref[...]
    cos = jnp.cos(ang)
    sin = jnp.sin(ang)
    lane = lax.broadcasted_iota(jnp.int32, ang.shape, 1)
    sin_hi = jnp.where(lane >= 80, sin, 0.0)
    sin_lo = jnp.where(lane < 80, -sin, 0.0)

    def rope(t):
        return t * cos + pltpu.roll(t, 16, 1) * sin_hi + pltpu.roll(t, LANE - 16, 1) * sin_lo

    cqn = _rms(_dot_nt(u2, win_ref[C_CQ:C_CKV, :]), qn_ref[...]).astype(BF16)
    for p in range(MLA_HEADS // 2):
        qp = _dot(cqn, wuq_ref[:, 2 * p * HEAD_PAD:(2 * p + 2) * HEAD_PAD])
        for j in range(2):
            h = 2 * p + j
            q_ref[p, :, j * HEAD_PAD:(j + 1) * HEAD_PAD] = (
                rope(qp[:, j * HEAD_PAD:(j + 1) * HEAD_PAD]) * MLA_EXP2_SCALE).astype(BF16)

    ckvn = _rms(_dot_nt(u2, win_ref[C_CKV:C_KR, :]), kvn_ref[...]).astype(BF16)
    kr_raw = _dot_nt(u2, win_ref[C_KR:C_GQ, :])
    kr = rope(jnp.concatenate([jnp.zeros((kr_raw.shape[0], MLA_NOPE), F32), kr_raw,
                               jnp.zeros((kr_raw.shape[0], HEAD_PAD - MLA_NOPE - MLA_ROPE), F32)], axis=1))
    kr2 = jnp.concatenate([kr, kr], axis=1)
    for p in range(MLA_HEADS // 2):
        kn = _dot(ckvn, wuk_ref[:, 2 * p * HEAD_PAD:(2 * p + 2) * HEAD_PAD])
        k_ref[p] = (kn + kr2).astype(BF16)
    vv = _dot(ckvn, wuv_ref[...]).astype(BF16)
    for p in range(MLA_HEADS // 2):
        v_ref[p] = vv[:, 2 * p * MLA_V:(2 * p + 2) * MLA_V]

    ga = _dot_nt(u2, win_ref[C_GA:C_GR, :]).astype(BF16)
    gdk, gdv = GLA_HEADS * GLA_DK, GLA_HEADS * GLA_DV
    gf_ref[:, :gdk] = _dot_nt(u2, win_ref[C_GQ:C_GK, :])
    gf_ref[:, gdk:2 * gdk] = _dot_nt(u2, win_ref[C_GK:C_GV, :])
    gvr_ref[:, :gdv] = _dot_nt(u2, win_ref[C_GV:C_GA, :]).astype(BF16)
    t = _dot(ga, wa2_ref[...]) + ba_ref[...]
    log_sig = jnp.minimum(t, 0.0) - jnp.log(1.0 + jnp.exp(-jnp.abs(t)))
    gf_ref[:, 2 * gdk:] = log_sig * (1.0 / GLA_TAU)
    gvr_ref[:, gdv:] = _dot_nt(u2, win_ref[C_GR:C_XQ, :]).astype(BF16)
    xq_ref[...] = _dot_nt(u2, win_ref[C_XQ:C_GZ, :]).astype(BF16)


def _mla_kernel(q_ref, k_ref, v_ref, o_ref, vt_ref):
    def group(g, carry):
        sl = pl.ds(g * MLA_GROUP, MLA_GROUP)
        _mla_pairs(q_ref.at[sl], k_ref.at[sl], v_ref.at[sl], o_ref.at[sl], vt_ref)
        return carry

    lax.fori_loop(0, q_ref.shape[0] // MLA_GROUP, group, 0)


def _mla_pairs(q_ref, k_ref, v_ref, o_ref, vt_ref):
    npair = q_ref.shape[0]
    tq = TQ
    nq = q_ref.shape[1] // tq
    key = lax.broadcasted_iota(jnp.int32, (tq, tq), 0)
    qry = lax.broadcasted_iota(jnp.int32, (tq, tq), 1)
    causal = key <= qry
    ones_row = jnp.where(lax.broadcasted_iota(jnp.int32, (MLA_VT_ROWS - MLA_V, q_ref.shape[1]), 0) == 0,
                         1.0, 0.0).astype(BF16)

    def prepare(p):
        vt = v_ref[p].astype(F32).T.astype(BF16)
        for h in range(2):
            vt_ref[p, h * MLA_VT_ROWS:h * MLA_VT_ROWS + MLA_V, :] = vt[h * MLA_V:(h + 1) * MLA_V, :]
            vt_ref[p, h * MLA_VT_ROWS + MLA_V:(h + 1) * MLA_VT_ROWS, :] = ones_row

    items = [(p, qi, c) for p in range(npair) for qi in range(nq) for c in range(qi + 1)]
    pending = [{}, {}]
    state = [{}, {}]
    done = {}

    def score(h, i):
        p, qi, c = items[i]
        q = q_ref[p, qi * tq:(qi + 1) * tq, h * HEAD_PAD:(h + 1) * HEAD_PAD]
        s = _dot_nt(k_ref[p, c * tq:(c + 1) * tq, h * HEAD_PAD:(h + 1) * HEAD_PAD], q)
        pending[h][i] = jnp.where(causal, s, -1e30) if c == qi else s

    def consume(h, i):
        p, qi, c = items[i]
        st = state[h]
        s = pending[h].pop(i)
        cm = jnp.max(s, axis=0, keepdims=True)
        m_new = cm if c == 0 else jnp.maximum(st["m"], cm)
        pr = jnp.exp2(s - m_new).astype(BF16)
        pv = _dot(vt_ref[p, h * MLA_VT_ROWS:(h + 1) * MLA_VT_ROWS, c * tq:(c + 1) * tq], pr)
        st["acc"] = pv if c == 0 else st["acc"] * jnp.exp2(st["m"] - m_new) + pv
        st["m"] = m_new
        if c == qi:
            done[(p, qi, h)] = st["acc"][:MLA_V, :] / st["acc"][MLA_V:MLA_V + 1, :]
            if (p, qi, 1 - h) in done:
                o_ref[p, qi * tq:(qi + 1) * tq, :] = jnp.concatenate(
                    [done.pop((p, qi, 0)), done.pop((p, qi, 1))], axis=0).T.astype(BF16)

    for t in range(len(items) + MLA_LOOKAHEAD):
        if t < len(items) and items[t][1:] == (0, 0):
            prepare(items[t][0])
        for h in range(2):
            if t < len(items):
                score(h, t)
            if t >= MLA_LOOKAHEAD:
                consume(h, t - MLA_LOOKAHEAD)


def _xattn_kernel(q_ref, mem_ref, n_ref, w_ref, o_ref, kv_ref):
    @pl.when(pl.program_id(1) == 0)
    def _():
        mn = _rms(mem_ref[...], n_ref[...]).astype(BF16)
        kv_ref[...] = _dot(mn, w_ref[...]).astype(BF16)

    scale = 1.0 / math.sqrt(X_DH)
    hw = X_HEADS * X_DH
    for h in range(X_HEADS):
        q = q_ref[:, h * X_DH:(h + 1) * X_DH]
        k = kv_ref[:, h * X_DH:(h + 1) * X_DH]
        v = kv_ref[:, hw + h * X_DH:hw + (h + 1) * X_DH]
        s = _dot_nt(q, k) * scale
        m = jnp.max(s, axis=-1, keepdims=True)
        p = jnp.exp(s - m)
        l = jnp.sum(p, axis=-1, keepdims=True)
        o = _dot(p.astype(BF16), v) / l
        o_ref[:, h * X_DH:(h + 1) * X_DH] = o.astype(BF16)


def _gla_kernel(gf_ref, gvr_ref, on_ref, o_ref):
    sb, c = GLA_SB, GLA_CHUNK
    nsb = gf_ref.shape[0] // sb
    nch = sb // c
    npair = GLA_HEADS // 2
    pw, vw = 2 * GLA_DK, 2 * GLA_DV
    gdk, gdv = GLA_HEADS * GLA_DK, GLA_HEADS * GLA_DV
    row = lax.broadcasted_iota(jnp.int32, (sb, sb), 0)
    col = lax.broadcasted_iota(jnp.int32, (sb, sb), 1)
    tri = jnp.logical_and(row // c == col // c, col <= row)
    tri_bf = jnp.where(tri, 1.0, 0.0).astype(BF16)
    lane = lax.broadcasted_iota(jnp.int32, (1, pw), 1)
    head_lanes = (lane < GLA_DK, lane >= GLA_DK)
    srow = lax.broadcasted_iota(jnp.int32, (pw, vw), 0)
    scol = lax.broadcasted_iota(jnp.int32, (pw, vw), 1)
    same_head = (srow < GLA_DK) == (scol < GLA_DV)

    def local(i):
        r0 = i * sb
        la = gf_ref[r0:r0 + sb, 2 * gdk:]
        hi = la.astype(BF16)
        lo = (la - hi.astype(F32)).astype(BF16)
        bcum_all = _dot(tri_bf, hi) + _dot(tri_bf, lo)
        out = []
        for p in range(npair):
            bcum = bcum_all[:, p * pw:(p + 1) * pw]
            e_last = [jnp.exp(bcum[j * c + c - 1:j * c + c, :]) for j in range(nch)]
            e_last_rows = jnp.concatenate([jnp.broadcast_to(e, (c, pw)) for e in e_last], axis=0)
            q_t = gf_ref[r0:r0 + sb, p * pw:(p + 1) * pw] * (GLA_DK ** -0.5) * jnp.exp(bcum)
            kk = gf_ref[r0:r0 + sb, gdk + p * pw:gdk + (p + 1) * pw]
            k_undecayed = kk * jnp.exp(-bcum)
            k_t = k_undecayed.astype(BF16)
            k_d = (k_undecayed * e_last_rows).astype(BF16)
            v = gvr_ref[r0:r0 + sb, p * vw:(p + 1) * vw]
            o_intra = []
            for h in range(2):
                qm = jnp.where(head_lanes[h], q_t, 0.0).astype(BF16)
                att = jnp.where(tri, _dot_nt(qm, k_t), 0.0).astype(BF16)
                o_intra.append(_dot(att, v[:, h * GLA_DV:(h + 1) * GLA_DV]))
            d_st, decay = [], []
            for j in range(nch):
                d = _dot_tn(k_d[j * c:(j + 1) * c, :], v[j * c:(j + 1) * c, :])
                d_st.append(jnp.where(same_head, d, 0.0))
                dcol = jnp.broadcast_to(e_last[j], (pw, pw)).T
                decay.append(jnp.concatenate([dcol, dcol], axis=1))
            out.append((q_t.astype(BF16), o_intra, d_st, decay))
        return out

    def recurrent(i, loc, states):
        r0 = i * sb
        for p in range(npair):
            q_bf, o_intra, d_st, decay = loc[p]
            st = states[p]
            o_inter = []
            for j in range(nch):
                o_inter.append(_dot(q_bf[j * c:(j + 1) * c, :], st.astype(BF16)))
                st = decay[j] * st + d_st[j]
            states[p] = st
            o_inter = jnp.concatenate(o_inter, axis=0)
            for h in range(2):
                lo_, hi_ = p * vw + h * GLA_DV, p * vw + (h + 1) * GLA_DV
                o = _rms(o_intra[h] + o_inter[:, h * GLA_DV:(h + 1) * GLA_DV], on_ref[...])
                r = gvr_ref[r0:r0 + sb, gdv + lo_:gdv + hi_].astype(F32)
                o_ref[r0:r0 + sb, lo_:hi_] = (o * _silu(r)).astype(BF16)

    states = [jnp.zeros((pw, vw), F32) for _ in range(npair)]
    loc = local(0)
    for i in range(nsb):
        nxt = local(i + 1) if i + 1 < nsb else None
        recurrent(i, loc, states)
        loc = nxt


def _merge_kernel(h1_ref, om_ref, og_ref, ox_ref, nm_ref, wgz_ref, gb_ref, wom_ref, wog_ref, wox_ref,
                  wout_ref, o_ref):
    h1 = h1_ref[...]
    u = _rms(h1, nm_ref[...]).astype(BF16)
    d = D_MODEL
    branches = ((jnp.concatenate([om_ref[p] for p in range(om_ref.shape[0])], axis=1), wom_ref),
                (og_ref[...], wog_ref), (ox_ref[...], wox_ref))
    tiles = []
    for c0 in range(0, d, MERGE_TILE):
        acc = None
        for n, (o_in, w_br) in enumerate(branches):
            gate = _sigmoid(_dot_nt(u, wgz_ref[n * d + c0:n * d + c0 + MERGE_TILE, :])
                            + gb_ref[:, n * d + c0:n * d + c0 + MERGE_TILE])
            term = gate * _dot(o_in, w_br[:, c0:c0 + MERGE_TILE])
            acc = term if acc is None else acc + term
        tiles.append(acc.astype(BF16))
    o_ref[...] = h1 + _dot(jnp.concatenate(tiles, axis=1), wout_ref[...])


def _cparams(sem, vmem=None, flags=None):
    return pltpu.CompilerParams(dimension_semantics=sem, vmem_limit_bytes=vmem, flags=flags)


def kernel(x, mem, positions, ffn1_norm, ffn1_wg, ffn1_wu, ffn1_wd, mix_norm, mem_norm, w_in, gate_bias,
           mla_q_norm, mla_w_uq, mla_kv_norm, mla_w_ukv, mla_w_o, gla_w_a2, gla_b_a, gla_o_norm, gla_w_o,
           x_w_kv, x_w_o, w_out, ffn2_norm, ffn2_wg, ffn2_wu, ffn2_wd, final_norm):
    B, S, D = x.shape
    T = B * S
    M = mem.shape[1]
    F = D_FF
    H = MLA_HEADS
    bf = lambda a: a.astype(BF16)
    row = lambda a: a.reshape(1, -1)

    w_in_t = bf(jnp.swapaxes(w_in[0], 0, 1))
    w_uq = mla_w_uq[0].reshape(MLA_Q_RANK, H, MLA_NOPE + MLA_ROPE)
    w_uq = bf(jnp.pad(w_uq, ((0, 0), (0, 0), (0, HEAD_PAD - MLA_NOPE - MLA_ROPE))).reshape(MLA_Q_RANK, H * HEAD_PAD))
    w_ukv = mla_w_ukv[0].reshape(MLA_KV_RANK, H, MLA_NOPE + MLA_V)
    w_uk = bf(jnp.pad(w_ukv[:, :, :MLA_NOPE], ((0, 0), (0, 0), (0, HEAD_PAD - MLA_NOPE))).reshape(MLA_KV_RANK, H * HEAD_PAD))
    w_uv = bf(w_ukv[:, :, MLA_NOPE:].reshape(MLA_KV_RANK, H * MLA_V))
    w_a2 = bf(gla_w_a2[0])
    half = MLA_ROPE // 2
    inv_freq = ROPE_THETA ** (-jnp.arange(half, dtype=F32) / half)
    invf = jnp.concatenate([jnp.zeros((MLA_NOPE,), F32), inv_freq, inv_freq,
                            jnp.zeros((HEAD_PAD - MLA_NOPE - MLA_ROPE,), F32)]).reshape(1, HEAD_PAD)

    x2 = x.reshape(T, D)
    pos2 = jnp.broadcast_to(positions.reshape(T, 1), (T, HEAD_PAD))
    npair = H // 2
    gdk, gdv, xw = GLA_HEADS * GLA_DK, GLA_HEADS * GLA_DV, X_HEADS * X_DH

    ffn_tok = pl.BlockSpec((TM_FFN, D), lambda i: (jnp.maximum(i - W_STEPS, 0), 0))
    wrows = lambda shape: pl.BlockSpec(
        (shape[0] // W_STEPS, shape[1]), lambda i: (jnp.minimum(i, W_STEPS - 1), 0))

    def ffn(name, xin, norm, wg, wu, wd, final):
        return pl.pallas_call(
            functools.partial(_ffn_kernel, final_norm=final),
            out_shape=jax.ShapeDtypeStruct((T, D), F32),
            grid=(W_STEPS + T // TM_FFN,),
            in_specs=[ffn_tok, _const_spec((1, D)), wrows((D, F)), wrows((D, F)), wrows((F, D)),
                      _const_spec((1, D))],
            out_specs=ffn_tok,
            scratch_shapes=[pltpu.VMEM((D, 2 * F), BF16), pltpu.VMEM((F, D), BF16)],
            compiler_params=_cparams(("arbitrary",), VMEM_LIMIT),
            name=name,
        )(xin, row(norm), wg, wu, wd, row(final_norm))

    h1 = ffn("ffn1", x2, ffn1_norm[0], ffn1_wg[0], ffn1_wu[0], ffn1_wd[0], False)

    tok = lambda w: pl.BlockSpec((TM, w), lambda i: (i, 0))
    tok_pairs = lambda w: pl.BlockSpec((npair, TM, w), lambda i: (0, i, 0))
    proj_out_shapes = (
        jax.ShapeDtypeStruct((npair, T, 2 * HEAD_PAD), BF16),
        jax.ShapeDtypeStruct((npair, T, 2 * HEAD_PAD), BF16),
        jax.ShapeDtypeStruct((npair, T, 2 * MLA_V), BF16),
        jax.ShapeDtypeStruct((T, 3 * gdk), F32),
        jax.ShapeDtypeStruct((T, 2 * gdv), BF16),
        jax.ShapeDtypeStruct((T, xw), BF16),
    )
    q, k, v, gf, gvr, xq = pl.pallas_call(
        _proj_kernel,
        out_shape=proj_out_shapes,
        grid=(T // TM,),
        in_specs=[tok(D), tok(HEAD_PAD), _const_spec((1, HEAD_PAD)), _const_spec((1, D)), _const_spec((C_GZ, D)),
                  _const_spec((1, MLA_Q_RANK)), _const_spec((MLA_Q_RANK, H * HEAD_PAD)),
                  _const_spec((1, MLA_KV_RANK)), _const_spec((MLA_KV_RANK, H * HEAD_PAD)),
                  _const_spec((MLA_KV_RANK, H * MLA_V)), _const_spec((GLA_GATE_RANK, gdk)), _const_spec((1, gdk))],
        out_specs=tuple(tok(s.shape[1]) if len(s.shape) == 2 else tok_pairs(s.shape[2])
                        for s in proj_out_shapes),
        compiler_params=_cparams(("parallel",), VMEM_LIMIT),
        name="proj",
    )(h1, pos2, invf, row(mix_norm[0]), w_in_t, row(mla_q_norm[0]), w_uq, row(mla_kv_norm[0]), w_uk, w_uv,
      w_a2, row(gla_b_a[0]))

    seq_pairs = lambda w: pl.BlockSpec((npair, S, w), lambda b: (0, b, 0))
    o_mla = pl.pallas_call(
        _mla_kernel,
        out_shape=jax.ShapeDtypeStruct((npair, T, 2 * MLA_V), BF16),
        grid=(B,),
        in_specs=[seq_pairs(2 * HEAD_PAD), seq_pairs(2 * HEAD_PAD), seq_pairs(2 * MLA_V)],
        out_specs=seq_pairs(2 * MLA_V),
        scratch_shapes=[pltpu.VMEM((MLA_GROUP, 2 * MLA_VT_ROWS, S), BF16)],
        compiler_params=_cparams(("parallel",), VMEM_LIMIT),
        name="mla",
    )(q, k, v)

    nqx = S // TQX
    o_x = pl.pallas_call(
        _xattn_kernel,
        out_shape=jax.ShapeDtypeStruct((T, xw), BF16),
        grid=(B, nqx),
        in_specs=[pl.BlockSpec((TQX, xw), lambda b, i: (b * nqx + i, 0)),
                  pl.BlockSpec((M, D), lambda b, i: (b, 0)),
                  _const_spec((1, D)), _const_spec((D, 2 * xw))],
        out_specs=pl.BlockSpec((TQX, xw), lambda b, i: (b * nqx + i, 0)),
        scratch_shapes=[pltpu.VMEM((M, 2 * xw), BF16)],
        compiler_params=_cparams(("arbitrary", "arbitrary")),
        name="xattn",
    )(xq, mem.reshape(B * M, D), row(mem_norm[0]), bf(x_w_kv[0]))

    o_gla = pl.pallas_call(
        _gla_kernel,
        out_shape=jax.ShapeDtypeStruct((T, gdv), BF16),
        grid=(B,),
        in_specs=[pl.BlockSpec((S, 3 * gdk), lambda b: (b, 0)),
                  pl.BlockSpec((S, 2 * gdv), lambda b: (b, 0)),
                  _const_spec((1, GLA_DV))],
        out_specs=pl.BlockSpec((S, gdv), lambda b: (b, 0)),
        compiler_params=_cparams(("parallel",), VMEM_LIMIT),
        name="gla",
    )(gf, gvr, row(gla_o_norm[0]))

    h2 = pl.pallas_call(
        _merge_kernel,
        out_shape=jax.ShapeDtypeStruct((T, D), F32),
        grid=(T // TM,),
        in_specs=[tok(D), tok_pairs(2 * MLA_V), tok(gdv), tok(xw), _const_spec((1, D)),
                  pl.BlockSpec((pl.Element(N_BRANCH * D), pl.Element(D)), lambda i: (C_GZ, 0),
                               pipeline_mode=pl.Buffered(1)),
                  _const_spec((1, N_BRANCH * D)),
                  _const_spec((H * MLA_V, D)), _const_spec((gdv, D)), _const_spec((xw, D)),
                  _const_spec((D, D))],
        out_specs=tok(D),
        compiler_params=_cparams(("parallel",), VMEM_LIMIT),
        name="merge",
    )(h1, o_mla, o_gla, o_x, row(mix_norm[0]), w_in_t, gate_bias[0].reshape(1, N_BRANCH * D),
      bf(mla_w_o[0]), bf(gla_w_o[0]), bf(x_w_o[0]), bf(w_out[0]))
    out = ffn("ffn2", h2, ffn2_norm[0], ffn2_wg[0], ffn2_wu[0], ffn2_wd[0], True)
    return out.reshape(B, S, D)
```

```python
import math

import jax
import jax.numpy as jnp
from jax import lax
from jax.experimental import pallas as pl
from jax.experimental.pallas import tpu as pltpu

F32 = jnp.float32
BF16 = jnp.bfloat16

D_MODEL = 1024
EPS = 1e-6
MLA_HEADS = 8
MLA_NOPE = 64
MLA_ROPE = 32
MLA_V = 64
MLA_Q_RANK = 384
MLA_KV_RANK = 256
ROPE_THETA = 10000.0
GLA_HEADS = 4
GLA_DK = 64
GLA_DV = 128
GLA_GATE_RANK = 16
GLA_TAU = 16.0
GLA_CHUNK = 64
X_HEADS = 4
X_DH = 128
D_FF = 2816
N_BRANCH = 3

LANE = 128
HEAD_PAD = 128
VMEM_LIMIT = 60 * 1024 * 1024
MLA_EXP2_SCALE = math.log2(math.e) / math.sqrt(MLA_NOPE + MLA_ROPE)
MLA_VT_ROWS = MLA_V + 16
MLA_GROUP = 2
MLA_LOOKAHEAD = 3

C_CQ, C_CKV, C_KR, C_GQ, C_GK, C_GV, C_GA, C_GR, C_XQ, C_GZ, C_END = (
    0, 384, 640, 672, 928, 1184, 1696, 1712, 2224, 2736, 5808)
W_STEPS = 8
WIN_PRE_ROWS = -(-C_GZ // (16 * W_STEPS)) * 16
WIN_POST_ROWS = (C_END - C_GZ) // W_STEPS

TM = 512
MERGE_TILE = 256
TQ = 256
TQX = 2048
GLA_SB = 256


def _dot(a, b):
    return jnp.dot(a, b, preferred_element_type=F32)


def _dot_nt(a, b):
    return lax.dot_general(a, b, (((1,), (1,)), ((), ())), preferred_element_type=F32)


def _dot_tn(a, b):
    return lax.dot_general(a, b, (((0,), (0,)), ((), ())), preferred_element_type=F32)


def _rms(x, g):
    return x * lax.rsqrt(jnp.mean(x * x, axis=-1, keepdims=True) + EPS) * g


def _sigmoid(x):
    return 1.0 / (1.0 + jnp.exp(-x))


def _silu(x):
    return x * _sigmoid(x)


def _const_spec(shape):
    return pl.BlockSpec(shape, lambda *_: (0,) * len(shape), pipeline_mode=pl.Buffered(1))


def _swiglu_half(xn_bf16, wgu_ref, wd_ref):
    tiles = []
    for f0 in range(0, 2 * D_FF, 2 * LANE):
        gu = _dot(xn_bf16, wgu_ref[:, f0:f0 + 2 * LANE])
        tiles.append((_silu(gu[:, :LANE]) * gu[:, LANE:]).astype(BF16))
    return _dot(jnp.concatenate(tiles, axis=1), wd_ref[...])


def _convert_rows(dst_ref, src, step, rows):
    r0 = pl.multiple_of(step * rows, rows)
    dst_ref[pl.ds(r0, rows), :] = src.astype(BF16)


def _convert_ffn(i, wg32_ref, wu32_ref, wd32_ref, wgu_ref, wd_ref):
    rows = D_MODEL // W_STEPS
    r0 = pl.multiple_of(i * rows, rows)
    wg = wg32_ref[...].astype(BF16)
    wu = wu32_ref[...].astype(BF16)
    for j in range(D_FF // LANE):
        wgu_ref[pl.ds(r0, rows), 2 * j * LANE:(2 * j + 1) * LANE] = wg[:, j * LANE:(j + 1) * LANE]
        wgu_ref[pl.ds(r0, rows), (2 * j + 1) * LANE:(2 * j + 2) * LANE] = wu[:, j * LANE:(j + 1) * LANE]
    _convert_rows(wd_ref, wd32_ref[...], i, D_FF // W_STEPS)


def _pre_kernel(x_ref, pos_ref, invf_ref, n1_ref, nm_ref, wg32_ref, wu32_ref, wd32_ref, win32_ref,
                qn_ref, wuq_ref, kvn_ref, wuk_ref, wuv_ref, wa2_ref, ba_ref,
                h1_ref, q_ref, k_ref, v_ref, gf_ref, gvr_ref, xq_ref,
                wgu_ref, wd_ref, win_ref):
    i = pl.program_id(0)

    @pl.when(i < W_STEPS)
    def _():
        _convert_ffn(i, wg32_ref, wu32_ref, wd32_ref, wgu_ref, wd_ref)
        _convert_rows(win_ref, win32_ref[...], i, WIN_PRE_ROWS)

    @pl.when(i >= W_STEPS)
    def _():
        _pre_tokens(x_ref, pos_ref, invf_ref, n1_ref, nm_ref, wgu_ref, wd_ref, win_ref,
                    qn_ref, wuq_ref, kvn_ref, wuk_ref, wuv_ref, wa2_ref, ba_ref,
                    h1_ref, q_ref, k_ref, v_ref, gf_ref, gvr_ref, xq_ref)


def _pre_tokens(x_ref, pos_ref, invf_ref, n1_ref, nm_ref, wgu_ref, wd_ref, win_ref,
                qn_ref, wuq_ref, kvn_ref, wuk_ref, wuv_ref, wa2_ref, ba_ref,
                h1_ref, q_ref, k_ref, v_ref, gf_ref, gvr_ref, xq_ref):
    x = x_ref[...]
    u1 = _rms(x, n1_ref[...]).astype(BF16)
    h1 = x + 0.5 * _swiglu_half(u1, wgu_ref, wd_ref)
    h1_ref[...] = h1
    u2 = _rms(h1, nm_ref[...]).astype(BF16)

    ang = pos_ref[...].astype(F32) * invf_ref[...]
    cos = jnp.cos(ang)
    sin = jnp.sin(ang)
    lane = lax.broadcasted_iota(jnp.int32, ang.shape, 1)
    sin_hi = jnp.where(lane >= 80, sin, 0.0)
    sin_lo = jnp.where(lane < 80, -sin, 0.0)

    def rope(t):
        return t * cos + pltpu.roll(t, 16, 1) * sin_hi + pltpu.roll(t, LANE - 16, 1) * sin_lo

    cqn = _rms(_dot_nt(u2, win_ref[C_CQ:C_CKV, :]), qn_ref[...]).astype(BF16)
    for p in range(MLA_HEADS // 2):
        qp = _dot(cqn, wuq_ref[:, 2 * p * HEAD_PAD:(2 * p + 2) * HEAD_PAD])
        for j in range(2):
            h = 2 * p + j
            q_ref[p, :, j * HEAD_PAD:(j + 1) * HEAD_PAD] = (
                rope(qp[:, j * HEAD_PAD:(j + 1) * HEAD_PAD]) * MLA_EXP2_SCALE).astype(BF16)

    ckvn = _rms(_dot_nt(u2, win_ref[C_CKV:C_KR, :]), kvn_ref[...]).astype(BF16)
    kr_raw = _dot_nt(u2, win_ref[C_KR:C_GQ, :])
    kr = rope(jnp.concatenate([jnp.zeros((kr_raw.shape[0], MLA_NOPE), F32), kr_raw,
                               jnp.zeros((kr_raw.shape[0], HEAD_PAD - MLA_NOPE - MLA_ROPE), F32)], axis=1))
    kr2 = jnp.concatenate([kr, kr], axis=1)
    for p in range(MLA_HEADS // 2):
        kn = _dot(ckvn, wuk_ref[:, 2 * p * HEAD_PAD:(2 * p + 2) * HEAD_PAD])
        k_ref[p] = (kn + kr2).astype(BF16)
    vv = _dot(ckvn, wuv_ref[...]).astype(BF16)
    for p in range(MLA_HEADS // 2):
        v_ref[p] = vv[:, 2 * p * MLA_V:(2 * p + 2) * MLA_V]

    ga = _dot_nt(u2, win_ref[C_GA:C_GR, :]).astype(BF16)
    gdk, gdv = GLA_HEADS * GLA_DK, GLA_HEADS * GLA_DV
    gf_ref[:, :gdk] = _dot_nt(u2, win_ref[C_GQ:C_GK, :])
    gf_ref[:, gdk:2 * gdk] = _dot_nt(u2, win_ref[C_GK:C_GV, :])
    gvr_ref[:, :gdv] = _dot_nt(u2, win_ref[C_GV:C_GA, :]).astype(BF16)
    t = _dot(ga, wa2_ref[...]) + ba_ref[...]
    log_sig = jnp.minimum(t, 0.0) - jnp.log(1.0 + jnp.exp(-jnp.abs(t)))
    gf_ref[:, 2 * gdk:] = log_sig * (1.0 / GLA_TAU)
    gvr_ref[:, gdv:] = _dot_nt(u2, win_ref[C_GR:C_XQ, :]).astype(BF16)
    xq_ref[...] = _dot_nt(u2, win_ref[C_XQ:C_GZ, :]).astype(BF16)


def _mla_kernel(q_ref, k_ref, v_ref, o_ref, vt_ref):
    def group(g, carry):
        sl = pl.ds(g * MLA_GROUP, MLA_GROUP)
        _mla_pairs(q_ref.at[sl], k_ref.at[sl], v_ref.at[sl], o_ref.at[sl], vt_ref)
        return carry

    lax.fori_loop(0, q_ref.shape[0] // MLA_GROUP, group, 0)


def _mla_pairs(q_ref, k_ref, v_ref, o_ref, vt_ref):
    npair = q_ref.shape[0]
    tq = TQ
    nq = q_ref.shape[1] // tq
    key = lax.broadcasted_iota(jnp.int32, (tq, tq), 0)
    qry = lax.broadcasted_iota(jnp.int32, (tq, tq), 1)
    causal = key <= qry
    ones_row = jnp.where(lax.broadcasted_iota(jnp.int32, (MLA_VT_ROWS - MLA_V, q_ref.shape[1]), 0) == 0,
                         1.0, 0.0).astype(BF16)

    def prepare(p):
        vt = v_ref[p].astype(F32).T.astype(BF16)
        for h in range(2):
            vt_ref[p, h * MLA_VT_ROWS:h * MLA_VT_ROWS + MLA_V, :] = vt[h * MLA_V:(h + 1) * MLA_V, :]
            vt_ref[p, h * MLA_VT_ROWS + MLA_V:(h + 1) * MLA_VT_ROWS, :] = ones_row

    items = [(p, qi, c) for p in range(npair) for qi in range(nq) for c in range(qi + 1)]
    pending = [{}, {}]
    state = [{}, {}]
    done = {}

    def score(h, i):
        p, qi, c = items[i]
        q = q_ref[p, qi * tq:(qi + 1) * tq, h * HEAD_PAD:(h + 1) * HEAD_PAD]
        s = _dot_nt(k_ref[p, c * tq:(c + 1) * tq, h * HEAD_PAD:(h + 1) * HEAD_PAD], q)
        pending[h][i] = jnp.where(causal, s, -1e30) if c == qi else s

    def consume(h, i):
        p, qi, c = items[i]
        st = state[h]
        s = pending[h].pop(i)
        cm = jnp.max(s, axis=0, keepdims=True)
        m_new = cm if c == 0 else jnp.maximum(st["m"], cm)
        pr = jnp.exp2(s - m_new).astype(BF16)
        pv = _dot(vt_ref[p, h * MLA_VT_ROWS:(h + 1) * MLA_VT_ROWS, c * tq:(c + 1) * tq], pr)
        st["acc"] = pv if c == 0 else st["acc"] * jnp.exp2(st["m"] - m_new) + pv
        st["m"] = m_new
        if c == qi:
            done[(p, qi, h)] = st["acc"][:MLA_V, :] / st["acc"][MLA_V:MLA_V + 1, :]
            if (p, qi, 1 - h) in done:
                o_ref[p, qi * tq:(qi + 1) * tq, :] = jnp.concatenate(
                    [done.pop((p, qi, 0)), done.pop((p, qi, 1))], axis=0).T.astype(BF16)

    for t in range(len(items) + MLA_LOOKAHEAD):
        if t < len(items) and items[t][1:] == (0, 0):
            prepare(items[t][0])
        for h in range(2):
            if t < len(items):
                score(h, t)
            if t >= MLA_LOOKAHEAD:
                consume(h, t - MLA_LOOKAHEAD)


def _xattn_kernel(q_ref, mem_ref, n_ref, w_ref, o_ref, kv_ref):
    @pl.when(pl.program_id(1) == 0)
    def _():
        mn = _rms(mem_ref[...], n_ref[...]).astype(BF16)
        kv_ref[...] = _dot(mn, w_ref[...]).astype(BF16)

    scale = 1.0 / math.sqrt(X_DH)
    hw = X_HEADS * X_DH
    for h in range(X_HEADS):
        q = q_ref[:, h * X_DH:(h + 1) * X_DH]
        k = kv_ref[:, h * X_DH:(h + 1) * X_DH]
        v = kv_ref[:, hw + h * X_DH:hw + (h + 1) * X_DH]
        s = _dot_nt(q, k) * scale
        m = jnp.max(s, axis=-1, keepdims=True)
        p = jnp.exp(s - m)
        l = jnp.sum(p, axis=-1, keepdims=True)
        o = _dot(p.astype(BF16), v) / l
        o_ref[:, h * X_DH:(h + 1) * X_DH] = o.astype(BF16)


def _gla_kernel(gf_ref, gvr_ref, on_ref, o_ref):
    sb, c = GLA_SB, GLA_CHUNK
    nsb = gf_ref.shape[0] // sb
    nch = sb // c
    npair = GLA_HEADS // 2
    pw, vw = 2 * GLA_DK, 2 * GLA_DV
    gdk, gdv = GLA_HEADS * GLA_DK, GLA_HEADS * GLA_DV
    row = lax.broadcasted_iota(jnp.int32, (sb, sb), 0)
    col = lax.broadcasted_iota(jnp.int32, (sb, sb), 1)
    tri = jnp.logical_and(row // c == col // c, col <= row)
    tri_bf = jnp.where(tri, 1.0, 0.0).astype(BF16)
    lane = lax.broadcasted_iota(jnp.int32, (1, pw), 1)
    head_lanes = (lane < GLA_DK, lane >= GLA_DK)
    srow = lax.broadcasted_iota(jnp.int32, (pw, vw), 0)
    scol = lax.broadcasted_iota(jnp.int32, (pw, vw), 1)
    same_head = (srow < GLA_DK) == (scol < GLA_DV)

    def local(i):
        r0 = i * sb
        la = gf_ref[r0:r0 + sb, 2 * gdk:]
        hi = la.astype(BF16)
        lo = (la - hi.astype(F32)).astype(BF16)
        bcum_all = _dot(tri_bf, hi) + _dot(tri_bf, lo)
        out = []
        for p in range(npair):
            bcum = bcum_all[:, p * pw:(p + 1) * pw]
            e_last = [jnp.exp(bcum[j * c + c - 1:j * c + c, :]) for j in range(nch)]
            e_last_rows = jnp.concatenate([jnp.broadcast_to(e, (c, pw)) for e in e_last], axis=0)
            q_t = gf_ref[r0:r0 + sb, p * pw:(p + 1) * pw] * (GLA_DK ** -0.5) * jnp.exp(bcum)
            kk = gf_ref[r0:r0 + sb, gdk + p * pw:gdk + (p + 1) * pw]
            k_undecayed = kk * jnp.exp(-bcum)
            k_t = k_undecayed.astype(BF16)
            k_d = (k_undecayed * e_last_rows).astype(BF16)
            v = gvr_ref[r0:r0 + sb, p * vw:(p + 1) * vw]
            o_intra = []
            for h in range(2):
                qm = jnp.where(head_lanes[h], q_t, 0.0).astype(BF16)
                att = jnp.where(tri, _dot_nt(qm, k_t), 0.0).astype(BF16)
                o_intra.append(_dot(att, v[:, h * GLA_DV:(h + 1) * GLA_DV]))
            d_st, decay = [], []
            for j in range(nch):
                d = _dot_tn(k_d[j * c:(j + 1) * c, :], v[j * c:(j + 1) * c, :])
                d_st.append(jnp.where(same_head, d, 0.0))
                dcol = jnp.broadcast_to(e_last[j], (pw, pw)).T
                decay.append(jnp.concatenate([dcol, dcol], axis=1))
            out.append((q_t.astype(BF16), o_intra, d_st, decay))
        return out

    def recurrent(i, loc, states):
        r0 = i * sb
        for p in range(npair):
            q_bf, o_intra, d_st, decay = loc[p]
            st = states[p]
            o_inter = []
            for j in range(nch):
                o_inter.append(_dot(q_bf[j * c:(j + 1) * c, :], st.astype(BF16)))
                st = decay[j] * st + d_st[j]
            states[p] = st
            o_inter = jnp.concatenate(o_inter, axis=0)
            for h in range(2):
                lo_, hi_ = p * vw + h * GLA_DV, p * vw + (h + 1) * GLA_DV
                o = _rms(o_intra[h] + o_inter[:, h * GLA_DV:(h + 1) * GLA_DV], on_ref[...])
                r = gvr_ref[r0:r0 + sb, gdv + lo_:gdv + hi_].astype(F32)
                o_ref[r0:r0 + sb, lo_:hi_] = (o * _silu(r)).astype(BF16)

    states = [jnp.zeros((pw, vw), F32) for _ in range(npair)]
    loc = local(0)
    for i in range(nsb):
        nxt = local(i + 1) if i + 1 < nsb else None
        recurrent(i, loc, states)
        loc = nxt


def _post_kernel(h1_ref, om_ref, og_ref, ox_ref, nm_ref, wgz32_ref, gb_ref, wom32_ref, wog32_ref, wox32_ref,
                 wout32_ref, n2_ref, wg32_ref, wu32_ref, wd32_ref, nf_ref, o_ref,
                 wgz_ref, wom_ref, wog_ref, wox_ref, wout_ref, wgu_ref, wd_ref):
    i = pl.program_id(0)

    @pl.when(i < W_STEPS)
    def _():
        _convert_ffn(i, wg32_ref, wu32_ref, wd32_ref, wgu_ref, wd_ref)
        _convert_rows(wgz_ref, wgz32_ref[...], i, WIN_POST_ROWS)
        for dst, src in ((wom_ref, wom32_ref), (wog_ref, wog32_ref), (wox_ref, wox32_ref)):
            _convert_rows(dst, src[...], i, dst.shape[0] // W_STEPS)
        _convert_rows(wout_ref, wout32_ref[...], i, D_MODEL // W_STEPS)

    @pl.when(i >= W_STEPS)
    def _():
        _post_tokens(h1_ref, om_ref, og_ref, ox_ref, nm_ref, wgz_ref, gb_ref, wom_ref, wog_ref, wox_ref,
                     wout_ref, n2_ref, wgu_ref, wd_ref, nf_ref, o_ref)


def _post_tokens(h1_ref, om_ref, og_ref, ox_ref, nm_ref, wgz_ref, gb_ref, wom_ref, wog_ref, wox_ref,
                 wout_ref, n2_ref, wgu_ref, wd_ref, nf_ref, o_ref):
    h1 = h1_ref[...]
    u = _rms(h1, nm_ref[...]).astype(BF16)
    d = D_MODEL
    branches = ((jnp.concatenate([om_ref[p] for p in range(om_ref.shape[0])], axis=1), wom_ref),
                (og_ref[...], wog_ref), (ox_ref[...], wox_ref))
    tiles = []
    for c0 in range(0, d, MERGE_TILE):
        acc = None
        for n, (o_in, w_br) in enumerate(branches):
            gate = _sigmoid(_dot_nt(u, wgz_ref[n * d + c0:n * d + c0 + MERGE_TILE, :])
                            + gb_ref[:, n * d + c0:n * d + c0 + MERGE_TILE])
            term = gate * _dot(o_in, w_br[:, c0:c0 + MERGE_TILE])
            acc = term if acc is None else acc + term
        tiles.append(acc.astype(BF16))
    h2 = h1 + _dot(jnp.concatenate(tiles, axis=1), wout_ref[...])
    u2 = _rms(h2, n2_ref[...]).astype(BF16)
    h3 = h2 + 0.5 * _swiglu_half(u2, wgu_ref, wd_ref)
    o_ref[...] = _rms(h3, nf_ref[...])


def _cparams(sem, vmem=None, flags=None):
    return pltpu.CompilerParams(dimension_semantics=sem, vmem_limit_bytes=vmem, flags=flags)


def kernel(x, mem, positions, ffn1_norm, ffn1_wg, ffn1_wu, ffn1_wd, mix_norm, mem_norm, w_in, gate_bias,
           mla_q_norm, mla_w_uq, mla_kv_norm, mla_w_ukv, mla_w_o, gla_w_a2, gla_b_a, gla_o_norm, gla_w_o,
           x_w_kv, x_w_o, w_out, ffn2_norm, ffn2_wg, ffn2_wu, ffn2_wd, final_norm):
    B, S, D = x.shape
    T = B * S
    M = mem.shape[1]
    F = D_FF
    H = MLA_HEADS
    bf = lambda a: a.astype(BF16)
    row = lambda a: a.reshape(1, -1)

    w_in_t = jnp.swapaxes(w_in[0], 0, 1)
    w_uq = mla_w_uq[0].reshape(MLA_Q_RANK, H, MLA_NOPE + MLA_ROPE)
    w_uq = bf(jnp.pad(w_uq, ((0, 0), (0, 0), (0, HEAD_PAD - MLA_NOPE - MLA_ROPE))).reshape(MLA_Q_RANK, H * HEAD_PAD))
    w_ukv = mla_w_ukv[0].reshape(MLA_KV_RANK, H, MLA_NOPE + MLA_V)
    w_uk = bf(jnp.pad(w_ukv[:, :, :MLA_NOPE], ((0, 0), (0, 0), (0, HEAD_PAD - MLA_NOPE))).reshape(MLA_KV_RANK, H * HEAD_PAD))
    w_uv = bf(w_ukv[:, :, MLA_NOPE:].reshape(MLA_KV_RANK, H * MLA_V))
    w_a2 = bf(gla_w_a2[0])
    half = MLA_ROPE // 2
    inv_freq = ROPE_THETA ** (-jnp.arange(half, dtype=F32) / half)
    invf = jnp.concatenate([jnp.zeros((MLA_NOPE,), F32), inv_freq, inv_freq,
                            jnp.zeros((HEAD_PAD - MLA_NOPE - MLA_ROPE,), F32)]).reshape(1, HEAD_PAD)

    x2 = x.reshape(T, D)
    pos2 = jnp.broadcast_to(positions.reshape(T, 1), (T, HEAD_PAD))
    nt = T // TM
    tok = lambda w: pl.BlockSpec((TM, w), lambda i: (jnp.maximum(i - W_STEPS, 0), 0))
    npair = H // 2
    tok_pairs = lambda w: pl.BlockSpec((npair, TM, w), lambda i: (0, jnp.maximum(i - W_STEPS, 0), 0))
    wrows = lambda shape: pl.BlockSpec(
        (shape[0] // W_STEPS, shape[1]), lambda i: (jnp.minimum(i, W_STEPS - 1), 0))
    ffn_specs = [wrows((D, F)), wrows((D, F)), wrows((F, D))]
    ffn_scratch = [pltpu.VMEM((D, 2 * F), BF16), pltpu.VMEM((F, D), BF16)]

    gdk, gdv, xw = GLA_HEADS * GLA_DK, GLA_HEADS * GLA_DV, X_HEADS * X_DH
    pre_out_shapes = (
        jax.ShapeDtypeStruct((T, D), F32),
        jax.ShapeDtypeStruct((npair, T, 2 * HEAD_PAD), BF16),
        jax.ShapeDtypeStruct((npair, T, 2 * HEAD_PAD), BF16),
        jax.ShapeDtypeStruct((npair, T, 2 * MLA_V), BF16),
        jax.ShapeDtypeStruct((T, 3 * gdk), F32),
        jax.ShapeDtypeStruct((T, 2 * gdv), BF16),
        jax.ShapeDtypeStruct((T, xw), BF16),
    )
    h1, q, k, v, gf, gvr, xq = pl.pallas_call(
        _pre_kernel,
        out_shape=pre_out_shapes,
        grid=(W_STEPS + nt,),
        in_specs=[tok(D), tok(HEAD_PAD), _const_spec((1, HEAD_PAD)), _const_spec((1, D)), _const_spec((1, D)),
                  *ffn_specs, pl.BlockSpec((WIN_PRE_ROWS, D), lambda i: (jnp.minimum(i, W_STEPS - 1), 0)),
                  _const_spec((1, MLA_Q_RANK)), _const_spec((MLA_Q_RANK, H * HEAD_PAD)),
                  _const_spec((1, MLA_KV_RANK)), _const_spec((MLA_KV_RANK, H * HEAD_PAD)),
                  _const_spec((MLA_KV_RANK, H * MLA_V)), _const_spec((GLA_GATE_RANK, gdk)), _const_spec((1, gdk))],
        out_specs=tuple(tok(s.shape[1]) if len(s.shape) == 2 else tok_pairs(s.shape[2])
                        for s in pre_out_shapes),
        scratch_shapes=[*ffn_scratch, pltpu.VMEM((W_STEPS * WIN_PRE_ROWS, D), BF16)],
        compiler_params=_cparams(("arbitrary",), VMEM_LIMIT),
        name="pre",
    )(x2, pos2, invf, row(ffn1_norm[0]), row(mix_norm[0]), ffn1_wg[0], ffn1_wu[0], ffn1_wd[0],
      w_in_t, row(mla_q_norm[0]), w_uq, row(mla_kv_norm[0]), w_uk, w_uv, w_a2, row(gla_b_a[0]))

    seq_pairs = lambda w: pl.BlockSpec((npair, S, w), lambda b: (0, b, 0))
    o_mla = pl.pallas_call(
        _mla_kernel,
        out_shape=jax.ShapeDtypeStruct((npair, T, 2 * MLA_V), BF16),
        grid=(B,),
        in_specs=[seq_pairs(2 * HEAD_PAD), seq_pairs(2 * HEAD_PAD), seq_pairs(2 * MLA_V)],
        out_specs=seq_pairs(2 * MLA_V),
        scratch_shapes=[pltpu.VMEM((MLA_GROUP, 2 * MLA_VT_ROWS, S), BF16)],
        compiler_params=_cparams(("parallel",), VMEM_LIMIT),
        name="mla",
    )(q, k, v)

    nqx = S // TQX
    o_x = pl.pallas_call(
        _xattn_kernel,
        out_shape=jax.ShapeDtypeStruct((T, xw), BF16),
        grid=(B, nqx),
        in_specs=[pl.BlockSpec((TQX, xw), lambda b, i: (b * nqx + i, 0)),
                  pl.BlockSpec((M, D), lambda b, i: (b, 0)),
                  _const_spec((1, D)), _const_spec((D, 2 * xw))],
        out_specs=pl.BlockSpec((TQX, xw), lambda b, i: (b * nqx + i, 0)),
        scratch_shapes=[pltpu.VMEM((M, 2 * xw), BF16)],
        compiler_params=_cparams(("arbitrary", "arbitrary")),
        name="xattn",
    )(xq, mem.reshape(B * M, D), row(mem_norm[0]), bf(x_w_kv[0]))

    o_gla = pl.pallas_call(
        _gla_kernel,
        out_shape=jax.ShapeDtypeStruct((T, gdv), BF16),
        grid=(B,),
        in_specs=[pl.BlockSpec((S, 3 * gdk), lambda b: (b, 0)),
                  pl.BlockSpec((S, 2 * gdv), lambda b: (b, 0)),
                  _const_spec((1, GLA_DV))],
        out_specs=pl.BlockSpec((S, gdv), lambda b: (b, 0)),
        compiler_params=_cparams(("parallel",), VMEM_LIMIT),
        name="gla",
    )(gf, gvr, row(gla_o_norm[0]))

    out = pl.pallas_call(
        _post_kernel,
        out_shape=jax.ShapeDtypeStruct((T, D), F32),
        grid=(W_STEPS + nt,),
        in_specs=[tok(D), tok_pairs(2 * MLA_V), tok(gdv), tok(xw), _const_spec((1, D)),
                  pl.BlockSpec((pl.Element(WIN_POST_ROWS), pl.Element(D)),
                               lambda i: (pl.multiple_of(C_GZ + jnp.minimum(i, W_STEPS - 1) * WIN_POST_ROWS, 16), 0)),
                  _const_spec((1, N_BRANCH * D)),
                  wrows((H * MLA_V, D)), wrows((gdv, D)), wrows((xw, D)),
                  wrows((D, D)), _const_spec((1, D)),
                  *ffn_specs, _const_spec((1, D))],
        out_specs=tok(D),
        scratch_shapes=[pltpu.VMEM((N_BRANCH * D, D), BF16), pltpu.VMEM((H * MLA_V, D), BF16),
                        pltpu.VMEM((gdv, D), BF16), pltpu.VMEM((xw, D), BF16), pltpu.VMEM((D, D), BF16),
                        *ffn_scratch],
        compiler_params=_cparams(("arbitrary",), VMEM_LIMIT),
        name="post",
    )(h1, o_mla, o_gla, o_x, row(mix_norm[0]), w_in_t, gate_bias[0].reshape(1, N_BRANCH * D),
      mla_w_o[0], gla_w_o[0], x_w_o[0], w_out[0], row(ffn2_norm[0]),
      ffn2_wg[0], ffn2_wu[0], ffn2_wd[0], row(final_norm))
    return out.reshape(B, S, D)
```

```python
import math

import jax
import jax.numpy as jnp
from jax import lax
from jax.experimental import pallas as pl
from jax.experimental.pallas import tpu as pltpu

F32 = jnp.float32
BF16 = jnp.bfloat16

D_MODEL = 1024
EPS = 1e-6
MLA_HEADS = 8
MLA_NOPE = 64
MLA_ROPE = 32
MLA_V = 64
MLA_Q_RANK = 384
MLA_KV_RANK = 256
ROPE_THETA = 10000.0
GLA_HEADS = 4
GLA_DK = 64
GLA_DV = 128
GLA_GATE_RANK = 16
GLA_TAU = 16.0
GLA_CHUNK = 64
X_HEADS = 4
X_DH = 128
D_FF = 2816
N_BRANCH = 3

LANE = 128
HEAD_PAD = 128
VMEM_LIMIT = 62 * 1024 * 1024
MLA_EXP2_SCALE = math.log2(math.e) / math.sqrt(MLA_NOPE + MLA_ROPE)
MLA_VT_ROWS = MLA_V + 16
MLA_GROUP = 2
MLA_LOOKAHEAD = 3

C_CQ, C_CKV, C_KR, C_GQ, C_GK, C_GV, C_GA, C_GR, C_XQ, C_GZ, C_END = (
    0, 384, 640, 672, 928, 1184, 1696, 1712, 2224, 2736, 5808)
W_STEPS = 8
WIN_PRE_ROWS = -(-C_GZ // (16 * W_STEPS)) * 16
WIN_POST_ROWS = (C_END - C_GZ) // W_STEPS

TM = 512
MERGE_TILE = 256
TQ = 256
TQX = 2048
GLA_SB = 256


def _dot(a, b):
    return jnp.dot(a, b, preferred_element_type=F32)


def _dot_nt(a, b):
    return lax.dot_general(a, b, (((1,), (1,)), ((), ())), preferred_element_type=F32)


def _dot_tn(a, b):
    return lax.dot_general(a, b, (((0,), (0,)), ((), ())), preferred_element_type=F32)


def _rms(x, g):
    return x * lax.rsqrt(jnp.mean(x * x, axis=-1, keepdims=True) + EPS) * g


def _sigmoid(x):
    return 1.0 / (1.0 + jnp.exp(-x))


def _silu(x):
    return x * _sigmoid(x)


def _const_spec(shape):
    return pl.BlockSpec(shape, lambda *_: (0,) * len(shape), pipeline_mode=pl.Buffered(1))


def _swiglu_half(xn_bf16, wgu_ref, wd_ref):
    tiles = []
    for f0 in range(0, 2 * D_FF, 2 * LANE):
        gu = _dot(xn_bf16, wgu_ref[:, f0:f0 + 2 * LANE])
        tiles.append((_silu(gu[:, :LANE]) * gu[:, LANE:]).astype(BF16))
    return _dot(jnp.concatenate(tiles, axis=1), wd_ref[...])


def _convert_rows(dst_ref, src, step, rows):
    r0 = pl.multiple_of(step * rows, rows)
    dst_ref[pl.ds(r0, rows), :] = src.astype(BF16)


def _convert_ffn(i, wg32_ref, wu32_ref, wd32_ref, wgu_ref, wd_ref):
    rows = D_MODEL // W_STEPS
    r0 = pl.multiple_of(i * rows, rows)
    wg = wg32_ref[...].astype(BF16)
    wu = wu32_ref[...].astype(BF16)
    for j in range(D_FF // LANE):
        wgu_ref[pl.ds(r0, rows), 2 * j * LANE:(2 * j + 1) * LANE] = wg[:, j * LANE:(j + 1) * LANE]
        wgu_ref[pl.ds(r0, rows), (2 * j + 1) * LANE:(2 * j + 2) * LANE] = wu[:, j * LANE:(j + 1) * LANE]
    _convert_rows(wd_ref, wd32_ref[...], i, D_FF // W_STEPS)


def _pre_kernel(x_ref, pos_ref, invf_ref, n1_ref, nm_ref, wg32_ref, wu32_ref, wd32_ref, win32_ref,
                qn_ref, wuq_ref, kvn_ref, wuk_ref, wuv_ref, wa2_ref, ba_ref,
                h1_ref, q_ref, k_ref, v_ref, gf_ref, gvr_ref, xq_ref,
                wgu_ref, wd_ref, win_ref, u2_ref):
    i = pl.program_id(0)
    last = pl.num_programs(0) - 1
    slot = lax.rem(i, 2)
    refs = (x_ref, pos_ref, invf_ref, n1_ref, nm_ref, wgu_ref, wd_ref, win_ref,
            qn_ref, wuq_ref, kvn_ref, wuk_ref, wuv_ref, wa2_ref, ba_ref,
            h1_ref, q_ref, k_ref, v_ref, gf_ref, gvr_ref, xq_ref, u2_ref.at[1 - slot], u2_ref.at[slot])

    @pl.when(i < W_STEPS)
    def _():
        _convert_ffn(i, wg32_ref, wu32_ref, wd32_ref, wgu_ref, wd_ref)
        _convert_rows(win_ref, win32_ref[...], i, WIN_PRE_ROWS)

    @pl.when(i == W_STEPS)
    def _():
        _pre_tokens(*refs, ffn=True, proj=False)

    @pl.when(jnp.logical_and(i > W_STEPS, i < last))
    def _():
        _pre_tokens(*refs, ffn=True, proj=True)

    @pl.when(i == last)
    def _():
        _pre_tokens(*refs, ffn=False, proj=True)


def _pre_tokens(x_ref, pos_ref, invf_ref, n1_ref, nm_ref, wgu_ref, wd_ref, win_ref,
                qn_ref, wuq_ref, kvn_ref, wuk_ref, wuv_ref, wa2_ref, ba_ref,
                h1_ref, q_ref, k_ref, v_ref, gf_ref, gvr_ref, xq_ref, u2p_ref, u2_ref, *, ffn, proj):
    npair = MLA_HEADS // 2
    gdk, gdv = GLA_HEADS * GLA_DK, GLA_HEADS * GLA_DV

    if proj:
        cq_raw = _dot_nt(u2p_ref[...], win_ref[C_CQ:C_CKV, :])
        ckv_raw = _dot_nt(u2p_ref[...], win_ref[C_CKV:C_KR, :])
        kr_raw = _dot_nt(u2p_ref[...], win_ref[C_KR:C_GQ, :])
        ga = _dot_nt(u2p_ref[...], win_ref[C_GA:C_GR, :]).astype(BF16)
        gf_ref[:, :gdk] = _dot_nt(u2p_ref[...], win_ref[C_GQ:C_GK, :])
        gf_ref[:, gdk:2 * gdk] = _dot_nt(u2p_ref[...], win_ref[C_GK:C_GV, :])

    if ffn:
        x = x_ref[...]
        u1 = _rms(x, n1_ref[...]).astype(BF16)

    if proj:
        ang = pos_ref[...].astype(F32) * invf_ref[...]
        cos = jnp.cos(ang)
        sin = jnp.sin(ang)
        lane = lax.broadcasted_iota(jnp.int32, ang.shape, 1)
        sin_hi = jnp.where(lane >= 80, sin, 0.0)
        sin_lo = jnp.where(lane < 80, -sin, 0.0)

        def rope(t):
            return t * cos + pltpu.roll(t, 16, 1) * sin_hi + pltpu.roll(t, LANE - 16, 1) * sin_lo

        cqn = _rms(cq_raw, qn_ref[...]).astype(BF16)
        ckvn = _rms(ckv_raw, kvn_ref[...]).astype(BF16)
        kr = rope(jnp.concatenate([jnp.zeros((kr_raw.shape[0], MLA_NOPE), F32), kr_raw,
                                   jnp.zeros((kr_raw.shape[0], HEAD_PAD - MLA_NOPE - MLA_ROPE), F32)], axis=1))
        kr2 = jnp.concatenate([kr, kr], axis=1)

        def q_pair(p):
            qp = _dot(cqn, wuq_ref[:, 2 * p * HEAD_PAD:(2 * p + 2) * HEAD_PAD])
            for j in range(2):
                q_ref[p, :, j * HEAD_PAD:(j + 1) * HEAD_PAD] = (
                    rope(qp[:, j * HEAD_PAD:(j + 1) * HEAD_PAD]) * MLA_EXP2_SCALE).astype(BF16)

    if ffn:
        h1 = x + 0.5 * _swiglu_half(u1, wgu_ref, wd_ref)
        h1_ref[...] = h1

    if proj:
        for p in range(npair // 2):
            q_pair(p)
        gvr_ref[:, :gdv] = _dot_nt(u2p_ref[...], win_ref[C_GV:C_GA, :]).astype(BF16)
        for p in range(npair // 2, npair):
            q_pair(p)
        for p in range(npair):
            kn = _dot(ckvn, wuk_ref[:, 2 * p * HEAD_PAD:(2 * p + 2) * HEAD_PAD])
            k_ref[p] = (kn + kr2).astype(BF16)
        vv = _dot(ckvn, wuv_ref[...]).astype(BF16)
        for p in range(npair):
            v_ref[p] = vv[:, 2 * p * MLA_V:(2 * p + 2) * MLA_V]
        t = _dot(ga, wa2_ref[...]) + ba_ref[...]
        log_sig = jnp.minimum(t, 0.0) - jnp.log(1.0 + jnp.exp(-jnp.abs(t)))
        gf_ref[:, 2 * gdk:] = log_sig * (1.0 / GLA_TAU)

    if ffn:
        u2_ref[...] = _rms(h1, nm_ref[...]).astype(BF16)

    if proj:
        gvr_ref[:, gdv:] = _dot_nt(u2p_ref[...], win_ref[C_GR:C_XQ, :]).astype(BF16)
        xq_ref[...] = _dot_nt(u2p_ref[...], win_ref[C_XQ:C_GZ, :]).astype(BF16)


def _mla_kernel(q_ref, k_ref, v_ref, o_ref, vt_ref):
    def group(g, carry):
        sl = pl.ds(g * MLA_GROUP, MLA_GROUP)
        _mla_pairs(q_ref.at[sl], k_ref.at[sl], v_ref.at[sl], o_ref.at[sl], vt_ref)
        return carry

    lax.fori_loop(0, q_ref.shape[0] // MLA_GROUP, group, 0)


def _mla_pairs(q_ref, k_ref, v_ref, o_ref, vt_ref):
    npair = q_ref.shape[0]
    tq = TQ
    nq = q_ref.shape[1] // tq
    key = lax.broadcasted_iota(jnp.int32, (tq, tq), 0)
    qry = lax.broadcasted_iota(jnp.int32, (tq, tq), 1)
    causal = key <= qry
    ones_row = jnp.where(lax.broadcasted_iota(jnp.int32, (MLA_VT_ROWS - MLA_V, q_ref.shape[1]), 0) == 0,
                         1.0, 0.0).astype(BF16)

    def prepare(p):
        vt = v_ref[p].astype(F32).T.astype(BF16)
        for h in range(2):
            vt_ref[p, h * MLA_VT_ROWS:h * MLA_VT_ROWS + MLA_V, :] = vt[h * MLA_V:(h + 1) * MLA_V, :]
            vt_ref[p, h * MLA_VT_ROWS + MLA_V:(h + 1) * MLA_VT_ROWS, :] = ones_row

    items = [(p, qi, c) for p in range(npair) for qi in range(nq) for c in range(qi + 1)]
    pending = [{}, {}]
    state = [{}, {}]
    done = {}

    def score(h, i):
        p, qi, c = items[i]
        q = q_ref[p, qi * tq:(qi + 1) * tq, h * HEAD_PAD:(h + 1) * HEAD_PAD]
        s = _dot_nt(k_ref[p, c * tq:(c + 1) * tq, h * HEAD_PAD:(h + 1) * HEAD_PAD], q)
        pending[h][i] = jnp.where(causal, s, -1e30) if c == qi else s

    def consume(h, i):
        p, qi, c = items[i]
        st = state[h]
        s = pending[h].pop(i)
        cm = jnp.max(s, axis=0, keepdims=True)
        m_new = cm if c == 0 else jnp.maximum(st["m"], cm)
        pr = jnp.exp2(s - m_new).astype(BF16)
        pv = _dot(vt_ref[p, h * MLA_VT_ROWS:(h + 1) * MLA_VT_ROWS, c * tq:(c + 1) * tq], pr)
        st["acc"] = pv if c == 0 else st["acc"] * jnp.exp2(st["m"] - m_new) + pv
        st["m"] = m_new
        if c == qi:
            done[(p, qi, h)] = st["acc"][:MLA_V, :] / st["acc"][MLA_V:MLA_V + 1, :]
            if (p, qi, 1 - h) in done:
                o_ref[p, qi * tq:(qi + 1) * tq, :] = jnp.concatenate(
                    [done.pop((p, qi, 0)), done.pop((p, qi, 1))], axis=0).T.astype(BF16)

    for t in range(len(items) + MLA_LOOKAHEAD):
        if t < len(items) and items[t][1:] == (0, 0):
            prepare(items[t][0])
        for h in range(2):
            if t < len(items):
                score(h, t)
            if t >= MLA_LOOKAHEAD:
                consume(h, t - MLA_LOOKAHEAD)


def _xattn_kernel(q_ref, mem_ref, n_ref, w_ref, o_ref, kv_ref):
    @pl.when(pl.program_id(1) == 0)
    def _():
        mn = _rms(mem_ref[...], n_ref[...]).astype(BF16)
        kv_ref[...] = _dot(mn, w_ref[...]).astype(BF16)

    scale = 1.0 / math.sqrt(X_DH)
    hw = X_HEADS * X_DH
    for h in range(X_HEADS):
        q = q_ref[:, h * X_DH:(h + 1) * X_DH]
        k = kv_ref[:, h * X_DH:(h + 1) * X_DH]
        v = kv_ref[:, hw + h * X_DH:hw + (h + 1) * X_DH]
        s = _dot_nt(q, k) * scale
        m = jnp.max(s, axis=-1, keepdims=True)
        p = jnp.exp(s - m)
        l = jnp.sum(p, axis=-1, keepdims=True)
        o = _dot(p.astype(BF16), v) / l
        o_ref[:, h * X_DH:(h + 1) * X_DH] = o.astype(BF16)


def _gla_kernel(gf_ref, gvr_ref, on_ref, o_ref):
    sb, c = GLA_SB, GLA_CHUNK
    nsb = gf_ref.shape[0] // sb
    nch = sb // c
    npair = GLA_HEADS // 2
    pw, vw = 2 * GLA_DK, 2 * GLA_DV
    gdk, gdv = GLA_HEADS * GLA_DK, GLA_HEADS * GLA_DV
    row = lax.broadcasted_iota(jnp.int32, (sb, sb), 0)
    col = lax.broadcasted_iota(jnp.int32, (sb, sb), 1)
    tri = jnp.logical_and(row // c == col // c, col <= row)
    tri_bf = jnp.where(tri, 1.0, 0.0).astype(BF16)
    lane = lax.broadcasted_iota(jnp.int32, (1, pw), 1)
    head_lanes = (lane < GLA_DK, lane >= GLA_DK)
    srow = lax.broadcasted_iota(jnp.int32, (pw, vw), 0)
    scol = lax.broadcasted_iota(jnp.int32, (pw, vw), 1)
    same_head = (srow < GLA_DK) == (scol < GLA_DV)

    def local(i):
        r0 = i * sb
        la = gf_ref[r0:r0 + sb, 2 * gdk:]
        hi = la.astype(BF16)
        lo = (la - hi.astype(F32)).astype(BF16)
        bcum_all = _dot(tri_bf, hi) + _dot(tri_bf, lo)
        out = []
        for p in range(npair):
            bcum = bcum_all[:, p * pw:(p + 1) * pw]
            e_last = [jnp.exp(bcum[j * c + c - 1:j * c + c, :]) for j in range(nch)]
            e_last_rows = jnp.concatenate([jnp.broadcast_to(e, (c, pw)) for e in e_last], axis=0)
            q_t = gf_ref[r0:r0 + sb, p * pw:(p + 1) * pw] * (GLA_DK ** -0.5) * jnp.exp(bcum)
            kk = gf_ref[r0:r0 + sb, gdk + p * pw:gdk + (p + 1) * pw]
            k_undecayed = kk * jnp.exp(-bcum)
            k_t = k_undecayed.astype(BF16)
            k_d = (k_undecayed * e_last_rows).astype(BF16)
            v = gvr_ref[r0:r0 + sb, p * vw:(p + 1) * vw]
            o_intra = []
            for h in range(2):
                qm = jnp.where(head_lanes[h], q_t, 0.0).astype(BF16)
                att = jnp.where(tri, _dot_nt(qm, k_t), 0.0).astype(BF16)
                o_intra.append(_dot(att, v[:, h * GLA_DV:(h + 1) * GLA_DV]))
            d_st, decay = [], []
            for j in range(nch):
                d = _dot_tn(k_d[j * c:(j + 1) * c, :], v[j * c:(j + 1) * c, :])
                d_st.append(jnp.where(same_head, d, 0.0))
                dcol = jnp.broadcast_to(e_last[j], (pw, pw)).T
                decay.append(jnp.concatenate([dcol, dcol], axis=1))
            out.append((q_t.astype(BF16), o_intra, d_st, decay))
        return out

    def recurrent(i, loc, states):
        r0 = i * sb
        for p in range(npair):
            q_bf, o_intra, d_st, decay = loc[p]
            st = states[p]
            o_inter = []
            for j in range(nch):
                o_inter.append(_dot(q_bf[j * c:(j + 1) * c, :], st.astype(BF16)))
                st = decay[j] * st + d_st[j]
            states[p] = st
            o_inter = jnp.concatenate(o_inter, axis=0)
            for h in range(2):
                lo_, hi_ = p * vw + h * GLA_DV, p * vw + (h + 1) * GLA_DV
                o = _rms(o_intra[h] + o_inter[:, h * GLA_DV:(h + 1) * GLA_DV], on_ref[...])
                r = gvr_ref[r0:r0 + sb, gdv + lo_:gdv + hi_].astype(F32)
                o_ref[r0:r0 + sb, lo_:hi_] = (o * _silu(r)).astype(BF16)

    states = [jnp.zeros((pw, vw), F32) for _ in range(npair)]
    loc = local(0)
    for i in range(nsb):
        nxt = local(i + 1) if i + 1 < nsb else None
        recurrent(i, loc, states)
        loc = nxt


def _post_kernel(h1_ref, om_ref, og_ref, ox_ref, nm_ref, wgz32_ref, gb_ref, wom32_ref, wog32_ref, wox32_ref,
                 wout32_ref, n2_ref, wg32_ref, wu32_ref, wd32_ref, nf_ref, o_ref,
                 wgz_ref, wom_ref, wog_ref, wox_ref, wout_ref, wgu_ref, wd_ref):
    i = pl.program_id(0)

    @pl.when(i < W_STEPS)
    def _():
        _convert_ffn(i, wg32_ref, wu32_ref, wd32_ref, wgu_ref, wd_ref)
        _convert_rows(wgz_ref, wgz32_ref[...], i, WIN_POST_ROWS)
        for dst, src in ((wom_ref, wom32_ref), (wog_ref, wog32_ref), (wox_ref, wox32_ref)):
            _convert_rows(dst, src[...], i, dst.shape[0] // W_STEPS)
        _convert_rows(wout_ref, wout32_ref[...], i, D_MODEL // W_STEPS)

    @pl.when(i >= W_STEPS)
    def _():
        _post_tokens(h1_ref, om_ref, og_ref, ox_ref, nm_ref, wgz_ref, gb_ref, wom_ref, wog_ref, wox_ref,
                     wout_ref, n2_ref, wgu_ref, wd_ref, nf_ref, o_ref)


def _post_tokens(h1_ref, om_ref, og_ref, ox_ref, nm_ref, wgz_ref, gb_ref, wom_ref, wog_ref, wox_ref,
                 wout_ref, n2_ref, wgu_ref, wd_ref, nf_ref, o_ref):
    h1 = h1_ref[...]
    u = _rms(h1, nm_ref[...]).astype(BF16)
    d = D_MODEL
    branches = ((jnp.concatenate([om_ref[p] for p in range(om_ref.shape[0])], axis=1), wom_ref),
                (og_ref[...], wog_ref), (ox_ref[...], wox_ref))
    tiles = []
    for c0 in range(0, d, MERGE_TILE):
        acc = None
        for n, (o_in, w_br) in enumerate(branches):
            gate = _sigmoid(_dot_nt(u, wgz_ref[n * d + c0:n * d + c0 + MERGE_TILE, :])
                            + gb_ref[:, n * d + c0:n * d + c0 + MERGE_TILE])
            term = gate * _dot(o_in, w_br[:, c0:c0 + MERGE_TILE])
            acc = term if acc is None else acc + term
        tiles.append(acc.astype(BF16))
    h2 = h1 + _dot(jnp.concatenate(tiles, axis=1), wout_ref[...])
    u2 = _rms(h2, n2_ref[...]).astype(BF16)
    h3 = h2 + 0.5 * _swiglu_half(u2, wgu_ref, wd_ref)
    o_ref[...] = _rms(h3, nf_ref[...])


def _cparams(sem, vmem=None, flags=None):
    return pltpu.CompilerParams(dimension_semantics=sem, vmem_limit_bytes=vmem, flags=flags)


def kernel(x, mem, positions, ffn1_norm, ffn1_wg, ffn1_wu, ffn1_wd, mix_norm, mem_norm, w_in, gate_bias,
           mla_q_norm, mla_w_uq, mla_kv_norm, mla_w_ukv, mla_w_o, gla_w_a2, gla_b_a, gla_o_norm, gla_w_o,
           x_w_kv, x_w_o, w_out, ffn2_norm, ffn2_wg, ffn2_wu, ffn2_wd, final_norm):
    B, S, D = x.shape
    T = B * S
    M = mem.shape[1]
    F = D_FF
    H = MLA_HEADS
    bf = lambda a: a.astype(BF16)
    row = lambda a: a.reshape(1, -1)

    w_in_t = jnp.swapaxes(w_in[0], 0, 1)
    w_uq = mla_w_uq[0].reshape(MLA_Q_RANK, H, MLA_NOPE + MLA_ROPE)
    w_uq = bf(jnp.pad(w_uq, ((0, 0), (0, 0), (0, HEAD_PAD - MLA_NOPE - MLA_ROPE))).reshape(MLA_Q_RANK, H * HEAD_PAD))
    w_ukv = mla_w_ukv[0].reshape(MLA_KV_RANK, H, MLA_NOPE + MLA_V)
    w_uk = bf(jnp.pad(w_ukv[:, :, :MLA_NOPE], ((0, 0), (0, 0), (0, HEAD_PAD - MLA_NOPE))).reshape(MLA_KV_RANK, H * HEAD_PAD))
    w_uv = bf(w_ukv[:, :, MLA_NOPE:].reshape(MLA_KV_RANK, H * MLA_V))
    w_a2 = bf(gla_w_a2[0])
    half = MLA_ROPE // 2
    inv_freq = ROPE_THETA ** (-jnp.arange(half, dtype=F32) / half)
    invf = jnp.concatenate([jnp.zeros((MLA_NOPE,), F32), inv_freq, inv_freq,
                            jnp.zeros((HEAD_PAD - MLA_NOPE - MLA_ROPE,), F32)]).reshape(1, HEAD_PAD)

    x2 = x.reshape(T, D)
    pos2 = jnp.broadcast_to(positions.reshape(T, 1), (T, HEAD_PAD))
    nt = T // TM
    tok = lambda w: pl.BlockSpec((TM, w), lambda i: (jnp.clip(i - W_STEPS, 0, nt - 1), 0))
    tok_prev = lambda w: pl.BlockSpec((TM, w), lambda i: (jnp.clip(i - W_STEPS - 1, 0, nt - 1), 0))
    tok_pairs_prev = lambda w: pl.BlockSpec(
        (MLA_HEADS // 2, TM, w), lambda i: (0, jnp.clip(i - W_STEPS - 1, 0, nt - 1), 0))
    npair = H // 2
    tok_pairs = lambda w: pl.BlockSpec((npair, TM, w), lambda i: (0, jnp.maximum(i - W_STEPS, 0), 0))
    wrows = lambda shape: pl.BlockSpec(
        (shape[0] // W_STEPS, shape[1]), lambda i: (jnp.minimum(i, W_STEPS - 1), 0))
    ffn_specs = [wrows((D, F)), wrows((D, F)), wrows((F, D))]
    ffn_scratch = [pltpu.VMEM((D, 2 * F), BF16), pltpu.VMEM((F, D), BF16)]

    gdk, gdv, xw = GLA_HEADS * GLA_DK, GLA_HEADS * GLA_DV, X_HEADS * X_DH
    pre_out_shapes = (
        jax.ShapeDtypeStruct((T, D), F32),
        jax.ShapeDtypeStruct((npair, T, 2 * HEAD_PAD), BF16),
        jax.ShapeDtypeStruct((npair, T, 2 * HEAD_PAD), BF16),
        jax.ShapeDtypeStruct((npair, T, 2 * MLA_V), BF16),
        jax.ShapeDtypeStruct((T, 3 * gdk), F32),
        jax.ShapeDtypeStruct((T, 2 * gdv), BF16),
        jax.ShapeDtypeStruct((T, xw), BF16),
    )
    h1, q, k, v, gf, gvr, xq = pl.pallas_call(
        _pre_kernel,
        out_shape=pre_out_shapes,
        grid=(W_STEPS + nt + 1,),
        in_specs=[tok(D), tok_prev(HEAD_PAD), _const_spec((1, HEAD_PAD)), _const_spec((1, D)), _const_spec((1, D)),
                  *ffn_specs, pl.BlockSpec((WIN_PRE_ROWS, D), lambda i: (jnp.minimum(i, W_STEPS - 1), 0)),
                  _const_spec((1, MLA_Q_RANK)), _const_spec((MLA_Q_RANK, H * HEAD_PAD)),
                  _const_spec((1, MLA_KV_RANK)), _const_spec((MLA_KV_RANK, H * HEAD_PAD)),
                  _const_spec((MLA_KV_RANK, H * MLA_V)), _const_spec((GLA_GATE_RANK, gdk)), _const_spec((1, gdk))],
        out_specs=(tok(D),) + tuple(tok_prev(s.shape[1]) if len(s.shape) == 2 else tok_pairs_prev(s.shape[2])
                                    for s in pre_out_shapes[1:]),
        scratch_shapes=[*ffn_scratch, pltpu.VMEM((W_STEPS * WIN_PRE_ROWS, D), BF16),
                        pltpu.VMEM((2, TM, D), BF16)],
        compiler_params=_cparams(("arbitrary",), VMEM_LIMIT),
        name="pre",
    )(x2, pos2, invf, row(ffn1_norm[0]), row(mix_norm[0]), ffn1_wg[0], ffn1_wu[0], ffn1_wd[0],
      w_in_t, row(mla_q_norm[0]), w_uq, row(mla_kv_norm[0]), w_uk, w_uv, w_a2, row(gla_b_a[0]))

    seq_pairs = lambda w: pl.BlockSpec((npair, S, w), lambda b: (0, b, 0))
    o_mla = pl.pallas_call(
        _mla_kernel,
        out_shape=jax.ShapeDtypeStruct((npair, T, 2 * MLA_V), BF16),
        grid=(B,),
        in_specs=[seq_pairs(2 * HEAD_PAD), seq_pairs(2 * HEAD_PAD), seq_pairs(2 * MLA_V)],
        out_specs=seq_pairs(2 * MLA_V),
        scratch_shapes=[pltpu.VMEM((MLA_GROUP, 2 * MLA_VT_ROWS, S), BF16)],
        compiler_params=_cparams(("parallel",), VMEM_LIMIT),
        name="mla",
    )(q, k, v)

    nqx = S // TQX
    o_x = pl.pallas_call(
        _xattn_kernel,
        out_shape=jax.ShapeDtypeStruct((T, xw), BF16),
        grid=(B, nqx),
        in_specs=[pl.BlockSpec((TQX, xw), lambda b, i: (b * nqx + i, 0)),
                  pl.BlockSpec((M, D), lambda b, i: (b, 0)),
                  _const_spec((1, D)), _const_spec((D, 2 * xw))],
        out_specs=pl.BlockSpec((TQX, xw), lambda b, i: (b * nqx + i, 0)),
        scratch_shapes=[pltpu.VMEM((M, 2 * xw), BF16)],
        compiler_params=_cparams(("arbitrary", "arbitrary")),
        name="xattn",
    )(xq, mem.reshape(B * M, D), row(mem_norm[0]), bf(x_w_kv[0]))

    o_gla = pl.pallas_call(
        _gla_kernel,
        out_shape=jax.ShapeDtypeStruct((T, gdv), BF16),
        grid=(B,),
        in_specs=[pl.BlockSpec((S, 3 * gdk), lambda b: (b, 0)),
                  pl.BlockSpec((S, 2 * gdv), lambda b: (b, 0)),
                  _const_spec((1, GLA_DV))],
        out_specs=pl.BlockSpec((S, gdv), lambda b: (b, 0)),
        compiler_params=_cparams(("parallel",), VMEM_LIMIT),
        name="gla",
    )(gf, gvr, row(gla_o_norm[0]))

    out = pl.pallas_call(
        _post_kernel,
        out_shape=jax.ShapeDtypeStruct((T, D), F32),
        grid=(W_STEPS + nt,),
        in_specs=[tok(D), tok_pairs(2 * MLA_V), tok(gdv), tok(xw), _const_spec((1, D)),
                  pl.BlockSpec((pl.Element(WIN_POST_ROWS), pl.Element(D)),
                               lambda i: (pl.multiple_of(C_GZ + jnp.minimum(i, W_STEPS - 1) * WIN_POST_ROWS, 16), 0)),
                  _const_spec((1, N_BRANCH * D)),
                  wrows((H * MLA_V, D)), wrows((gdv, D)), wrows((xw, D)),
                  wrows((D, D)), _const_spec((1, D)),
                  *ffn_specs, _const_spec((1, D))],
        out_specs=tok(D),
        scratch_shapes=[pltpu.VMEM((N_BRANCH * D, D), BF16), pltpu.VMEM((H * MLA_V, D), BF16),
                        pltpu.VMEM((gdv, D), BF16), pltpu.VMEM((xw, D), BF16), pltpu.VMEM((D, D), BF16),
                        *ffn_scratch],
        compiler_params=_cparams(("arbitrary",), VMEM_LIMIT),
        name="post",
    )(h1, o_mla, o_gla, o_x, row(mix_norm[0]), w_in_t, gate_bias[0].reshape(1, N_BRANCH * D),
      mla_w_o[0], gla_w_o[0], x_w_o[0], w_out[0], row(ffn2_norm[0]),
      ffn2_wg[0], ffn2_wu[0], ffn2_wd[0], row(final_norm))
    return out.reshape(B, S, D)
```

```python
import math

import jax
import jax.numpy as jnp
from jax import lax
from jax.experimental import pallas as pl
from jax.experimental.pallas import tpu as pltpu

F32 = jnp.float32
BF16 = jnp.bfloat16

D_MODEL = 1024
EPS = 1e-6
MLA_HEADS = 8
MLA_NOPE = 64
MLA_ROPE = 32
MLA_V = 64
MLA_Q_RANK = 384
MLA_KV_RANK = 256
ROPE_THETA = 10000.0
GLA_HEADS = 4
GLA_DK = 64
GLA_DV = 128
GLA_GATE_RANK = 16
GLA_TAU = 16.0
GLA_CHUNK = 64
X_HEADS = 4
X_DH = 128
D_FF = 2816
N_BRANCH = 3

LANE = 128
HEAD_PAD = 128
VMEM_LIMIT = 62 * 1024 * 1024
MLA_EXP2_SCALE = math.log2(math.e) / math.sqrt(MLA_NOPE + MLA_ROPE)
MLA_VT_ROWS = MLA_V + 16
MLA_GROUP = 2
MLA_LOOKAHEAD = 3

C_CQ, C_CKV, C_KR, C_GQ, C_GK, C_GV, C_GA, C_GR, C_XQ, C_GZ, C_END = (
    0, 384, 640, 672, 928, 1184, 1696, 1712, 2224, 2736, 5808)
W_STEPS = 8
WIN_PRE_ROWS = -(-C_GZ // (16 * W_STEPS)) * 16
WIN_POST_ROWS = (C_END - C_GZ) // W_STEPS

TM = 512
MERGE_TILE = 256
TQ = 256
TQX = 2048
GLA_SB = 256


def _dot(a, b):
    return jnp.dot(a, b, preferred_element_type=F32)


def _dot_nt(a, b):
    return lax.dot_general(a, b, (((1,), (1,)), ((), ())), preferred_element_type=F32)


def _dot_tn(a, b):
    return lax.dot_general(a, b, (((0,), (0,)), ((), ())), preferred_element_type=F32)


def _rms(x, g):
    return x * lax.rsqrt(jnp.mean(x * x, axis=-1, keepdims=True) + EPS) * g


def _sigmoid(x):
    return 1.0 / (1.0 + jnp.exp(-x))


def _silu(x):
    return x * _sigmoid(x)


def _const_spec(shape):
    return pl.BlockSpec(shape, lambda *_: (0,) * len(shape), pipeline_mode=pl.Buffered(1))


def _swiglu_half(xn_bf16, wgu_ref, wd_ref):
    tiles = []
    for f0 in range(0, 2 * D_FF, 2 * LANE):
        gu = _dot(xn_bf16, wgu_ref[:, f0:f0 + 2 * LANE])
        tiles.append((_silu(gu[:, :LANE]) * gu[:, LANE:]).astype(BF16))
    return _dot(jnp.concatenate(tiles, axis=1), wd_ref[...])


def _convert_rows(dst_ref, src, step, rows):
    r0 = pl.multiple_of(step * rows, rows)
    dst_ref[pl.ds(r0, rows), :] = src.astype(BF16)


def _convert_ffn(i, wg32_ref, wu32_ref, wd32_ref, wgu_ref, wd_ref):
    rows = D_MODEL // W_STEPS
    r0 = pl.multiple_of(i * rows, rows)
    wg = wg32_ref[...].astype(BF16)
    wu = wu32_ref[...].astype(BF16)
    for j in range(D_FF // LANE):
        wgu_ref[pl.ds(r0, rows), 2 * j * LANE:(2 * j + 1) * LANE] = wg[:, j * LANE:(j + 1) * LANE]
        wgu_ref[pl.ds(r0, rows), (2 * j + 1) * LANE:(2 * j + 2) * LANE] = wu[:, j * LANE:(j + 1) * LANE]
    _convert_rows(wd_ref, wd32_ref[...], i, D_FF // W_STEPS)


def _pre_kernel(x_ref, pos_ref, invf_ref, n1_ref, nm_ref, wg32_ref, wu32_ref, wd32_ref, win32_ref,
                qn_ref, wuq_ref, kvn_ref, wuk_ref, wuv_ref, wa2_ref, ba_ref,
                h1_ref, q_ref, k_ref, v_ref, gf_ref, gvr_ref, xq_ref,
                wgu_ref, wd_ref, win_ref, u2_ref):
    i = pl.program_id(0)
    last = pl.num_programs(0) - 1
    slot = lax.rem(i, 2)
    refs = (x_ref, pos_ref, invf_ref, n1_ref, nm_ref, wgu_ref, wd_ref, win_ref,
            qn_ref, wuq_ref, kvn_ref, wuk_ref, wuv_ref, wa2_ref, ba_ref,
            h1_ref, q_ref, k_ref, v_ref, gf_ref, gvr_ref, xq_ref, u2_ref.at[1 - slot], u2_ref.at[slot])

    @pl.when(i < W_STEPS)
    def _():
        _convert_ffn(i, wg32_ref, wu32_ref, wd32_ref, wgu_ref, wd_ref)
        _convert_rows(win_ref, win32_ref[...], i, WIN_PRE_ROWS)

    @pl.when(i == W_STEPS)
    def _():
        _pre_tokens(*refs, ffn=True, proj=False)

    @pl.when(jnp.logical_and(i > W_STEPS, i < last))
    def _():
        _pre_tokens(*refs, ffn=True, proj=True)

    @pl.when(i == last)
    def _():
        _pre_tokens(*refs, ffn=False, proj=True)


def _pre_tokens(x_ref, pos_ref, invf_ref, n1_ref, nm_ref, wgu_ref, wd_ref, win_ref,
                qn_ref, wuq_ref, kvn_ref, wuk_ref, wuv_ref, wa2_ref, ba_ref,
                h1_ref, q_ref, k_ref, v_ref, gf_ref, gvr_ref, xq_ref, u2p_ref, u2_ref, *, ffn, proj):
    npair = MLA_HEADS // 2
    gdk, gdv = GLA_HEADS * GLA_DK, GLA_HEADS * GLA_DV

    if proj:
        cq_raw = _dot_nt(u2p_ref[...], win_ref[C_CQ:C_CKV, :])
        ckv_raw = _dot_nt(u2p_ref[...], win_ref[C_CKV:C_KR, :])
        kr_raw = _dot_nt(u2p_ref[...], win_ref[C_KR:C_GQ, :])
        ga = _dot_nt(u2p_ref[...], win_ref[C_GA:C_GR, :]).astype(BF16)
        gf_ref[:, :gdk] = _dot_nt(u2p_ref[...], win_ref[C_GQ:C_GK, :])
        gf_ref[:, gdk:2 * gdk] = _dot_nt(u2p_ref[...], win_ref[C_GK:C_GV, :])

    if ffn:
        x = x_ref[...]
        u1 = _rms(x, n1_ref[...]).astype(BF16)

    if proj:
        ang = pos_ref[...].astype(F32) * invf_ref[...]
        cos = jnp.cos(ang)
        sin = jnp.sin(ang)
        lane = lax.broadcasted_iota(jnp.int32, ang.shape, 1)
        sin_hi = jnp.where(lane >= 80, sin, 0.0)
        sin_lo = jnp.where(lane < 80, -sin, 0.0)

        def rope(t):
            return t * cos + pltpu.roll(t, 16, 1) * sin_hi + pltpu.roll(t, LANE - 16, 1) * sin_lo

        cqn = _rms(cq_raw, qn_ref[...]).astype(BF16)
        ckvn = _rms(ckv_raw, kvn_ref[...]).astype(BF16)
        kr = rope(jnp.concatenate([jnp.zeros((kr_raw.shape[0], MLA_NOPE), F32), kr_raw,
                                   jnp.zeros((kr_raw.shape[0], HEAD_PAD - MLA_NOPE - MLA_ROPE), F32)], axis=1))
        kr2 = jnp.concatenate([kr, kr], axis=1)

        def q_pair(p):
            qp = _dot(cqn, wuq_ref[:, 2 * p * HEAD_PAD:(2 * p + 2) * HEAD_PAD])
            for j in range(2):
                q_ref[p, :, j * HEAD_PAD:(j + 1) * HEAD_PAD] = (
                    rope(qp[:, j * HEAD_PAD:(j + 1) * HEAD_PAD]) * MLA_EXP2_SCALE).astype(BF16)

    if ffn:
        h1 = x + 0.5 * _swiglu_half(u1, wgu_ref, wd_ref)
        h1_ref[...] = h1

    if proj:
        for p in range(npair // 2):
            q_pair(p)
        gvr_ref[:, :gdv] = _dot_nt(u2p_ref[...], win_ref[C_GV:C_GA, :]).astype(BF16)
        for p in range(npair // 2, npair):
            q_pair(p)
        for p in range(npair):
            kn = _dot(ckvn, wuk_ref[:, 2 * p * HEAD_PAD:(2 * p + 2) * HEAD_PAD])
            k_ref[p] = (kn + kr2).astype(BF16)
        vv = _dot(ckvn, wuv_ref[...]).astype(BF16)
        for p in range(npair):
            v_ref[p] = vv[:, 2 * p * MLA_V:(2 * p + 2) * MLA_V]
        t = _dot(ga, wa2_ref[...]) + ba_ref[...]
        log_sig = jnp.minimum(t, 0.0) - jnp.log(1.0 + jnp.exp(-jnp.abs(t)))
        gf_ref[:, 2 * gdk:] = log_sig * (1.0 / GLA_TAU)

    if ffn:
        u2_ref[...] = _rms(h1, nm_ref[...]).astype(BF16)

    if proj:
        gvr_ref[:, gdv:] = _dot_nt(u2p_ref[...], win_ref[C_GR:C_XQ, :]).astype(BF16)
        xq_ref[...] = _dot_nt(u2p_ref[...], win_ref[C_XQ:C_GZ, :]).astype(BF16)


def _mla_kernel(q_ref, k_ref, v_ref, o_ref, vt_ref):
    def group(g, carry):
        sl = pl.ds(g * MLA_GROUP, MLA_GROUP)
        _mla_pairs(q_ref.at[sl], k_ref.at[sl], v_ref.at[sl], o_ref.at[sl], vt_ref)
        return carry

    lax.fori_loop(0, q_ref.shape[0] // MLA_GROUP, group, 0)


def _mla_pairs(q_ref, k_ref, v_ref, o_ref, vt_ref):
    npair = q_ref.shape[0]
    tq = TQ
    nq = q_ref.shape[1] // tq
    key = lax.broadcasted_iota(jnp.int32, (tq, tq), 0)
    qry = lax.broadcasted_iota(jnp.int32, (tq, tq), 1)
    causal = key <= qry
    ones_row = jnp.where(lax.broadcasted_iota(jnp.int32, (MLA_VT_ROWS - MLA_V, q_ref.shape[1]), 0) == 0,
                         1.0, 0.0).astype(BF16)

    def prepare(p):
        vt = v_ref[p].astype(F32).T.astype(BF16)
        for h in range(2):
            vt_ref[p, h * MLA_VT_ROWS:h * MLA_VT_ROWS + MLA_V, :] = vt[h * MLA_V:(h + 1) * MLA_V, :]
            vt_ref[p, h * MLA_VT_ROWS + MLA_V:(h + 1) * MLA_VT_ROWS, :] = ones_row

    items = [(p, qi, c) for p in range(npair) for qi in range(nq) for c in range(qi + 1)]
    pending = [{}, {}]
    state = [{}, {}]
    done = {}

    def score(h, i):
        p, qi, c = items[i]
        q = q_ref[p, qi * tq:(qi + 1) * tq, h * HEAD_PAD:(h + 1) * HEAD_PAD]
        s = _dot_nt(k_ref[p, c * tq:(c + 1) * tq, h * HEAD_PAD:(h + 1) * HEAD_PAD], q)
        pending[h][i] = jnp.where(causal, s, -1e30) if c == qi else s

    def consume(h, i):
        p, qi, c = items[i]
        st = state[h]
        s = pending[h].pop(i)
        cm = jnp.max(s, axis=0, keepdims=True)
        m_new = cm if c == 0 else jnp.maximum(st["m"], cm)
        pr = jnp.exp2(s - m_new).astype(BF16)
        pv = _dot(vt_ref[p, h * MLA_VT_ROWS:(h + 1) * MLA_VT_ROWS, c * tq:(c + 1) * tq], pr)
        st["acc"] = pv if c == 0 else st["acc"] * jnp.exp2(st["m"] - m_new) + pv
        st["m"] = m_new
        if c == qi:
            done[(p, qi, h)] = st["acc"][:MLA_V, :] / st["acc"][MLA_V:MLA_V + 1, :]
            if (p, qi, 1 - h) in done:
                o_ref[p, qi * tq:(qi + 1) * tq, :] = jnp.concatenate(
                    [done.pop((p, qi, 0)), done.pop((p, qi, 1))], axis=0).T.astype(BF16)

    for t in range(len(items) + MLA_LOOKAHEAD):
        if t < len(items) and items[t][1:] == (0, 0):
            prepare(items[t][0])
        for h in range(2):
            if t < len(items):
                score(h, t)
            if t >= MLA_LOOKAHEAD:
                consume(h, t - MLA_LOOKAHEAD)


def _xattn_kernel(q_ref, mem_ref, n_ref, w_ref, o_ref, kv_ref):
    @pl.when(pl.program_id(1) == 0)
    def _():
        mn = _rms(mem_ref[...], n_ref[...]).astype(BF16)
        kv_ref[...] = _dot(mn, w_ref[...]).astype(BF16)

    scale = 1.0 / math.sqrt(X_DH)
    hw = X_HEADS * X_DH
    for h in range(X_HEADS):
        q = q_ref[:, h * X_DH:(h + 1) * X_DH]
        k = kv_ref[:, h * X_DH:(h + 1) * X_DH]
        v = kv_ref[:, hw + h * X_DH:hw + (h + 1) * X_DH]
        s = _dot_nt(q, k) * scale
        m = jnp.max(s, axis=-1, keepdims=True)
        p = jnp.exp(s - m)
        l = jnp.sum(p, axis=-1, keepdims=True)
        o = _dot(p.astype(BF16), v) / l
        o_ref[:, h * X_DH:(h + 1) * X_DH] = o.astype(BF16)


def _gla_kernel(gf_ref, gvr_ref, on_ref, o_ref):
    sb, c = GLA_SB, GLA_CHUNK
    nsb = gf_ref.shape[0] // sb
    nch = sb // c
    npair = GLA_HEADS // 2
    pw, vw = 2 * GLA_DK, 2 * GLA_DV
    gdk, gdv = GLA_HEADS * GLA_DK, GLA_HEADS * GLA_DV
    row = lax.broadcasted_iota(jnp.int32, (sb, sb), 0)
    col = lax.broadcasted_iota(jnp.int32, (sb, sb), 1)
    tri = jnp.logical_and(row // c == col // c, col <= row)
    tri_bf = jnp.where(tri, 1.0, 0.0).astype(BF16)
    lane = lax.broadcasted_iota(jnp.int32, (1, pw), 1)
    head_lanes = (lane < GLA_DK, lane >= GLA_DK)
    srow = lax.broadcasted_iota(jnp.int32, (pw, vw), 0)
    scol = lax.broadcasted_iota(jnp.int32, (pw, vw), 1)
    same_head = (srow < GLA_DK) == (scol < GLA_DV)

    def local(i):
        r0 = i * sb
        la = gf_ref[r0:r0 + sb, 2 * gdk:]
        hi = la.astype(BF16)
        lo = (la - hi.astype(F32)).astype(BF16)
        bcum_all = _dot(tri_bf, hi) + _dot(tri_bf, lo)
        out = []
        for p in range(npair):
            bcum = bcum_all[:, p * pw:(p + 1) * pw]
            e_last = [jnp.exp(bcum[j * c + c - 1:j * c + c, :]) for j in range(nch)]
            e_last_rows = jnp.concatenate([jnp.broadcast_to(e, (c, pw)) for e in e_last], axis=0)
            q_t = gf_ref[r0:r0 + sb, p * pw:(p + 1) * pw] * (GLA_DK ** -0.5) * jnp.exp(bcum)
            kk = gf_ref[r0:r0 + sb, gdk + p * pw:gdk + (p + 1) * pw]
            k_undecayed = kk * jnp.exp(-bcum)
            k_t = k_undecayed.astype(BF16)
            k_d = (k_undecayed * e_last_rows).astype(BF16)
            v = gvr_ref[r0:r0 + sb, p * vw:(p + 1) * vw]
            o_intra = []
            for h in range(2):
                qm = jnp.where(head_lanes[h], q_t, 0.0).astype(BF16)
                att = jnp.where(tri, _dot_nt(qm, k_t), 0.0).astype(BF16)
                o_intra.append(_dot(att, v[:, h * GLA_DV:(h + 1) * GLA_DV]))
            d_st, decay = [], []
            for j in range(nch):
                d = _dot_tn(k_d[j * c:(j + 1) * c, :], v[j * c:(j + 1) * c, :])
                d_st.append(jnp.where(same_head, d, 0.0))
                dcol = jnp.broadcast_to(e_last[j], (pw, pw)).T
                decay.append(jnp.concatenate([dcol, dcol], axis=1))
            out.append((q_t.astype(BF16), o_intra, d_st, decay))
        return out

    def recurrent(i, loc, states):
        r0 = i * sb
        for p in range(npair):
            q_bf, o_intra, d_st, decay = loc[p]
            st = states[p]
            o_inter = []
            for j in range(nch):
                o_inter.append(_dot(q_bf[j * c:(j + 1) * c, :], st.astype(BF16)))
                st = decay[j] * st + d_st[j]
            states[p] = st
            o_inter = jnp.concatenate(o_inter, axis=0)
            for h in range(2):
                lo_, hi_ = p * vw + h * GLA_DV, p * vw + (h + 1) * GLA_DV
                o = _rms(o_intra[h] + o_inter[:, h * GLA_DV:(h + 1) * GLA_DV], on_ref[...])
                r = gvr_ref[r0:r0 + sb, gdv + lo_:gdv + hi_].astype(F32)
                o_ref[r0:r0 + sb, lo_:hi_] = (o * _silu(r)).astype(BF16)

    states = [jnp.zeros((pw, vw), F32) for _ in range(npair)]
    loc = local(0)
    for i in range(nsb):
        nxt = local(i + 1) if i + 1 < nsb else None
        recurrent(i, loc, states)
        loc = nxt


def _post_kernel(h1_ref, om_ref, og_ref, ox_ref, nm_ref, wgz32_ref, gb_ref, wom32_ref, wog32_ref, wox32_ref,
                 wout32_ref, n2_ref, wg32_ref, wu32_ref, wd32_ref, nf_ref, o_ref,
                 wgz_ref, wom_ref, wog_ref, wox_ref, wout_ref, wgu_ref, wd_ref, h2_ref, u2_ref):
    i = pl.program_id(0)
    last = pl.num_programs(0) - 1
    refs = (h1_ref, om_ref, og_ref, ox_ref, nm_ref, wgz_ref, gb_ref, wom_ref, wog_ref, wox_ref,
            wout_ref, n2_ref, wgu_ref, wd_ref, nf_ref, o_ref, h2_ref, u2_ref)

    @pl.when(i < W_STEPS)
    def _():
        _convert_ffn(i, wg32_ref, wu32_ref, wd32_ref, wgu_ref, wd_ref)
        _convert_rows(wgz_ref, wgz32_ref[...], i, WIN_POST_ROWS)
        for dst, src in ((wom_ref, wom32_ref), (wog_ref, wog32_ref), (wox_ref, wox32_ref)):
            _convert_rows(dst, src[...], i, dst.shape[0] // W_STEPS)
        _convert_rows(wout_ref, wout32_ref[...], i, D_MODEL // W_STEPS)

    @pl.when(i == W_STEPS)
    def _():
        _post_tokens(*refs, merge=True, ffn=False)

    @pl.when(jnp.logical_and(i > W_STEPS, i < last))
    def _():
        _post_tokens(*refs, merge=True, ffn=True)

    @pl.when(i == last)
    def _():
        _post_tokens(*refs, merge=False, ffn=True)


def _post_tokens(h1_ref, om_ref, og_ref, ox_ref, nm_ref, wgz_ref, gb_ref, wom_ref, wog_ref, wox_ref,
                 wout_ref, n2_ref, wgu_ref, wd_ref, nf_ref, o_ref, h2_ref, u2_ref, *, merge, ffn):
    d = D_MODEL
    if merge:
        h1 = h1_ref[...]
        u = _rms(h1, nm_ref[...]).astype(BF16)
        branches = ((jnp.concatenate([om_ref[p] for p in range(om_ref.shape[0])], axis=1), wom_ref),
                    (og_ref[...], wog_ref), (ox_ref[...], wox_ref))

        def merge_tile(c0):
            acc = None
            for n, (o_in, w_br) in enumerate(branches):
                gate = _sigmoid(_dot_nt(u, wgz_ref[n * d + c0:n * d + c0 + MERGE_TILE, :])
                                + gb_ref[:, n * d + c0:n * d + c0 + MERGE_TILE])
                term = gate * _dot(o_in, w_br[:, c0:c0 + MERGE_TILE])
                acc = term if acc is None else acc + term
            return acc.astype(BF16)

    tiles = []
    if ffn:
        half = 0.5 * _swiglu_half(u2_ref[...], wgu_ref, wd_ref)
    if merge:
        tiles.append(merge_tile(0))
    if ffn:
        o_ref[...] = _rms(h2_ref[...] + half, nf_ref[...])
    if merge:
        for c0 in range(MERGE_TILE, d, MERGE_TILE):
            tiles.append(merge_tile(c0))
        h2 = h1 + _dot(jnp.concatenate(tiles, axis=1), wout_ref[...])
        h2_ref[...] = h2
        u2_ref[...] = _rms(h2, n2_ref[...]).astype(BF16)


def _cparams(sem, vmem=None, flags=None):
    return pltpu.CompilerParams(dimension_semantics=sem, vmem_limit_bytes=vmem, flags=flags)


def kernel(x, mem, positions, ffn1_norm, ffn1_wg, ffn1_wu, ffn1_wd, mix_norm, mem_norm, w_in, gate_bias,
           mla_q_norm, mla_w_uq, mla_kv_norm, mla_w_ukv, mla_w_o, gla_w_a2, gla_b_a, gla_o_norm, gla_w_o,
           x_w_kv, x_w_o, w_out, ffn2_norm, ffn2_wg, ffn2_wu, ffn2_wd, final_norm):
    B, S, D = x.shape
    T = B * S
    M = mem.shape[1]
    F = D_FF
    H = MLA_HEADS
    bf = lambda a: a.astype(BF16)
    row = lambda a: a.reshape(1, -1)

    w_in_t = jnp.swapaxes(w_in[0], 0, 1)
    w_uq = mla_w_uq[0].reshape(MLA_Q_RANK, H, MLA_NOPE + MLA_ROPE)
    w_uq = bf(jnp.pad(w_uq, ((0, 0), (0, 0), (0, HEAD_PAD - MLA_NOPE - MLA_ROPE))).reshape(MLA_Q_RANK, H * HEAD_PAD))
    w_ukv = mla_w_ukv[0].reshape(MLA_KV_RANK, H, MLA_NOPE + MLA_V)
    w_uk = bf(jnp.pad(w_ukv[:, :, :MLA_NOPE], ((0, 0), (0, 0), (0, HEAD_PAD - MLA_NOPE))).reshape(MLA_KV_RANK, H * HEAD_PAD))
    w_uv = bf(w_ukv[:, :, MLA_NOPE:].reshape(MLA_KV_RANK, H * MLA_V))
    w_a2 = bf(gla_w_a2[0])
    half = MLA_ROPE // 2
    inv_freq = ROPE_THETA ** (-jnp.arange(half, dtype=F32) / half)
    invf = jnp.concatenate([jnp.zeros((MLA_NOPE,), F32), inv_freq, inv_freq,
                            jnp.zeros((HEAD_PAD - MLA_NOPE - MLA_ROPE,), F32)]).reshape(1, HEAD_PAD)

    x2 = x.reshape(T, D)
    pos2 = jnp.broadcast_to(positions.reshape(T, 1), (T, HEAD_PAD))
    nt = T // TM
    tok = lambda w: pl.BlockSpec((TM, w), lambda i: (jnp.clip(i - W_STEPS, 0, nt - 1), 0))
    tok_prev = lambda w: pl.BlockSpec((TM, w), lambda i: (jnp.clip(i - W_STEPS - 1, 0, nt - 1), 0))
    tok_pairs_prev = lambda w: pl.BlockSpec(
        (MLA_HEADS // 2, TM, w), lambda i: (0, jnp.clip(i - W_STEPS - 1, 0, nt - 1), 0))
    npair = H // 2
    tok_pairs = lambda w: pl.BlockSpec((npair, TM, w), lambda i: (0, jnp.clip(i - W_STEPS, 0, nt - 1), 0))
    wrows = lambda shape: pl.BlockSpec(
        (shape[0] // W_STEPS, shape[1]), lambda i: (jnp.minimum(i, W_STEPS - 1), 0))
    ffn_specs = [wrows((D, F)), wrows((D, F)), wrows((F, D))]
    ffn_scratch = [pltpu.VMEM((D, 2 * F), BF16), pltpu.VMEM((F, D), BF16)]

    gdk, gdv, xw = GLA_HEADS * GLA_DK, GLA_HEADS * GLA_DV, X_HEADS * X_DH
    pre_out_shapes = (
        jax.ShapeDtypeStruct((T, D), F32),
        jax.ShapeDtypeStruct((npair, T, 2 * HEAD_PAD), BF16),
        jax.ShapeDtypeStruct((npair, T, 2 * HEAD_PAD), BF16),
        jax.ShapeDtypeStruct((npair, T, 2 * MLA_V), BF16),
        jax.ShapeDtypeStruct((T, 3 * gdk), F32),
        jax.ShapeDtypeStruct((T, 2 * gdv), BF16),
        jax.ShapeDtypeStruct((T, xw), BF16),
    )
    h1, q, k, v, gf, gvr, xq = pl.pallas_call(
        _pre_kernel,
        out_shape=pre_out_shapes,
        grid=(W_STEPS + nt + 1,),
        in_specs=[tok(D), tok_prev(HEAD_PAD), _const_spec((1, HEAD_PAD)), _const_spec((1, D)), _const_spec((1, D)),
                  *ffn_specs, pl.BlockSpec((WIN_PRE_ROWS, D), lambda i: (jnp.minimum(i, W_STEPS - 1), 0)),
                  _const_spec((1, MLA_Q_RANK)), _const_spec((MLA_Q_RANK, H * HEAD_PAD)),
                  _const_spec((1, MLA_KV_RANK)), _const_spec((MLA_KV_RANK, H * HEAD_PAD)),
                  _const_spec((MLA_KV_RANK, H * MLA_V)), _const_spec((GLA_GATE_RANK, gdk)), _const_spec((1, gdk))],
        out_specs=(tok(D),) + tuple(tok_prev(s.shape[1]) if len(s.shape) == 2 else tok_pairs_prev(s.shape[2])
                                    for s in pre_out_shapes[1:]),
        scratch_shapes=[*ffn_scratch, pltpu.VMEM((W_STEPS * WIN_PRE_ROWS, D), BF16),
                        pltpu.VMEM((2, TM, D), BF16)],
        compiler_params=_cparams(("arbitrary",), VMEM_LIMIT),
        name="pre",
    )(x2, pos2, invf, row(ffn1_norm[0]), row(mix_norm[0]), ffn1_wg[0], ffn1_wu[0], ffn1_wd[0],
      w_in_t, row(mla_q_norm[0]), w_uq, row(mla_kv_norm[0]), w_uk, w_uv, w_a2, row(gla_b_a[0]))

    seq_pairs = lambda w: pl.BlockSpec((npair, S, w), lambda b: (0, b, 0))
    o_mla = pl.pallas_call(
        _mla_kernel,
        out_shape=jax.ShapeDtypeStruct((npair, T, 2 * MLA_V), BF16),
        grid=(B,),
        in_specs=[seq_pairs(2 * HEAD_PAD), seq_pairs(2 * HEAD_PAD), seq_pairs(2 * MLA_V)],
        out_specs=seq_pairs(2 * MLA_V),
        scratch_shapes=[pltpu.VMEM((MLA_GROUP, 2 * MLA_VT_ROWS, S), BF16)],
        compiler_params=_cparams(("parallel",), VMEM_LIMIT),
        name="mla",
    )(q, k, v)

    nqx = S // TQX
    o_x = pl.pallas_call(
        _xattn_kernel,
        out_shape=jax.ShapeDtypeStruct((T, xw), BF16),
        grid=(B, nqx),
        in_specs=[pl.BlockSpec((TQX, xw), lambda b, i: (b * nqx + i, 0)),
                  pl.BlockSpec((M, D), lambda b, i: (b, 0)),
                  _const_spec((1, D)), _const_spec((D, 2 * xw))],
        out_specs=pl.BlockSpec((TQX, xw), lambda b, i: (b * nqx + i, 0)),
        scratch_shapes=[pltpu.VMEM((M, 2 * xw), BF16)],
        compiler_params=_cparams(("arbitrary", "arbitrary")),
        name="xattn",
    )(xq, mem.reshape(B * M, D), row(mem_norm[0]), bf(x_w_kv[0]))

    o_gla = pl.pallas_call(
        _gla_kernel,
        out_shape=jax.ShapeDtypeStruct((T, gdv), BF16),
        grid=(B,),
        in_specs=[pl.BlockSpec((S, 3 * gdk), lambda b: (b, 0)),
                  pl.BlockSpec((S, 2 * gdv), lambda b: (b, 0)),
                  _const_spec((1, GLA_DV))],
        out_specs=pl.BlockSpec((S, gdv), lambda b: (b, 0)),
        compiler_params=_cparams(("parallel",), VMEM_LIMIT),
        name="gla",
    )(gf, gvr, row(gla_o_norm[0]))

    out = pl.pallas_call(
        _post_kernel,
        out_shape=jax.ShapeDtypeStruct((T, D), F32),
        grid=(W_STEPS + nt + 1,),
        in_specs=[tok(D), tok_pairs(2 * MLA_V), tok(gdv), tok(xw), _const_spec((1, D)),
                  pl.BlockSpec((pl.Element(WIN_POST_ROWS), pl.Element(D)),
                               lambda i: (pl.multiple_of(C_GZ + jnp.minimum(i, W_STEPS - 1) * WIN_POST_ROWS, 16), 0)),
                  _const_spec((1, N_BRANCH * D)),
                  wrows((H * MLA_V, D)), wrows((gdv, D)), wrows((xw, D)),
                  wrows((D, D)), _const_spec((1, D)),
                  *ffn_specs, _const_spec((1, D))],
        out_specs=tok_prev(D),
        scratch_shapes=[pltpu.VMEM((N_BRANCH * D, D), BF16), pltpu.VMEM((H * MLA_V, D), BF16),
                        pltpu.VMEM((gdv, D), BF16), pltpu.VMEM((xw, D), BF16), pltpu.VMEM((D, D), BF16),
                        *ffn_scratch, pltpu.VMEM((TM, D), F32), pltpu.VMEM((TM, D), BF16)],
        compiler_params=_cparams(("arbitrary",), VMEM_LIMIT),
        name="post",
    )(h1, o_mla, o_gla, o_x, row(mix_norm[0]), w_in_t, gate_bias[0].reshape(1, N_BRANCH * D),
      mla_w_o[0], gla_w_o[0], x_w_o[0], w_out[0], row(ffn2_norm[0]),
      ffn2_wg[0], ffn2_wu[0], ffn2_wd[0], row(final_norm))
    return out.reshape(B, S, D)
```

```python
import math

import jax
import jax.numpy as jnp
from jax import lax
from jax.experimental import pallas as pl
from jax.experimental.pallas import tpu as pltpu

F32 = jnp.float32
BF16 = jnp.bfloat16

D_MODEL = 1024
EPS = 1e-6
MLA_HEADS = 8
MLA_NOPE = 64
MLA_ROPE = 32
MLA_V = 64
MLA_Q_RANK = 384
MLA_KV_RANK = 256
ROPE_THETA = 10000.0
GLA_HEADS = 4
GLA_DK = 64
GLA_DV = 128
GLA_GATE_RANK = 16
GLA_TAU = 16.0
GLA_CHUNK = 64
X_HEADS = 4
X_DH = 128
D_FF = 2816
N_BRANCH = 3

LANE = 128
HEAD_PAD = 128
VMEM_LIMIT = 62 * 1024 * 1024
MLA_EXP2_SCALE = math.log2(math.e) / math.sqrt(MLA_NOPE + MLA_ROPE)
MLA_VT_ROWS = MLA_V + 16
MLA_GROUP = 2
MLA_LOOKAHEAD = 3

C_CQ, C_CKV, C_KR, C_GQ, C_GK, C_GV, C_GA, C_GR, C_XQ, C_GZ, C_END = (
    0, 384, 640, 672, 928, 1184, 1696, 1712, 2224, 2736, 5808)
W_STEPS = 8
WIN_PRE_ROWS = -(-C_GZ // (16 * W_STEPS)) * 16
WIN_POST_ROWS = (C_END - C_GZ) // W_STEPS

TM = 512
MERGE_TILE = 256
TQ = 256
TQX = 2048
GLA_SB = 256


def _dot(a, b):
    return jnp.dot(a, b, preferred_element_type=F32)


def _dot_nt(a, b):
    return lax.dot_general(a, b, (((1,), (1,)), ((), ())), preferred_element_type=F32)


def _dot_tn(a, b):
    return lax.dot_general(a, b, (((0,), (0,)), ((), ())), preferred_element_type=F32)


def _rms(x, g):
    return x * lax.rsqrt(jnp.mean(x * x, axis=-1, keepdims=True) + EPS) * g


def _sigmoid(x):
    return 1.0 / (1.0 + jnp.exp(-x))


def _silu(x):
    return x * _sigmoid(x)


def _const_spec(shape):
    return pl.BlockSpec(shape, lambda *_: (0,) * len(shape), pipeline_mode=pl.Buffered(1))


def _swiglu_half(xn_bf16, wgu_ref, wd_ref):
    tiles = []
    for f0 in range(0, 2 * D_FF, 2 * LANE):
        gu = _dot(xn_bf16, wgu_ref[:, f0:f0 + 2 * LANE])
        tiles.append((_silu(gu[:, :LANE]) * gu[:, LANE:]).astype(BF16))
    return _dot(jnp.concatenate(tiles, axis=1), wd_ref[...])


def _convert_rows(dst_ref, src, step, rows):
    r0 = pl.multiple_of(step * rows, rows)
    dst_ref[pl.ds(r0, rows), :] = src.astype(BF16)


def _convert_ffn(i, wg32_ref, wu32_ref, wd32_ref, wgu_ref, wd_ref):
    rows = D_MODEL // W_STEPS
    r0 = pl.multiple_of(i * rows, rows)
    wg = wg32_ref[...].astype(BF16)
    wu = wu32_ref[...].astype(BF16)
    for j in range(D_FF // LANE):
        wgu_ref[pl.ds(r0, rows), 2 * j * LANE:(2 * j + 1) * LANE] = wg[:, j * LANE:(j + 1) * LANE]
        wgu_ref[pl.ds(r0, rows), (2 * j + 1) * LANE:(2 * j + 2) * LANE] = wu[:, j * LANE:(j + 1) * LANE]
    _convert_rows(wd_ref, wd32_ref[...], i, D_FF // W_STEPS)


def _pre_kernel(x_ref, pos_ref, invf_ref, n1_ref, nm_ref, wg32_ref, wu32_ref, wd32_ref, win32_ref,
                qn_ref, wuq_ref, kvn_ref, wuk_ref, wuv_ref, wa2_ref, ba_ref,
                h1_ref, q_ref, k_ref, v_ref, gf_ref, gvr_ref, xq_ref,
                wgu_ref, wd_ref, win_ref, u2_ref):
    i = pl.program_id(0)
    last = pl.num_programs(0) - 1
    slot = lax.rem(i, 2)
    refs = (x_ref, pos_ref, invf_ref, n1_ref, nm_ref, wgu_ref, wd_ref, win_ref,
            qn_ref, wuq_ref, kvn_ref, wuk_ref, wuv_ref, wa2_ref, ba_ref,
            h1_ref, q_ref, k_ref, v_ref, gf_ref, gvr_ref, xq_ref, u2_ref.at[1 - slot], u2_ref.at[slot])

    @pl.when(i < W_STEPS)
    def _():
        _convert_ffn(i, wg32_ref, wu32_ref, wd32_ref, wgu_ref, wd_ref)
        _convert_rows(win_ref, win32_ref[...], i, WIN_PRE_ROWS)

    @pl.when(i == W_STEPS)
    def _():
        _pre_tokens(*refs, ffn=True, proj=False)

    @pl.when(jnp.logical_and(i > W_STEPS, i < last))
    def _():
        _pre_tokens(*refs, ffn=True, proj=True)

    @pl.when(i == last)
    def _():
        _pre_tokens(*refs, ffn=False, proj=True)


def _pre_tokens(x_ref, pos_ref, invf_ref, n1_ref, nm_ref, wgu_ref, wd_ref, win_ref,
                qn_ref, wuq_ref, kvn_ref, wuk_ref, wuv_ref, wa2_ref, ba_ref,
                h1_ref, q_ref, k_ref, v_ref, gf_ref, gvr_ref, xq_ref, u2p_ref, u2_ref, *, ffn, proj):
    npair = MLA_HEADS // 2
    gdk, gdv = GLA_HEADS * GLA_DK, GLA_HEADS * GLA_DV

    if proj:
        lat = _dot_nt(u2p_ref[...], win_ref[C_CQ:C_GQ, :])
        cq_raw, ckv_raw, kr_raw = lat[:, C_CQ:C_CKV], lat[:, C_CKV:C_KR], lat[:, C_KR:C_GQ]
        ga = _dot_nt(u2p_ref[...], win_ref[C_GA:C_GR, :]).astype(BF16)
        gf_ref[:, :gdk] = _dot_nt(u2p_ref[...], win_ref[C_GQ:C_GK, :])
        gf_ref[:, gdk:2 * gdk] = _dot_nt(u2p_ref[...], win_ref[C_GK:C_GV, :])

    if ffn:
        x = x_ref[...]
        u1 = _rms(x, n1_ref[...]).astype(BF16)

    if proj:
        ang = pos_ref[...].astype(F32) * invf_ref[...]
        cos = jnp.cos(ang)
        sin = jnp.sin(ang)
        lane = lax.broadcasted_iota(jnp.int32, ang.shape, 1)
        sin_hi = jnp.where(lane >= 80, sin, 0.0)
        sin_lo = jnp.where(lane < 80, -sin, 0.0)

        def rope(t):
            return t * cos + pltpu.roll(t, 16, 1) * sin_hi + pltpu.roll(t, LANE - 16, 1) * sin_lo

        cqn = _rms(cq_raw, qn_ref[...]).astype(BF16)
        ckvn = _rms(ckv_raw, kvn_ref[...]).astype(BF16)
        kr = rope(jnp.concatenate([jnp.zeros((kr_raw.shape[0], MLA_NOPE), F32), kr_raw,
                                   jnp.zeros((kr_raw.shape[0], HEAD_PAD - MLA_NOPE - MLA_ROPE), F32)], axis=1))
        kr_bf = kr.astype(BF16)
        nope_lanes = lax.broadcasted_iota(jnp.int32, kr.shape, 1) < MLA_NOPE

        def q_pair(p):
            qp = _dot(cqn, wuq_ref[:, 2 * p * HEAD_PAD:(2 * p + 2) * HEAD_PAD])
            for j in range(2):
                q_ref[p, :, j * HEAD_PAD:(j + 1) * HEAD_PAD] = (
                    rope(qp[:, j * HEAD_PAD:(j + 1) * HEAD_PAD]) * MLA_EXP2_SCALE).astype(BF16)

    if ffn:
        h1 = x + 0.5 * _swiglu_half(u1, wgu_ref, wd_ref)
        h1_ref[...] = h1

    if proj:
        for p in range(npair // 2):
            q_pair(p)
        gvr_ref[:, :gdv] = _dot_nt(u2p_ref[...], win_ref[C_GV:C_GA, :]).astype(BF16)
        for p in range(npair // 2, npair):
            q_pair(p)
        kn = _dot(ckvn, wuk_ref[...])
        for p in range(npair):
            kp = kn[:, p * LANE:(p + 1) * LANE]
            k_ref[p, :, :HEAD_PAD] = jnp.where(nope_lanes, kp.astype(BF16), kr_bf)
            k_ref[p, :, HEAD_PAD:] = jnp.where(nope_lanes, pltpu.roll(kp, MLA_NOPE, 1).astype(BF16), kr_bf)
        vv = _dot(ckvn, wuv_ref[...]).astype(BF16)
        for p in range(npair):
            v_ref[p] = vv[:, 2 * p * MLA_V:(2 * p + 2) * MLA_V]
        t = _dot(ga, wa2_ref[...]) + ba_ref[...]
        log_sig = jnp.minimum(t, 0.0) - jnp.log(1.0 + jnp.exp(-jnp.abs(t)))
        gf_ref[:, 2 * gdk:] = log_sig * (1.0 / GLA_TAU)

    if ffn:
        u2_ref[...] = _rms(h1, nm_ref[...]).astype(BF16)

    if proj:
        gvr_ref[:, gdv:] = _dot_nt(u2p_ref[...], win_ref[C_GR:C_XQ, :]).astype(BF16)
        xq_ref[...] = _dot_nt(u2p_ref[...], win_ref[C_XQ:C_GZ, :]).astype(BF16)


def _mla_kernel(q_ref, k_ref, v_ref, o_ref, vt_ref):
    def group(g, carry):
        sl = pl.ds(g * MLA_GROUP, MLA_GROUP)
        _mla_pairs(q_ref.at[sl], k_ref.at[sl], v_ref.at[sl], o_ref.at[sl], vt_ref)
        return carry

    lax.fori_loop(0, q_ref.shape[0] // MLA_GROUP, group, 0)


def _mla_pairs(q_ref, k_ref, v_ref, o_ref, vt_ref):
    npair = q_ref.shape[0]
    tq = TQ
    nq = q_ref.shape[1] // tq
    key = lax.broadcasted_iota(jnp.int32, (tq, tq), 0)
    qry = lax.broadcasted_iota(jnp.int32, (tq, tq), 1)
    causal = key <= qry
    ones_row = jnp.where(lax.broadcasted_iota(jnp.int32, (MLA_VT_ROWS - MLA_V, q_ref.shape[1]), 0) == 0,
                         1.0, 0.0).astype(BF16)

    def prepare(p):
        vt = v_ref[p].astype(F32).T.astype(BF16)
        for h in range(2):
            vt_ref[p, h * MLA_VT_ROWS:h * MLA_VT_ROWS + MLA_V, :] = vt[h * MLA_V:(h + 1) * MLA_V, :]
            vt_ref[p, h * MLA_VT_ROWS + MLA_V:(h + 1) * MLA_VT_ROWS, :] = ones_row

    items = [(p, qi, c) for p in range(npair) for qi in range(nq) for c in range(qi + 1)]
    pending = [{}, {}]
    state = [{}, {}]
    done = {}

    def score(h, i):
        p, qi, c = items[i]
        q = q_ref[p, qi * tq:(qi + 1) * tq, h * HEAD_PAD:(h + 1) * HEAD_PAD]
        s = _dot_nt(k_ref[p, c * tq:(c + 1) * tq, h * HEAD_PAD:(h + 1) * HEAD_PAD], q)
        pending[h][i] = jnp.where(causal, s, -1e30) if c == qi else s

    def consume(h, i):
        p, qi, c = items[i]
        st = state[h]
        s = pending[h].pop(i)
        cm = jnp.max(s, axis=0, keepdims=True)
        m_new = cm if c == 0 else jnp.maximum(st["m"], cm)
        pr = jnp.exp2(s - m_new).astype(BF16)
        pv = _dot(vt_ref[p, h * MLA_VT_ROWS:(h + 1) * MLA_VT_ROWS, c * tq:(c + 1) * tq], pr)
        st["acc"] = pv if c == 0 else st["acc"] * jnp.exp2(st["m"] - m_new) + pv
        st["m"] = m_new
        if c == qi:
            done[(p, qi, h)] = st["acc"][:MLA_V, :] / st["acc"][MLA_V:MLA_V + 1, :]
            if (p, qi, 1 - h) in done:
                o_ref[p, qi * tq:(qi + 1) * tq, :] = jnp.concatenate(
                    [done.pop((p, qi, 0)), done.pop((p, qi, 1))], axis=0).T.astype(BF16)

    for t in range(len(items) + MLA_LOOKAHEAD):
        if t < len(items) and items[t][1:] == (0, 0):
            prepare(items[t][0])
        for h in range(2):
            if t < len(items):
                score(h, t)
            if t >= MLA_LOOKAHEAD:
                consume(h, t - MLA_LOOKAHEAD)


def _xattn_kernel(q_ref, mem_ref, n_ref, w_ref, o_ref, kv_ref):
    @pl.when(pl.program_id(1) == 0)
    def _():
        mn = _rms(mem_ref[...], n_ref[...]).astype(BF16)
        kv_ref[...] = _dot(mn, w_ref[...]).astype(BF16)

    scale = 1.0 / math.sqrt(X_DH)
    hw = X_HEADS * X_DH
    for h in range(X_HEADS):
        q = q_ref[:, h * X_DH:(h + 1) * X_DH]
        k = kv_ref[:, h * X_DH:(h + 1) * X_DH]
        v = kv_ref[:, hw + h * X_DH:hw + (h + 1) * X_DH]
        s = _dot_nt(q, k) * scale
        m = jnp.max(s, axis=-1, keepdims=True)
        p = jnp.exp(s - m)
        l = jnp.sum(p, axis=-1, keepdims=True)
        o = _dot(p.astype(BF16), v) / l
        o_ref[:, h * X_DH:(h + 1) * X_DH] = o.astype(BF16)


def _gla_kernel(gf_ref, gvr_ref, on_ref, o_ref):
    sb, c = GLA_SB, GLA_CHUNK
    nsb = gf_ref.shape[0] // sb
    nch = sb // c
    npair = GLA_HEADS // 2
    pw, vw = 2 * GLA_DK, 2 * GLA_DV
    gdk, gdv = GLA_HEADS * GLA_DK, GLA_HEADS * GLA_DV
    row = lax.broadcasted_iota(jnp.int32, (sb, sb), 0)
    col = lax.broadcasted_iota(jnp.int32, (sb, sb), 1)
    tri = jnp.logical_and(row // c == col // c, col <= row)
    tri_bf = jnp.where(tri, 1.0, 0.0).astype(BF16)
    lane = lax.broadcasted_iota(jnp.int32, (1, pw), 1)
    head_lanes = (lane < GLA_DK, lane >= GLA_DK)
    srow = lax.broadcasted_iota(jnp.int32, (pw, vw), 0)
    scol = lax.broadcasted_iota(jnp.int32, (pw, vw), 1)
    same_head = (srow < GLA_DK) == (scol < GLA_DV)

    def local(i):
        r0 = i * sb
        la = gf_ref[r0:r0 + sb, 2 * gdk:]
        hi = la.astype(BF16)
        lo = (la - hi.astype(F32)).astype(BF16)
        bcum_all = _dot(tri_bf, hi) + _dot(tri_bf, lo)
        out = []
        for p in range(npair):
            bcum = bcum_all[:, p * pw:(p + 1) * pw]
            e_last = [jnp.exp(bcum[j * c + c - 1:j * c + c, :]) for j in range(nch)]
            e_last_rows = jnp.concatenate([jnp.broadcast_to(e, (c, pw)) for e in e_last], axis=0)
            q_t = gf_ref[r0:r0 + sb, p * pw:(p + 1) * pw] * (GLA_DK ** -0.5) * jnp.exp(bcum)
            kk = gf_ref[r0:r0 + sb, gdk + p * pw:gdk + (p + 1) * pw]
            k_undecayed = kk * jnp.exp(-bcum)
            k_t = k_undecayed.astype(BF16)
            k_d = (k_undecayed * e_last_rows).astype(BF16)
            v = gvr_ref[r0:r0 + sb, p * vw:(p + 1) * vw]
            o_intra = []
            for h in range(2):
                qm = jnp.where(head_lanes[h], q_t, 0.0).astype(BF16)
                att = jnp.where(tri, _dot_nt(qm, k_t), 0.0).astype(BF16)
                o_intra.append(_dot(att, v[:, h * GLA_DV:(h + 1) * GLA_DV]))
            d_st, decay = [], []
            for j in range(nch):
                d = _dot_tn(k_d[j * c:(j + 1) * c, :], v[j * c:(j + 1) * c, :])
                d_st.append(jnp.where(same_head, d, 0.0))
                dcol = jnp.broadcast_to(e_last[j], (pw, pw)).T
                decay.append(jnp.concatenate([dcol, dcol], axis=1))
            out.append((q_t.astype(BF16), o_intra, d_st, decay))
        return out

    def recurrent(i, loc, states):
        r0 = i * sb
        for p in range(npair):
            q_bf, o_intra, d_st, decay = loc[p]
            st = states[p]
            o_inter = []
            for j in range(nch):
                o_inter.append(_dot(q_bf[j * c:(j + 1) * c, :], st.astype(BF16)))
                st = decay[j] * st + d_st[j]
            states[p] = st
            o_inter = jnp.concatenate(o_inter, axis=0)
            for h in range(2):
                lo_, hi_ = p * vw + h * GLA_DV, p * vw + (h + 1) * GLA_DV
                o = _rms(o_intra[h] + o_inter[:, h * GLA_DV:(h + 1) * GLA_DV], on_ref[...])
                r = gvr_ref[r0:r0 + sb, gdv + lo_:gdv + hi_].astype(F32)
                o_ref[r0:r0 + sb, lo_:hi_] = (o * _silu(r)).astype(BF16)

    states = [jnp.zeros((pw, vw), F32) for _ in range(npair)]
    loc = local(0)
    for i in range(nsb):
        nxt = local(i + 1) if i + 1 < nsb else None
        recurrent(i, loc, states)
        loc = nxt


def _post_kernel(h1_ref, om_ref, og_ref, ox_ref, nm_ref, wgz32_ref, gb_ref, wom32_ref, wog32_ref, wox32_ref,
                 wout32_ref, n2_ref, wg32_ref, wu32_ref, wd32_ref, nf_ref, o_ref,
                 wgz_ref, wom_ref, wog_ref, wox_ref, wout_ref, wgu_ref, wd_ref, h2_ref, u2_ref):
    i = pl.program_id(0)
    last = pl.num_programs(0) - 1
    refs = (h1_ref, om_ref, og_ref, ox_ref, nm_ref, wgz_ref, gb_ref, wom_ref, wog_ref, wox_ref,
            wout_ref, n2_ref, wgu_ref, wd_ref, nf_ref, o_ref, h2_ref, u2_ref)

    @pl.when(i < W_STEPS)
    def _():
        _convert_ffn(i, wg32_ref, wu32_ref, wd32_ref, wgu_ref, wd_ref)
        _convert_rows(wgz_ref, wgz32_ref[...], i, WIN_POST_ROWS)
        for dst, src in ((wom_ref, wom32_ref), (wog_ref, wog32_ref), (wox_ref, wox32_ref)):
            _convert_rows(dst, src[...], i, dst.shape[0] // W_STEPS)
        _convert_rows(wout_ref, wout32_ref[...], i, D_MODEL // W_STEPS)

    @pl.when(i == W_STEPS)
    def _():
        _post_tokens(*refs, merge=True, ffn=False)

    @pl.when(jnp.logical_and(i > W_STEPS, i < last))
    def _():
        _post_tokens(*refs, merge=True, ffn=True)

    @pl.when(i == last)
    def _():
        _post_tokens(*refs, merge=False, ffn=True)


def _post_tokens(h1_ref, om_ref, og_ref, ox_ref, nm_ref, wgz_ref, gb_ref, wom_ref, wog_ref, wox_ref,
                 wout_ref, n2_ref, wgu_ref, wd_ref, nf_ref, o_ref, h2_ref, u2_ref, *, merge, ffn):
    d = D_MODEL
    if merge:
        h1 = h1_ref[...]
        u = _rms(h1, nm_ref[...]).astype(BF16)
        branches = ((jnp.concatenate([om_ref[p] for p in range(om_ref.shape[0])], axis=1), wom_ref),
                    (og_ref[...], wog_ref), (ox_ref[...], wox_ref))

        def merge_tile(c0):
            acc = None
            for n, (o_in, w_br) in enumerate(branches):
                gate = _sigmoid(_dot_nt(u, wgz_ref[n * d + c0:n * d + c0 + MERGE_TILE, :])
                                + gb_ref[:, n * d + c0:n * d + c0 + MERGE_TILE])
                term = gate * _dot(o_in, w_br[:, c0:c0 + MERGE_TILE])
                acc = term if acc is None else acc + term
            return acc.astype(BF16)

    tiles = []
    if ffn:
        half = 0.5 * _swiglu_half(u2_ref[...], wgu_ref, wd_ref)
    if merge:
        tiles.append(merge_tile(0))
    if ffn:
        o_ref[...] = _rms(h2_ref[...] + half, nf_ref[...])
    if merge:
        for c0 in range(MERGE_TILE, d, MERGE_TILE):
            tiles.append(merge_tile(c0))
        h2 = h1 + _dot(jnp.concatenate(tiles, axis=1), wout_ref[...])
        h2_ref[...] = h2
        u2_ref[...] = _rms(h2, n2_ref[...]).astype(BF16)


def _cparams(sem, vmem=None, flags=None):
    return pltpu.CompilerParams(dimension_semantics=sem, vmem_limit_bytes=vmem, flags=flags)


def kernel(x, mem, positions, ffn1_norm, ffn1_wg, ffn1_wu, ffn1_wd, mix_norm, mem_norm, w_in, gate_bias,
           mla_q_norm, mla_w_uq, mla_kv_norm, mla_w_ukv, mla_w_o, gla_w_a2, gla_b_a, gla_o_norm, gla_w_o,
           x_w_kv, x_w_o, w_out, ffn2_norm, ffn2_wg, ffn2_wu, ffn2_wd, final_norm):
    B, S, D = x.shape
    T = B * S
    M = mem.shape[1]
    F = D_FF
    H = MLA_HEADS
    bf = lambda a: a.astype(BF16)
    row = lambda a: a.reshape(1, -1)

    w_in_t = jnp.swapaxes(w_in[0], 0, 1)
    w_uq = mla_w_uq[0].reshape(MLA_Q_RANK, H, MLA_NOPE + MLA_ROPE)
    w_uq = bf(jnp.pad(w_uq, ((0, 0), (0, 0), (0, HEAD_PAD - MLA_NOPE - MLA_ROPE))).reshape(MLA_Q_RANK, H * HEAD_PAD))
    w_ukv = mla_w_ukv[0].reshape(MLA_KV_RANK, H, MLA_NOPE + MLA_V)
    w_uk = bf(w_ukv[:, :, :MLA_NOPE].reshape(MLA_KV_RANK, H * MLA_NOPE))
    w_uv = bf(w_ukv[:, :, MLA_NOPE:].reshape(MLA_KV_RANK, H * MLA_V))
    w_a2 = bf(gla_w_a2[0])
    half = MLA_ROPE // 2
    inv_freq = ROPE_THETA ** (-jnp.arange(half, dtype=F32) / half)
    invf = jnp.concatenate([jnp.zeros((MLA_NOPE,), F32), inv_freq, inv_freq,
                            jnp.zeros((HEAD_PAD - MLA_NOPE - MLA_ROPE,), F32)]).reshape(1, HEAD_PAD)

    x2 = x.reshape(T, D)
    pos2 = jnp.broadcast_to(positions.reshape(T, 1), (T, HEAD_PAD))
    nt = T // TM
    tok = lambda w: pl.BlockSpec((TM, w), lambda i: (jnp.clip(i - W_STEPS, 0, nt - 1), 0))
    tok_prev = lambda w: pl.BlockSpec((TM, w), lambda i: (jnp.clip(i - W_STEPS - 1, 0, nt - 1), 0))
    tok_pairs_prev = lambda w: pl.BlockSpec(
        (MLA_HEADS // 2, TM, w), lambda i: (0, jnp.clip(i - W_STEPS - 1, 0, nt - 1), 0))
    npair = H // 2
    tok_pairs = lambda w: pl.BlockSpec((npair, TM, w), lambda i: (0, jnp.clip(i - W_STEPS, 0, nt - 1), 0))
    wrows = lambda shape: pl.BlockSpec(
        (shape[0] // W_STEPS, shape[1]), lambda i: (jnp.minimum(i, W_STEPS - 1), 0))
    ffn_specs = [wrows((D, F)), wrows((D, F)), wrows((F, D))]
    ffn_scratch = [pltpu.VMEM((D, 2 * F), BF16), pltpu.VMEM((F, D), BF16)]

    gdk, gdv, xw = GLA_HEADS * GLA_DK, GLA_HEADS * GLA_DV, X_HEADS * X_DH
    pre_out_shapes = (
        jax.ShapeDtypeStruct((T, D), F32),
        jax.ShapeDtypeStruct((npair, T, 2 * HEAD_PAD), BF16),
        jax.ShapeDtypeStruct((npair, T, 2 * HEAD_PAD), BF16),
        jax.ShapeDtypeStruct((npair, T, 2 * MLA_V), BF16),
        jax.ShapeDtypeStruct((T, 3 * gdk), F32),
        jax.ShapeDtypeStruct((T, 2 * gdv), BF16),
        jax.ShapeDtypeStruct((T, xw), BF16),
    )
    h1, q, k, v, gf, gvr, xq = pl.pallas_call(
        _pre_kernel,
        out_shape=pre_out_shapes,
        grid=(W_STEPS + nt + 1,),
        in_specs=[tok(D), tok_prev(HEAD_PAD), _const_spec((1, HEAD_PAD)), _const_spec((1, D)), _const_spec((1, D)),
                  *ffn_specs, pl.BlockSpec((WIN_PRE_ROWS, D), lambda i: (jnp.minimum(i, W_STEPS - 1), 0)),
                  _const_spec((1, MLA_Q_RANK)), _const_spec((MLA_Q_RANK, H * HEAD_PAD)),
                  _const_spec((1, MLA_KV_RANK)), _const_spec((MLA_KV_RANK, H * MLA_NOPE)),
                  _const_spec((MLA_KV_RANK, H * MLA_V)), _const_spec((GLA_GATE_RANK, gdk)), _const_spec((1, gdk))],
        out_specs=(tok(D),) + tuple(tok_prev(s.shape[1]) if len(s.shape) == 2 else tok_pairs_prev(s.shape[2])
                                    for s in pre_out_shapes[1:]),
        scratch_shapes=[*ffn_scratch, pltpu.VMEM((W_STEPS * WIN_PRE_ROWS, D), BF16),
                        pltpu.VMEM((2, TM, D), BF16)],
        compiler_params=_cparams(("arbitrary",), VMEM_LIMIT),
        name="pre",
    )(x2, pos2, invf, row(ffn1_norm[0]), row(mix_norm[0]), ffn1_wg[0], ffn1_wu[0], ffn1_wd[0],
      w_in_t, row(mla_q_norm[0]), w_uq, row(mla_kv_norm[0]), w_uk, w_uv, w_a2, row(gla_b_a[0]))

    seq_pairs = lambda w: pl.BlockSpec((npair, S, w), lambda b: (0, b, 0))
    o_mla = pl.pallas_call(
        _mla_kernel,
        out_shape=jax.ShapeDtypeStruct((npair, T, 2 * MLA_V), BF16),
        grid=(B,),
        in_specs=[seq_pairs(2 * HEAD_PAD), seq_pairs(2 * HEAD_PAD), seq_pairs(2 * MLA_V)],
        out_specs=seq_pairs(2 * MLA_V),
        scratch_shapes=[pltpu.VMEM((MLA_GROUP, 2 * MLA_VT_ROWS, S), BF16)],
        compiler_params=_cparams(("parallel",), VMEM_LIMIT),
        name="mla",
    )(q, k, v)

    nqx = S // TQX
    o_x = pl.pallas_call(
        _xattn_kernel,
        out_shape=jax.ShapeDtypeStruct((T, xw), BF16),
        grid=(B, nqx),
        in_specs=[pl.BlockSpec((TQX, xw), lambda b, i: (b * nqx + i, 0)),
                  pl.BlockSpec((M, D), lambda b, i: (b, 0)),
                  _const_spec((1, D)), _const_spec((D, 2 * xw))],
        out_specs=pl.BlockSpec((TQX, xw), lambda b, i: (b * nqx + i, 0)),
        scratch_shapes=[pltpu.VMEM((M, 2 * xw), BF16)],
        compiler_params=_cparams(("arbitrary", "arbitrary")),
        name="xattn",
    )(xq, mem.reshape(B * M, D), row(mem_norm[0]), bf(x_w_kv[0]))

    o_gla = pl.pallas_call(
        _gla_kernel,
        out_shape=jax.ShapeDtypeStruct((T, gdv), BF16),
        grid=(B,),
        in_specs=[pl.BlockSpec((S, 3 * gdk), lambda b: (b, 0)),
                  pl.BlockSpec((S, 2 * gdv), lambda b: (b, 0)),
                  _const_spec((1, GLA_DV))],
        out_specs=pl.BlockSpec((S, gdv), lambda b: (b, 0)),
        compiler_params=_cparams(("parallel",), VMEM_LIMIT),
        name="gla",
    )(gf, gvr, row(gla_o_norm[0]))

    out = pl.pallas_call(
        _post_kernel,
        out_shape=jax.ShapeDtypeStruct((T, D), F32),
        grid=(W_STEPS + nt + 1,),
        in_specs=[tok(D), tok_pairs(2 * MLA_V), tok(gdv), tok(xw), _const_spec((1, D)),
                  pl.BlockSpec((pl.Element(WIN_POST_ROWS), pl.Element(D)),
                               lambda i: (pl.multiple_of(C_GZ + jnp.minimum(i, W_STEPS - 1) * WIN_POST_ROWS, 16), 0)),
                  _const_spec((1, N_BRANCH * D)),
                  wrows((H * MLA_V, D)), wrows((gdv, D)), wrows((xw, D)),
                  wrows((D, D)), _const_spec((1, D)),
                  *ffn_specs, _const_spec((1, D))],
        out_specs=tok_prev(D),
        scratch_shapes=[pltpu.VMEM((N_BRANCH * D, D), BF16), pltpu.VMEM((H * MLA_V, D), BF16),
                        pltpu.VMEM((gdv, D), BF16), pltpu.VMEM((xw, D), BF16), pltpu.VMEM((D, D), BF16),
                        *ffn_scratch, pltpu.VMEM((TM, D), F32), pltpu.VMEM((TM, D), BF16)],
        compiler_params=_cparams(("arbitrary",), VMEM_LIMIT),
        name="post",
    )(h1, o_mla, o_gla, o_x, row(mix_norm[0]), w_in_t, gate_bias[0].reshape(1, N_BRANCH * D),
      mla_w_o[0], gla_w_o[0], x_w_o[0], w_out[0], row(ffn2_norm[0]),
      ffn2_wg[0], ffn2_wu[0], ffn2_wd[0], row(final_norm))
    return out.reshape(B, S, D)
```

```python
import math

import jax
import jax.numpy as jnp
from jax import lax
from jax.experimental import pallas as pl
from jax.experimental.pallas import tpu as pltpu

F32 = jnp.float32
BF16 = jnp.bfloat16

D_MODEL = 1024
EPS = 1e-6
MLA_HEADS = 8
MLA_NOPE = 64
MLA_ROPE = 32
MLA_V = 64
MLA_Q_RANK = 384
MLA_KV_RANK = 256
ROPE_THETA = 10000.0
GLA_HEADS = 4
GLA_DK = 64
GLA_DV = 128
GLA_GATE_RANK = 16
GLA_TAU = 16.0
GLA_CHUNK = 64
X_HEADS = 4
X_DH = 128
D_FF = 2816
N_BRANCH = 3

LANE = 128
HEAD_PAD = 128
VMEM_LIMIT = 63 * 1024 * 1024
MLA_EXP2_SCALE = math.log2(math.e) / math.sqrt(MLA_NOPE + MLA_ROPE)
MLA_VT_ROWS = MLA_V + 16
MLA_GROUP = 2
MLA_LOOKAHEAD = 3

C_CQ, C_CKV, C_KR, C_GQ, C_GK, C_GV, C_GA, C_GR, C_XQ, C_GZ, C_END = (
    0, 384, 640, 672, 928, 1184, 1696, 1712, 2224, 2736, 5808)
W_STEPS = 8
WIN_PRE_ROWS = -(-C_GZ // (16 * W_STEPS)) * 16
WIN_POST_ROWS = (C_END - C_GZ) // W_STEPS

TM = 512
MERGE_TILE = 256
TQ = 256
TQX = 2048
GLA_SB = 256


def _dot(a, b):
    return jnp.dot(a, b, preferred_element_type=F32)


def _dot_nt(a, b):
    return lax.dot_general(a, b, (((1,), (1,)), ((), ())), preferred_element_type=F32)


def _dot_tn(a, b):
    return lax.dot_general(a, b, (((0,), (0,)), ((), ())), preferred_element_type=F32)


def _rms(x, g):
    return x * lax.rsqrt(jnp.mean(x * x, axis=-1, keepdims=True) + EPS) * g


def _sigmoid(x):
    return 1.0 / (1.0 + jnp.exp(-x))


def _silu(x):
    return x * _sigmoid(x)


def _const_spec(shape):
    return pl.BlockSpec(shape, lambda *_: (0,) * len(shape), pipeline_mode=pl.Buffered(1))


def _swiglu_half(xn_bf16, wgu_ref, wd_ref):
    tiles = []
    for f0 in range(0, 2 * D_FF, 2 * LANE):
        gu = _dot(xn_bf16, wgu_ref[:, f0:f0 + 2 * LANE])
        tiles.append((_silu(gu[:, :LANE]) * gu[:, LANE:]).astype(BF16))
    return _dot(jnp.concatenate(tiles, axis=1), wd_ref[...])


def _convert_rows(dst_ref, src, step, rows):
    r0 = pl.multiple_of(step * rows, rows)
    dst_ref[pl.ds(r0, rows), :] = src.astype(BF16)


def _convert_ffn(i, wg32_ref, wu32_ref, wd32_ref, wgu_ref, wd_ref):
    rows = D_MODEL // W_STEPS
    r0 = pl.multiple_of(i * rows, rows)
    wg = wg32_ref[...].astype(BF16)
    wu = wu32_ref[...].astype(BF16)
    for j in range(D_FF // LANE):
        wgu_ref[pl.ds(r0, rows), 2 * j * LANE:(2 * j + 1) * LANE] = wg[:, j * LANE:(j + 1) * LANE]
        wgu_ref[pl.ds(r0, rows), (2 * j + 1) * LANE:(2 * j + 2) * LANE] = wu[:, j * LANE:(j + 1) * LANE]
    _convert_rows(wd_ref, wd32_ref[...], i, D_FF // W_STEPS)


def _pre_kernel(x_ref, pos_ref, invf_ref, n1_ref, nm_ref, wg32_ref, wu32_ref, wd32_ref, win32_ref,
                qn_ref, wuq_ref, kvn_ref, wuk_ref, wuv_ref, wa2_ref, ba_ref,
                h1_ref, q_ref, k_ref, v_ref, gf_ref, gvr_ref, xq_ref,
                wgu_ref, wd_ref, win_ref, u2_ref):
    i = pl.program_id(0)
    last = pl.num_programs(0) - 1
    slot = lax.rem(i, 2)
    refs = (x_ref, pos_ref, invf_ref, n1_ref, nm_ref, wgu_ref, wd_ref, win_ref,
            qn_ref, wuq_ref, kvn_ref, wuk_ref, wuv_ref, wa2_ref, ba_ref,
            h1_ref, q_ref, k_ref, v_ref, gf_ref, gvr_ref, xq_ref, u2_ref.at[1 - slot], u2_ref.at[slot])

    @pl.when(i < W_STEPS)
    def _():
        _convert_ffn(i, wg32_ref, wu32_ref, wd32_ref, wgu_ref, wd_ref)
        _convert_rows(win_ref, win32_ref[...], i, WIN_PRE_ROWS)

        @pl.when(i == W_STEPS - 1)
        def _():
            ga_rows = win_ref[C_GA:C_GR, :]
            for r0 in range(C_GA - 64, C_GQ - 1, -64):
                win_ref[r0 + GLA_GATE_RANK:r0 + 64 + GLA_GATE_RANK, :] = win_ref[r0:r0 + 64, :]
            win_ref[C_GQ:C_GQ + GLA_GATE_RANK, :] = ga_rows

    @pl.when(i == W_STEPS)
    def _():
        _pre_tokens(*refs, ffn=True, proj=False)

    @pl.when(jnp.logical_and(i > W_STEPS, i < last))
    def _():
        _pre_tokens(*refs, ffn=True, proj=True)

    @pl.when(i == last)
    def _():
        _pre_tokens(*refs, ffn=False, proj=True)


def _pre_tokens(x_ref, pos_ref, invf_ref, n1_ref, nm_ref, wgu_ref, wd_ref, win_ref,
                qn_ref, wuq_ref, kvn_ref, wuk_ref, wuv_ref, wa2_ref, ba_ref,
                h1_ref, q_ref, k_ref, v_ref, gf_ref, gvr_ref, xq_ref, u2p_ref, u2_ref, *, ffn, proj):
    npair = MLA_HEADS // 2
    gdk, gdv = GLA_HEADS * GLA_DK, GLA_HEADS * GLA_DV

    if proj:
        sh = GLA_GATE_RANK
        lat = _dot_nt(u2p_ref[...], win_ref[C_CQ:C_GQ + sh, :])
        cq_raw, ckv_raw, kr_raw = lat[:, C_CQ:C_CKV], lat[:, C_CKV:C_KR], lat[:, C_KR:C_GQ]
        ga = lat[:, C_GQ:C_GQ + sh].astype(BF16)
        gf_ref[:, :gdk] = _dot_nt(u2p_ref[...], win_ref[C_GQ + sh:C_GK + sh, :])
        gf_ref[:, gdk:2 * gdk] = _dot_nt(u2p_ref[...], win_ref[C_GK + sh:C_GV + sh, :])

    if ffn:
        x = x_ref[...]
        u1 = _rms(x, n1_ref[...]).astype(BF16)

    if proj:
        ang = pos_ref[...].astype(F32) * invf_ref[...]
        cos = jnp.cos(ang)
        sin = jnp.sin(ang)
        lane = lax.broadcasted_iota(jnp.int32, ang.shape, 1)
        sin_hi = jnp.where(lane >= 80, sin, 0.0)
        sin_lo = jnp.where(lane < 80, -sin, 0.0)

        def rope(t):
            return t * cos + pltpu.roll(t, 16, 1) * sin_hi + pltpu.roll(t, LANE - 16, 1) * sin_lo

        cqn = _rms(cq_raw, qn_ref[...]).astype(BF16)
        ckvn = _rms(ckv_raw, kvn_ref[...]).astype(BF16)
        kr = rope(jnp.concatenate([jnp.zeros((kr_raw.shape[0], MLA_NOPE), F32), kr_raw,
                                   jnp.zeros((kr_raw.shape[0], HEAD_PAD - MLA_NOPE - MLA_ROPE), F32)], axis=1))
        kr_bf = kr.astype(BF16)
        nope_lanes = lax.broadcasted_iota(jnp.int32, kr.shape, 1) < MLA_NOPE

        def q_pair(p):
            qp = _dot(cqn, wuq_ref[:, 2 * p * HEAD_PAD:(2 * p + 2) * HEAD_PAD])
            for j in range(2):
                q_ref[p, :, j * HEAD_PAD:(j + 1) * HEAD_PAD] = (
                    rope(qp[:, j * HEAD_PAD:(j + 1) * HEAD_PAD]) * MLA_EXP2_SCALE).astype(BF16)

    if ffn:
        h1 = x + 0.5 * _swiglu_half(u1, wgu_ref, wd_ref)
        h1_ref[...] = h1

    if proj:
        for p in range(npair // 2):
            q_pair(p)
        gvr_ref[:, :gdv] = _dot_nt(u2p_ref[...], win_ref[C_GV + sh:C_GA + sh, :]).astype(BF16)
        for p in range(npair // 2, npair):
            q_pair(p)
        kn = _dot(ckvn, wuk_ref[...])
        for p in range(npair):
            kp = kn[:, p * LANE:(p + 1) * LANE]
            k_ref[p, :, :HEAD_PAD] = jnp.where(nope_lanes, kp.astype(BF16), kr_bf)
            k_ref[p, :, HEAD_PAD:] = jnp.where(nope_lanes, pltpu.roll(kp, MLA_NOPE, 1).astype(BF16), kr_bf)
        vv = _dot(ckvn, wuv_ref[...]).astype(BF16)
        for p in range(npair):
            v_ref[p] = vv[:, 2 * p * MLA_V:(2 * p + 2) * MLA_V]
        t = _dot(ga, wa2_ref[...]) + ba_ref[...]
        log_sig = jnp.minimum(t, 0.0) - jnp.log(1.0 + jnp.exp(-jnp.abs(t)))
        gf_ref[:, 2 * gdk:] = log_sig * (1.0 / GLA_TAU)

    if ffn:
        u2_ref[...] = _rms(h1, nm_ref[...]).astype(BF16)

    if proj:
        gvr_ref[:, gdv:] = _dot_nt(u2p_ref[...], win_ref[C_GR:C_XQ, :]).astype(BF16)
        xq_ref[...] = _dot_nt(u2p_ref[...], win_ref[C_XQ:C_GZ, :]).astype(BF16)


def _mla_kernel(q_ref, k_ref, v_ref, o_ref, vt_ref):
    def group(g, carry):
        sl = pl.ds(g * MLA_GROUP, MLA_GROUP)
        _mla_pairs(q_ref.at[sl], k_ref.at[sl], v_ref.at[sl], o_ref.at[sl], vt_ref)
        return carry

    lax.fori_loop(0, q_ref.shape[0] // MLA_GROUP, group, 0)


def _mla_pairs(q_ref, k_ref, v_ref, o_ref, vt_ref):
    npair = q_ref.shape[0]
    tq = TQ
    nq = q_ref.shape[1] // tq
    key = lax.broadcasted_iota(jnp.int32, (tq, tq), 0)
    qry = lax.broadcasted_iota(jnp.int32, (tq, tq), 1)
    causal = key <= qry
    ones_row = jnp.where(lax.broadcasted_iota(jnp.int32, (MLA_VT_ROWS - MLA_V, q_ref.shape[1]), 0) == 0,
                         1.0, 0.0).astype(BF16)

    def prepare(p):
        vt = v_ref[p].astype(F32).T.astype(BF16)
        for h in range(2):
            vt_ref[p, h * MLA_VT_ROWS:h * MLA_VT_ROWS + MLA_V, :] = vt[h * MLA_V:(h + 1) * MLA_V, :]
            vt_ref[p, h * MLA_VT_ROWS + MLA_V:(h + 1) * MLA_VT_ROWS, :] = ones_row

    items = [(p, qi, c) for p in range(npair) for qi in range(nq) for c in range(qi + 1)]
    pending = [{}, {}]
    state = [{}, {}]
    done = {}

    def score(h, i):
        p, qi, c = items[i]
        q = q_ref[p, qi * tq:(qi + 1) * tq, h * HEAD_PAD:(h + 1) * HEAD_PAD]
        s = _dot_nt(k_ref[p, c * tq:(c + 1) * tq, h * HEAD_PAD:(h + 1) * HEAD_PAD], q)
        pending[h][i] = jnp.where(causal, s, -1e30) if c == qi else s

    def consume(h, i):
        p, qi, c = items[i]
        st = state[h]
        s = pending[h].pop(i)
        cm = jnp.max(s, axis=0, keepdims=True)
        m_new = cm if c == 0 else jnp.maximum(st["m"], cm)
        pr = jnp.exp2(s - m_new).astype(BF16)
        pv = _dot(vt_ref[p, h * MLA_VT_ROWS:(h + 1) * MLA_VT_ROWS, c * tq:(c + 1) * tq], pr)
        st["acc"] = pv if c == 0 else st["acc"] * jnp.exp2(st["m"] - m_new) + pv
        st["m"] = m_new
        if c == qi:
            done[(p, qi, h)] = st["acc"][:MLA_V, :] / st["acc"][MLA_V:MLA_V + 1, :]
            if (p, qi, 1 - h) in done:
                o_ref[p, qi * tq:(qi + 1) * tq, :] = jnp.concatenate(
                    [done.pop((p, qi, 0)), done.pop((p, qi, 1))], axis=0).T.astype(BF16)

    for t in range(len(items) + MLA_LOOKAHEAD):
        if t < len(items) and items[t][1:] == (0, 0):
            prepare(items[t][0])
        for h in range(2):
            if t < len(items):
                score(h, t)
            if t >= MLA_LOOKAHEAD:
                consume(h, t - MLA_LOOKAHEAD)


def _xattn_kernel(q_ref, mem_ref, n_ref, w_ref, o_ref, kv_ref):
    @pl.when(pl.program_id(1) == 0)
    def _():
        mn = _rms(mem_ref[...], n_ref[...]).astype(BF16)
        kv_ref[...] = _dot(mn, w_ref[...]).astype(BF16)

    scale = 1.0 / math.sqrt(X_DH)
    hw = X_HEADS * X_DH
    for h in range(X_HEADS):
        q = q_ref[:, h * X_DH:(h + 1) * X_DH]
        k = kv_ref[:, h * X_DH:(h + 1) * X_DH]
        v = kv_ref[:, hw + h * X_DH:hw + (h + 1) * X_DH]
        s = _dot_nt(q, k) * scale
        m = jnp.max(s, axis=-1, keepdims=True)
        p = jnp.exp(s - m)
        l = jnp.sum(p, axis=-1, keepdims=True)
        o = _dot(p.astype(BF16), v) / l
        o_ref[:, h * X_DH:(h + 1) * X_DH] = o.astype(BF16)


def _gla_kernel(gf_ref, gvr_ref, on_ref, o_ref):
    sb, c = GLA_SB, GLA_CHUNK
    nsb = gf_ref.shape[0] // sb
    nch = sb // c
    npair = GLA_HEADS // 2
    pw, vw = 2 * GLA_DK, 2 * GLA_DV
    gdk, gdv = GLA_HEADS * GLA_DK, GLA_HEADS * GLA_DV
    row = lax.broadcasted_iota(jnp.int32, (sb, sb), 0)
    col = lax.broadcasted_iota(jnp.int32, (sb, sb), 1)
    tri = jnp.logical_and(row // c == col // c, col <= row)
    tri_bf = jnp.where(tri, 1.0, 0.0).astype(BF16)
    lane = lax.broadcasted_iota(jnp.int32, (1, pw), 1)
    head_lanes = (lane < GLA_DK, lane >= GLA_DK)
    srow = lax.broadcasted_iota(jnp.int32, (pw, vw), 0)
    scol = lax.broadcasted_iota(jnp.int32, (pw, vw), 1)
    same_head = (srow < GLA_DK) == (scol < GLA_DV)

    def local(i):
        r0 = i * sb
        la = gf_ref[r0:r0 + sb, 2 * gdk:]
        hi = la.astype(BF16)
        lo = (la - hi.astype(F32)).astype(BF16)
        bcum_all = _dot(tri_bf, hi) + _dot(tri_bf, lo)
        out = []
        for p in range(npair):
            bcum = bcum_all[:, p * pw:(p + 1) * pw]
            e_last = [jnp.exp(bcum[j * c + c - 1:j * c + c, :]) for j in range(nch)]
            e_last_rows = jnp.concatenate([jnp.broadcast_to(e, (c, pw)) for e in e_last], axis=0)
            q_t = gf_ref[r0:r0 + sb, p * pw:(p + 1) * pw] * (GLA_DK ** -0.5) * jnp.exp(bcum)
            kk = gf_ref[r0:r0 + sb, gdk + p * pw:gdk + (p + 1) * pw]
            k_undecayed = kk * jnp.exp(-bcum)
            k_t = k_undecayed.astype(BF16)
            k_d = (k_undecayed * e_last_rows).astype(BF16)
            v = gvr_ref[r0:r0 + sb, p * vw:(p + 1) * vw]
            o_intra = []
            for h in range(2):
                qm = jnp.where(head_lanes[h], q_t, 0.0).astype(BF16)
                att = jnp.where(tri, _dot_nt(qm, k_t), 0.0).astype(BF16)
                o_intra.append(_dot(att, v[:, h * GLA_DV:(h + 1) * GLA_DV]))
            d_st, decay = [], []
            for j in range(nch):
                d = _dot_tn(k_d[j * c:(j + 1) * c, :], v[j * c:(j + 1) * c, :])
                d_st.append(jnp.where(same_head, d, 0.0))
                dcol = jnp.broadcast_to(e_last[j], (pw, pw)).T
                decay.append(jnp.concatenate([dcol, dcol], axis=1))
            out.append((q_t.astype(BF16), o_intra, d_st, decay))
        return out

    def recurrent(i, loc, states):
        r0 = i * sb
        for p in range(npair):
            q_bf, o_intra, d_st, decay = loc[p]
            st = states[p]
            o_inter = []
            for j in range(nch):
                o_inter.append(_dot(q_bf[j * c:(j + 1) * c, :], st.astype(BF16)))
                st = decay[j] * st + d_st[j]
            states[p] = st
            o_inter = jnp.concatenate(o_inter, axis=0)
            for h in range(2):
                lo_, hi_ = p * vw + h * GLA_DV, p * vw + (h + 1) * GLA_DV
                o = _rms(o_intra[h] + o_inter[:, h * GLA_DV:(h + 1) * GLA_DV], on_ref[...])
                r = gvr_ref[r0:r0 + sb, gdv + lo_:gdv + hi_].astype(F32)
                o_ref[r0:r0 + sb, lo_:hi_] = (o * _silu(r)).astype(BF16)

    states = [jnp.zeros((pw, vw), F32) for _ in range(npair)]
    loc = local(0)
    for i in range(nsb):
        nxt = local(i + 1) if i + 1 < nsb else None
        recurrent(i, loc, states)
        loc = nxt


def _post_kernel(h1_ref, om_ref, og_ref, ox_ref, nm_ref, wgz32_ref, gb_ref, wom32_ref, wog32_ref, wox32_ref,
                 wout32_ref, n2_ref, wg32_ref, wu32_ref, wd32_ref, nf_ref, o_ref,
                 wgz_ref, wom_ref, wog_ref, wox_ref, wout_ref, wgu_ref, wd_ref, h2_ref, u2_ref):
    i = pl.program_id(0)
    last = pl.num_programs(0) - 1
    refs = (h1_ref, om_ref, og_ref, ox_ref, nm_ref, wgz_ref, gb_ref, wom_ref, wog_ref, wox_ref,
            wout_ref, n2_ref, wgu_ref, wd_ref, nf_ref, o_ref, h2_ref, u2_ref)

    @pl.when(i < W_STEPS)
    def _():
        _convert_ffn(i, wg32_ref, wu32_ref, wd32_ref, wgu_ref, wd_ref)
        _convert_rows(wgz_ref, wgz32_ref[...], i, WIN_POST_ROWS)
        for dst, src in ((wom_ref, wom32_ref), (wog_ref, wog32_ref), (wox_ref, wox32_ref)):
            _convert_rows(dst, src[...], i, dst.shape[0] // W_STEPS)
        _convert_rows(wout_ref, wout32_ref[...], i, D_MODEL // W_STEPS)

    @pl.when(i == W_STEPS)
    def _():
        _post_tokens(*refs, merge=True, ffn=False)

    @pl.when(jnp.logical_and(i > W_STEPS, i < last))
    def _():
        _post_tokens(*refs, merge=True, ffn=True)

    @pl.when(i == last)
    def _():
        _post_tokens(*refs, merge=False, ffn=True)


def _post_tokens(h1_ref, om_ref, og_ref, ox_ref, nm_ref, wgz_ref, gb_ref, wom_ref, wog_ref, wox_ref,
                 wout_ref, n2_ref, wgu_ref, wd_ref, nf_ref, o_ref, h2_ref, u2_ref, *, merge, ffn):
    d = D_MODEL
    if merge:
        h1 = h1_ref[...]
        u = _rms(h1, nm_ref[...]).astype(BF16)
        branches = ((jnp.concatenate([om_ref[p] for p in range(om_ref.shape[0])], axis=1), wom_ref),
                    (og_ref[...], wog_ref), (ox_ref[...], wox_ref))

        def merge_tile(c0):
            acc = None
            for n, (o_in, w_br) in enumerate(branches):
                gate = _sigmoid(_dot_nt(u, wgz_ref[n * d + c0:n * d + c0 + MERGE_TILE, :])
                                + gb_ref[:, n * d + c0:n * d + c0 + MERGE_TILE])
                term = gate * _dot(o_in, w_br[:, c0:c0 + MERGE_TILE])
                acc = term if acc is None else acc + term
            return acc.astype(BF16)

    tiles = []
    if ffn:
        half = 0.5 * _swiglu_half(u2_ref[...], wgu_ref, wd_ref)
    if merge:
        tiles.append(merge_tile(0))
    if ffn:
        o_ref[...] = _rms(h2_ref[...] + half, nf_ref[...])
    if merge:
        for c0 in range(MERGE_TILE, d, MERGE_TILE):
            tiles.append(merge_tile(c0))
        h2 = h1 + _dot(jnp.concatenate(tiles, axis=1), wout_ref[...])
        h2_ref[...] = h2
        u2_ref[...] = _rms(h2, n2_ref[...]).astype(BF16)


def _cparams(sem, vmem=None, flags=None):
    return pltpu.CompilerParams(dimension_semantics=sem, vmem_limit_bytes=vmem, flags=flags)


def kernel(x, mem, positions, ffn1_norm, ffn1_wg, ffn1_wu, ffn1_wd, mix_norm, mem_norm, w_in, gate_bias,
           mla_q_norm, mla_w_uq, mla_kv_norm, mla_w_ukv, mla_w_o, gla_w_a2, gla_b_a, gla_o_norm, gla_w_o,
           x_w_kv, x_w_o, w_out, ffn2_norm, ffn2_wg, ffn2_wu, ffn2_wd, final_norm):
    B, S, D = x.shape
    T = B * S
    M = mem.shape[1]
    F = D_FF
    H = MLA_HEADS
    bf = lambda a: a.astype(BF16)
    row = lambda a: a.reshape(1, -1)

    w_in_t = jnp.swapaxes(w_in[0], 0, 1)
    w_uq = mla_w_uq[0].reshape(MLA_Q_RANK, H, MLA_NOPE + MLA_ROPE)
    w_uq = bf(jnp.pad(w_uq, ((0, 0), (0, 0), (0, HEAD_PAD - MLA_NOPE - MLA_ROPE))).reshape(MLA_Q_RANK, H * HEAD_PAD))
    w_ukv = mla_w_ukv[0].reshape(MLA_KV_RANK, H, MLA_NOPE + MLA_V)
    w_uk = bf(w_ukv[:, :, :MLA_NOPE].reshape(MLA_KV_RANK, H * MLA_NOPE))
    w_uv = bf(w_ukv[:, :, MLA_NOPE:].reshape(MLA_KV_RANK, H * MLA_V))
    w_a2 = bf(gla_w_a2[0])
    half = MLA_ROPE // 2
    inv_freq = ROPE_THETA ** (-jnp.arange(half, dtype=F32) / half)
    invf = jnp.concatenate([jnp.zeros((MLA_NOPE,), F32), inv_freq, inv_freq,
                            jnp.zeros((HEAD_PAD - MLA_NOPE - MLA_ROPE,), F32)]).reshape(1, HEAD_PAD)

    x2 = x.reshape(T, D)
    pos2 = jnp.broadcast_to(positions.reshape(T, 1), (T, HEAD_PAD))
    nt = T // TM
    tok = lambda w: pl.BlockSpec((TM, w), lambda i: (jnp.clip(i - W_STEPS, 0, nt - 1), 0))
    tok_prev = lambda w: pl.BlockSpec((TM, w), lambda i: (jnp.clip(i - W_STEPS - 1, 0, nt - 1), 0))
    tok_pairs_prev = lambda w: pl.BlockSpec(
        (MLA_HEADS // 2, TM, w), lambda i: (0, jnp.clip(i - W_STEPS - 1, 0, nt - 1), 0))
    npair = H // 2
    tok_pairs = lambda w: pl.BlockSpec((npair, TM, w), lambda i: (0, jnp.clip(i - W_STEPS, 0, nt - 1), 0))
    wrows = lambda shape: pl.BlockSpec(
        (shape[0] // W_STEPS, shape[1]), lambda i: (jnp.minimum(i, W_STEPS - 1), 0))
    ffn_specs = [wrows((D, F)), wrows((D, F)), wrows((F, D))]
    ffn_scratch = [pltpu.VMEM((D, 2 * F), BF16), pltpu.VMEM((F, D), BF16)]

    gdk, gdv, xw = GLA_HEADS * GLA_DK, GLA_HEADS * GLA_DV, X_HEADS * X_DH
    pre_out_shapes = (
        jax.ShapeDtypeStruct((T, D), F32),
        jax.ShapeDtypeStruct((npair, T, 2 * HEAD_PAD), BF16),
        jax.ShapeDtypeStruct((npair, T, 2 * HEAD_PAD), BF16),
        jax.ShapeDtypeStruct((npair, T, 2 * MLA_V), BF16),
        jax.ShapeDtypeStruct((T, 3 * gdk), F32),
        jax.ShapeDtypeStruct((T, 2 * gdv), BF16),
        jax.ShapeDtypeStruct((T, xw), BF16),
    )
    h1, q, k, v, gf, gvr, xq = pl.pallas_call(
        _pre_kernel,
        out_shape=pre_out_shapes,
        grid=(W_STEPS + nt + 1,),
        in_specs=[tok(D), tok_prev(HEAD_PAD), _const_spec((1, HEAD_PAD)), _const_spec((1, D)), _const_spec((1, D)),
                  *ffn_specs, pl.BlockSpec((WIN_PRE_ROWS, D), lambda i: (jnp.minimum(i, W_STEPS - 1), 0)),
                  _const_spec((1, MLA_Q_RANK)), _const_spec((MLA_Q_RANK, H * HEAD_PAD)),
                  _const_spec((1, MLA_KV_RANK)), _const_spec((MLA_KV_RANK, H * MLA_NOPE)),
                  _const_spec((MLA_KV_RANK, H * MLA_V)), _const_spec((GLA_GATE_RANK, gdk)), _const_spec((1, gdk))],
        out_specs=(tok(D),) + tuple(tok_prev(s.shape[1]) if len(s.shape) == 2 else tok_pairs_prev(s.shape[2])
                                    for s in pre_out_shapes[1:]),
        scratch_shapes=[*ffn_scratch, pltpu.VMEM((W_STEPS * WIN_PRE_ROWS, D), BF16),
                        pltpu.VMEM((2, TM, D), BF16)],
        compiler_params=_cparams(("arbitrary",), VMEM_LIMIT),
        name="pre",
    )(x2, pos2, invf, row(ffn1_norm[0]), row(mix_norm[0]), ffn1_wg[0], ffn1_wu[0], ffn1_wd[0],
      w_in_t, row(mla_q_norm[0]), w_uq, row(mla_kv_norm[0]), w_uk, w_uv, w_a2, row(gla_b_a[0]))

    seq_pairs = lambda w: pl.BlockSpec((npair, S, w), lambda b: (0, b, 0))
    o_mla = pl.pallas_call(
        _mla_kernel,
        out_shape=jax.ShapeDtypeStruct((npair, T, 2 * MLA_V), BF16),
        grid=(B,),
        in_specs=[seq_pairs(2 * HEAD_PAD), seq_pairs(2 * HEAD_PAD), seq_pairs(2 * MLA_V)],
        out_specs=seq_pairs(2 * MLA_V),
        scratch_shapes=[pltpu.VMEM((MLA_GROUP, 2 * MLA_VT_ROWS, S), BF16)],
        compiler_params=_cparams(("parallel",), VMEM_LIMIT),
        name="mla",
    )(q, k, v)

    nqx = S // TQX
    o_x = pl.pallas_call(
        _xattn_kernel,
        out_shape=jax.ShapeDtypeStruct((T, xw), BF16),
        grid=(B, nqx),
        in_specs=[pl.BlockSpec((TQX, xw), lambda b, i: (b * nqx + i, 0)),
                  pl.BlockSpec((M, D), lambda b, i: (b, 0)),
                  _const_spec((1, D)), _const_spec((D, 2 * xw))],
        out_specs=pl.BlockSpec((TQX, xw), lambda b, i: (b * nqx + i, 0)),
        scratch_shapes=[pltpu.VMEM((M, 2 * xw), BF16)],
        compiler_params=_cparams(("arbitrary", "arbitrary")),
        name="xattn",
    )(xq, mem.reshape(B * M, D), row(mem_norm[0]), bf(x_w_kv[0]))

    o_gla = pl.pallas_call(
        _gla_kernel,
        out_shape=jax.ShapeDtypeStruct((T, gdv), BF16),
        grid=(B,),
        in_specs=[pl.BlockSpec((S, 3 * gdk), lambda b: (b, 0)),
                  pl.BlockSpec((S, 2 * gdv), lambda b: (b, 0)),
                  _const_spec((1, GLA_DV))],
        out_specs=pl.BlockSpec((S, gdv), lambda b: (b, 0)),
        compiler_params=_cparams(("parallel",), VMEM_LIMIT),
        name="gla",
    )(gf, gvr, row(gla_o_norm[0]))

    out = pl.pallas_call(
        _post_kernel,
        out_shape=jax.ShapeDtypeStruct((T, D), F32),
        grid=(W_STEPS + nt + 1,),
        in_specs=[tok(D), tok_pairs(2 * MLA_V), tok(gdv), tok(xw), _const_spec((1, D)),
                  pl.BlockSpec((pl.Element(WIN_POST_ROWS), pl.Element(D)),
                               lambda i: (pl.multiple_of(C_GZ + jnp.minimum(i, W_STEPS - 1) * WIN_POST_ROWS, 16), 0)),
                  _const_spec((1, N_BRANCH * D)),
                  wrows((H * MLA_V, D)), wrows((gdv, D)), wrows((xw, D)),
                  wrows((D, D)), _const_spec((1, D)),
                  *ffn_specs, _const_spec((1, D))],
        out_specs=tok_prev(D),
        scratch_shapes=[pltpu.VMEM((N_BRANCH * D, D), BF16), pltpu.VMEM((H * MLA_V, D), BF16),
                        pltpu.VMEM((gdv, D), BF16), pltpu.VMEM((xw, D), BF16), pltpu.VMEM((D, D), BF16),
                        *ffn_scratch, pltpu.VMEM((TM, D), F32), pltpu.VMEM((TM, D), BF16)],
        compiler_params=_cparams(("arbitrary",), VMEM_LIMIT),
        name="post",
    )(h1, o_mla, o_gla, o_x, row(mix_norm[0]), w_in_t, gate_bias[0].reshape(1, N_BRANCH * D),
      mla_w_o[0], gla_w_o[0], x_w_o[0], w_out[0], row(ffn2_norm[0]),
      ffn2_wg[0], ffn2_wu[0], ffn2_wd[0], row(final_norm))
    return out.reshape(B, S, D)
```

```python
import math

import jax
import jax.numpy as jnp
from jax import lax
from jax.experimental import pallas as pl
from jax.experimental.pallas import tpu as pltpu

F32 = jnp.float32
BF16 = jnp.bfloat16

D_MODEL = 1024
EPS = 1e-6
MLA_HEADS = 8
MLA_NOPE = 64
MLA_ROPE = 32
MLA_V = 64
MLA_Q_RANK = 384
MLA_KV_RANK = 256
ROPE_THETA = 10000.0
GLA_HEADS = 4
GLA_DK = 64
GLA_DV = 128
GLA_GATE_RANK = 16
GLA_TAU = 16.0
GLA_CHUNK = 64
X_HEADS = 4
X_DH = 128
D_FF = 2816
N_BRANCH = 3

LANE = 128
HEAD_PAD = 128
VMEM_LIMIT = 63 * 1024 * 1024
MLA_EXP2_SCALE = math.log2(math.e) / math.sqrt(MLA_NOPE + MLA_ROPE)
MLA_VT_ROWS = MLA_V + 16
MLA_GROUP = 2
MLA_LOOKAHEAD = 3

C_CQ, C_CKV, C_KR, C_GQ, C_GK, C_GV, C_GA, C_GR, C_XQ, C_GZ, C_END = (
    0, 384, 640, 672, 928, 1184, 1696, 1712, 2224, 2736, 5808)
W_STEPS = 8
WIN_PRE_ROWS = -(-C_GZ // (16 * W_STEPS)) * 16
WIN_POST_ROWS = (C_END - C_GZ) // W_STEPS

TM = 512
MERGE_TILE = 256
TQ = 256
TQX = 2048
GLA_SB = 256


def _dot(a, b):
    return jnp.dot(a, b, preferred_element_type=F32)


def _dot_nt(a, b):
    return lax.dot_general(a, b, (((1,), (1,)), ((), ())), preferred_element_type=F32)


def _dot_tn(a, b):
    return lax.dot_general(a, b, (((0,), (0,)), ((), ())), preferred_element_type=F32)


def _rms(x, g):
    return x * lax.rsqrt(jnp.mean(x * x, axis=-1, keepdims=True) + EPS) * g


def _sigmoid(x):
    return 1.0 / (1.0 + jnp.exp(-x))


def _silu(x):
    return x * _sigmoid(x)


def _const_spec(shape):
    return pl.BlockSpec(shape, lambda *_: (0,) * len(shape), pipeline_mode=pl.Buffered(1))


def _swiglu_half(xn_bf16, wgu_ref, wd_ref):
    tiles = []
    for f0 in range(0, 2 * D_FF, 2 * LANE):
        gu = _dot(xn_bf16, wgu_ref[:, f0:f0 + 2 * LANE])
        tiles.append((_silu(gu[:, :LANE]) * gu[:, LANE:]).astype(BF16))
    return _dot(jnp.concatenate(tiles, axis=1), wd_ref[...])


def _convert_rows(dst_ref, src, step, rows):
    r0 = pl.multiple_of(step * rows, rows)
    dst_ref[pl.ds(r0, rows), :] = src.astype(BF16)


def _convert_ffn(i, wg32_ref, wu32_ref, wd32_ref, wgu_ref, wd_ref):
    rows = D_MODEL // W_STEPS
    r0 = pl.multiple_of(i * rows, rows)
    wg = wg32_ref[...].astype(BF16)
    wu = wu32_ref[...].astype(BF16)
    for j in range(D_FF // LANE):
        wgu_ref[pl.ds(r0, rows), 2 * j * LANE:(2 * j + 1) * LANE] = wg[:, j * LANE:(j + 1) * LANE]
        wgu_ref[pl.ds(r0, rows), (2 * j + 1) * LANE:(2 * j + 2) * LANE] = wu[:, j * LANE:(j + 1) * LANE]
    _convert_rows(wd_ref, wd32_ref[...], i, D_FF // W_STEPS)


def _pre_kernel(x_ref, pos_ref, invf_ref, n1_ref, nm_ref, wg32_ref, wu32_ref, wd32_ref, win32_ref,
                qn_ref, wuq_ref, kvn_ref, wuk_ref, wuv_ref, wa2_ref, ba_ref,
                h1_ref, q_ref, k_ref, v_ref, gf_ref, gvr_ref, xq_ref,
                wgu_ref, wd_ref, win_ref, u2_ref):
    i = pl.program_id(0)
    last = pl.num_programs(0) - 1
    slot = lax.rem(i, 2)
    refs = (x_ref, pos_ref, invf_ref, n1_ref, nm_ref, wgu_ref, wd_ref, win_ref,
            qn_ref, wuq_ref, kvn_ref, wuk_ref, wuv_ref, wa2_ref, ba_ref,
            h1_ref, q_ref, k_ref, v_ref, gf_ref, gvr_ref, xq_ref, u2_ref.at[1 - slot], u2_ref.at[slot])

    @pl.when(i < W_STEPS)
    def _():
        _convert_ffn(i, wg32_ref, wu32_ref, wd32_ref, wgu_ref, wd_ref)
        _convert_rows(win_ref, win32_ref[...], i, WIN_PRE_ROWS)

        @pl.when(i == W_STEPS - 1)
        def _():
            ga_rows = win_ref[C_GA:C_GR, :]
            for r0 in range(C_GA - 64, C_GQ - 1, -64):
                win_ref[r0 + GLA_GATE_RANK:r0 + 64 + GLA_GATE_RANK, :] = win_ref[r0:r0 + 64, :]
            win_ref[C_GQ:C_GQ + GLA_GATE_RANK, :] = ga_rows

    @pl.when(i == W_STEPS)
    def _():
        _pre_tokens(*refs, ffn=True, proj=False)

    @pl.when(jnp.logical_and(i > W_STEPS, i < last))
    def _():
        _pre_tokens(*refs, ffn=True, proj=True)

    @pl.when(i == last)
    def _():
        _pre_tokens(*refs, ffn=False, proj=True)


def _pre_tokens(x_ref, pos_ref, invf_ref, n1_ref, nm_ref, wgu_ref, wd_ref, win_ref,
                qn_ref, wuq_ref, kvn_ref, wuk_ref, wuv_ref, wa2_ref, ba_ref,
                h1_ref, q_ref, k_ref, v_ref, gf_ref, gvr_ref, xq_ref, u2p_ref, u2_ref, *, ffn, proj):
    npair = MLA_HEADS // 2
    gdk, gdv = GLA_HEADS * GLA_DK, GLA_HEADS * GLA_DV

    if proj:
        sh = GLA_GATE_RANK
        lat = _dot_nt(u2p_ref[...], win_ref[C_CQ:C_GQ + sh, :])
        cq_raw, ckv_raw, kr_raw = lat[:, C_CQ:C_CKV], lat[:, C_CKV:C_KR], lat[:, C_KR:C_GQ]
        ga = lat[:, C_GQ:C_GQ + sh].astype(BF16)
        gf_ref[:, :gdk] = _dot_nt(u2p_ref[...], win_ref[C_GQ + sh:C_GK + sh, :])
        gf_ref[:, gdk:2 * gdk] = _dot_nt(u2p_ref[...], win_ref[C_GK + sh:C_GV + sh, :])

    if ffn:
        x = x_ref[...]
        u1 = _rms(x, n1_ref[...]).astype(BF16)

    if proj:
        pos = pos_ref[0].astype(F32)
        p_hi = pos.astype(BF16).astype(F32)
        p_mid = (pos - p_hi).astype(BF16).astype(F32)
        p_lo = pos - p_hi - p_mid
        piece = lax.broadcasted_iota(jnp.int32, (16, pos.shape[1]), 0)
        pieces = jnp.where(piece == 0, p_hi, jnp.where(piece == 1, p_mid, jnp.where(piece == 2, p_lo, 0.0)))
        ang = _dot_tn(pieces.astype(BF16), jnp.ones((16, LANE), BF16)) * invf_ref[...]
        cos = jnp.cos(ang)
        sin = jnp.sin(ang)
        lane = lax.broadcasted_iota(jnp.int32, ang.shape, 1)
        sin_hi = jnp.where(lane >= 80, sin, 0.0)
        sin_lo = jnp.where(lane < 80, -sin, 0.0)

        def rope(t):
            return t * cos + pltpu.roll(t, 16, 1) * sin_hi + pltpu.roll(t, LANE - 16, 1) * sin_lo

        cqn = _rms(cq_raw, qn_ref[...]).astype(BF16)
        ckvn = _rms(ckv_raw, kvn_ref[...]).astype(BF16)
        kr = rope(jnp.concatenate([jnp.zeros((kr_raw.shape[0], MLA_NOPE), F32), kr_raw,
                                   jnp.zeros((kr_raw.shape[0], HEAD_PAD - MLA_NOPE - MLA_ROPE), F32)], axis=1))
        kr_bf = kr.astype(BF16)
        nope_lanes = lax.broadcasted_iota(jnp.int32, kr.shape, 1) < MLA_NOPE

        def q_pair(p):
            qp = _dot(cqn, wuq_ref[:, 2 * p * HEAD_PAD:(2 * p + 2) * HEAD_PAD])
            for j in range(2):
                q_ref[p, :, j * HEAD_PAD:(j + 1) * HEAD_PAD] = (
                    rope(qp[:, j * HEAD_PAD:(j + 1) * HEAD_PAD]) * MLA_EXP2_SCALE).astype(BF16)

    if ffn:
        h1 = x + 0.5 * _swiglu_half(u1, wgu_ref, wd_ref)
        h1_ref[...] = h1

    if proj:
        for p in range(npair // 2):
            q_pair(p)
        gvr_ref[:, :gdv] = _dot_nt(u2p_ref[...], win_ref[C_GV + sh:C_GA + sh, :]).astype(BF16)
        for p in range(npair // 2, npair):
            q_pair(p)
        kn = _dot(ckvn, wuk_ref[...])
        for p in range(npair):
            kp = kn[:, p * LANE:(p + 1) * LANE]
            k_ref[p, :, :HEAD_PAD] = jnp.where(nope_lanes, kp.astype(BF16), kr_bf)
            k_ref[p, :, HEAD_PAD:] = jnp.where(nope_lanes, pltpu.roll(kp, MLA_NOPE, 1).astype(BF16), kr_bf)
        vv = _dot(ckvn, wuv_ref[...]).astype(BF16)
        for p in range(npair):
            v_ref[p] = vv[:, 2 * p * MLA_V:(2 * p + 2) * MLA_V]
        t = _dot(ga, wa2_ref[...]) + ba_ref[...]
        log_sig = jnp.minimum(t, 0.0) - jnp.log(1.0 + jnp.exp(-jnp.abs(t)))
        gf_ref[:, 2 * gdk:] = log_sig * (1.0 / GLA_TAU)

    if ffn:
        u2_ref[...] = _rms(h1, nm_ref[...]).astype(BF16)

    if proj:
        gvr_ref[:, gdv:] = _dot_nt(u2p_ref[...], win_ref[C_GR:C_XQ, :]).astype(BF16)
        xq_ref[...] = _dot_nt(u2p_ref[...], win_ref[C_XQ:C_GZ, :]).astype(BF16)


def _mla_kernel(q_ref, k_ref, v_ref, o_ref, vt_ref):
    def group(g, carry):
        sl = pl.ds(g * MLA_GROUP, MLA_GROUP)
        _mla_pairs(q_ref.at[sl], k_ref.at[sl], v_ref.at[sl], o_ref.at[sl], vt_ref)
        return carry

    lax.fori_loop(0, q_ref.shape[0] // MLA_GROUP, group, 0)


def _mla_pairs(q_ref, k_ref, v_ref, o_ref, vt_ref):
    npair = q_ref.shape[0]
    tq = TQ
    nq = q_ref.shape[1] // tq
    key = lax.broadcasted_iota(jnp.int32, (tq, tq), 0)
    qry = lax.broadcasted_iota(jnp.int32, (tq, tq), 1)
    causal = key <= qry
    ones_row = jnp.where(lax.broadcasted_iota(jnp.int32, (MLA_VT_ROWS - MLA_V, q_ref.shape[1]), 0) == 0,
                         1.0, 0.0).astype(BF16)

    def prepare(p):
        vt = v_ref[p].astype(F32).T.astype(BF16)
        for h in range(2):
            vt_ref[p, h * MLA_VT_ROWS:h * MLA_VT_ROWS + MLA_V, :] = vt[h * MLA_V:(h + 1) * MLA_V, :]
            vt_ref[p, h * MLA_VT_ROWS + MLA_V:(h + 1) * MLA_VT_ROWS, :] = ones_row

    items = [(p, qi, c) for p in range(npair) for qi in range(nq) for c in range(qi + 1)]
    pending = [{}, {}]
    state = [{}, {}]
    done = {}

    def score(h, i):
        p, qi, c = items[i]
        q = q_ref[p, qi * tq:(qi + 1) * tq, h * HEAD_PAD:(h + 1) * HEAD_PAD]
        s = _dot_nt(k_ref[p, c * tq:(c + 1) * tq, h * HEAD_PAD:(h + 1) * HEAD_PAD], q)
        pending[h][i] = jnp.where(causal, s, -1e30) if c == qi else s

    def consume(h, i):
        p, qi, c = items[i]
        st = state[h]
        s = pending[h].pop(i)
        cm = jnp.max(s, axis=0, keepdims=True)
        m_new = cm if c == 0 else jnp.maximum(st["m"], cm)
        pr = jnp.exp2(s - m_new).astype(BF16)
        pv = _dot(vt_ref[p, h * MLA_VT_ROWS:(h + 1) * MLA_VT_ROWS, c * tq:(c + 1) * tq], pr)
        st["acc"] = pv if c == 0 else st["acc"] * jnp.exp2(st["m"] - m_new) + pv
        st["m"] = m_new
        if c == qi:
            done[(p, qi, h)] = st["acc"][:MLA_V, :] / st["acc"][MLA_V:MLA_V + 1, :]
            if (p, qi, 1 - h) in done:
                o_ref[p, qi * tq:(qi + 1) * tq, :] = jnp.concatenate(
                    [done.pop((p, qi, 0)), done.pop((p, qi, 1))], axis=0).T.astype(BF16)

    for t in range(len(items) + MLA_LOOKAHEAD):
        if t < len(items) and items[t][1:] == (0, 0):
            prepare(items[t][0])
        for h in range(2):
            if t < len(items):
                score(h, t)
            if t >= MLA_LOOKAHEAD:
                consume(h, t - MLA_LOOKAHEAD)


def _xattn_kernel(q_ref, mem_ref, n_ref, w_ref, o_ref, kv_ref):
    @pl.when(pl.program_id(1) == 0)
    def _():
        mn = _rms(mem_ref[...], n_ref[...]).astype(BF16)
        kv_ref[...] = _dot(mn, w_ref[...]).astype(BF16)

    scale = 1.0 / math.sqrt(X_DH)
    hw = X_HEADS * X_DH
    for h in range(X_HEADS):
        q = q_ref[:, h * X_DH:(h + 1) * X_DH]
        k = kv_ref[:, h * X_DH:(h + 1) * X_DH]
        v = kv_ref[:, hw + h * X_DH:hw + (h + 1) * X_DH]
        s = _dot_nt(q, k) * scale
        m = jnp.max(s, axis=-1, keepdims=True)
        p = jnp.exp(s - m)
        l = jnp.sum(p, axis=-1, keepdims=True)
        o = _dot(p.astype(BF16), v) / l
        o_ref[:, h * X_DH:(h + 1) * X_DH] = o.astype(BF16)


def _gla_kernel(gf_ref, gvr_ref, on_ref, o_ref):
    sb, c = GLA_SB, GLA_CHUNK
    nsb = gf_ref.shape[0] // sb
    nch = sb // c
    npair = GLA_HEADS // 2
    pw, vw = 2 * GLA_DK, 2 * GLA_DV
    gdk, gdv = GLA_HEADS * GLA_DK, GLA_HEADS * GLA_DV
    row = lax.broadcasted_iota(jnp.int32, (sb, sb), 0)
    col = lax.broadcasted_iota(jnp.int32, (sb, sb), 1)
    tri = jnp.logical_and(row // c == col // c, col <= row)
    tri_bf = jnp.where(tri, 1.0, 0.0).astype(BF16)
    lane = lax.broadcasted_iota(jnp.int32, (1, pw), 1)
    head_lanes = (lane < GLA_DK, lane >= GLA_DK)
    srow = lax.broadcasted_iota(jnp.int32, (pw, vw), 0)
    scol = lax.broadcasted_iota(jnp.int32, (pw, vw), 1)
    same_head = (srow < GLA_DK) == (scol < GLA_DV)

    def local(i):
        r0 = i * sb
        la = gf_ref[r0:r0 + sb, 2 * gdk:]
        hi = la.astype(BF16)
        lo = (la - hi.astype(F32)).astype(BF16)
        bcum_all = _dot(tri_bf, hi) + _dot(tri_bf, lo)
        out = []
        for p in range(npair):
            bcum = bcum_all[:, p * pw:(p + 1) * pw]
            e_last = [jnp.exp(bcum[j * c + c - 1:j * c + c, :]) for j in range(nch)]
            e_last_rows = jnp.concatenate([jnp.broadcast_to(e, (c, pw)) for e in e_last], axis=0)
            q_t = gf_ref[r0:r0 + sb, p * pw:(p + 1) * pw] * (GLA_DK ** -0.5) * jnp.exp(bcum)
            kk = gf_ref[r0:r0 + sb, gdk + p * pw:gdk + (p + 1) * pw]
            k_undecayed = kk * jnp.exp(-bcum)
            k_t = k_undecayed.astype(BF16)
            k_d = (k_undecayed * e_last_rows).astype(BF16)
            v = gvr_ref[r0:r0 + sb, p * vw:(p + 1) * vw]
            o_intra = []
            for h in range(2):
                qm = jnp.where(head_lanes[h], q_t, 0.0).astype(BF16)
                att = jnp.where(tri, _dot_nt(qm, k_t), 0.0).astype(BF16)
                o_intra.append(_dot(att, v[:, h * GLA_DV:(h + 1) * GLA_DV]))
            d_st, decay = [], []
            for j in range(nch):
                d = _dot_tn(k_d[j * c:(j + 1) * c, :], v[j * c:(j + 1) * c, :])
                d_st.append(jnp.where(same_head, d, 0.0))
                dcol = jnp.broadcast_to(e_last[j], (pw, pw)).T
                decay.append(jnp.concatenate([dcol, dcol], axis=1))
            out.append((q_t.astype(BF16), o_intra, d_st, decay))
        return out

    def recurrent(i, loc, states):
        r0 = i * sb
        for p in range(npair):
            q_bf, o_intra, d_st, decay = loc[p]
            st = states[p]
            o_inter = []
            for j in range(nch):
                o_inter.append(_dot(q_bf[j * c:(j + 1) * c, :], st.astype(BF16)))
                st = decay[j] * st + d_st[j]
            states[p] = st
            o_inter = jnp.concatenate(o_inter, axis=0)
            for h in range(2):
                lo_, hi_ = p * vw + h * GLA_DV, p * vw + (h + 1) * GLA_DV
                o = _rms(o_intra[h] + o_inter[:, h * GLA_DV:(h + 1) * GLA_DV], on_ref[...])
                r = gvr_ref[r0:r0 + sb, gdv + lo_:gdv + hi_].astype(F32)
                o_ref[r0:r0 + sb, lo_:hi_] = (o * _silu(r)).astype(BF16)

    states = [jnp.zeros((pw, vw), F32) for _ in range(npair)]
    loc = local(0)
    for i in range(nsb):
        nxt = local(i + 1) if i + 1 < nsb else None
        recurrent(i, loc, states)
        loc = nxt


def _post_kernel(h1_ref, om_ref, og_ref, ox_ref, nm_ref, wgz32_ref, gb_ref, wom32_ref, wog32_ref, wox32_ref,
                 wout32_ref, n2_ref, wg32_ref, wu32_ref, wd32_ref, nf_ref, o_ref,
                 wgz_ref, wom_ref, wog_ref, wox_ref, wout_ref, wgu_ref, wd_ref, h2_ref, u2_ref):
    i = pl.program_id(0)
    last = pl.num_programs(0) - 1
    refs = (h1_ref, om_ref, og_ref, ox_ref, nm_ref, wgz_ref, gb_ref, wom_ref, wog_ref, wox_ref,
            wout_ref, n2_ref, wgu_ref, wd_ref, nf_ref, o_ref, h2_ref, u2_ref)

    @pl.when(i < W_STEPS)
    def _():
        _convert_ffn(i, wg32_ref, wu32_ref, wd32_ref, wgu_ref, wd_ref)
        _convert_rows(wgz_ref, wgz32_ref[...], i, WIN_POST_ROWS)
        for dst, src in ((wom_ref, wom32_ref), (wog_ref, wog32_ref), (wox_ref, wox32_ref)):
            _convert_rows(dst, src[...], i, dst.shape[0] // W_STEPS)
        _convert_rows(wout_ref, wout32_ref[...], i, D_MODEL // W_STEPS)

    @pl.when(i == W_STEPS)
    def _():
        _post_tokens(*refs, merge=True, ffn=False)

    @pl.when(jnp.logical_and(i > W_STEPS, i < last))
    def _():
        _post_tokens(*refs, merge=True, ffn=True)

    @pl.when(i == last)
    def _():
        _post_tokens(*refs, merge=False, ffn=True)


def _post_tokens(h1_ref, om_ref, og_ref, ox_ref, nm_ref, wgz_ref, gb_ref, wom_ref, wog_ref, wox_ref,
                 wout_ref, n2_ref, wgu_ref, wd_ref, nf_ref, o_ref, h2_ref, u2_ref, *, merge, ffn):
    d = D_MODEL
    if merge:
        h1 = h1_ref[...]
        u = _rms(h1, nm_ref[...]).astype(BF16)
        branches = ((jnp.concatenate([om_ref[p] for p in range(om_ref.shape[0])], axis=1), wom_ref),
                    (og_ref[...], wog_ref), (ox_ref[...], wox_ref))

        def merge_tile(c0):
            acc = None
            for n, (o_in, w_br) in enumerate(branches):
                gate = _sigmoid(_dot_nt(u, wgz_ref[n * d + c0:n * d + c0 + MERGE_TILE, :])
                                + gb_ref[:, n * d + c0:n * d + c0 + MERGE_TILE])
                term = gate * _dot(o_in, w_br[:, c0:c0 + MERGE_TILE])
                acc = term if acc is None else acc + term
            return acc.astype(BF16)

    tiles = []
    if ffn:
        half = 0.5 * _swiglu_half(u2_ref[...], wgu_ref, wd_ref)
    if merge:
        tiles.append(merge_tile(0))
    if ffn:
        o_ref[...] = _rms(h2_ref[...] + half, nf_ref[...])
    if merge:
        for c0 in range(MERGE_TILE, d, MERGE_TILE):
            tiles.append(merge_tile(c0))
        h2 = h1 + _dot(jnp.concatenate(tiles, axis=1), wout_ref[...])
        h2_ref[...] = h2
        u2_ref[...] = _rms(h2, n2_ref[...]).astype(BF16)


def _cparams(sem, vmem=None, flags=None):
    return pltpu.CompilerParams(dimension_semantics=sem, vmem_limit_bytes=vmem, flags=flags)


def kernel(x, mem, positions, ffn1_norm, ffn1_wg, ffn1_wu, ffn1_wd, mix_norm, mem_norm, w_in, gate_bias,
           mla_q_norm, mla_w_uq, mla_kv_norm, mla_w_ukv, mla_w_o, gla_w_a2, gla_b_a, gla_o_norm, gla_w_o,
           x_w_kv, x_w_o, w_out, ffn2_norm, ffn2_wg, ffn2_wu, ffn2_wd, final_norm):
    B, S, D = x.shape
    T = B * S
    M = mem.shape[1]
    F = D_FF
    H = MLA_HEADS
    bf = lambda a: a.astype(BF16)
    row = lambda a: a.reshape(1, -1)

    w_in_t = jnp.swapaxes(w_in[0], 0, 1)
    w_uq = mla_w_uq[0].reshape(MLA_Q_RANK, H, MLA_NOPE + MLA_ROPE)
    w_uq = bf(jnp.pad(w_uq, ((0, 0), (0, 0), (0, HEAD_PAD - MLA_NOPE - MLA_ROPE))).reshape(MLA_Q_RANK, H * HEAD_PAD))
    w_ukv = mla_w_ukv[0].reshape(MLA_KV_RANK, H, MLA_NOPE + MLA_V)
    w_uk = bf(w_ukv[:, :, :MLA_NOPE].reshape(MLA_KV_RANK, H * MLA_NOPE))
    w_uv = bf(w_ukv[:, :, MLA_NOPE:].reshape(MLA_KV_RANK, H * MLA_V))
    w_a2 = bf(gla_w_a2[0])
    half = MLA_ROPE // 2
    inv_freq = ROPE_THETA ** (-jnp.arange(half, dtype=F32) / half)
    invf = jnp.concatenate([jnp.zeros((MLA_NOPE,), F32), inv_freq, inv_freq,
                            jnp.zeros((HEAD_PAD - MLA_NOPE - MLA_ROPE,), F32)]).reshape(1, HEAD_PAD)

    x2 = x.reshape(T, D)
    pos2 = positions.reshape(T // TM, 1, TM)
    nt = T // TM
    tok = lambda w: pl.BlockSpec((TM, w), lambda i: (jnp.clip(i - W_STEPS, 0, nt - 1), 0))
    tok_prev = lambda w: pl.BlockSpec((TM, w), lambda i: (jnp.clip(i - W_STEPS - 1, 0, nt - 1), 0))
    tok_pairs_prev = lambda w: pl.BlockSpec(
        (MLA_HEADS // 2, TM, w), lambda i: (0, jnp.clip(i - W_STEPS - 1, 0, nt - 1), 0))
    npair = H // 2
    tok_pairs = lambda w: pl.BlockSpec((npair, TM, w), lambda i: (0, jnp.clip(i - W_STEPS, 0, nt - 1), 0))
    wrows = lambda shape: pl.BlockSpec(
        (shape[0] // W_STEPS, shape[1]), lambda i: (jnp.minimum(i, W_STEPS - 1), 0))
    ffn_specs = [wrows((D, F)), wrows((D, F)), wrows((F, D))]
    ffn_scratch = [pltpu.VMEM((D, 2 * F), BF16), pltpu.VMEM((F, D), BF16)]

    gdk, gdv, xw = GLA_HEADS * GLA_DK, GLA_HEADS * GLA_DV, X_HEADS * X_DH
    pre_out_shapes = (
        jax.ShapeDtypeStruct((T, D), F32),
        jax.ShapeDtypeStruct((npair, T, 2 * HEAD_PAD), BF16),
        jax.ShapeDtypeStruct((npair, T, 2 * HEAD_PAD), BF16),
        jax.ShapeDtypeStruct((npair, T, 2 * MLA_V), BF16),
        jax.ShapeDtypeStruct((T, 3 * gdk), F32),
        jax.ShapeDtypeStruct((T, 2 * gdv), BF16),
        jax.ShapeDtypeStruct((T, xw), BF16),
    )
    h1, q, k, v, gf, gvr, xq = pl.pallas_call(
        _pre_kernel,
        out_shape=pre_out_shapes,
        grid=(W_STEPS + nt + 1,),
        in_specs=[tok(D), pl.BlockSpec((1, 1, TM), lambda i: (jnp.clip(i - W_STEPS - 1, 0, nt - 1), 0, 0)),
                  _const_spec((1, HEAD_PAD)), _const_spec((1, D)), _const_spec((1, D)),
                  *ffn_specs, pl.BlockSpec((WIN_PRE_ROWS, D), lambda i: (jnp.minimum(i, W_STEPS - 1), 0)),
                  _const_spec((1, MLA_Q_RANK)), _const_spec((MLA_Q_RANK, H * HEAD_PAD)),
                  _const_spec((1, MLA_KV_RANK)), _const_spec((MLA_KV_RANK, H * MLA_NOPE)),
                  _const_spec((MLA_KV_RANK, H * MLA_V)), _const_spec((GLA_GATE_RANK, gdk)), _const_spec((1, gdk))],
        out_specs=(tok(D),) + tuple(tok_prev(s.shape[1]) if len(s.shape) == 2 else tok_pairs_prev(s.shape[2])
                                    for s in pre_out_shapes[1:]),
        scratch_shapes=[*ffn_scratch, pltpu.VMEM((W_STEPS * WIN_PRE_ROWS, D), BF16),
                        pltpu.VMEM((2, TM, D), BF16)],
        compiler_params=_cparams(("arbitrary",), VMEM_LIMIT),
        name="pre",
    )(x2, pos2, invf, row(ffn1_norm[0]), row(mix_norm[0]), ffn1_wg[0], ffn1_wu[0], ffn1_wd[0],
      w_in_t, row(mla_q_norm[0]), w_uq, row(mla_kv_norm[0]), w_uk, w_uv, w_a2, row(gla_b_a[0]))

    seq_pairs = lambda w: pl.BlockSpec((npair, S, w), lambda b: (0, b, 0))
    o_mla = pl.pallas_call(
        _mla_kernel,
        out_shape=jax.ShapeDtypeStruct((npair, T, 2 * MLA_V), BF16),
        grid=(B,),
        in_specs=[seq_pairs(2 * HEAD_PAD), seq_pairs(2 * HEAD_PAD), seq_pairs(2 * MLA_V)],
        out_specs=seq_pairs(2 * MLA_V),
        scratch_shapes=[pltpu.VMEM((MLA_GROUP, 2 * MLA_VT_ROWS, S), BF16)],
        compiler_params=_cparams(("parallel",), VMEM_LIMIT),
        name="mla",
    )(q, k, v)

    nqx = S // TQX
    o_x = pl.pallas_call(
        _xattn_kernel,
        out_shape=jax.ShapeDtypeStruct((T, xw), BF16),
        grid=(B, nqx),
        in_specs=[pl.BlockSpec((TQX, xw), lambda b, i: (b * nqx + i, 0)),
                  pl.BlockSpec((M, D), lambda b, i: (b, 0)),
                  _const_spec((1, D)), _const_spec((D, 2 * xw))],
        out_specs=pl.BlockSpec((TQX, xw), lambda b, i: (b * nqx + i, 0)),
        scratch_shapes=[pltpu.VMEM((M, 2 * xw), BF16)],
        compiler_params=_cparams(("arbitrary", "arbitrary")),
        name="xattn",
    )(xq, mem.reshape(B * M, D), row(mem_norm[0]), bf(x_w_kv[0]))

    o_gla = pl.pallas_call(
        _gla_kernel,
        out_shape=jax.ShapeDtypeStruct((T, gdv), BF16),
        grid=(B,),
        in_specs=[pl.BlockSpec((S, 3 * gdk), lambda b: (b, 0)),
                  pl.BlockSpec((S, 2 * gdv), lambda b: (b, 0)),
                  _const_spec((1, GLA_DV))],
        out_specs=pl.BlockSpec((S, gdv), lambda b: (b, 0)),
        compiler_params=_cparams(("parallel",), VMEM_LIMIT),
        name="gla",
    )(gf, gvr, row(gla_o_norm[0]))

    out = pl.pallas_call(
        _post_kernel,
        out_shape=jax.ShapeDtypeStruct((T, D), F32),
        grid=(W_STEPS + nt + 1,),
        in_specs=[tok(D), tok_pairs(2 * MLA_V), tok(gdv), tok(xw), _const_spec((1, D)),
                  pl.BlockSpec((pl.Element(WIN_POST_ROWS), pl.Element(D)),
                               lambda i: (pl.multiple_of(C_GZ + jnp.minimum(i, W_STEPS - 1) * WIN_POST_ROWS, 16), 0)),
                  _const_spec((1, N_BRANCH * D)),
                  wrows((H * MLA_V, D)), wrows((gdv, D)), wrows((xw, D)),
                  wrows((D, D)), _const_spec((1, D)),
                  *ffn_specs, _const_spec((1, D))],
        out_specs=tok_prev(D),
        scratch_shapes=[pltpu.VMEM((N_BRANCH * D, D), BF16), pltpu.VMEM((H * MLA_V, D), BF16),
                        pltpu.VMEM((gdv, D), BF16), pltpu.VMEM((xw, D), BF16), pltpu.VMEM((D, D), BF16),
                        *ffn_scratch, pltpu.VMEM((TM, D), F32), pltpu.VMEM((TM, D), BF16)],
        compiler_params=_cparams(("arbitrary",), VMEM_LIMIT),
        name="post",
    )(h1, o_mla, o_gla, o_x, row(mix_norm[0]), w_in_t, gate_bias[0].reshape(1, N_BRANCH * D),
      mla_w_o[0], gla_w_o[0], x_w_o[0], w_out[0], row(ffn2_norm[0]),
      ffn2_wg[0], ffn2_wu[0], ffn2_wd[0], row(final_norm))
    return out.reshape(B, S, D)
```
